```python
import functools
import jax, jax.numpy as jnp
from jax import lax
import numpy as np

D_MODEL = 1024
BATCH = 4
SEQ = 4096
DEPTH = 2
DEC_BATCH = 128
DEC_SEQ = 1
PAST_LEN = 2048
PAGE_SIZE = 128

N_MIXERS = 2
N_RET_LAYERS = (DEPTH + N_MIXERS - 1) // N_MIXERS
N_DSA_LAYERS = DEPTH // N_MIXERS
ALPHA = (2.0 * DEPTH) ** 0.25
BETA = (8.0 * DEPTH) ** -0.25
LN_EPS = 1e-5

RET_HEADS = 4
RET_DK = 256
RET_DV = 512
RET_QK = RET_HEADS * RET_DK
RET_V = RET_HEADS * RET_DV
RET_IN = 2 * RET_QK + 2 * RET_V
RET_CHUNK = 128
RET_ANGLE_BASE = 10000.0
GN_EPS = 1e-6

DSA_HEADS = 8
DSA_KV_HEADS = 2
DSA_HEAD_DIM = 128
DSA_GROUP = DSA_HEADS // DSA_KV_HEADS
DSA_Q = DSA_HEADS * DSA_HEAD_DIM
DSA_KV = DSA_KV_HEADS * DSA_HEAD_DIM
IDX_HEADS = 16
IDX_DIM = 64
DSA_TOPK = 256
Q_BLOCK = 128
ROPE_THETA = 500000.0
ROT_DIM = DSA_HEAD_DIM // 4
IDX_ROT_DIM = IDX_DIM // 4
DSA_SPLITS = [DSA_Q, DSA_Q + DSA_KV, DSA_Q + 2 * DSA_KV,
              DSA_Q + 2 * DSA_KV + IDX_HEADS * IDX_DIM,
              DSA_Q + 2 * DSA_KV + IDX_HEADS * IDX_DIM + IDX_DIM]
DSA_IN = DSA_Q + 2 * DSA_KV + IDX_HEADS * IDX_DIM + IDX_DIM + IDX_HEADS

N_EXPERTS = 16
N_GROUPS = 4
EXPERTS_PER_GROUP = N_EXPERTS // N_GROUPS
TOPK_EXPERTS = 2
D_EXPERT = 512

kernel_name = "retention_dsa_hybrid_step"

F32 = jnp.float32


def layer_norm(x, g, b):
    xf = x.astype(F32)
    mu = jnp.mean(xf, -1, keepdims=True)
    var = jnp.mean(jnp.square(xf - mu), -1, keepdims=True)
    return ((xf - mu) * lax.rsqrt(var + LN_EPS) * g.astype(F32) + b.astype(F32)).astype(x.dtype)


def rot_tables(pos, inv_freq):
    ang = pos.astype(F32)[:, None] * inv_freq[None, :]
    return jnp.cos(ang), jnp.sin(ang)


def rope_inv_freq(rot_dim):
    return 1.0 / (ROPE_THETA ** (jnp.arange(0, rot_dim, 2, dtype=F32) / rot_dim))


def retnet_inv_freq():
    return jnp.power(RET_ANGLE_BASE, -jnp.linspace(0.0, 1.0, RET_DK // 2, dtype=F32))


def rotate(x, cos, sin):
    half = cos.shape[-1]
    x1 = x[..., :half].astype(F32)
    x2 = x[..., half:2 * half].astype(F32)
    return jnp.concatenate([x1 * cos - x2 * sin, x1 * sin + x2 * cos,
                            x[..., 2 * half:].astype(F32)], -1).astype(x.dtype)


def retention_scan(q, k, v, s0):
    B, T, H, _ = q.shape
    C = min(RET_CHUNK, T)
    n = T // C
    lg = jnp.log(1.0 - jnp.power(2.0, -5.0 - jnp.arange(H, dtype=F32)))
    idx = jnp.arange(C, dtype=F32)
    diff = idx[:, None] - idx[None, :]
    decay = jnp.where(diff >= 0, jnp.exp(lg[:, None, None] * jnp.maximum(diff, 0.0)), 0.0)
    q_dec = jnp.exp(lg[None, :] * (idx[:, None] + 1.0))
    k_dec = jnp.exp(lg[None, :] * (C - 1.0 - idx)[:, None])
    chunk_dec = jnp.exp(lg * C)

    def step(s, qkv):
        qc, kc, vc = (a.astype(F32) for a in qkv)
        att = jnp.einsum('bnhd,bmhd->bhnm', qc, kc) * decay
        o = (jnp.einsum('bhnm,bmhv->bnhv', att, vc)
             + jnp.einsum('bnhd,bhdv->bnhv', qc, s) * q_dec[None, :, :, None])
        s = (chunk_dec[None, :, None, None] * s
             + jnp.einsum('bmhd,bmhv->bhdv', kc * k_dec[None, :, :, None], vc))
        return s, o

    to_chunks = lambda a: a.reshape(B, n, C, *a.shape[2:]).swapaxes(0, 1)
    s, o = lax.scan(step, s0.astype(F32), (to_chunks(q), to_chunks(k), to_chunks(v)))
    return o.swapaxes(0, 1).reshape(B, T, H, RET_DV), s


def retention_mixer(h, s0, w_in, gn_g, w_out, cos, sin):
    B, T, _ = h.shape
    q, k, v, g = jnp.split(h @ w_in, [RET_QK, 2 * RET_QK, 2 * RET_QK + RET_V], axis=-1)
    q = rotate(q.reshape(B, T, RET_HEADS, RET_DK), cos[:, None], sin[:, None])
    k = rotate(k.reshape(B, T, RET_HEADS, RET_DK), cos[:, None], sin[:, None]) * (RET_DK ** -0.5)
    v = v.reshape(B, T, RET_HEADS, RET_DV)
    o, s = retention_scan(q, k, v, s0)
    mu = jnp.mean(o, -1, keepdims=True)
    var = jnp.mean(jnp.square(o - mu), -1, keepdims=True)
    on = (o - mu) * lax.rsqrt(var + GN_EPS) * gn_g.astype(F32).reshape(RET_HEADS, RET_DV)
    y = (on.reshape(B, T, RET_V).astype(h.dtype) * jax.nn.silu(g)) @ w_out
    return y, s


def dsa_project(h, w_in, pos):
    B, T, _ = h.shape
    q, k, v, qi, ki, wi = jnp.split(h @ w_in, DSA_SPLITS, axis=-1)
    cos, sin = rot_tables(pos, rope_inv_freq(ROT_DIM))
    cos_i, sin_i = rot_tables(pos, rope_inv_freq(IDX_ROT_DIM))
    q = rotate(q.reshape(B, T, DSA_HEADS, DSA_HEAD_DIM), cos[:, None], sin[:, None])
    k = rotate(k.reshape(B, T, DSA_KV_HEADS, DSA_HEAD_DIM), cos[:, None], sin[:, None])
    v = v.reshape(B, T, DSA_KV_HEADS, DSA_HEAD_DIM)
    qi = rotate(qi.reshape(B, T, IDX_HEADS, IDX_DIM), cos_i[:, None], sin_i[:, None])
    ki = rotate(ki, cos_i, sin_i)
    return q, k, v, qi, ki, wi


def sparse_attend(q, qi, wi, q_pos, k_all, v_all, ki_all, topk):
    B, Q = q.shape[0], q.shape[1]
    L = k_all.shape[1]
    dots = jnp.einsum('bqhd,bld->bqhl', qi.astype(F32), ki_all.astype(F32)) * (IDX_DIM ** -0.5)
    score = jnp.einsum('bqh,bqhl->bql', wi.astype(F32) * (IDX_HEADS ** -0.5), jax.nn.relu(dots))
    allowed = jnp.arange(L, dtype=jnp.int32)[None, :] <= q_pos[:, None]
    score = jnp.where(allowed[None], score, -jnp.inf)
    _, idx = lax.top_k(score, topk)
    valid = idx <= q_pos[None, :, None]
    gather = jax.vmap(lambda a, i: a[i])
    k_sel = gather(k_all, idx)
    v_sel = gather(v_all, idx)
    qg = q.reshape(B, Q, DSA_KV_HEADS, DSA_GROUP, DSA_HEAD_DIM)
    logits = jnp.einsum('bqngd,bqknd->bqngk', qg, k_sel).astype(F32) * (DSA_HEAD_DIM ** -0.5)
    logits = jnp.where(valid[:, :, None, None, :], logits, -jnp.inf)
    p = jax.nn.softmax(logits, axis=-1).astype(v_sel.dtype)
    o = jnp.einsum('bqngk,bqknd->bqngd', p, v_sel)
    return o.reshape(B, Q, DSA_Q)


def dsa_prompt(h, w_in, w_out, pos):
    B, T, _ = h.shape
    q, k, v, qi, ki, wi = dsa_project(h, w_in, pos)
    topk = min(DSA_TOPK, T // 4)
    qb = min(Q_BLOCK, T)
    nb = T // qb
    blk = lambda a: a.reshape(B, nb, qb, *a.shape[2:]).swapaxes(0, 1)
    o = lax.map(lambda xs: sparse_attend(xs[0], xs[1], xs[2], xs[3], k, v, ki, topk),
                (blk(q), blk(qi), blk(wi), pos.reshape(nb, qb)))
    o = o.swapaxes(0, 1).reshape(B, T, DSA_Q)
    return o @ w_out, (k, v, ki)


def dsa_sample(h, cache_k, cache_v, cache_kidx, page_table, w_in, w_out, pos):
    DB, T, _ = h.shape
    q, k, v, qi, ki, wi = dsa_project(h, w_in, pos)
    past = lambda c: c[page_table].reshape(DB, -1, *c.shape[2:])
    k_all = jnp.concatenate([past(cache_k), k.astype(cache_k.dtype)], axis=1)
    v_all = jnp.concatenate([past(cache_v), v.astype(cache_v.dtype)], axis=1)
    ki_all = jnp.concatenate([past(cache_kidx), ki.astype(cache_kidx.dtype)], axis=1)
    topk = min(DSA_TOPK, k_all.shape[1] // 4)
    o = sparse_attend(q, qi, wi, pos, k_all, v_all, ki_all, topk)
    return o @ w_out, (k, v, ki)


def grouped_moe(h, w_router, b_router, w_gate, w_up, w_down):
    logits = h.astype(F32) @ w_router.astype(F32) + b_router.astype(F32)
    probs = jax.nn.softmax(logits, axis=-1)
    grp = probs.reshape(*probs.shape[:-1], N_GROUPS, EXPERTS_PER_GROUP)
    grp_score = lax.top_k(grp, TOPK_EXPERTS)[0].sum(-1)
    g_sel = jnp.argmax(grp_score, axis=-1)
    in_grp = (jnp.arange(N_EXPERTS) // EXPERTS_PER_GROUP) == g_sel[..., None]
    vals, idx = lax.top_k(jnp.where(in_grp, probs, -jnp.inf), TOPK_EXPERTS)
    wts = vals / jnp.sum(vals, -1, keepdims=True)
    gates = jnp.sum(jax.nn.one_hot(idx, N_EXPERTS, dtype=F32) * wts[..., None], axis=-2).astype(h.dtype)
    y = jnp.zeros_like(h)
    for e in range(N_EXPERTS):
        a = jax.nn.silu(h @ w_gate[e]) * (h @ w_up[e])
        y = y + gates[..., e:e + 1] * (a @ w_down[e])
    return y


def block(x, c, l, mixer, w_mod, b_mod, ln1_g, ln1_b, ln2_g, ln2_b,
          w_router, b_router, w_gate, w_up, w_down):
    mod = (jax.nn.silu(c) @ w_mod[l] + b_mod[l])[:, None, :]
    sh1, sc1, g1, sh2, sc2, g2 = jnp.split(mod, 6, axis=-1)
    y, st = mixer(x * (1 + sc1) + sh1)
    x = layer_norm(ALPHA * x + g1 * y, ln1_g[l], ln1_b[l])
    f = grouped_moe(x * (1 + sc2) + sh2, w_router, b_router, w_gate[l], w_up[l], w_down[l])
    x = layer_norm(ALPHA * x + g2 * f, ln2_g[l], ln2_b[l])
    return x, st


def run_trunk(x, c, mix_ret, mix_dsa, block_params):
    states_ret, states_dsa = [], []
    for l in range(DEPTH):
        j = l // N_MIXERS
        if l % N_MIXERS == 0:
            x, st = block(x, c, l, functools.partial(mix_ret, j=j), *block_params)
            states_ret.append(st)
        else:
            x, st = block(x, c, l, functools.partial(mix_dsa, j=j), *block_params)
            states_dsa.append(st)
    return x, states_ret, states_dsa


def setup_inputs(seed: int = 0) -> dict:
    key = jax.random.key(seed)
    ks = jax.random.split(key, 32)
    nrm = lambda k, shape, s: jax.random.normal(k, shape, F32) * s
    n_pages = PAST_LEN // PAGE_SIZE
    n_used = DEC_BATCH * n_pages
    n_pool = n_used + max(1, n_used // 4)
    page_table = jax.random.permutation(ks[0], n_pool)[:n_used].reshape(DEC_BATCH, n_pages).astype(jnp.int32)
    ret_cols = jnp.concatenate([jnp.ones((2 * RET_QK,), F32), jnp.full((RET_V,), BETA, F32),
                                jnp.ones((RET_V,), F32)])
    dsa_cols = jnp.concatenate([jnp.ones((DSA_Q + DSA_KV,), F32), jnp.full((DSA_KV,), BETA, F32),
                                jnp.ones((DSA_IN - DSA_Q - 2 * DSA_KV,), F32)])
    return {
        "x_prompt": nrm(ks[1], (BATCH, SEQ, D_MODEL), 1.0),
        "x_sample": nrm(ks[2], (DEC_BATCH, DEC_SEQ, D_MODEL), 1.0),
        "state_ret": nrm(ks[3], (N_RET_LAYERS, DEC_BATCH, RET_HEADS, RET_DK, RET_DV), 0.5),
        "cache_k": nrm(ks[4], (N_DSA_LAYERS, n_pool, PAGE_SIZE, DSA_KV_HEADS, DSA_HEAD_DIM), 1.0),
        "cache_v": nrm(ks[5], (N_DSA_LAYERS, n_pool, PAGE_SIZE, DSA_KV_HEADS, DSA_HEAD_DIM), BETA),
        "cache_kidx": nrm(ks[6], (N_DSA_LAYERS, n_pool, PAGE_SIZE, IDX_DIM), 1.0),
        "page_table": page_table,
        "c_prompt": nrm(ks[7], (BATCH, D_MODEL), 1.0),
        "c_sample": nrm(ks[8], (DEC_BATCH, D_MODEL), 1.0),
        "w_mod": nrm(ks[9], (DEPTH, D_MODEL, 6 * D_MODEL), D_MODEL ** -0.5),
        "b_mod": nrm(ks[10], (DEPTH, 6 * D_MODEL), 0.02),
        "ln1_g": 1.0 + nrm(ks[11], (DEPTH, D_MODEL), 0.02),
        "ln1_b": nrm(ks[12], (DEPTH, D_MODEL), 0.02),
        "ln2_g": 1.0 + nrm(ks[13], (DEPTH, D_MODEL), 0.02),
        "ln2_b": nrm(ks[14], (DEPTH, D_MODEL), 0.02),
        "w_in_ret": nrm(ks[15], (N_RET_LAYERS, D_MODEL, RET_IN), D_MODEL ** -0.5) * ret_cols,
        "gn_ret_g": 1.0 + nrm(ks[16], (N_RET_LAYERS, RET_V), 0.02),
        "w_out_ret": nrm(ks[17], (N_RET_LAYERS, RET_V, D_MODEL), BETA * RET_V ** -0.5),
        "w_in_dsa": nrm(ks[18], (N_DSA_LAYERS, D_MODEL, DSA_IN), D_MODEL ** -0.5) * dsa_cols,
        "w_out_dsa": nrm(ks[19], (N_DSA_LAYERS, DSA_Q, D_MODEL), BETA * DSA_Q ** -0.5),
        "w_router": nrm(ks[20], (D_MODEL, N_EXPERTS), D_MODEL ** -0.5),
        "b_router": nrm(ks[21], (N_EXPERTS,), 0.01),
        "w_gate": nrm(ks[22], (DEPTH, N_EXPERTS, D_MODEL, D_EXPERT), D_MODEL ** -0.5),
        "w_up": nrm(ks[23], (DEPTH, N_EXPERTS, D_MODEL, D_EXPERT), BETA * D_MODEL ** -0.5),
        "w_down": nrm(ks[24], (DEPTH, N_EXPERTS, D_EXPERT, D_MODEL), BETA * D_EXPERT ** -0.5),
    }


def reference(x_prompt, x_sample, state_ret, cache_k, cache_v, cache_kidx, page_table,
              c_prompt, c_sample, w_mod, b_mod, ln1_g, ln1_b, ln2_g, ln2_b,
              w_in_ret, gn_ret_g, w_out_ret, w_in_dsa, w_out_dsa,
              w_router, b_router, w_gate, w_up, w_down):
    block_params = (w_mod, b_mod, ln1_g, ln1_b, ln2_g, ln2_b, w_router, b_router, w_gate, w_up, w_down)
    pos_p = jnp.arange(SEQ, dtype=jnp.int32)
    pos_s = PAST_LEN + jnp.arange(x_sample.shape[1], dtype=jnp.int32)
    cos_rp, sin_rp = rot_tables(pos_p, retnet_inv_freq())
    cos_rs, sin_rs = rot_tables(pos_s, retnet_inv_freq())

    ret_p = lambda h, j: retention_mixer(
        h, jnp.zeros((h.shape[0], RET_HEADS, RET_DK, RET_DV), F32),
        w_in_ret[j], gn_ret_g[j], w_out_ret[j], cos_rp, sin_rp)
    dsa_p = lambda h, j: dsa_prompt(h, w_in_dsa[j], w_out_dsa[j], pos_p)
    y_prompt, sr_p, sd_p = run_trunk(x_prompt, c_prompt, ret_p, dsa_p, block_params)

    ret_s = lambda h, j: retention_mixer(
        h, state_ret[j], w_in_ret[j], gn_ret_g[j], w_out_ret[j], cos_rs, sin_rs)
    dsa_s = lambda h, j: dsa_sample(h, cache_k[j], cache_v[j], cache_kidx[j], page_table,
                                    w_in_dsa[j], w_out_dsa[j], pos_s)
    y_sample, sr_s, sd_s = run_trunk(x_sample, c_sample, ret_s, dsa_s, block_params)

    ret_prompt = jnp.stack(sr_p)
    ret_sample = jnp.stack(sr_s)
    k_prompt = jnp.stack([s[0] for s in sd_p])
    v_prompt = jnp.stack([s[1] for s in sd_p])
    kidx_prompt = jnp.stack([s[2] for s in sd_p])
    k_sample = jnp.stack([s[0] for s in sd_s])
    v_sample = jnp.stack([s[1] for s in sd_s])
    kidx_sample = jnp.stack([s[2] for s in sd_s])
    return (y_prompt, y_sample, ret_prompt, ret_sample, k_prompt, v_prompt, kidx_prompt,
            k_sample, v_sample, kidx_sample)
```

```python
import functools

import jax
import jax.numpy as jnp
from jax import lax
from jax.experimental import pallas as pl
from jax.experimental.pallas import tpu as pltpu

F32 = jnp.float32
BF16 = jnp.bfloat16
I32 = jnp.int32

D_MODEL = 1024
DEPTH = 2
ALPHA = (2.0 * DEPTH) ** 0.25
LN_EPS = 1e-5
GN_EPS = 1e-6

RET_HEADS = 4
RET_DK = 256
RET_DV = 512
RET_QK = RET_HEADS * RET_DK
RET_V = RET_HEADS * RET_DV
RET_IN = 2 * RET_QK + 2 * RET_V
RET_ANGLE_BASE = 10000.0

DSA_HEADS = 8
DSA_KV_HEADS = 2
DSA_HEAD_DIM = 128
DSA_GROUP = DSA_HEADS // DSA_KV_HEADS
DSA_Q = DSA_HEADS * DSA_HEAD_DIM
DSA_KV = DSA_KV_HEADS * DSA_HEAD_DIM
IDX_HEADS = 16
IDX_DIM = 64
DSA_TOPK = 256
ROPE_THETA = 500000.0
ROT_DIM = DSA_HEAD_DIM // 4
IDX_ROT_DIM = IDX_DIM // 4
DSA_IN = DSA_Q + 2 * DSA_KV + IDX_HEADS * IDX_DIM + IDX_DIM + IDX_HEADS

N_EXPERTS = 16
N_GROUPS = 4
EXPERTS_PER_GROUP = N_EXPERTS // N_GROUPS
D_EXPERT = 512

LANES = 128
SUBLANES = 8
VMEM_LIMIT = 56 * 1024 * 1024
NEG_BIG = -1e30
INT_MIN = -2147483648


def _cparams(sem):
    return pltpu.CompilerParams(dimension_semantics=sem, vmem_limit_bytes=VMEM_LIMIT)


def _silu(x):
    return x * (1.0 / (1.0 + jnp.exp(-x)))


def _layer_norm(z, g, b):
    mu = jnp.mean(z, -1, keepdims=True)
    d = z - mu
    var = jnp.mean(d * d, -1, keepdims=True)
    return d * lax.rsqrt(var + LN_EPS) * g + b


def _mod_kernel(c_ref, w_ref, b_ref, o_ref):
    a = _silu(c_ref[...]).astype(BF16)
    o_ref[...] = jnp.dot(a, w_ref[...], preferred_element_type=F32) + b_ref[...]


def modulation(c, w_mod_bf, b_mod):
    R = c.shape[0]
    tn = 1536
    return pl.pallas_call(
        _mod_kernel,
        grid=(DEPTH, 6 * D_MODEL // tn),
        in_specs=[pl.BlockSpec((R, D_MODEL), lambda l, j: (0, 0)),
                  pl.BlockSpec((None, D_MODEL, tn), lambda l, j: (l, 0, j)),
                  pl.BlockSpec((None, 1, tn), lambda l, j: (l, 0, j))],
        out_specs=pl.BlockSpec((None, R, tn), lambda l, j: (l, 0, j)),
        out_shape=jax.ShapeDtypeStruct((DEPTH, R, 6 * D_MODEL), F32),
        compiler_params=_cparams(("parallel", "parallel")),
        name="modulation",
    )(c, w_mod_bf, b_mod.reshape(DEPTH, 1, 6 * D_MODEL))


def _proj_kernel(x_ref, sc_ref, sh_ref, w_ref, o_ref, h_scr):
    @pl.when(pl.program_id(1) == 0)
    def _():
        h_scr[...] = (x_ref[...] * (1.0 + sc_ref[...]) + sh_ref[...]).astype(BF16)

    o_ref[...] = jnp.dot(h_scr[...], w_ref[...], preferred_element_type=F32)


def _mod_spec(R, tiles_per_group):
    return pl.BlockSpec((None, R, D_MODEL), lambda i, *_: (i // tiles_per_group, 0, 0))


def mod_proj(x, sc, sh, w_bf, tm, tn):
    N = x.shape[0]
    G, R, _ = sc.shape
    n_out = w_bf.shape[1]
    tpg = (N // G) // tm
    return pl.pallas_call(
        _proj_kernel,
        grid=(N // tm, n_out // tn),
        in_specs=[pl.BlockSpec((tm, D_MODEL), lambda i, j: (i, 0)),
                  _mod_spec(R, tpg), _mod_spec(R, tpg),
                  pl.BlockSpec((D_MODEL, tn), lambda i, j: (0, j))],
        out_specs=pl.BlockSpec((tm, tn), lambda i, j: (i, j)),
        out_shape=jax.ShapeDtypeStruct((N, n_out), F32),
        scratch_shapes=[pltpu.VMEM((tm, D_MODEL), BF16)],
        compiler_params=_cparams(("parallel", "arbitrary")),
        name="mod_proj",
    )(x, sc, sh, w_bf)


def _rot_half(x, cos, sin):
    half = cos.shape[-1]
    x1, x2 = x[:, :half], x[:, half:]
    return jnp.concatenate([x1 * cos - x2 * sin, x1 * sin + x2 * cos], axis=1)


def _group_norm_gate(o, gn, g):
    mu = jnp.mean(o, -1, keepdims=True)
    d = o - mu
    var = jnp.mean(d * d, -1, keepdims=True)
    return d * lax.rsqrt(var + GN_EPS) * gn * _silu(g)


def _ret_prompt_kernel(q_ref, k_ref, v_ref, g_ref, cos_ref, sin_ref, lg_ref, gn_ref,
                       o_ref, s_ref, s_scr, *, chunk):
    c = pl.program_id(2)

    @pl.when(c == 0)
    def _():
        s_scr[...] = jnp.zeros_like(s_scr)

    cos, sin = cos_ref[...], sin_ref[...]
    lg_row = lg_ref[...]
    lg = lg_row[:, :1]
    q = _rot_half(q_ref[...], cos, sin)
    k = _rot_half(k_ref[...], cos, sin) * (RET_DK ** -0.5)
    vb = v_ref[...].astype(BF16)
    qb = q.astype(BF16)

    row = lax.broadcasted_iota(I32, (chunk, chunk), 0)
    col = lax.broadcasted_iota(I32, (chunk, chunk), 1)
    diff = (row - col).astype(F32)
    decay = jnp.where(diff >= 0, jnp.exp(lg_row * jnp.maximum(diff, 0.0)), 0.0)
    idx = lax.broadcasted_iota(I32, (chunk, 1), 0).astype(F32)
    q_dec = jnp.exp(lg * (idx + 1.0))
    k_dec = jnp.exp(lg * (chunk - 1.0 - idx))
    chunk_dec = jnp.exp(lg * chunk)

    s = s_scr[...]
    att = lax.dot_general(qb, k.astype(BF16), (((1,), (1,)), ((), ())),
                          preferred_element_type=F32) * decay
    o = (jnp.dot(att.astype(BF16), vb, preferred_element_type=F32)
         + jnp.dot(qb, s.astype(BF16), preferred_element_type=F32) * q_dec)
    kd_t = (k * k_dec).T.astype(BF16)
    s_new = chunk_dec * s + jnp.dot(kd_t, vb, preferred_element_type=F32)
    s_scr[...] = s_new
    o_ref[...] = _group_norm_gate(o, gn_ref[...], g_ref[...]).astype(o_ref.dtype)

    @pl.when(c == pl.num_programs(2) - 1)
    def _():
        s_ref[...] = s_new


def _log_gamma():
    return jnp.log(1.0 - jnp.power(2.0, -5.0 - jnp.arange(RET_HEADS, dtype=F32)))


def retention_prompt(proj, gn_g, cos, sin, chunk):
    B, T, _ = proj.shape
    lg_tab = jnp.broadcast_to(_log_gamma()[:, None, None], (RET_HEADS, 1, chunk))
    qk_blk = lambda off: pl.BlockSpec((None, chunk, RET_DK), lambda b, h, c: (b, c, off + h))
    v_blk = lambda off: pl.BlockSpec((None, chunk, RET_DV), lambda b, h, c: (b, c, off + h))
    tab = pl.BlockSpec((chunk, RET_DK // 2), lambda b, h, c: (c, 0))
    return pl.pallas_call(
        functools.partial(_ret_prompt_kernel, chunk=chunk),
        grid=(B, RET_HEADS, T // chunk),
        in_specs=[qk_blk(0), qk_blk(RET_QK // RET_DK),
                  v_blk(2 * RET_QK // RET_DV), v_blk((2 * RET_QK + RET_V) // RET_DV),
                  tab, tab,
                  pl.BlockSpec((None, 1, chunk), lambda b, h, c: (h, 0, 0)),
                  pl.BlockSpec((1, RET_DV), lambda b, h, c: (0, h))],
        out_specs=[pl.BlockSpec((None, chunk, RET_DV), lambda b, h, c: (b, c, h)),
                   pl.BlockSpec((None, None, RET_DK, RET_DV), lambda b, h, c: (b, h, 0, 0))],
        out_shape=[jax.ShapeDtypeStruct((B, T, RET_V), BF16),
                   jax.ShapeDtypeStruct((B, RET_HEADS, RET_DK, RET_DV), F32)],
        scratch_shapes=[pltpu.VMEM((RET_DK, RET_DV), F32)],
        compiler_params=_cparams(("parallel", "parallel", "arbitrary")),
        name="retention_prompt",
    )(proj, proj, proj, proj, cos, sin, lg_tab, gn_g.reshape(1, RET_V))


def _ret_sample_kernel(qk_ref, v_ref, g_ref, s0_ref, cos_ref, sin_ref, gam_ref, gn_ref,
                       o_ref, s_ref):
    t = _rot_half(qk_ref[...], cos_ref[...], sin_ref[...])
    row = lax.broadcasted_iota(I32, t.shape, 0)
    t = jnp.where(row >= RET_HEADS, t * (RET_DK ** -0.5), t)
    pad = jnp.zeros((LANES - 2 * RET_HEADS, RET_DK), F32)
    t_t = jnp.concatenate([t, pad], axis=0).T
    for h in range(RET_HEADS):
        qc = t_t[:, h:h + 1]
        kc = t_t[:, RET_HEADS + h:RET_HEADS + h + 1]
        gam = gam_ref[h]
        vh = v_ref[h:h + 1, :]
        s = s0_ref[h]
        qk_dot = jnp.sum(t[h:h + 1, :] * t[RET_HEADS + h:RET_HEADS + h + 1, :], -1, keepdims=True)
        o = qk_dot * vh + gam * jnp.sum(s * qc, axis=0, keepdims=True)
        s_ref[h] = gam * s + kc * vh
        o_ref[h:h + 1, :] = _group_norm_gate(o, gn_ref[h:h + 1, :], g_ref[h:h + 1, :])


def retention_sample(proj, s0, gn_g, cos, sin):
    DB = proj.shape[0]
    qk = proj[:, :2 * RET_QK].reshape(DB, 2 * RET_HEADS, RET_DK)
    v = proj[:, 2 * RET_QK:2 * RET_QK + RET_V].reshape(DB, RET_HEADS, RET_DV)
    g = proj[:, 2 * RET_QK + RET_V:].reshape(DB, RET_HEADS, RET_DV)
    gam = jnp.broadcast_to(jnp.exp(_log_gamma())[:, None, None], (RET_HEADS, 1, RET_DV))
    full = lambda *shape: pl.BlockSpec(shape, lambda b: (0,) * len(shape))
    per_b = lambda *shape: pl.BlockSpec((None,) + shape, lambda b: (b,) + (0,) * len(shape))
    return pl.pallas_call(
        _ret_sample_kernel,
        grid=(DB,),
        in_specs=[per_b(2 * RET_HEADS, RET_DK), per_b(RET_HEADS, RET_DV), per_b(RET_HEADS, RET_DV),
                  per_b(RET_HEADS, RET_DK, RET_DV),
                  full(1, RET_DK // 2), full(1, RET_DK // 2),
                  full(RET_HEADS, 1, RET_DV), full(RET_HEADS, RET_DV)],
        out_specs=[per_b(RET_HEADS, RET_DV), per_b(RET_HEADS, RET_DK, RET_DV)],
        out_shape=[jax.ShapeDtypeStruct((DB, RET_HEADS, RET_DV), F32),
                   jax.ShapeDtypeStruct((DB, RET_HEADS, RET_DK, RET_DV), F32)],
        compiler_params=_cparams(("parallel",)),
        name="retention_sample",
    )(qk, v, g, s0, cos, sin, gam, gn_g.reshape(RET_HEADS, RET_DV))


def _out_ln_kernel(a_ref, w_ref, x_ref, g_ref, lng_ref, lnb_ref, o_ref):
    y = jnp.dot(a_ref[...].astype(BF16), w_ref[...], preferred_element_type=F32)
    o_ref[...] = _layer_norm(ALPHA * x_ref[...] + g_ref[...] * y, lng_ref[...], lnb_ref[...])


def out_proj_ln(a, w_bf, x, gate, ln_g, ln_b, tm):
    N, K = a.shape
    G, R, _ = gate.shape
    tpg = (N // G) // tm
    row = pl.BlockSpec((1, D_MODEL), lambda i: (0, 0))
    return pl.pallas_call(
        _out_ln_kernel,
        grid=(N // tm,),
        in_specs=[pl.BlockSpec((tm, K), lambda i: (i, 0)),
                  pl.BlockSpec((K, D_MODEL), lambda i: (0, 0)),
                  pl.BlockSpec((tm, D_MODEL), lambda i: (i, 0)),
                  _mod_spec(R, tpg), row, row],
        out_specs=pl.BlockSpec((tm, D_MODEL), lambda i: (i, 0)),
        out_shape=jax.ShapeDtypeStruct((N, D_MODEL), F32),
        compiler_params=_cparams(("parallel",)),
        name="out_proj_ln",
    )(a, w_bf, x, gate, ln_g.reshape(1, D_MODEL), ln_b.reshape(1, D_MODEL))


def _lane_roll(x, shift):
    return pltpu.roll(x, shift, axis=1)


def _router_kernel(x_ref, sc_ref, sh_ref, whi_ref, wlo_ref, b_ref, o_ref):
    h = x_ref[...] * (1.0 + sc_ref[...]) + sh_ref[...]
    hi = h.astype(BF16)
    lo = (h - hi.astype(F32)).astype(BF16)
    whi, wlo = whi_ref[...], wlo_ref[...]
    logits = (jnp.dot(hi, whi, preferred_element_type=F32)
              + jnp.dot(hi, wlo, preferred_element_type=F32)
              + jnp.dot(lo, whi, preferred_element_type=F32)) + b_ref[...]
    lane_i = lax.broadcasted_iota(I32, logits.shape, 1)
    lane = lane_i.astype(F32)
    m = jnp.max(logits, -1, keepdims=True)
    e = jnp.exp(logits - m)
    p = e / jnp.sum(e, -1, keepdims=True)
    s1 = p + _lane_roll(p, 1)
    s2 = p + _lane_roll(p, 2)
    s3 = p + _lane_roll(p, 3)
    pair_max = jnp.maximum(jnp.maximum(jnp.maximum(s1, s2), s3),
                           jnp.maximum(jnp.maximum(_lane_roll(s1, 1), _lane_roll(s2, 1)),
                                       _lane_roll(s1, 2)))
    is_last = ((lane_i % EXPERTS_PER_GROUP) == EXPERTS_PER_GROUP - 1) & (lane_i < N_EXPERTS)
    grp_score = jnp.where(is_last, pair_max, -jnp.inf)
    gmax = jnp.max(grp_score, -1, keepdims=True)
    big = float(4 * LANES)
    sel_last = jnp.min(jnp.where(grp_score == gmax, lane, big), -1, keepdims=True)
    in_grp = (lane <= sel_last) & (lane > sel_last - EXPERTS_PER_GROUP)
    masked = jnp.where(in_grp, p, -jnp.inf)
    v1 = jnp.max(masked, -1, keepdims=True)
    i1 = jnp.min(jnp.where(masked == v1, lane, big), -1, keepdims=True)
    masked2 = jnp.where(lane == i1, -jnp.inf, masked)
    v2 = jnp.max(masked2, -1, keepdims=True)
    i2 = jnp.min(jnp.where(masked2 == v2, lane, big), -1, keepdims=True)
    tot = v1 + v2
    o_ref[...] = jnp.where(lane == i1, v1 / tot, 0.0) + jnp.where(lane == i2, v2 / tot, 0.0)


def router(x, sc, sh, w_router, b_router, tm):
    N = x.shape[0]
    G, R, _ = sc.shape
    tpg = (N // G) // tm
    w_pad = jnp.zeros((D_MODEL, LANES), F32).at[:, :N_EXPERTS].set(w_router)
    w_hi = w_pad.astype(BF16)
    w_lo = (w_pad - w_hi.astype(F32)).astype(BF16)
    b_pad = jnp.full((1, LANES), NEG_BIG, F32).at[0, :N_EXPERTS].set(b_router)
    full = lambda r, c: pl.BlockSpec((r, c), lambda i: (0, 0))
    return pl.pallas_call(
        _router_kernel,
        grid=(N // tm,),
        in_specs=[pl.BlockSpec((tm, D_MODEL), lambda i: (i, 0)), _mod_spec(R, tpg), _mod_spec(R, tpg),
                  full(D_MODEL, LANES), full(D_MODEL, LANES), full(1, LANES)],
        out_specs=pl.BlockSpec((tm, LANES), lambda i: (i, 0)),
        out_shape=jax.ShapeDtypeStruct((N, LANES), F32),
        compiler_params=_cparams(("parallel",)),
        name="router",
    )(x, sc, sh, w_hi, w_lo, b_pad)


def _moe_kernel(x_ref, sc_ref, sh_ref, g2_ref, gates_ref, wg_ref, wu_ref, wd_ref, lng_ref, lnb_ref,
                o_ref, h_scr, acc_scr):
    e = pl.program_id(1)

    @pl.when(e == 0)
    def _():
        h_scr[...] = (x_ref[...] * (1.0 + sc_ref[...]) + sh_ref[...]).astype(BF16)
        acc_scr[...] = jnp.zeros_like(acc_scr)

    hb = h_scr[...]
    a = (_silu(jnp.dot(hb, wg_ref[...], preferred_element_type=F32))
         * jnp.dot(hb, wu_ref[...], preferred_element_type=F32))
    y = jnp.dot(a.astype(BF16), wd_ref[...], preferred_element_type=F32)
    gates = gates_ref[...]
    lane = lax.broadcasted_iota(I32, gates.shape, 1)
    gate_e = jnp.sum(jnp.where(lane == e, gates, 0.0), -1, keepdims=True)
    acc_scr[...] += gate_e * y

    @pl.when(e == pl.num_programs(1) - 1)
    def _():
        z = ALPHA * x_ref[...] + g2_ref[...] * acc_scr[...]
        o_ref[...] = _layer_norm(z, lng_ref[...], lnb_ref[...])


def moe_ln(x, sc, sh, g2, gates, wg_bf, wu_bf, wd_bf, ln_g, ln_b, tm):
    N = x.shape[0]
    G, R, _ = sc.shape
    tpg = (N // G) // tm
    ms = pl.BlockSpec((None, R, D_MODEL), lambda i, e: (i // tpg, 0, 0))
    row = pl.BlockSpec((1, D_MODEL), lambda i, e: (0, 0))
    return pl.pallas_call(
        _moe_kernel,
        grid=(N // tm, N_EXPERTS),
        in_specs=[pl.BlockSpec((tm, D_MODEL), lambda i, e: (i, 0)), ms, ms, ms,
                  pl.BlockSpec((tm, LANES), lambda i, e: (i, 0)),
                  pl.BlockSpec((None, D_MODEL, D_EXPERT), lambda i, e: (e, 0, 0)),
                  pl.BlockSpec((None, D_MODEL, D_EXPERT), lambda i, e: (e, 0, 0)),
                  pl.BlockSpec((None, D_EXPERT, D_MODEL), lambda i, e: (e, 0, 0)),
                  row, row],
        out_specs=pl.BlockSpec((tm, D_MODEL), lambda i, e: (i, 0)),
        out_shape=jax.ShapeDtypeStruct((N, D_MODEL), F32),
        scratch_shapes=[pltpu.VMEM((tm, D_MODEL), BF16), pltpu.VMEM((tm, D_MODEL), F32)],
        compiler_params=_cparams(("parallel", "arbitrary")),
        name="moe_ln",
    )(x, sc, sh, g2, gates, wg_bf, wu_bf, wd_bf, ln_g.reshape(1, D_MODEL), ln_b.reshape(1, D_MODEL))


DSA_IN_PAD = -(-DSA_IN // LANES) * LANES
DSA_QI_OFF = DSA_Q + 2 * DSA_KV
DSA_KW_OFF = DSA_QI_OFF + IDX_HEADS * IDX_DIM


def _rope_tables(pos, rot_dim, period, n_periods):
    half = rot_dim // 2
    inv_freq = 1.0 / (ROPE_THETA ** (jnp.arange(0, rot_dim, 2, dtype=F32) / rot_dim))
    ang = pos.astype(F32)[:, None] * inv_freq[None, :]
    cos, sin = jnp.cos(ang), jnp.sin(ang)
    T = pos.shape[0]
    zeros = jnp.zeros((T, period - 2 * half), F32)
    zh = jnp.zeros((T, half), F32)
    a1 = jnp.concatenate([cos, cos, zeros + 1.0], 1)
    b1 = jnp.concatenate([zh, sin, zeros], 1)
    c1 = jnp.concatenate([-sin, zh, zeros], 1)
    rest = LANES - n_periods * period
    pad1 = jnp.ones((T, rest), F32)
    pad0 = jnp.zeros((T, rest), F32)
    a = jnp.concatenate([a1] * n_periods + [pad1], 1)
    b = jnp.concatenate([b1] * n_periods + [pad0], 1)
    c = jnp.concatenate([c1] * n_periods + [pad0], 1)
    return a, b, c


def _apply_rope(x, a, b, c, half):
    return x * a + _lane_roll(x, half) * b + _lane_roll(x, LANES - half) * c


def _dsa_proj_kernel(x_ref, sc_ref, sh_ref, w_ref,
                     ma_ref, mb_ref, mc_ref, ia_ref, ib_ref, ic_ref, ka_ref, kb_ref, kc_ref,
                     q_ref, k_ref, v_ref, qi_ref, kw_ref):
    hb = (x_ref[...] * (1.0 + sc_ref[...]) + sh_ref[...]).astype(BF16)
    ma, mb, mc = ma_ref[...], mb_ref[...], mc_ref[...]
    ia, ib, ic = ia_ref[...], ib_ref[...], ic_ref[...]

    def sec(off, width):
        return jnp.dot(hb, w_ref[:, off:off + width], preferred_element_type=F32)

    q = sec(0, DSA_Q)
    for h in range(DSA_HEADS):
        sl = slice(h * LANES, (h + 1) * LANES)
        q_ref[:, sl] = _apply_rope(q[:, sl], ma, mb, mc, ROT_DIM // 2).astype(q_ref.dtype)
    k = sec(DSA_Q, DSA_KV)
    for h in range(DSA_KV_HEADS):
        sl = slice(h * LANES, (h + 1) * LANES)
        k_ref[:, sl] = _apply_rope(k[:, sl], ma, mb, mc, ROT_DIM // 2)
    v_ref[...] = sec(DSA_Q + DSA_KV, DSA_KV)
    qi = sec(DSA_QI_OFF, IDX_HEADS * IDX_DIM)
    for h in range(IDX_HEADS * IDX_DIM // LANES):
        sl = slice(h * LANES, (h + 1) * LANES)
        qi_ref[:, sl] = _apply_rope(qi[:, sl], ia, ib, ic, IDX_ROT_DIM // 2).astype(qi_ref.dtype)
    kw = sec(DSA_KW_OFF, LANES)
    kw_ref[...] = _apply_rope(kw, ka_ref[...], kb_ref[...], kc_ref[...], IDX_ROT_DIM // 2)


def dsa_project(x, sc, sh, w_pad_bf, pos, tm):
    N = x.shape[0]
    G, R, _ = sc.shape
    tpg = (N // G) // tm
    n_tab = pos.shape[0] // tm
    tabs = (_rope_tables(pos, ROT_DIM, DSA_HEAD_DIM, 1)
            + _rope_tables(pos, IDX_ROT_DIM, IDX_DIM, 2)
            + _rope_tables(pos, IDX_ROT_DIM, IDX_DIM, 1))
    tab = pl.BlockSpec((tm, LANES), lambda i: (i % n_tab, 0))
    out = lambda w: pl.BlockSpec((tm, w), lambda i: (i, 0))
    return pl.pallas_call(
        _dsa_proj_kernel,
        grid=(N // tm,),
        in_specs=[pl.BlockSpec((tm, D_MODEL), lambda i: (i, 0)), _mod_spec(R, tpg), _mod_spec(R, tpg),
                  pl.BlockSpec((D_MODEL, DSA_IN_PAD), lambda i: (0, 0))] + [tab] * 9,
        out_specs=[out(DSA_Q), out(DSA_KV), out(DSA_KV), out(IDX_HEADS * IDX_DIM), out(LANES)],
        out_shape=[jax.ShapeDtypeStruct((N, DSA_Q), BF16),
                   jax.ShapeDtypeStruct((N, DSA_KV), F32),
                   jax.ShapeDtypeStruct((N, DSA_KV), F32),
                   jax.ShapeDtypeStruct((N, IDX_HEADS * IDX_DIM), BF16),
                   jax.ShapeDtypeStruct((N, LANES), F32)],
        compiler_params=_cparams(("parallel",)),
        name="dsa_project",
    )(x, sc, sh, w_pad_bf, *tabs)


def _sortable_key(score):
    b = pltpu.bitcast(jnp.where(score == 0.0, 0.0, score), I32)
    return b ^ ((b >> 31) & jnp.int32(0x7FFFFFFF))


KEY_NEG_INF = -2139095041


def _kth_largest_key(count_ge, shape, k):
    def body(s, p):
        cand = p + lax.shift_left(jnp.int32(1), 31 - s)
        return jnp.where(count_ge(cand) >= k, cand, p)
    return lax.fori_loop(0, 32, body, jnp.full(shape, INT_MIN, I32))


def _tie_index_bound(count_eq_below, shape, need, n_bits):
    def body(s, m):
        cand = m + lax.shift_left(jnp.int32(1), n_bits - 1 - s)
        return jnp.where(count_eq_below(cand) < need, cand, m)
    return lax.fori_loop(0, n_bits, body, jnp.zeros(shape, I32))


def _dsa_prompt_kernel(ki_ref, qit_ref, wit_ref, k_ref, vt_ref, qt_ref, o_ref,
                       keys_scr, bias_scr, midx_scr, *, tq, kb_size, topk, idx_bits):
    i = pl.program_id(1)
    n_kb = ((i + 1) * tq + kb_size - 1) // kb_size
    q_pos = i * tq + lax.broadcasted_iota(I32, (1, tq), 1)
    heads_per_dot = 2

    def key_rows(kb):
        return pl.ds(pl.multiple_of(kb * kb_size, kb_size), kb_size)

    def l_index(kb):
        return kb * kb_size + lax.broadcasted_iota(I32, (kb_size, 1), 0)

    def score_body(kb, carry):
        kib = ki_ref[key_rows(kb), :]
        acc = jnp.zeros((kb_size, tq), F32)
        for hp in range(IDX_HEADS // heads_per_dot):
            s = jnp.dot(kib, qit_ref[:, hp * heads_per_dot * tq:(hp + 1) * heads_per_dot * tq],
                        preferred_element_type=F32)
            for j in range(heads_per_dot):
                h = hp * heads_per_dot + j
                w = wit_ref[h:h + 1, :] * (IDX_HEADS ** -0.5 * IDX_DIM ** -0.5)
                acc = acc + jnp.maximum(s[:, j * tq:(j + 1) * tq], 0.0) * w
        allowed = l_index(kb) <= q_pos
        keys_scr[key_rows(kb), :] = _sortable_key(jnp.where(allowed, acc, -jnp.inf))
        return carry

    lax.fori_loop(0, n_kb, score_body, 0)

    def count(pred_fn):
        def body(kb, acc):
            m = jnp.where(pred_fn(keys_scr[key_rows(kb), :], kb), 1.0, 0.0)
            return acc + m.reshape(kb_size // SUBLANES, SUBLANES, tq).sum(axis=0)
        acc = lax.fori_loop(0, n_kb, body, jnp.zeros((SUBLANES, tq), F32))
        return acc.sum(axis=0, keepdims=True)

    thr = _kth_largest_key(lambda c: count(lambda key, kb: key >= c), (1, tq), float(topk))
    n_gt = count(lambda key, kb: key > thr)
    n_ge = count(lambda key, kb: key >= thr)
    need = float(topk) - n_gt
    excess = (n_ge - n_gt > need) & (thr != KEY_NEG_INF)
    midx_scr[...] = jnp.full((1, tq), 2 ** idx_bits, I32)

    @pl.when(jnp.max(jnp.where(excess, 1.0, 0.0)) > 0.5)
    def _():
        m = _tie_index_bound(
            lambda c: count(lambda key, kb: (key == thr) & (l_index(kb) < c)),
            (1, tq), need, idx_bits)
        midx_scr[...] = jnp.where(excess, m, 2 ** idx_bits)

    midx = midx_scr[...]

    def bias_body(kb, carry):
        key = keys_scr[key_rows(kb), :]
        l = l_index(kb)
        sel = ((key > thr) | ((key == thr) & (l <= midx))) & (l <= q_pos)
        bias_scr[key_rows(kb), :] = jnp.where(sel, 0.0, NEG_BIG)
        return carry

    lax.fori_loop(0, n_kb, bias_body, 0)

    gq = DSA_GROUP * tq
    for n in range(DSA_KV_HEADS):
        qt = qt_ref[n]

        def att_body(kb, carry):
            m_run, l_run, acc = carry
            kblk = k_ref[key_rows(kb), n * DSA_HEAD_DIM:(n + 1) * DSA_HEAD_DIM]
            logits = jnp.dot(kblk, qt, preferred_element_type=F32) * (DSA_HEAD_DIM ** -0.5)
            bias = bias_scr[key_rows(kb), :]
            logits = logits + jnp.concatenate([bias] * DSA_GROUP, axis=1)
            m_new = jnp.maximum(m_run, jnp.max(logits, axis=0, keepdims=True))
            p = jnp.exp(logits - m_new)
            alpha = jnp.exp(m_run - m_new)
            l_new = alpha * l_run + jnp.sum(p, axis=0, keepdims=True)
            acc = acc * alpha + jnp.dot(vt_ref[n, kb], p.astype(BF16), preferred_element_type=F32)
            return m_new, l_new, acc

        init = (jnp.full((1, gq), NEG_BIG, F32), jnp.zeros((1, gq), F32),
                jnp.zeros((DSA_HEAD_DIM, gq), F32))
        _, l_fin, acc = lax.fori_loop(0, n_kb, att_body, init)
        o_ref[n] = (acc / l_fin).astype(o_ref.dtype)


def dsa_prompt_attend(q_bf, k, v, qi_bf, kw, B, T, topk, tq=128, kb_size=256):
    nqt = T // tq
    nkb = T // kb_size
    gq = DSA_GROUP * tq
    ki_bf = kw[:, :IDX_DIM].astype(BF16).reshape(B, T, IDX_DIM)
    wi_t = kw[:, IDX_DIM:IDX_DIM + IDX_HEADS].reshape(B, nqt, tq, IDX_HEADS).transpose(0, 1, 3, 2)
    qi_t = (qi_bf.reshape(B, nqt, tq, IDX_HEADS, IDX_DIM).transpose(0, 1, 4, 3, 2)
            .reshape(B, nqt, IDX_DIM, IDX_HEADS * tq))
    q_t = (q_bf.reshape(B, nqt, tq, DSA_KV_HEADS, DSA_GROUP, DSA_HEAD_DIM).transpose(0, 1, 3, 5, 4, 2)
           .reshape(B, nqt, DSA_KV_HEADS, DSA_HEAD_DIM, gq))
    k_bf = k.astype(BF16).reshape(B, T, DSA_KV)
    v_t = (v.astype(BF16).reshape(B, nkb, kb_size, DSA_KV_HEADS, DSA_HEAD_DIM)
           .transpose(0, 3, 1, 4, 2))
    o_t = pl.pallas_call(
        functools.partial(_dsa_prompt_kernel, tq=tq, kb_size=kb_size, topk=topk,
                          idx_bits=max(1, (T - 1).bit_length())),
        grid=(B, nqt),
        in_specs=[pl.BlockSpec((None, T, IDX_DIM), lambda b, i: (b, 0, 0)),
                  pl.BlockSpec((None, None, IDX_DIM, IDX_HEADS * tq), lambda b, i: (b, i, 0, 0)),
                  pl.BlockSpec((None, None, IDX_HEADS, tq), lambda b, i: (b, i, 0, 0)),
                  pl.BlockSpec((None, T, DSA_KV), lambda b, i: (b, 0, 0)),
                  pl.BlockSpec((None, DSA_KV_HEADS, nkb, DSA_HEAD_DIM, kb_size),
                               lambda b, i: (b, 0, 0, 0, 0)),
                  pl.BlockSpec((None, None, DSA_KV_HEADS, DSA_HEAD_DIM, gq),
                               lambda b, i: (b, i, 0, 0, 0))],
        out_specs=pl.BlockSpec((None, None, DSA_KV_HEADS, DSA_HEAD_DIM, gq),
                               lambda b, i: (b, i, 0, 0, 0)),
        out_shape=jax.ShapeDtypeStruct((B, nqt, DSA_KV_HEADS, DSA_HEAD_DIM, gq), BF16),
        scratch_shapes=[pltpu.VMEM((T, tq), I32), pltpu.VMEM((T, tq), F32), pltpu.VMEM((1, tq), I32)],
        compiler_params=_cparams(("parallel", "arbitrary")),
        name="dsa_prompt_attend",
    )(ki_bf, qi_t, wi_t, k_bf, v_t, q_t)
    return (o_t.reshape(B, nqt, DSA_KV_HEADS, DSA_HEAD_DIM, DSA_GROUP, tq)
            .transpose(0, 1, 5, 2, 4, 3).reshape(B * T, DSA_Q))


def _dsa_sample_score_kernel(pt_ref, qi_ref, wi_ref, kis_ref, *rest, n_pages, page):
    pages, o_ref = rest[:n_pages], rest[n_pages]
    qi = qi_ref[...]
    w = wi_ref[...] * (IDX_HEADS ** -0.5 * IDX_DIM ** -0.5)
    for j in range(n_pages):
        dots = lax.dot_general(qi, pages[j][...].astype(BF16), (((1,), (1,)), ((), ())),
                               preferred_element_type=F32)
        o_ref[:, j * page:(j + 1) * page] = jnp.sum(jnp.maximum(dots, 0.0) * w, axis=0, keepdims=True)
    ki_self = kis_ref[...].astype(BF16).astype(F32)
    d_self = jnp.sum(qi.astype(F32) * ki_self, -1, keepdims=True)
    s_self = jnp.sum(jnp.maximum(d_self, 0.0) * w, axis=0, keepdims=True)
    lane = lax.broadcasted_iota(I32, (1, LANES), 1)
    o_ref[:, n_pages * page:] = jnp.where(lane == 0, s_self, -jnp.inf)


def _page_specs(n_pages, page, width):
    return [pl.BlockSpec((None, page, width), functools.partial(lambda j, b, pt: (pt[b, j], 0, 0), j))
            for j in range(n_pages)]


def dsa_sample_scores(page_table, qi_bf, wi, ki_self, cache_kidx):
    DB, n_pages = page_table.shape
    page = cache_kidx.shape[1]
    lp = n_pages * page + LANES
    per_b = lambda *shape: pl.BlockSpec((None,) + shape, lambda b, pt: (b,) + (0,) * len(shape))
    return pl.pallas_call(
        functools.partial(_dsa_sample_score_kernel, n_pages=n_pages, page=page),
        grid_spec=pltpu.PrefetchScalarGridSpec(
            num_scalar_prefetch=1, grid=(DB,),
            in_specs=[per_b(IDX_HEADS, IDX_DIM), per_b(IDX_HEADS, 1), per_b(1, IDX_DIM)]
            + _page_specs(n_pages, page, IDX_DIM),
            out_specs=per_b(1, lp)),
        out_shape=jax.ShapeDtypeStruct((DB, 1, lp), F32),
        compiler_params=_cparams(("arbitrary",)),
        name="dsa_sample_scores",
    )(page_table, qi_bf.reshape(DB, IDX_HEADS, IDX_DIM), wi.reshape(DB, IDX_HEADS, 1),
      ki_self.reshape(DB, 1, IDX_DIM), *([cache_kidx] * n_pages))


def _dsa_sample_select_kernel(s_ref, o_ref, *, n_keys, topk, idx_bits):
    score = s_ref[...]
    lane = lax.broadcasted_iota(I32, score.shape, 1)
    valid = lane < n_keys
    key = _sortable_key(jnp.where(valid, score, -jnp.inf))

    def count(pred):
        return jnp.sum(jnp.where(pred, 1.0, 0.0), -1, keepdims=True)

    shape = (score.shape[0], 1)
    thr = _kth_largest_key(lambda c: count(key >= c), shape, float(topk))
    n_gt = count(key > thr)
    need = float(topk) - n_gt
    excess = (count(key == thr) > need) & (thr != KEY_NEG_INF)
    m = _tie_index_bound(lambda c: count((key == thr) & (lane < c)), shape, need, idx_bits)
    midx = jnp.where(excess, m, 2 ** idx_bits)
    sel = ((key > thr) | ((key == thr) & (lane <= midx))) & valid
    o_ref[...] = jnp.where(sel, 0.0, NEG_BIG)


def dsa_sample_select(score, n_keys, topk):
    DB, lp = score.shape
    return pl.pallas_call(
        functools.partial(_dsa_sample_select_kernel, n_keys=n_keys, topk=topk,
                          idx_bits=max(1, (lp - 1).bit_length())),
        grid=(1,),
        in_specs=[pl.BlockSpec((DB, lp), lambda i: (0, 0))],
        out_specs=pl.BlockSpec((DB, lp), lambda i: (0, 0)),
        out_shape=jax.ShapeDtypeStruct((DB, lp), F32),
        compiler_params=_cparams(("arbitrary",)),
        name="dsa_sample_select",
    )(score)


def _dsa_sample_attend_kernel(pt_ref, qb_ref, bias_ref, ks_ref, vs_ref, *rest, n_pages, page):
    k_pages, v_pages = rest[:n_pages], rest[n_pages:2 * n_pages]
    o_ref, logit_scr = rest[2 * n_pages], rest[2 * n_pages + 1]
    qb = qb_ref[...]
    scale = DSA_HEAD_DIM ** -0.5
    for j in range(n_pages):
        lg = lax.dot_general(qb, k_pages[j][...].astype(BF16), (((1,), (1,)), ((), ())),
                             preferred_element_type=F32)
        logit_scr[:, j * page:(j + 1) * page] = lg * scale + bias_ref[:, j * page:(j + 1) * page]
    k_self = ks_ref[...].astype(BF16).astype(F32)
    lg_self = jnp.sum(qb.astype(F32) * k_self, -1, keepdims=True) * scale
    lane = lax.broadcasted_iota(I32, (DSA_HEADS, LANES), 1)
    logit_scr[:, n_pages * page:] = jnp.where(lane == 0, lg_self, 0.0) + bias_ref[:, n_pages * page:]
    logits = logit_scr[...]
    m = jnp.max(logits, -1, keepdims=True)
    p = jnp.exp(logits - m)
    p = p / jnp.sum(p, -1, keepdims=True)
    p_bf = p.astype(BF16)
    v_self = vs_ref[...].astype(BF16).astype(F32)
    acc = p_bf.astype(F32)[:, n_pages * page:n_pages * page + 1] * v_self
    for j in range(n_pages):
        acc = acc + jnp.dot(p_bf[:, j * page:(j + 1) * page], v_pages[j][...].astype(BF16),
                            preferred_element_type=F32)
    head = lax.broadcasted_iota(I32, (DSA_HEADS, DSA_HEAD_DIM), 0)
    o_ref[...] = jnp.where(head < DSA_GROUP, acc[:, :DSA_HEAD_DIM], acc[:, DSA_HEAD_DIM:])


def dsa_sample_attend(page_table, q_bf, bias, k_self, v_self, cache_k, cache_v):
    DB, n_pages = page_table.shape
    page = cache_k.shape[1]
    lp = n_pages * page + LANES
    assert DSA_KV_HEADS == 2
    q3 = q_bf.reshape(DB, DSA_HEADS, DSA_HEAD_DIM)
    zero = jnp.zeros_like(q3)
    first = (jnp.arange(DSA_HEADS) < DSA_GROUP)[None, :, None]
    q_blk = jnp.concatenate([jnp.where(first, q3, zero), jnp.where(first, zero, q3)], -1)
    per_b = lambda *shape: pl.BlockSpec((None,) + shape, lambda b, pt: (b,) + (0,) * len(shape))
    o = pl.pallas_call(
        functools.partial(_dsa_sample_attend_kernel, n_pages=n_pages, page=page),
        grid_spec=pltpu.PrefetchScalarGridSpec(
            num_scalar_prefetch=1, grid=(DB,),
            in_specs=[per_b(DSA_HEADS, DSA_KV), per_b(1, lp), per_b(1, DSA_KV), per_b(1, DSA_KV)]
            + _page_specs(n_pages, page, DSA_KV) + _page_specs(n_pages, page, DSA_KV),
            out_specs=per_b(DSA_HEADS, DSA_HEAD_DIM),
            scratch_shapes=[pltpu.VMEM((DSA_HEADS, lp), F32)]),
        out_shape=jax.ShapeDtypeStruct((DB, DSA_HEADS, DSA_HEAD_DIM), F32),
        compiler_params=_cparams(("arbitrary",)),
        name="dsa_sample_attend",
    )(page_table, q_blk, bias.reshape(DB, 1, lp), k_self.reshape(DB, 1, DSA_KV),
      v_self.reshape(DB, 1, DSA_KV), *([cache_k] * n_pages), *([cache_v] * n_pages))
    return o.reshape(DB, DSA_Q)


def _retnet_tables(pos):
    inv_freq = jnp.power(RET_ANGLE_BASE, -jnp.linspace(0.0, 1.0, RET_DK // 2, dtype=F32))
    ang = pos.astype(F32)[:, None] * inv_freq[None, :]
    return jnp.cos(ang), jnp.sin(ang)


def _split_mod(mod_l, G, R):
    return [m.reshape(G, R, D_MODEL) for m in jnp.split(mod_l, 6, axis=-1)]


def _channel_mixer(x, sc2, sh2, g2, l, wts, tm):
    gates = router(x, sc2, sh2, wts["w_router"], wts["b_router"], tm)
    return moe_ln(x, sc2, sh2, g2, gates, wts["w_gate"][l], wts["w_up"][l], wts["w_down"][l],
                  wts["ln2_g"][l], wts["ln2_b"][l], tm)


def kernel(x_prompt, x_sample, state_ret, cache_k, cache_v, cache_kidx, page_table,
           c_prompt, c_sample, w_mod, b_mod, ln1_g, ln1_b, ln2_g, ln2_b,
           w_in_ret, gn_ret_g, w_out_ret, w_in_dsa, w_out_dsa,
           w_router, b_router, w_gate, w_up, w_down):
    B, T, _ = x_prompt.shape
    DB = x_sample.shape[0]
    assert x_sample.shape[1] == 1
    n_pages = page_table.shape[1]
    page = cache_k.shape[2]
    past = n_pages * page
    n_pool = cache_k.shape[1]

    wts = dict(w_router=w_router, b_router=b_router, ln2_g=ln2_g, ln2_b=ln2_b,
               w_gate=w_gate.astype(BF16), w_up=w_up.astype(BF16), w_down=w_down.astype(BF16))
    w_mod_bf = w_mod.astype(BF16)
    w_in_ret_bf = w_in_ret[0].astype(BF16)
    w_out_ret_bf = w_out_ret[0].astype(BF16)
    w_in_dsa_bf = jnp.pad(w_in_dsa[0], ((0, 0), (0, DSA_IN_PAD - DSA_IN))).astype(BF16)
    w_out_dsa_bf = w_out_dsa[0].astype(BF16)

    pos_p = jnp.arange(T, dtype=I32)
    pos_s = jnp.full((1,), past, I32)

    tm_p = min(512, T)
    xp = x_prompt.reshape(B * T, D_MODEL)
    mod_p = modulation(c_prompt, w_mod_bf, b_mod)
    sh1, sc1, g1, sh2, sc2, g2 = _split_mod(mod_p[0], B, 1)
    proj = mod_proj(xp, sc1, sh1, w_in_ret_bf, tm_p, 1024)
    cos_p, sin_p = _retnet_tables(pos_p)
    gated, ret_p = retention_prompt(proj.reshape(B, T, RET_IN), gn_ret_g[0], cos_p, sin_p, min(128, T))
    xp = out_proj_ln(gated.reshape(B * T, RET_V), w_out_ret_bf, xp, g1, ln1_g[0], ln1_b[0], tm_p)
    xp = _channel_mixer(xp, sc2, sh2, g2, 0, wts, tm_p)

    sh1, sc1, g1, sh2, sc2, g2 = _split_mod(mod_p[1], B, 1)
    q_bf, k_p, v_p, qi_bf, kw_p = dsa_project(xp, sc1, sh1, w_in_dsa_bf, pos_p, tm_p)
    o_p = dsa_prompt_attend(q_bf, k_p, v_p, qi_bf, kw_p, B, T, min(DSA_TOPK, T // 4))
    xp = out_proj_ln(o_p, w_out_dsa_bf, xp, g1, ln1_g[1], ln1_b[1], tm_p)
    xp = _channel_mixer(xp, sc2, sh2, g2, 1, wts, tm_p)

    xs = x_sample.reshape(DB, D_MODEL)
    mod_s = modulation(c_sample, w_mod_bf, b_mod)
    sh1, sc1, g1, sh2, sc2, g2 = _split_mod(mod_s[0], 1, DB)
    proj_s = mod_proj(xs, sc1, sh1, w_in_ret_bf, DB, 1024)
    cos_s, sin_s = _retnet_tables(pos_s)
    gated_s, ret_s = retention_sample(proj_s, state_ret[0], gn_ret_g[0], cos_s, sin_s)
    xs = out_proj_ln(gated_s.reshape(DB, RET_V), w_out_ret_bf, xs, g1, ln1_g[0], ln1_b[0], DB)
    xs = _channel_mixer(xs, sc2, sh2, g2, 0, wts, DB)

    sh1, sc1, g1, sh2, sc2, g2 = _split_mod(mod_s[1], 1, DB)
    pos_rows = jnp.full((DB,), past, I32)
    q_s, k_s, v_s, qi_s, kw_s = dsa_project(xs, sc1, sh1, w_in_dsa_bf, pos_rows, DB)
    ki_s = kw_s[:, :IDX_DIM]
    score = dsa_sample_scores(page_table, qi_s, kw_s[:, IDX_DIM:IDX_DIM + IDX_HEADS], ki_s,
                              cache_kidx[0])
    bias = dsa_sample_select(score.reshape(DB, past + LANES), past + 1, min(DSA_TOPK, (past + 1) // 4))
    o_s = dsa_sample_attend(page_table, q_s, bias, k_s, v_s,
                            cache_k[0].reshape(n_pool, page, DSA_KV),
                            cache_v[0].reshape(n_pool, page, DSA_KV))
    xs = out_proj_ln(o_s, w_out_dsa_bf, xs, g1, ln1_g[1], ln1_b[1], DB)
    xs = _channel_mixer(xs, sc2, sh2, g2, 1, wts, DB)

    kv_shape = (DSA_KV_HEADS, DSA_HEAD_DIM)
    return (xp.reshape(B, T, D_MODEL), xs.reshape(DB, 1, D_MODEL),
            ret_p[None], ret_s[None],
            k_p.reshape(1, B, T, *kv_shape), v_p.reshape(1, B, T, *kv_shape),
            kw_p[:, :IDX_DIM].reshape(1, B, T, IDX_DIM),
            k_s.reshape(1, DB, 1, *kv_shape), v_s.reshape(1, DB, 1, *kv_shape),
            ki_s.reshape(1, DB, 1, IDX_DIM))
```

```python
import functools

import jax
import jax.numpy as jnp
from jax import lax
from jax.experimental import pallas as pl
from jax.experimental.pallas import tpu as pltpu

F32 = jnp.float32
BF16 = jnp.bfloat16
I32 = jnp.int32

D_MODEL = 1024
DEPTH = 2
ALPHA = (2.0 * DEPTH) ** 0.25
LN_EPS = 1e-5
GN_EPS = 1e-6

RET_HEADS = 4
RET_DK = 256
RET_DV = 512
RET_QK = RET_HEADS * RET_DK
RET_V = RET_HEADS * RET_DV
RET_IN = 2 * RET_QK + 2 * RET_V
RET_ANGLE_BASE = 10000.0

DSA_HEADS = 8
DSA_KV_HEADS = 2
DSA_HEAD_DIM = 128
DSA_GROUP = DSA_HEADS // DSA_KV_HEADS
DSA_Q = DSA_HEADS * DSA_HEAD_DIM
DSA_KV = DSA_KV_HEADS * DSA_HEAD_DIM
IDX_HEADS = 16
IDX_DIM = 64
DSA_TOPK = 256
ROPE_THETA = 500000.0
ROT_DIM = DSA_HEAD_DIM // 4
IDX_ROT_DIM = IDX_DIM // 4
DSA_IN = DSA_Q + 2 * DSA_KV + IDX_HEADS * IDX_DIM + IDX_DIM + IDX_HEADS

N_EXPERTS = 16
N_GROUPS = 4
EXPERTS_PER_GROUP = N_EXPERTS // N_GROUPS
D_EXPERT = 512

LANES = 128
SUBLANES = 8
VMEM_LIMIT = 56 * 1024 * 1024
NEG_BIG = -1e30
INT_MIN = -2147483648


def _cparams(sem):
    return pltpu.CompilerParams(dimension_semantics=sem, vmem_limit_bytes=VMEM_LIMIT)


def _silu(x):
    return x * (1.0 / (1.0 + jnp.exp(-x)))


def _layer_norm(z, g, b):
    mu = jnp.mean(z, -1, keepdims=True)
    d = z - mu
    var = jnp.mean(d * d, -1, keepdims=True)
    return d * lax.rsqrt(var + LN_EPS) * g + b


def _mod_kernel(c_ref, w_ref, b_ref, o_ref):
    a = _silu(c_ref[...]).astype(BF16)
    o_ref[...] = jnp.dot(a, w_ref[...], preferred_element_type=F32) + b_ref[...]


def modulation(c, w_mod_bf, b_mod):
    R = c.shape[0]
    tn = 1536
    return pl.pallas_call(
        _mod_kernel,
        grid=(DEPTH, 6 * D_MODEL // tn),
        in_specs=[pl.BlockSpec((R, D_MODEL), lambda l, j: (0, 0)),
                  pl.BlockSpec((None, D_MODEL, tn), lambda l, j: (l, 0, j)),
                  pl.BlockSpec((None, 1, tn), lambda l, j: (l, 0, j))],
        out_specs=pl.BlockSpec((None, R, tn), lambda l, j: (l, 0, j)),
        out_shape=jax.ShapeDtypeStruct((DEPTH, R, 6 * D_MODEL), F32),
        compiler_params=_cparams(("parallel", "parallel")),
        name="modulation",
    )(c, w_mod_bf, b_mod.reshape(DEPTH, 1, 6 * D_MODEL))


def _proj_kernel(x_ref, sc_ref, sh_ref, w_ref, o_ref, h_scr):
    @pl.when(pl.program_id(1) == 0)
    def _():
        h_scr[...] = (x_ref[...] * (1.0 + sc_ref[...]) + sh_ref[...]).astype(BF16)

    o_ref[...] = jnp.dot(h_scr[...], w_ref[...], preferred_element_type=F32).astype(o_ref.dtype)


def _mod_spec(R, tiles_per_group):
    return pl.BlockSpec((None, R, D_MODEL), lambda i, *_: (i // tiles_per_group, 0, 0))


def mod_proj(x, sc, sh, w_bf, tm, tn, out_dtype):
    N = x.shape[0]
    G, R, _ = sc.shape
    n_out = w_bf.shape[1]
    tpg = (N // G) // tm
    return pl.pallas_call(
        _proj_kernel,
        grid=(N // tm, n_out // tn),
        in_specs=[pl.BlockSpec((tm, D_MODEL), lambda i, j: (i, 0)),
                  _mod_spec(R, tpg), _mod_spec(R, tpg),
                  pl.BlockSpec((D_MODEL, tn), lambda i, j: (0, j))],
        out_specs=pl.BlockSpec((tm, tn), lambda i, j: (i, j)),
        out_shape=jax.ShapeDtypeStruct((N, n_out), out_dtype),
        scratch_shapes=[pltpu.VMEM((tm, D_MODEL), BF16)],
        compiler_params=_cparams(("parallel", "arbitrary")),
        name="mod_proj",
    )(x, sc, sh, w_bf)


def _rot_half(x, cos, sin):
    half = cos.shape[-1]
    x1, x2 = x[:, :half], x[:, half:]
    return jnp.concatenate([x1 * cos - x2 * sin, x1 * sin + x2 * cos], axis=1)


def _group_norm_gate(o, gn, g):
    mu = jnp.mean(o, -1, keepdims=True)
    d = o - mu
    var = jnp.mean(d * d, -1, keepdims=True)
    return d * lax.rsqrt(var + GN_EPS) * gn * _silu(g)


def _ret_prompt_kernel(q_ref, k_ref, v_ref, g_ref, cos_ref, sin_ref, lg_ref, gn_ref,
                       o_ref, s_ref, s_scr, *, chunk):
    c = pl.program_id(2)

    @pl.when(c == 0)
    def _():
        s_scr[...] = jnp.zeros_like(s_scr)

    cos, sin = cos_ref[...], sin_ref[...]
    lg_row = lg_ref[...]
    lg = lg_row[:, :1]
    q = _rot_half(q_ref[...].astype(F32), cos, sin)
    k = _rot_half(k_ref[...].astype(F32), cos, sin) * (RET_DK ** -0.5)
    vb = v_ref[...].astype(BF16)
    qb = q.astype(BF16)

    row = lax.broadcasted_iota(I32, (chunk, chunk), 0)
    col = lax.broadcasted_iota(I32, (chunk, chunk), 1)
    diff = (row - col).astype(F32)
    decay = jnp.where(diff >= 0, jnp.exp(lg_row * jnp.maximum(diff, 0.0)), 0.0)
    idx = lax.broadcasted_iota(I32, (chunk, 1), 0).astype(F32)
    q_dec = jnp.exp(lg * (idx + 1.0))
    k_dec = jnp.exp(lg * (chunk - 1.0 - idx))
    chunk_dec = jnp.exp(lg * chunk)

    s = s_scr[...]
    att = lax.dot_general(qb, k.astype(BF16), (((1,), (1,)), ((), ())),
                          preferred_element_type=F32) * decay
    o = (jnp.dot(att.astype(BF16), vb, preferred_element_type=F32)
         + jnp.dot(qb, s.astype(BF16), preferred_element_type=F32) * q_dec)
    kd_t = (k * k_dec).T.astype(BF16)
    s_new = chunk_dec * s + jnp.dot(kd_t, vb, preferred_element_type=F32)
    s_scr[...] = s_new
    o_ref[...] = _group_norm_gate(o, gn_ref[...], g_ref[...].astype(F32)).astype(o_ref.dtype)

    @pl.when(c == pl.num_programs(2) - 1)
    def _():
        s_ref[...] = s_new


def _log_gamma():
    return jnp.log(1.0 - jnp.power(2.0, -5.0 - jnp.arange(RET_HEADS, dtype=F32)))


def retention_prompt(proj, gn_g, cos, sin, chunk):
    B, T, _ = proj.shape
    lg_tab = jnp.broadcast_to(_log_gamma()[:, None, None], (RET_HEADS, 1, chunk))
    qk_blk = lambda off: pl.BlockSpec((None, chunk, RET_DK), lambda b, h, c: (b, c, off + h))
    v_blk = lambda off: pl.BlockSpec((None, chunk, RET_DV), lambda b, h, c: (b, c, off + h))
    tab = pl.BlockSpec((chunk, RET_DK // 2), lambda b, h, c: (c, 0))
    return pl.pallas_call(
        functools.partial(_ret_prompt_kernel, chunk=chunk),
        grid=(B, RET_HEADS, T // chunk),
        in_specs=[qk_blk(0), qk_blk(RET_QK // RET_DK),
                  v_blk(2 * RET_QK // RET_DV), v_blk((2 * RET_QK + RET_V) // RET_DV),
                  tab, tab,
                  pl.BlockSpec((None, 1, chunk), lambda b, h, c: (h, 0, 0)),
                  pl.BlockSpec((1, RET_DV), lambda b, h, c: (0, h))],
        out_specs=[pl.BlockSpec((None, chunk, RET_DV), lambda b, h, c: (b, c, h)),
                   pl.BlockSpec((None, None, RET_DK, RET_DV), lambda b, h, c: (b, h, 0, 0))],
        out_shape=[jax.ShapeDtypeStruct((B, T, RET_V), BF16),
                   jax.ShapeDtypeStruct((B, RET_HEADS, RET_DK, RET_DV), F32)],
        scratch_shapes=[pltpu.VMEM((RET_DK, RET_DV), F32)],
        compiler_params=_cparams(("parallel", "parallel", "arbitrary")),
        name="retention_prompt",
    )(proj, proj, proj, proj, cos, sin, lg_tab, gn_g.reshape(1, RET_V))


def _ret_sample_kernel(qk_ref, v_ref, g_ref, s0_ref, cos_ref, sin_ref, gam_ref, gn_ref,
                       o_ref, s_ref):
    t = _rot_half(qk_ref[...], cos_ref[...], sin_ref[...])
    row = lax.broadcasted_iota(I32, t.shape, 0)
    t = jnp.where(row >= RET_HEADS, t * (RET_DK ** -0.5), t)
    pad = jnp.zeros((LANES - 2 * RET_HEADS, RET_DK), F32)
    t_t = jnp.concatenate([t, pad], axis=0).T
    for h in range(RET_HEADS):
        qc = t_t[:, h:h + 1]
        kc = t_t[:, RET_HEADS + h:RET_HEADS + h + 1]
        gam = gam_ref[h]
        vh = v_ref[h:h + 1, :]
        s = s0_ref[h]
        qk_dot = jnp.sum(t[h:h + 1, :] * t[RET_HEADS + h:RET_HEADS + h + 1, :], -1, keepdims=True)
        o = qk_dot * vh + gam * jnp.sum(s * qc, axis=0, keepdims=True)
        s_ref[h] = gam * s + kc * vh
        o_ref[h:h + 1, :] = _group_norm_gate(o, gn_ref[h:h + 1, :], g_ref[h:h + 1, :])


def retention_sample(proj, s0, gn_g, cos, sin):
    DB = proj.shape[0]
    qk = proj[:, :2 * RET_QK].reshape(DB, 2 * RET_HEADS, RET_DK)
    v = proj[:, 2 * RET_QK:2 * RET_QK + RET_V].reshape(DB, RET_HEADS, RET_DV)
    g = proj[:, 2 * RET_QK + RET_V:].reshape(DB, RET_HEADS, RET_DV)
    gam = jnp.broadcast_to(jnp.exp(_log_gamma())[:, None, None], (RET_HEADS, 1, RET_DV))
    full = lambda *shape: pl.BlockSpec(shape, lambda b: (0,) * len(shape))
    per_b = lambda *shape: pl.BlockSpec((None,) + shape, lambda b: (b,) + (0,) * len(shape))
    return pl.pallas_call(
        _ret_sample_kernel,
        grid=(DB,),
        in_specs=[per_b(2 * RET_HEADS, RET_DK), per_b(RET_HEADS, RET_DV), per_b(RET_HEADS, RET_DV),
                  per_b(RET_HEADS, RET_DK, RET_DV),
                  full(1, RET_DK // 2), full(1, RET_DK // 2),
                  full(RET_HEADS, 1, RET_DV), full(RET_HEADS, RET_DV)],
        out_specs=[per_b(RET_HEADS, RET_DV), per_b(RET_HEADS, RET_DK, RET_DV)],
        out_shape=[jax.ShapeDtypeStruct((DB, RET_HEADS, RET_DV), F32),
                   jax.ShapeDtypeStruct((DB, RET_HEADS, RET_DK, RET_DV), F32)],
        compiler_params=_cparams(("parallel",)),
        name="retention_sample",
    )(qk, v, g, s0, cos, sin, gam, gn_g.reshape(RET_HEADS, RET_DV))


def _out_ln_kernel(a_ref, w_ref, x_ref, g_ref, lng_ref, lnb_ref, o_ref):
    y = jnp.dot(a_ref[...].astype(BF16), w_ref[...], preferred_element_type=F32)
    o_ref[...] = _layer_norm(ALPHA * x_ref[...] + g_ref[...] * y, lng_ref[...], lnb_ref[...])


def out_proj_ln(a, w_bf, x, gate, ln_g, ln_b, tm):
    N, K = a.shape
    G, R, _ = gate.shape
    tpg = (N // G) // tm
    row = pl.BlockSpec((1, D_MODEL), lambda i: (0, 0))
    return pl.pallas_call(
        _out_ln_kernel,
        grid=(N // tm,),
        in_specs=[pl.BlockSpec((tm, K), lambda i: (i, 0)),
                  pl.BlockSpec((K, D_MODEL), lambda i: (0, 0)),
                  pl.BlockSpec((tm, D_MODEL), lambda i: (i, 0)),
                  _mod_spec(R, tpg), row, row],
        out_specs=pl.BlockSpec((tm, D_MODEL), lambda i: (i, 0)),
        out_shape=jax.ShapeDtypeStruct((N, D_MODEL), F32),
        compiler_params=_cparams(("parallel",)),
        name="out_proj_ln",
    )(a, w_bf, x, gate, ln_g.reshape(1, D_MODEL), ln_b.reshape(1, D_MODEL))


def _lane_roll(x, shift):
    return pltpu.roll(x, shift, axis=1)


def _router_kernel(x_ref, sc_ref, sh_ref, whi_ref, wlo_ref, b_ref, o_ref):
    h = x_ref[...] * (1.0 + sc_ref[...]) + sh_ref[...]
    hi = h.astype(BF16)
    lo = (h - hi.astype(F32)).astype(BF16)
    whi, wlo = whi_ref[...], wlo_ref[...]
    logits = (jnp.dot(hi, whi, preferred_element_type=F32)
              + jnp.dot(hi, wlo, preferred_element_type=F32)
              + jnp.dot(lo, whi, preferred_element_type=F32)) + b_ref[...]
    lane_i = lax.broadcasted_iota(I32, logits.shape, 1)
    lane = lane_i.astype(F32)
    m = jnp.max(logits, -1, keepdims=True)
    e = jnp.exp(logits - m)
    p = e / jnp.sum(e, -1, keepdims=True)
    s1 = p + _lane_roll(p, 1)
    s2 = p + _lane_roll(p, 2)
    s3 = p + _lane_roll(p, 3)
    pair_max = jnp.maximum(jnp.maximum(jnp.maximum(s1, s2), s3),
                           jnp.maximum(jnp.maximum(_lane_roll(s1, 1), _lane_roll(s2, 1)),
                                       _lane_roll(s1, 2)))
    is_last = ((lane_i % EXPERTS_PER_GROUP) == EXPERTS_PER_GROUP - 1) & (lane_i < N_EXPERTS)
    grp_score = jnp.where(is_last, pair_max, -jnp.inf)
    gmax = jnp.max(grp_score, -1, keepdims=True)
    big = float(4 * LANES)
    sel_last = jnp.min(jnp.where(grp_score == gmax, lane, big), -1, keepdims=True)
    in_grp = (lane <= sel_last) & (lane > sel_last - EXPERTS_PER_GROUP)
    masked = jnp.where(in_grp, p, -jnp.inf)
    v1 = jnp.max(masked, -1, keepdims=True)
    i1 = jnp.min(jnp.where(masked == v1, lane, big), -1, keepdims=True)
    masked2 = jnp.where(lane == i1, -jnp.inf, masked)
    v2 = jnp.max(masked2, -1, keepdims=True)
    i2 = jnp.min(jnp.where(masked2 == v2, lane, big), -1, keepdims=True)
    tot = v1 + v2
    o_ref[...] = jnp.where(lane == i1, v1 / tot, 0.0) + jnp.where(lane == i2, v2 / tot, 0.0)


def router(x, sc, sh, w_router, b_router, tm):
    N = x.shape[0]
    G, R, _ = sc.shape
    tpg = (N // G) // tm
    w_pad = jnp.zeros((D_MODEL, LANES), F32).at[:, :N_EXPERTS].set(w_router)
    w_hi = w_pad.astype(BF16)
    w_lo = (w_pad - w_hi.astype(F32)).astype(BF16)
    b_pad = jnp.full((1, LANES), NEG_BIG, F32).at[0, :N_EXPERTS].set(b_router)
    full = lambda r, c: pl.BlockSpec((r, c), lambda i: (0, 0))
    return pl.pallas_call(
        _router_kernel,
        grid=(N // tm,),
        in_specs=[pl.BlockSpec((tm, D_MODEL), lambda i: (i, 0)), _mod_spec(R, tpg), _mod_spec(R, tpg),
                  full(D_MODEL, LANES), full(D_MODEL, LANES), full(1, LANES)],
        out_specs=pl.BlockSpec((tm, LANES), lambda i: (i, 0)),
        out_shape=jax.ShapeDtypeStruct((N, LANES), F32),
        compiler_params=_cparams(("parallel",)),
        name="router",
    )(x, sc, sh, w_hi, w_lo, b_pad)


def _moe_kernel(x_ref, sc_ref, sh_ref, g2_ref, gates_ref, wg_ref, wu_ref, wd_ref, lng_ref, lnb_ref,
                o_ref, h_scr, acc_scr):
    e = pl.program_id(1)

    @pl.when(e == 0)
    def _():
        h_scr[...] = (x_ref[...] * (1.0 + sc_ref[...]) + sh_ref[...]).astype(BF16)
        acc_scr[...] = jnp.zeros_like(acc_scr)

    hb = h_scr[...]
    a = (_silu(jnp.dot(hb, wg_ref[...], preferred_element_type=F32))
         * jnp.dot(hb, wu_ref[...], preferred_element_type=F32))
    y = jnp.dot(a.astype(BF16), wd_ref[...], preferred_element_type=F32)
    gates = gates_ref[...]
    lane = lax.broadcasted_iota(I32, gates.shape, 1)
    gate_e = jnp.sum(jnp.where(lane == e, gates, 0.0), -1, keepdims=True)
    acc_scr[...] += gate_e * y

    @pl.when(e == pl.num_programs(1) - 1)
    def _():
        z = ALPHA * x_ref[...] + g2_ref[...] * acc_scr[...]
        o_ref[...] = _layer_norm(z, lng_ref[...], lnb_ref[...])


def moe_ln(x, sc, sh, g2, gates, wg_bf, wu_bf, wd_bf, ln_g, ln_b, tm):
    N = x.shape[0]
    G, R, _ = sc.shape
    tpg = (N // G) // tm
    ms = pl.BlockSpec((None, R, D_MODEL), lambda i, e: (i // tpg, 0, 0))
    row = pl.BlockSpec((1, D_MODEL), lambda i, e: (0, 0))
    return pl.pallas_call(
        _moe_kernel,
        grid=(N // tm, N_EXPERTS),
        in_specs=[pl.BlockSpec((tm, D_MODEL), lambda i, e: (i, 0)), ms, ms, ms,
                  pl.BlockSpec((tm, LANES), lambda i, e: (i, 0)),
                  pl.BlockSpec((None, D_MODEL, D_EXPERT), lambda i, e: (e, 0, 0)),
                  pl.BlockSpec((None, D_MODEL, D_EXPERT), lambda i, e: (e, 0, 0)),
                  pl.BlockSpec((None, D_EXPERT, D_MODEL), lambda i, e: (e, 0, 0)),
                  row, row],
        out_specs=pl.BlockSpec((tm, D_MODEL), lambda i, e: (i, 0)),
        out_shape=jax.ShapeDtypeStruct((N, D_MODEL), F32),
        scratch_shapes=[pltpu.VMEM((tm, D_MODEL), BF16), pltpu.VMEM((tm, D_MODEL), F32)],
        compiler_params=_cparams(("parallel", "arbitrary")),
        name="moe_ln",
    )(x, sc, sh, g2, gates, wg_bf, wu_bf, wd_bf, ln_g.reshape(1, D_MODEL), ln_b.reshape(1, D_MODEL))


LOG2_E = 1.4426950408889634
Q_PRESCALE = DSA_HEAD_DIM ** -0.5 * LOG2_E
DSA_IN_PAD = -(-DSA_IN // LANES) * LANES
DSA_QI_OFF = DSA_Q + 2 * DSA_KV
DSA_KW_OFF = DSA_QI_OFF + IDX_HEADS * IDX_DIM


def _rope_tables(pos, rot_dim, period, n_periods):
    half = rot_dim // 2
    inv_freq = 1.0 / (ROPE_THETA ** (jnp.arange(0, rot_dim, 2, dtype=F32) / rot_dim))
    ang = pos.astype(F32)[:, None] * inv_freq[None, :]
    cos, sin = jnp.cos(ang), jnp.sin(ang)
    T = pos.shape[0]
    zeros = jnp.zeros((T, period - 2 * half), F32)
    zh = jnp.zeros((T, half), F32)
    a1 = jnp.concatenate([cos, cos, zeros + 1.0], 1)
    b1 = jnp.concatenate([zh, sin, zeros], 1)
    c1 = jnp.concatenate([-sin, zh, zeros], 1)
    rest = LANES - n_periods * period
    pad1 = jnp.ones((T, rest), F32)
    pad0 = jnp.zeros((T, rest), F32)
    a = jnp.concatenate([a1] * n_periods + [pad1], 1)
    b = jnp.concatenate([b1] * n_periods + [pad0], 1)
    c = jnp.concatenate([c1] * n_periods + [pad0], 1)
    return a, b, c


def _apply_rope(x, a, b, c, half):
    return x * a + _lane_roll(x, half) * b + _lane_roll(x, LANES - half) * c


N_ROPE_TABLES = 9


def _dsa_sections(x_ref, sc_ref, sh_ref, w_ref, tab_refs):
    hb = (x_ref[...] * (1.0 + sc_ref[...]) + sh_ref[...]).astype(BF16)
    ma, mb, mc, ia, ib, ic, ka, kb, kc = (t[...] for t in tab_refs)

    def sec(off, width):
        return jnp.dot(hb, w_ref[:, off:off + width], preferred_element_type=F32)

    def lanes(x, h):
        return x[:, h * LANES:(h + 1) * LANES]

    q = sec(0, DSA_Q)
    q_heads = [_apply_rope(lanes(q, h), ma, mb, mc, ROT_DIM // 2) * Q_PRESCALE for h in range(DSA_HEADS)]
    k = sec(DSA_Q, DSA_KV)
    k = jnp.concatenate([_apply_rope(lanes(k, h), ma, mb, mc, ROT_DIM // 2) for h in range(DSA_KV_HEADS)], 1)
    v = sec(DSA_Q + DSA_KV, DSA_KV)
    qi = sec(DSA_QI_OFF, IDX_HEADS * IDX_DIM)
    qi_pairs = [_apply_rope(lanes(qi, h), ia, ib, ic, IDX_ROT_DIM // 2)
                for h in range(IDX_HEADS * IDX_DIM // LANES)]
    kw = _apply_rope(sec(DSA_KW_OFF, LANES), ka, kb, kc, IDX_ROT_DIM // 2)
    return q_heads, k, v, qi_pairs, kw


def _dsa_proj_kernel(x_ref, sc_ref, sh_ref, w_ref, *rest):
    tabs = rest[:N_ROPE_TABLES]
    q_ref, k_ref, v_ref, qi_ref, kw_ref = rest[N_ROPE_TABLES:]
    q_heads, k, v, qi_pairs, kw = _dsa_sections(x_ref, sc_ref, sh_ref, w_ref, tabs)
    for h, qh in enumerate(q_heads):
        q_ref[:, h * LANES:(h + 1) * LANES] = qh.astype(q_ref.dtype)
    for h, qp in enumerate(qi_pairs):
        qi_ref[:, h * LANES:(h + 1) * LANES] = qp.astype(qi_ref.dtype)
    k_ref[...] = k
    v_ref[...] = v
    kw_ref[...] = kw


def _dsa_proj_prompt_kernel(x_ref, sc_ref, sh_ref, w_ref, *rest, tq):
    tabs = rest[:N_ROPE_TABLES]
    k_ref, v_ref, kw_ref, kbf_ref, kwbf_ref, vt_ref, qt_ref, qit_ref, wit_ref = rest[N_ROPE_TABLES:]
    q_heads, k, v, qi_pairs, kw = _dsa_sections(x_ref, sc_ref, sh_ref, w_ref, tabs)
    tm = k.shape[0]
    k_ref[...] = k
    kbf_ref[...] = k.astype(BF16)
    v_ref[...] = v
    kw_ref[...] = kw
    kwbf_ref[...] = kw.astype(BF16)
    for n in range(DSA_KV_HEADS):
        vt_ref[n, 0] = v[:, n * DSA_HEAD_DIM:(n + 1) * DSA_HEAD_DIM].T.astype(BF16)
    heads_per_pair = LANES // IDX_DIM
    for j in range(tm // tq):
        rows = slice(j * tq, (j + 1) * tq)
        for h, qh in enumerate(q_heads):
            n, g = divmod(h, DSA_GROUP)
            qt_ref[j, n, :, g * tq:(g + 1) * tq] = qh[rows].T.astype(BF16)
        for hp, qp in enumerate(qi_pairs):
            t = qp[rows].T.astype(BF16)
            for s in range(heads_per_pair):
                h = hp * heads_per_pair + s
                qit_ref[j, :IDX_DIM, h * tq:(h + 1) * tq] = t[s * IDX_DIM:(s + 1) * IDX_DIM]
        qit_ref[j, IDX_DIM:, :] = jnp.zeros((LANES - IDX_DIM, IDX_HEADS * tq), BF16)
        wit_ref[j] = kw[rows].T[IDX_DIM:IDX_DIM + IDX_HEADS]


def _dsa_project_call(body, x, sc, sh, w_pad_bf, pos, tm, out_specs, out_shape, name):
    N = x.shape[0]
    G, R, _ = sc.shape
    tpg = (N // G) // tm
    n_tab = pos.shape[0] // tm
    tabs = (_rope_tables(pos, ROT_DIM, DSA_HEAD_DIM, 1)
            + _rope_tables(pos, IDX_ROT_DIM, IDX_DIM, 2)
            + _rope_tables(pos, IDX_ROT_DIM, IDX_DIM, 1))
    tab = pl.BlockSpec((tm, LANES), lambda i: (i % n_tab, 0))
    return pl.pallas_call(
        body,
        grid=(N // tm,),
        in_specs=[pl.BlockSpec((tm, D_MODEL), lambda i: (i, 0)), _mod_spec(R, tpg), _mod_spec(R, tpg),
                  pl.BlockSpec((D_MODEL, DSA_IN_PAD), lambda i: (0, 0))] + [tab] * N_ROPE_TABLES,
        out_specs=out_specs, out_shape=out_shape,
        compiler_params=_cparams(("parallel",)),
        name=name,
    )(x, sc, sh, w_pad_bf, *tabs)


def dsa_project(x, sc, sh, w_pad_bf, pos, tm):
    N = x.shape[0]
    out = lambda w: pl.BlockSpec((tm, w), lambda i: (i, 0))
    return _dsa_project_call(
        _dsa_proj_kernel, x, sc, sh, w_pad_bf, pos, tm,
        [out(DSA_Q), out(DSA_KV), out(DSA_KV), out(IDX_HEADS * IDX_DIM), out(LANES)],
        [jax.ShapeDtypeStruct((N, DSA_Q), BF16), jax.ShapeDtypeStruct((N, DSA_KV), F32),
         jax.ShapeDtypeStruct((N, DSA_KV), F32), jax.ShapeDtypeStruct((N, IDX_HEADS * IDX_DIM), BF16),
         jax.ShapeDtypeStruct((N, LANES), F32)], "dsa_project")


def dsa_project_prompt(x, sc, sh, w_pad_bf, pos, B, T, tm, tq):
    N = B * T
    tiles, n_qt = T // tm, tm // tq
    gq = DSA_GROUP * tq
    out = lambda w: pl.BlockSpec((tm, w), lambda i: (i, 0))
    by_tile = lambda *shape: pl.BlockSpec((None,) + shape,
                                          lambda i: (i // tiles, i % tiles) + (0,) * (len(shape) - 1))
    return _dsa_project_call(
        functools.partial(_dsa_proj_prompt_kernel, tq=tq), x, sc, sh, w_pad_bf, pos, tm,
        [out(DSA_KV), out(DSA_KV), out(LANES), out(DSA_KV), out(LANES),
         pl.BlockSpec((None, DSA_KV_HEADS, 1, DSA_HEAD_DIM, tm), lambda i: (i // tiles, 0, i % tiles, 0, 0)),
         by_tile(n_qt, DSA_KV_HEADS, DSA_HEAD_DIM, gq),
         by_tile(n_qt, LANES, IDX_HEADS * tq),
         by_tile(n_qt, IDX_HEADS, tq)],
        [jax.ShapeDtypeStruct((N, DSA_KV), F32), jax.ShapeDtypeStruct((N, DSA_KV), F32),
         jax.ShapeDtypeStruct((N, LANES), F32), jax.ShapeDtypeStruct((N, DSA_KV), BF16),
         jax.ShapeDtypeStruct((N, LANES), BF16),
         jax.ShapeDtypeStruct((B, DSA_KV_HEADS, tiles, DSA_HEAD_DIM, tm), BF16),
         jax.ShapeDtypeStruct((B, T // tq, DSA_KV_HEADS, DSA_HEAD_DIM, gq), BF16),
         jax.ShapeDtypeStruct((B, T // tq, LANES, IDX_HEADS * tq), BF16),
         jax.ShapeDtypeStruct((B, T // tq, IDX_HEADS, tq), F32)], "dsa_project_prompt")


def _sortable_key(score):
    b = pltpu.bitcast(jnp.where(score == 0.0, 0.0, score), I32)
    return b ^ ((b >> 31) & jnp.int32(0x7FFFFFFF))


KEY_NEG_INF = -2139095041


def _reduce_row_groups(x, op, n_chains=8):
    parts = [x[r:r + SUBLANES] for r in range(0, x.shape[0], SUBLANES)]
    accs = parts[:n_chains]
    for j, part in enumerate(parts[n_chains:]):
        accs[j % len(accs)] = op(accs[j % len(accs)], part)
    while len(accs) > 1:
        accs = [op(a, b) for a, b in zip(accs[0::2], accs[1::2])] + (accs[-1:] if len(accs) % 2 else [])
    return accs[0]


def _sum_row_groups(x):
    return _reduce_row_groups(x, jnp.add)


def _max_row_groups(x):
    return _reduce_row_groups(x, jnp.maximum)


def _kth_largest_key(count_ge, shape, k):
    def body(s, p):
        cand = p + lax.shift_left(jnp.int32(1), 31 - s)
        return jnp.where(count_ge(cand) >= k, cand, p)
    return lax.fori_loop(0, 32, body, jnp.full(shape, INT_MIN, I32))


def _tie_index_bound(count_eq_below, shape, need, n_bits):
    def body(s, m):
        cand = m + lax.shift_left(jnp.int32(1), n_bits - 1 - s)
        return jnp.where(count_eq_below(cand) < need, cand, m)
    return lax.fori_loop(0, n_bits, body, jnp.zeros(shape, I32))


def _dsa_prompt_kernel(ki_ref, qit_ref, wit_ref, k_ref, vt_ref, qt_ref, o_ref,
                       keys_scr, bias_scr, midx_scr, logit_scr, *, tq, kb_size, cb_size, topk, idx_bits):
    i = pl.program_id(1)
    n_kb = ((i + 1) * tq + kb_size - 1) // kb_size
    n_cb = ((i + 1) * tq + cb_size - 1) // cb_size
    q_pos = i * tq + lax.broadcasted_iota(I32, (1, tq), 1)
    heads_per_dot = 2

    @pl.when(i == 0)
    def _():
        keys_scr[...] = jnp.full(keys_scr.shape, KEY_NEG_INF, I32)

    def key_rows(kb):
        return pl.ds(pl.multiple_of(kb * kb_size, kb_size), kb_size)

    def l_index(kb, size=kb_size):
        return kb * size + lax.broadcasted_iota(I32, (size, 1), 0)

    def score_body(kb, carry):
        kib = ki_ref[key_rows(kb), :]
        acc = jnp.zeros((kb_size, tq), F32)
        for hp in range(IDX_HEADS // heads_per_dot):
            s = jnp.dot(kib, qit_ref[:, hp * heads_per_dot * tq:(hp + 1) * heads_per_dot * tq],
                        preferred_element_type=F32)
            for j in range(heads_per_dot):
                h = hp * heads_per_dot + j
                w = wit_ref[h:h + 1, :] * (IDX_HEADS ** -0.5 * IDX_DIM ** -0.5)
                acc = acc + jnp.maximum(s[:, j * tq:(j + 1) * tq], 0.0) * w
        allowed = l_index(kb) <= q_pos
        keys_scr[key_rows(kb), :] = _sortable_key(jnp.where(allowed, acc, -jnp.inf))
        return carry

    lax.fori_loop(0, n_kb, score_body, 0)

    def count(pred_fn):
        def body(cb, acc):
            rows = pl.ds(pl.multiple_of(cb * cb_size, cb_size), cb_size)
            m = jnp.where(pred_fn(keys_scr[rows, :], cb), 1.0, 0.0)
            return acc + _sum_row_groups(m)
        acc = lax.fori_loop(0, n_cb, body, jnp.zeros((SUBLANES, tq), F32))
        return acc.sum(axis=0, keepdims=True)

    thr = _kth_largest_key(lambda c: count(lambda key, cb: key >= c), (1, tq), float(topk))
    n_gt = count(lambda key, cb: key > thr)
    n_ge = count(lambda key, cb: key >= thr)
    need = float(topk) - n_gt
    excess = (n_ge - n_gt > need) & (thr != KEY_NEG_INF)
    midx_scr[...] = jnp.full((1, tq), 2 ** idx_bits, I32)

    @pl.when(jnp.max(jnp.where(excess, 1.0, 0.0)) > 0.5)
    def _():
        m = _tie_index_bound(
            lambda c: count(lambda key, cb: (key == thr) & (l_index(cb, cb_size) < c)),
            (1, tq), need, idx_bits)
        midx_scr[...] = jnp.where(excess, m, 2 ** idx_bits)

    midx = midx_scr[...]

    def bias_body(kb, carry):
        key = keys_scr[key_rows(kb), :]
        l = l_index(kb)
        sel = ((key > thr) | ((key == thr) & (l <= midx))) & (l <= q_pos)
        bias_scr[key_rows(kb), :] = jnp.where(sel, 0.0, NEG_BIG)
        return carry

    lax.fori_loop(0, n_kb, bias_body, 0)

    gq = DSA_GROUP * tq
    heads = range(DSA_KV_HEADS)

    def logit_body(kb, m_run):
        bias = bias_scr[key_rows(kb), :]
        bias = jnp.concatenate([bias] * DSA_GROUP, axis=1)
        new = []
        for n in heads:
            kblk = k_ref[key_rows(kb), n * DSA_HEAD_DIM:(n + 1) * DSA_HEAD_DIM]
            logits = jnp.dot(kblk, qt_ref[n], preferred_element_type=F32) + bias
            logit_scr[n, key_rows(kb), :] = logits
            new.append(jnp.maximum(m_run[n], _max_row_groups(logits)))
        return tuple(new)

    m8 = lax.fori_loop(0, n_kb, logit_body,
                       tuple(jnp.full((SUBLANES, gq), NEG_BIG, F32) for _ in heads))
    m_fin = [jnp.max(m, axis=0, keepdims=True) for m in m8]

    def pv_body(kb, carry):
        new = []
        for n in heads:
            l_run, acc = carry[n]
            p = jnp.exp2(logit_scr[n, key_rows(kb), :] - m_fin[n])
            acc = acc + jnp.dot(vt_ref[n, kb], p.astype(BF16), preferred_element_type=F32)
            new.append((l_run + _sum_row_groups(p), acc))
        return tuple(new)

    fin = lax.fori_loop(0, n_kb, pv_body,
                        tuple((jnp.zeros((SUBLANES, gq), F32), jnp.zeros((DSA_HEAD_DIM, gq), F32))
                              for _ in heads))
    for n in heads:
        l8, acc = fin[n]
        o_t = acc / jnp.sum(l8, axis=0, keepdims=True)
        for g in range(DSA_GROUP):
            h = n * DSA_GROUP + g
            o_ref[:, h * DSA_HEAD_DIM:(h + 1) * DSA_HEAD_DIM] = (
                o_t[:, g * tq:(g + 1) * tq].T.astype(o_ref.dtype))


def dsa_prompt_attend(kw_bf, qi_t, wi_t, k_bf, v_t, q_t, B, T, topk, tq, kb_size):
    nqt = T // tq
    nkb = T // kb_size
    gq = DSA_GROUP * tq
    return pl.pallas_call(
        functools.partial(_dsa_prompt_kernel, tq=tq, kb_size=kb_size, cb_size=min(1024, T), topk=topk,
                          idx_bits=max(1, (T - 1).bit_length())),
        grid=(B, nqt),
        in_specs=[pl.BlockSpec((None, T, LANES), lambda b, i: (b, 0, 0)),
                  pl.BlockSpec((None, None, LANES, IDX_HEADS * tq), lambda b, i: (b, i, 0, 0)),
                  pl.BlockSpec((None, None, IDX_HEADS, tq), lambda b, i: (b, i, 0, 0)),
                  pl.BlockSpec((None, T, DSA_KV), lambda b, i: (b, 0, 0)),
                  pl.BlockSpec((None, DSA_KV_HEADS, nkb, DSA_HEAD_DIM, kb_size),
                               lambda b, i: (b, 0, 0, 0, 0)),
                  pl.BlockSpec((None, None, DSA_KV_HEADS, DSA_HEAD_DIM, gq),
                               lambda b, i: (b, i, 0, 0, 0))],
        out_specs=pl.BlockSpec((tq, DSA_Q), lambda b, i: (b * nqt + i, 0)),
        out_shape=jax.ShapeDtypeStruct((B * T, DSA_Q), BF16),
        scratch_shapes=[pltpu.VMEM((T, tq), I32), pltpu.VMEM((T, tq), F32), pltpu.VMEM((1, tq), I32),
                        pltpu.VMEM((DSA_KV_HEADS, T, gq), F32)],
        compiler_params=_cparams(("arbitrary", "arbitrary")),
        name="dsa_prompt_attend",
    )(kw_bf.reshape(B, T, LANES), qi_t, wi_t, k_bf.reshape(B, T, DSA_KV), v_t, q_t)


def _dsa_sample_score_kernel(pt_ref, qi_ref, wi_ref, kis_ref, *rest, n_pages, page):
    pages, o_ref = rest[:n_pages], rest[n_pages]
    qi = qi_ref[...]
    w = wi_ref[...] * (IDX_HEADS ** -0.5 * IDX_DIM ** -0.5)
    for j in range(n_pages):
        dots = lax.dot_general(qi, pages[j][...].astype(BF16), (((1,), (1,)), ((), ())),
                               preferred_element_type=F32)
        o_ref[:, j * page:(j + 1) * page] = jnp.sum(jnp.maximum(dots, 0.0) * w, axis=0, keepdims=True)
    ki_self = kis_ref[...].astype(BF16).astype(F32)
    d_self = jnp.sum(qi.astype(F32) * ki_self, -1, keepdims=True)
    s_self = jnp.sum(jnp.maximum(d_self, 0.0) * w, axis=0, keepdims=True)
    lane = lax.broadcasted_iota(I32, (1, LANES), 1)
    o_ref[:, n_pages * page:] = jnp.where(lane == 0, s_self, -jnp.inf)


def _page_specs(n_pages, page, width):
    return [pl.BlockSpec((None, page, width), functools.partial(lambda j, b, pt: (pt[b, j], 0, 0), j))
            for j in range(n_pages)]


def dsa_sample_scores(page_table, qi_bf, wi, ki_self, cache_kidx):
    DB, n_pages = page_table.shape
    page = cache_kidx.shape[1]
    lp = n_pages * page + LANES
    per_b = lambda *shape: pl.BlockSpec((None,) + shape, lambda b, pt: (b,) + (0,) * len(shape))
    return pl.pallas_call(
        functools.partial(_dsa_sample_score_kernel, n_pages=n_pages, page=page),
        grid_spec=pltpu.PrefetchScalarGridSpec(
            num_scalar_prefetch=1, grid=(DB,),
            in_specs=[per_b(IDX_HEADS, IDX_DIM), per_b(IDX_HEADS, 1), per_b(1, IDX_DIM)]
            + _page_specs(n_pages, page, IDX_DIM),
            out_specs=per_b(1, lp)),
        out_shape=jax.ShapeDtypeStruct((DB, 1, lp), F32),
        compiler_params=_cparams(("arbitrary",)),
        name="dsa_sample_scores",
    )(page_table, qi_bf.reshape(DB, IDX_HEADS, IDX_DIM), wi.reshape(DB, IDX_HEADS, 1),
      ki_self.reshape(DB, 1, IDX_DIM), *([cache_kidx] * n_pages))


def _dsa_sample_select_kernel(s_ref, o_ref, *, n_keys, topk, idx_bits):
    score = s_ref[...]
    lane = lax.broadcasted_iota(I32, score.shape, 1)
    valid = lane < n_keys
    key = _sortable_key(jnp.where(valid, score, -jnp.inf))

    def count(pred):
        return jnp.sum(jnp.where(pred, 1.0, 0.0), -1, keepdims=True)

    shape = (score.shape[0], 1)
    thr = _kth_largest_key(lambda c: count(key >= c), shape, float(topk))
    n_gt = count(key > thr)
    need = float(topk) - n_gt
    excess = (count(key == thr) > need) & (thr != KEY_NEG_INF)
    m = _tie_index_bound(lambda c: count((key == thr) & (lane < c)), shape, need, idx_bits)
    midx = jnp.where(excess, m, 2 ** idx_bits)
    sel = ((key > thr) | ((key == thr) & (lane <= midx))) & valid
    o_ref[...] = jnp.where(sel, 0.0, NEG_BIG)


def dsa_sample_select(score, n_keys, topk):
    DB, lp = score.shape
    return pl.pallas_call(
        functools.partial(_dsa_sample_select_kernel, n_keys=n_keys, topk=topk,
                          idx_bits=max(1, (lp - 1).bit_length())),
        grid=(1,),
        in_specs=[pl.BlockSpec((DB, lp), lambda i: (0, 0))],
        out_specs=pl.BlockSpec((DB, lp), lambda i: (0, 0)),
        out_shape=jax.ShapeDtypeStruct((DB, lp), F32),
        compiler_params=_cparams(("arbitrary",)),
        name="dsa_sample_select",
    )(score)


def _dsa_sample_attend_kernel(pt_ref, q_ref, bias_ref, ks_ref, vs_ref, *rest, n_pages, page):
    k_pages, v_pages = rest[:n_pages], rest[n_pages:2 * n_pages]
    o_ref, logit_scr = rest[2 * n_pages], rest[2 * n_pages + 1]
    q = q_ref[...]
    tail = n_pages * page
    lane = lax.broadcasted_iota(I32, (DSA_GROUP, LANES), 1)
    for n in range(DSA_KV_HEADS):
        hs = slice(n * DSA_GROUP, (n + 1) * DSA_GROUP)
        ds = slice(n * DSA_HEAD_DIM, (n + 1) * DSA_HEAD_DIM)
        head_rows = pl.ds(n, page, stride=DSA_KV_HEADS)
        qn = q[hs, :]
        for j in range(n_pages):
            lg = lax.dot_general(qn, k_pages[j][head_rows, :].astype(BF16), (((1,), (1,)), ((), ())),
                                 preferred_element_type=F32)
            logit_scr[hs, j * page:(j + 1) * page] = lg + bias_ref[:, j * page:(j + 1) * page]
        k_self = ks_ref[:, ds].astype(BF16).astype(F32)
        lg_self = jnp.sum(qn.astype(F32) * k_self, -1, keepdims=True)
        logit_scr[hs, tail:] = jnp.where(lane == 0, lg_self, 0.0) + bias_ref[:, tail:]
        logits = logit_scr[hs, :]
        m = jnp.max(logits, -1, keepdims=True)
        p = jnp.exp2(logits - m)
        p_bf = (p / jnp.sum(p, -1, keepdims=True)).astype(BF16)
        v_self = vs_ref[:, ds].astype(BF16).astype(F32)
        acc = p_bf[:, tail:].astype(F32)[:, :1] * v_self
        for j in range(n_pages):
            acc = acc + jnp.dot(p_bf[:, j * page:(j + 1) * page], v_pages[j][head_rows, :].astype(BF16),
                                preferred_element_type=F32)
        o_ref[hs, :] = acc


def dsa_sample_attend(page_table, q_bf, bias, k_self, v_self, cache_k, cache_v):
    DB, n_pages = page_table.shape
    rows = cache_k.shape[1]
    page = rows // DSA_KV_HEADS
    lp = n_pages * page + LANES
    per_b = lambda *shape: pl.BlockSpec((None,) + shape, lambda b, pt: (b,) + (0,) * len(shape))
    o = pl.pallas_call(
        functools.partial(_dsa_sample_attend_kernel, n_pages=n_pages, page=page),
        grid_spec=pltpu.PrefetchScalarGridSpec(
            num_scalar_prefetch=1, grid=(DB,),
            in_specs=[per_b(DSA_HEADS, DSA_HEAD_DIM), per_b(1, lp), per_b(1, DSA_KV), per_b(1, DSA_KV)]
            + _page_specs(n_pages, rows, DSA_HEAD_DIM) + _page_specs(n_pages, rows, DSA_HEAD_DIM),
            out_specs=per_b(DSA_HEADS, DSA_HEAD_DIM),
            scratch_shapes=[pltpu.VMEM((DSA_HEADS, lp), F32)]),
        out_shape=jax.ShapeDtypeStruct((DB, DSA_HEADS, DSA_HEAD_DIM), F32),
        compiler_params=_cparams(("arbitrary",)),
        name="dsa_sample_attend",
    )(page_table, q_bf.reshape(DB, DSA_HEADS, DSA_HEAD_DIM), bias.reshape(DB, 1, lp),
      k_self.reshape(DB, 1, DSA_KV), v_self.reshape(DB, 1, DSA_KV),
      *([cache_k] * n_pages), *([cache_v] * n_pages))
    return o.reshape(DB, DSA_Q)


def _retnet_tables(pos):
    inv_freq = jnp.power(RET_ANGLE_BASE, -jnp.linspace(0.0, 1.0, RET_DK // 2, dtype=F32))
    ang = pos.astype(F32)[:, None] * inv_freq[None, :]
    return jnp.cos(ang), jnp.sin(ang)


def _split_mod(mod_l, G, R):
    return [m.reshape(G, R, D_MODEL) for m in jnp.split(mod_l, 6, axis=-1)]


def _channel_mixer(x, sc2, sh2, g2, l, wts, tm):
    gates = router(x, sc2, sh2, wts["w_router"], wts["b_router"], tm)
    return moe_ln(x, sc2, sh2, g2, gates, wts["w_gate"][l], wts["w_up"][l], wts["w_down"][l],
                  wts["ln2_g"][l], wts["ln2_b"][l], tm)


def kernel(x_prompt, x_sample, state_ret, cache_k, cache_v, cache_kidx, page_table,
           c_prompt, c_sample, w_mod, b_mod, ln1_g, ln1_b, ln2_g, ln2_b,
           w_in_ret, gn_ret_g, w_out_ret, w_in_dsa, w_out_dsa,
           w_router, b_router, w_gate, w_up, w_down):
    B, T, _ = x_prompt.shape
    DB = x_sample.shape[0]
    assert x_sample.shape[1] == 1
    n_pages = page_table.shape[1]
    page = cache_k.shape[2]
    past = n_pages * page
    n_pool = cache_k.shape[1]

    wts = dict(w_router=w_router, b_router=b_router, ln2_g=ln2_g, ln2_b=ln2_b,
               w_gate=w_gate.astype(BF16), w_up=w_up.astype(BF16), w_down=w_down.astype(BF16))
    w_mod_bf = w_mod.astype(BF16)
    w_in_ret_bf = w_in_ret[0].astype(BF16)
    w_out_ret_bf = w_out_ret[0].astype(BF16)
    w_in_dsa_bf = jnp.pad(w_in_dsa[0], ((0, 0), (0, DSA_IN_PAD - DSA_IN))).astype(BF16)
    w_out_dsa_bf = w_out_dsa[0].astype(BF16)

    pos_p = jnp.arange(T, dtype=I32)
    pos_s = jnp.full((1,), past, I32)

    tm_p = min(512, T)
    tq = min(128, T)
    xp = x_prompt.reshape(B * T, D_MODEL)
    mod_p = modulation(c_prompt, w_mod_bf, b_mod)
    sh1, sc1, g1, sh2, sc2, g2 = _split_mod(mod_p[0], B, 1)
    proj = mod_proj(xp, sc1, sh1, w_in_ret_bf, min(1024, T), 1536, BF16)
    cos_p, sin_p = _retnet_tables(pos_p)
    gated, ret_p = retention_prompt(proj.reshape(B, T, RET_IN), gn_ret_g[0], cos_p, sin_p, min(256, T))
    xp = out_proj_ln(gated.reshape(B * T, RET_V), w_out_ret_bf, xp, g1, ln1_g[0], ln1_b[0], tm_p)
    xp = _channel_mixer(xp, sc2, sh2, g2, 0, wts, tm_p)

    sh1, sc1, g1, sh2, sc2, g2 = _split_mod(mod_p[1], B, 1)
    k_p, v_p, kw_p, k_bf, kw_bf, v_t, q_t, qi_t, wi_t = dsa_project_prompt(
        xp, sc1, sh1, w_in_dsa_bf, pos_p, B, T, tm_p, tq)
    o_p = dsa_prompt_attend(kw_bf, qi_t, wi_t, k_bf, v_t, q_t, B, T, min(DSA_TOPK, T // 4), tq, tm_p)
    xp = out_proj_ln(o_p, w_out_dsa_bf, xp, g1, ln1_g[1], ln1_b[1], tm_p)
    xp = _channel_mixer(xp, sc2, sh2, g2, 1, wts, tm_p)

    xs = x_sample.reshape(DB, D_MODEL)
    mod_s = modulation(c_sample, w_mod_bf, b_mod)
    sh1, sc1, g1, sh2, sc2, g2 = _split_mod(mod_s[0], 1, DB)
    proj_s = mod_proj(xs, sc1, sh1, w_in_ret_bf, DB, 1536, F32)
    cos_s, sin_s = _retnet_tables(pos_s)
    gated_s, ret_s = retention_sample(proj_s, state_ret[0], gn_ret_g[0], cos_s, sin_s)
    xs = out_proj_ln(gated_s.reshape(DB, RET_V), w_out_ret_bf, xs, g1, ln1_g[0], ln1_b[0], DB)
    xs = _channel_mixer(xs, sc2, sh2, g2, 0, wts, DB)

    sh1, sc1, g1, sh2, sc2, g2 = _split_mod(mod_s[1], 1, DB)
    pos_rows = jnp.full((DB,), past, I32)
    q_s, k_s, v_s, qi_s, kw_s = dsa_project(xs, sc1, sh1, w_in_dsa_bf, pos_rows, DB)
    ki_s = kw_s[:, :IDX_DIM]
    score = dsa_sample_scores(page_table, qi_s, kw_s[:, IDX_DIM:IDX_DIM + IDX_HEADS], ki_s,
                              cache_kidx[0])
    bias = dsa_sample_select(score.reshape(DB, past + LANES), past + 1, min(DSA_TOPK, (past + 1) // 4))
    o_s = dsa_sample_attend(page_table, q_s, bias, k_s, v_s,
                            cache_k[0].reshape(n_pool, page * DSA_KV_HEADS, DSA_HEAD_DIM),
                            cache_v[0].reshape(n_pool, page * DSA_KV_HEADS, DSA_HEAD_DIM))
    xs = out_proj_ln(o_s, w_out_dsa_bf, xs, g1, ln1_g[1], ln1_b[1], DB)
    xs = _channel_mixer(xs, sc2, sh2, g2, 1, wts, DB)

    kv_shape = (DSA_KV_HEADS, DSA_HEAD_DIM)
    return (xp.reshape(B, T, D_MODEL), xs.reshape(DB, 1, D_MODEL),
            ret_p[None], ret_s[None],
            k_p.reshape(1, B, T, *kv_shape), v_p.reshape(1, B, T, *kv_shape),
            kw_p[:, :IDX_DIM].reshape(1, B, T, IDX_DIM),
            k_s.reshape(1, DB, 1, *kv_shape), v_s.reshape(1, DB, 1, *kv_shape),
            ki_s.reshape(1, DB, 1, IDX_DIM))
```

```python
import functools

import jax
import jax.numpy as jnp
from jax import lax
from jax.experimental import pallas as pl
from jax.experimental.pallas import tpu as pltpu

F32 = jnp.float32
BF16 = jnp.bfloat16
I32 = jnp.int32

D_MODEL = 1024
DEPTH = 2
ALPHA = (2.0 * DEPTH) ** 0.25
LN_EPS = 1e-5
GN_EPS = 1e-6

RET_HEADS = 4
RET_DK = 256
RET_DV = 512
RET_QK = RET_HEADS * RET_DK
RET_V = RET_HEADS * RET_DV
RET_IN = 2 * RET_QK + 2 * RET_V
RET_ANGLE_BASE = 10000.0

DSA_HEADS = 8
DSA_KV_HEADS = 2
DSA_HEAD_DIM = 128
DSA_GROUP = DSA_HEADS // DSA_KV_HEADS
DSA_Q = DSA_HEADS * DSA_HEAD_DIM
DSA_KV = DSA_KV_HEADS * DSA_HEAD_DIM
IDX_HEADS = 16
IDX_DIM = 64
DSA_TOPK = 256
ROPE_THETA = 500000.0
ROT_DIM = DSA_HEAD_DIM // 4
IDX_ROT_DIM = IDX_DIM // 4
DSA_IN = DSA_Q + 2 * DSA_KV + IDX_HEADS * IDX_DIM + IDX_DIM + IDX_HEADS

N_EXPERTS = 16
N_GROUPS = 4
EXPERTS_PER_GROUP = N_EXPERTS // N_GROUPS
D_EXPERT = 512
GROUP_LANE0 = N_EXPERTS

LANES = 128
SUBLANES = 8
VMEM_LIMIT = 56 * 1024 * 1024
NEG_BIG = -1e30
INT_MIN = -2147483648


def _cparams(sem):
    return pltpu.CompilerParams(dimension_semantics=sem, vmem_limit_bytes=VMEM_LIMIT)


def _silu(x):
    return x * (1.0 / (1.0 + jnp.exp(-x)))


def _layer_norm(z, g, b):
    mu = jnp.mean(z, -1, keepdims=True)
    d = z - mu
    var = jnp.mean(d * d, -1, keepdims=True)
    return d * lax.rsqrt(var + LN_EPS) * g + b


def _mod_kernel(c_ref, w_ref, b_ref, o_ref):
    a = _silu(c_ref[...]).astype(BF16)
    o_ref[...] = jnp.dot(a, w_ref[...], preferred_element_type=F32) + b_ref[...]


def modulation(c, w_mod_bf, b_mod):
    R = c.shape[0]
    tn = 1536
    return pl.pallas_call(
        _mod_kernel,
        grid=(DEPTH, 6 * D_MODEL // tn),
        in_specs=[pl.BlockSpec((R, D_MODEL), lambda l, j: (0, 0)),
                  pl.BlockSpec((None, D_MODEL, tn), lambda l, j: (l, 0, j)),
                  pl.BlockSpec((None, 1, tn), lambda l, j: (l, 0, j))],
        out_specs=pl.BlockSpec((None, R, tn), lambda l, j: (l, 0, j)),
        out_shape=jax.ShapeDtypeStruct((DEPTH, R, 6 * D_MODEL), F32),
        compiler_params=_cparams(("parallel", "parallel")),
        name="modulation",
    )(c, w_mod_bf, b_mod.reshape(DEPTH, 1, 6 * D_MODEL))


def _proj_kernel(x_ref, sc_ref, sh_ref, w_ref, o_ref, h_scr):
    @pl.when(pl.program_id(1) == 0)
    def _():
        h_scr[...] = (x_ref[...] * (1.0 + sc_ref[...]) + sh_ref[...]).astype(BF16)

    o_ref[...] = jnp.dot(h_scr[...], w_ref[...], preferred_element_type=F32).astype(o_ref.dtype)


def _mod_spec(R, tiles_per_group):
    return pl.BlockSpec((None, R, D_MODEL), lambda i, *_: (i // tiles_per_group, 0, 0))


def mod_proj(x, sc, sh, w_bf, tm, tn, out_dtype):
    N = x.shape[0]
    G, R, _ = sc.shape
    n_out = w_bf.shape[1]
    tpg = (N // G) // tm
    return pl.pallas_call(
        _proj_kernel,
        grid=(N // tm, n_out // tn),
        in_specs=[pl.BlockSpec((tm, D_MODEL), lambda i, j: (i, 0)),
                  _mod_spec(R, tpg), _mod_spec(R, tpg),
                  pl.BlockSpec((D_MODEL, tn), lambda i, j: (0, j))],
        out_specs=pl.BlockSpec((tm, tn), lambda i, j: (i, j)),
        out_shape=jax.ShapeDtypeStruct((N, n_out), out_dtype),
        scratch_shapes=[pltpu.VMEM((tm, D_MODEL), BF16)],
        compiler_params=_cparams(("parallel", "arbitrary")),
        name="mod_proj",
    )(x, sc, sh, w_bf)


def _rot_half(x, cos, sin):
    half = cos.shape[-1]
    x1, x2 = x[:, :half], x[:, half:]
    return jnp.concatenate([x1 * cos - x2 * sin, x1 * sin + x2 * cos], axis=1)


def _group_norm_gate(o, gn, g):
    mu = jnp.mean(o, -1, keepdims=True)
    d = o - mu
    var = jnp.mean(d * d, -1, keepdims=True)
    return d * lax.rsqrt(var + GN_EPS) * gn * _silu(g)


def _ret_prompt_kernel(q_ref, k_ref, v_ref, g_ref, cos_ref, sin_ref, lg_ref, gn_ref,
                       o_ref, s_ref, s_scr, *, chunk):
    c = pl.program_id(2)

    @pl.when(c == 0)
    def _():
        s_scr[...] = jnp.zeros_like(s_scr)

    cos, sin = cos_ref[...], sin_ref[...]
    lg_row = lg_ref[...]
    lg = lg_row[:, :1]
    q = _rot_half(q_ref[...].astype(F32), cos, sin)
    k = _rot_half(k_ref[...].astype(F32), cos, sin) * (RET_DK ** -0.5)
    vb = v_ref[...].astype(BF16)
    qb = q.astype(BF16)

    row = lax.broadcasted_iota(I32, (chunk, chunk), 0)
    col = lax.broadcasted_iota(I32, (chunk, chunk), 1)
    diff = (row - col).astype(F32)
    decay = jnp.where(diff >= 0, jnp.exp(lg_row * jnp.maximum(diff, 0.0)), 0.0)
    idx = lax.broadcasted_iota(I32, (chunk, 1), 0).astype(F32)
    q_dec = jnp.exp(lg * (idx + 1.0))
    k_dec = jnp.exp(lg * (chunk - 1.0 - idx))
    chunk_dec = jnp.exp(lg * chunk)

    s = s_scr[...]
    att = lax.dot_general(qb, k.astype(BF16), (((1,), (1,)), ((), ())),
                          preferred_element_type=F32) * decay
    o = (jnp.dot(att.astype(BF16), vb, preferred_element_type=F32)
         + jnp.dot(qb, s.astype(BF16), preferred_element_type=F32) * q_dec)
    kd_t = (k * k_dec).T.astype(BF16)
    s_new = chunk_dec * s + jnp.dot(kd_t, vb, preferred_element_type=F32)
    s_scr[...] = s_new
    o_ref[...] = _group_norm_gate(o, gn_ref[...], g_ref[...].astype(F32)).astype(o_ref.dtype)

    @pl.when(c == pl.num_programs(2) - 1)
    def _():
        s_ref[...] = s_new


def _log_gamma():
    return jnp.log(1.0 - jnp.power(2.0, -5.0 - jnp.arange(RET_HEADS, dtype=F32)))


def retention_prompt(proj, gn_g, cos, sin, chunk):
    B, T, _ = proj.shape
    lg_tab = jnp.broadcast_to(_log_gamma()[:, None, None], (RET_HEADS, 1, chunk))
    qk_blk = lambda off: pl.BlockSpec((None, chunk, RET_DK), lambda b, h, c: (b, c, off + h))
    v_blk = lambda off: pl.BlockSpec((None, chunk, RET_DV), lambda b, h, c: (b, c, off + h))
    tab = pl.BlockSpec((chunk, RET_DK // 2), lambda b, h, c: (c, 0))
    return pl.pallas_call(
        functools.partial(_ret_prompt_kernel, chunk=chunk),
        grid=(B, RET_HEADS, T // chunk),
        in_specs=[qk_blk(0), qk_blk(RET_QK // RET_DK),
                  v_blk(2 * RET_QK // RET_DV), v_blk((2 * RET_QK + RET_V) // RET_DV),
                  tab, tab,
                  pl.BlockSpec((None, 1, chunk), lambda b, h, c: (h, 0, 0)),
                  pl.BlockSpec((1, RET_DV), lambda b, h, c: (0, h))],
        out_specs=[pl.BlockSpec((None, chunk, RET_DV), lambda b, h, c: (b, c, h)),
                   pl.BlockSpec((None, None, RET_DK, RET_DV), lambda b, h, c: (b, h, 0, 0))],
        out_shape=[jax.ShapeDtypeStruct((B, T, RET_V), BF16),
                   jax.ShapeDtypeStruct((B, RET_HEADS, RET_DK, RET_DV), F32)],
        scratch_shapes=[pltpu.VMEM((RET_DK, RET_DV), F32)],
        compiler_params=_cparams(("parallel", "parallel", "arbitrary")),
        name="retention_prompt",
    )(proj, proj, proj, proj, cos, sin, lg_tab, gn_g.reshape(1, RET_V))


def _ret_sample_kernel(qk_ref, v_ref, g_ref, s0_ref, cos_ref, sin_ref, gam_ref, gn_ref,
                       o_ref, s_ref):
    t = _rot_half(qk_ref[...], cos_ref[...], sin_ref[...])
    row = lax.broadcasted_iota(I32, t.shape, 0)
    t = jnp.where(row >= RET_HEADS, t * (RET_DK ** -0.5), t)
    pad = jnp.zeros((LANES - 2 * RET_HEADS, RET_DK), F32)
    t_t = jnp.concatenate([t, pad], axis=0).T
    for h in range(RET_HEADS):
        qc = t_t[:, h:h + 1]
        kc = t_t[:, RET_HEADS + h:RET_HEADS + h + 1]
        gam = gam_ref[h]
        vh = v_ref[h:h + 1, :]
        s = s0_ref[h]
        qk_dot = jnp.sum(t[h:h + 1, :] * t[RET_HEADS + h:RET_HEADS + h + 1, :], -1, keepdims=True)
        o = qk_dot * vh + gam * jnp.sum(s * qc, axis=0, keepdims=True)
        s_ref[h] = gam * s + kc * vh
        o_ref[h:h + 1, :] = _group_norm_gate(o, gn_ref[h:h + 1, :], g_ref[h:h + 1, :])


def retention_sample(proj, s0, gn_g, cos, sin):
    DB = proj.shape[0]
    qk = proj[:, :2 * RET_QK].reshape(DB, 2 * RET_HEADS, RET_DK)
    v = proj[:, 2 * RET_QK:2 * RET_QK + RET_V].reshape(DB, RET_HEADS, RET_DV)
    g = proj[:, 2 * RET_QK + RET_V:].reshape(DB, RET_HEADS, RET_DV)
    gam = jnp.broadcast_to(jnp.exp(_log_gamma())[:, None, None], (RET_HEADS, 1, RET_DV))
    full = lambda *shape: pl.BlockSpec(shape, lambda b: (0,) * len(shape))
    per_b = lambda *shape: pl.BlockSpec((None,) + shape, lambda b: (b,) + (0,) * len(shape))
    return pl.pallas_call(
        _ret_sample_kernel,
        grid=(DB,),
        in_specs=[per_b(2 * RET_HEADS, RET_DK), per_b(RET_HEADS, RET_DV), per_b(RET_HEADS, RET_DV),
                  per_b(RET_HEADS, RET_DK, RET_DV),
                  full(1, RET_DK // 2), full(1, RET_DK // 2),
                  full(RET_HEADS, 1, RET_DV), full(RET_HEADS, RET_DV)],
        out_specs=[per_b(RET_HEADS, RET_DV), per_b(RET_HEADS, RET_DK, RET_DV)],
        out_shape=[jax.ShapeDtypeStruct((DB, RET_HEADS, RET_DV), F32),
                   jax.ShapeDtypeStruct((DB, RET_HEADS, RET_DK, RET_DV), F32)],
        compiler_params=_cparams(("parallel",)),
        name="retention_sample",
    )(qk, v, g, s0, cos, sin, gam, gn_g.reshape(RET_HEADS, RET_DV))


def _out_ln_kernel(a_ref, w_ref, x_ref, g_ref, lng_ref, lnb_ref, o_ref):
    y = jnp.dot(a_ref[...].astype(BF16), w_ref[...], preferred_element_type=F32)
    o_ref[...] = _layer_norm(ALPHA * x_ref[...] + g_ref[...] * y, lng_ref[...], lnb_ref[...])


def out_proj_ln(a, w_bf, x, gate, ln_g, ln_b, tm):
    N, K = a.shape
    G, R, _ = gate.shape
    tpg = (N // G) // tm
    row = pl.BlockSpec((1, D_MODEL), lambda i: (0, 0))
    return pl.pallas_call(
        _out_ln_kernel,
        grid=(N // tm,),
        in_specs=[pl.BlockSpec((tm, K), lambda i: (i, 0)),
                  pl.BlockSpec((K, D_MODEL), lambda i: (0, 0)),
                  pl.BlockSpec((tm, D_MODEL), lambda i: (i, 0)),
                  _mod_spec(R, tpg), row, row],
        out_specs=pl.BlockSpec((tm, D_MODEL), lambda i: (i, 0)),
        out_shape=jax.ShapeDtypeStruct((N, D_MODEL), F32),
        compiler_params=_cparams(("parallel",)),
        name="out_proj_ln",
    )(a, w_bf, x, gate, ln_g.reshape(1, D_MODEL), ln_b.reshape(1, D_MODEL))


def _lane_roll(x, shift):
    return pltpu.roll(x, shift, axis=1)


def _router_kernel(x_ref, sc_ref, sh_ref, whi_ref, wlo_ref, b_ref, o_ref):
    h = x_ref[...] * (1.0 + sc_ref[...]) + sh_ref[...]
    hi = h.astype(BF16)
    lo = (h - hi.astype(F32)).astype(BF16)
    whi, wlo = whi_ref[...], wlo_ref[...]
    logits = (jnp.dot(hi, whi, preferred_element_type=F32)
              + jnp.dot(hi, wlo, preferred_element_type=F32)
              + jnp.dot(lo, whi, preferred_element_type=F32)) + b_ref[...]
    lane_i = lax.broadcasted_iota(I32, logits.shape, 1)
    lane = lane_i.astype(F32)
    m = jnp.max(logits, -1, keepdims=True)
    e = jnp.exp(logits - m)
    p = e / jnp.sum(e, -1, keepdims=True)
    s1 = p + _lane_roll(p, 1)
    s2 = p + _lane_roll(p, 2)
    s3 = p + _lane_roll(p, 3)
    pair_max = jnp.maximum(jnp.maximum(jnp.maximum(s1, s2), s3),
                           jnp.maximum(jnp.maximum(_lane_roll(s1, 1), _lane_roll(s2, 1)),
                                       _lane_roll(s1, 2)))
    is_last = ((lane_i % EXPERTS_PER_GROUP) == EXPERTS_PER_GROUP - 1) & (lane_i < N_EXPERTS)
    grp_score = jnp.where(is_last, pair_max, -jnp.inf)
    gmax = jnp.max(grp_score, -1, keepdims=True)
    big = float(4 * LANES)
    sel_last = jnp.min(jnp.where(grp_score == gmax, lane, big), -1, keepdims=True)
    in_grp = (lane <= sel_last) & (lane > sel_last - EXPERTS_PER_GROUP)
    masked = jnp.where(in_grp, p, -jnp.inf)
    v1 = jnp.max(masked, -1, keepdims=True)
    i1 = jnp.min(jnp.where(masked == v1, lane, big), -1, keepdims=True)
    masked2 = jnp.where(lane == i1, -jnp.inf, masked)
    v2 = jnp.max(masked2, -1, keepdims=True)
    i2 = jnp.min(jnp.where(masked2 == v2, lane, big), -1, keepdims=True)
    tot = v1 + v2
    grp_hot = lane == (sel_last - (EXPERTS_PER_GROUP - 1)) * (1.0 / EXPERTS_PER_GROUP) + GROUP_LANE0
    o_ref[...] = (jnp.where(lane == i1, v1 / tot, 0.0) + jnp.where(lane == i2, v2 / tot, 0.0)
                  + jnp.where(grp_hot, 1.0, 0.0))


def router(x, sc, sh, w_router, b_router, tm):
    N = x.shape[0]
    G, R, _ = sc.shape
    tpg = (N // G) // tm
    w_pad = jnp.zeros((D_MODEL, LANES), F32).at[:, :N_EXPERTS].set(w_router)
    w_hi = w_pad.astype(BF16)
    w_lo = (w_pad - w_hi.astype(F32)).astype(BF16)
    b_pad = jnp.full((1, LANES), NEG_BIG, F32).at[0, :N_EXPERTS].set(b_router)
    full = lambda r, c: pl.BlockSpec((r, c), lambda i: (0, 0))
    return pl.pallas_call(
        _router_kernel,
        grid=(N // tm,),
        in_specs=[pl.BlockSpec((tm, D_MODEL), lambda i: (i, 0)), _mod_spec(R, tpg), _mod_spec(R, tpg),
                  full(D_MODEL, LANES), full(D_MODEL, LANES), full(1, LANES)],
        out_specs=pl.BlockSpec((tm, LANES), lambda i: (i, 0)),
        out_shape=jax.ShapeDtypeStruct((N, LANES), F32),
        compiler_params=_cparams(("parallel",)),
        name="router",
    )(x, sc, sh, w_hi, w_lo, b_pad)


def _moe_kernel(x_ref, sc_ref, sh_ref, g2_ref, gates_ref, wg_ref, wu_ref, wd_ref, lng_ref, lnb_ref,
                o_ref, h_scr, acc_scr):
    e = pl.program_id(1)

    @pl.when(e == 0)
    def _():
        h_scr[...] = (x_ref[...] * (1.0 + sc_ref[...]) + sh_ref[...]).astype(BF16)
        acc_scr[...] = jnp.zeros_like(acc_scr)

    hb = h_scr[...]
    a = (_silu(jnp.dot(hb, wg_ref[...], preferred_element_type=F32))
         * jnp.dot(hb, wu_ref[...], preferred_element_type=F32))
    y = jnp.dot(a.astype(BF16), wd_ref[...], preferred_element_type=F32)
    gates = gates_ref[...]
    lane = lax.broadcasted_iota(I32, gates.shape, 1)
    gate_e = jnp.sum(jnp.where(lane == e, gates, 0.0), -1, keepdims=True)
    acc_scr[...] += gate_e * y

    @pl.when(e == pl.num_programs(1) - 1)
    def _():
        z = ALPHA * x_ref[...] + g2_ref[...] * acc_scr[...]
        o_ref[...] = _layer_norm(z, lng_ref[...], lnb_ref[...])


def moe_ln(x, sc, sh, g2, gates, wg_bf, wu_bf, wd_bf, ln_g, ln_b, tm):
    N = x.shape[0]
    G, R, _ = sc.shape
    tpg = (N // G) // tm
    ms = pl.BlockSpec((None, R, D_MODEL), lambda i, e: (i // tpg, 0, 0))
    row = pl.BlockSpec((1, D_MODEL), lambda i, e: (0, 0))
    return pl.pallas_call(
        _moe_kernel,
        grid=(N // tm, N_EXPERTS),
        in_specs=[pl.BlockSpec((tm, D_MODEL), lambda i, e: (i, 0)), ms, ms, ms,
                  pl.BlockSpec((tm, LANES), lambda i, e: (i, 0)),
                  pl.BlockSpec((None, D_MODEL, D_EXPERT), lambda i, e: (e, 0, 0)),
                  pl.BlockSpec((None, D_MODEL, D_EXPERT), lambda i, e: (e, 0, 0)),
                  pl.BlockSpec((None, D_EXPERT, D_MODEL), lambda i, e: (e, 0, 0)),
                  row, row],
        out_specs=pl.BlockSpec((tm, D_MODEL), lambda i, e: (i, 0)),
        out_shape=jax.ShapeDtypeStruct((N, D_MODEL), F32),
        scratch_shapes=[pltpu.VMEM((tm, D_MODEL), BF16), pltpu.VMEM((tm, D_MODEL), F32)],
        compiler_params=_cparams(("parallel", "arbitrary")),
        name="moe_ln",
    )(x, sc, sh, g2, gates, wg_bf, wu_bf, wd_bf, ln_g.reshape(1, D_MODEL), ln_b.reshape(1, D_MODEL))


def _split3_bf16(x):
    hi = x.astype(BF16)
    r = x - hi.astype(F32)
    mid = r.astype(BF16)
    lo = (r - mid.astype(F32)).astype(BF16)
    return hi, mid, lo


def _moe_group_kernel(x_ref, sc_ref, sh_ref, g2_ref, gates_ref, ltri_ref, wg_ref, wu_ref, wd_ref,
                      lng_ref, lnb_ref, o_ref,
                      h_scr, acc_scr, rank_scr, rank_t_scr, sel_t_scr, xs_scr, gs_scr, yacc_scr, cnt_smem,
                      *, cap):
    g, e = pl.program_id(1), pl.program_id(2)
    tm = x_ref.shape[0]
    lane = lax.broadcasted_iota(I32, (tm, LANES), 1)

    @pl.when((g == 0) & (e == 0))
    def _():
        h_scr[...] = (x_ref[...] * (1.0 + sc_ref[...]) + sh_ref[...]).astype(BF16)
        acc_scr[...] = jnp.zeros_like(acc_scr)
        gates = gates_ref[...]
        sel = jnp.where((lane >= GROUP_LANE0) & (lane < GROUP_LANE0 + N_GROUPS), gates, 0.0)
        rank = jnp.dot(ltri_ref[...], sel.astype(BF16), preferred_element_type=F32)
        rank_scr[...] = rank
        rank_t_scr[...] = rank.T
        sel_t_scr[...] = sel.T
        for gg in range(N_GROUPS):
            cnt_smem[gg] = jnp.sum(jnp.where(lane == GROUP_LANE0 + gg, sel, 0.0)).astype(I32)

    n_chunks = (cnt_smem[g] + cap - 1) // cap
    grp_row = pl.ds(GROUP_LANE0 + g, 1)

    def chunk_rows(c):
        return pl.ds(pl.multiple_of(c * cap, cap), cap)

    @pl.when(e == 0)
    def _():
        g_hi, g_mid, g_lo = _split3_bf16(gates_ref[...])
        rank_row, sel_row = rank_t_scr[grp_row, :], sel_t_scr[grp_row, :]

        def dispatch(c, carry):
            slot = (c * cap + lax.broadcasted_iota(I32, (cap, 1), 0)).astype(F32)
            p = jnp.where((rank_row == slot) & (sel_row > 0.5), 1.0, 0.0).astype(BF16)
            xs_scr[chunk_rows(c), :] = jnp.dot(p, h_scr[...], preferred_element_type=F32).astype(BF16)
            gs_scr[chunk_rows(c), :] = (jnp.dot(p, g_hi, preferred_element_type=F32)
                                        + jnp.dot(p, g_mid, preferred_element_type=F32)
                                        + jnp.dot(p, g_lo, preferred_element_type=F32))
            yacc_scr[chunk_rows(c), :] = jnp.zeros((cap, D_MODEL), F32)
            return carry

        lax.fori_loop(0, n_chunks, dispatch, 0)

    def expert(c, carry):
        xs = xs_scr[chunk_rows(c), :]
        a = (_silu(jnp.dot(xs, wg_ref[...], preferred_element_type=F32))
             * jnp.dot(xs, wu_ref[...], preferred_element_type=F32))
        y = jnp.dot(a.astype(BF16), wd_ref[...], preferred_element_type=F32)
        gs = gs_scr[chunk_rows(c), :]
        lane_c = lax.broadcasted_iota(I32, gs.shape, 1)
        gate = jnp.sum(jnp.where(lane_c == g * EXPERTS_PER_GROUP + e, gs, 0.0), -1, keepdims=True)
        yacc_scr[chunk_rows(c), :] += gate * y
        return carry

    lax.fori_loop(0, n_chunks, expert, 0)

    @pl.when(e == EXPERTS_PER_GROUP - 1)
    def _():
        in_grp = lane == GROUP_LANE0 + g
        rank_col = jnp.sum(jnp.where(in_grp, rank_scr[...], 0.0), -1, keepdims=True)
        sel_col = jnp.sum(jnp.where(in_grp, gates_ref[...], 0.0), -1, keepdims=True)

        def combine(c, carry):
            slot = (c * cap + lax.broadcasted_iota(I32, (1, cap), 1)).astype(F32)
            p_t = jnp.where((rank_col == slot) & (sel_col > 0.5), 1.0, 0.0).astype(BF16)
            y_hi, y_mid, _ = _split3_bf16(yacc_scr[chunk_rows(c), :])
            acc_scr[...] += (jnp.dot(p_t, y_hi, preferred_element_type=F32)
                             + jnp.dot(p_t, y_mid, preferred_element_type=F32))
            return carry

        lax.fori_loop(0, n_chunks, combine, 0)

    @pl.when((g == N_GROUPS - 1) & (e == EXPERTS_PER_GROUP - 1))
    def _():
        z = ALPHA * x_ref[...] + g2_ref[...] * acc_scr[...]
        o_ref[...] = _layer_norm(z, lng_ref[...], lnb_ref[...])


def moe_group_ln(x, sc, sh, g2, gates, wg_bf, wu_bf, wd_bf, ln_g, ln_b, tm, cap):
    N = x.shape[0]
    G, R, _ = sc.shape
    tpg = (N // G) // tm
    ltri = (jnp.arange(tm)[:, None] > jnp.arange(tm)[None, :]).astype(BF16)
    ms = pl.BlockSpec((None, R, D_MODEL), lambda i, g, e: (i // tpg, 0, 0))
    row = pl.BlockSpec((1, D_MODEL), lambda i, g, e: (0, 0))
    w_in = pl.BlockSpec((None, D_MODEL, D_EXPERT), lambda i, g, e: (g * EXPERTS_PER_GROUP + e, 0, 0))
    w_out = pl.BlockSpec((None, D_EXPERT, D_MODEL), lambda i, g, e: (g * EXPERTS_PER_GROUP + e, 0, 0))
    return pl.pallas_call(
        functools.partial(_moe_group_kernel, cap=cap),
        grid=(N // tm, N_GROUPS, EXPERTS_PER_GROUP),
        in_specs=[pl.BlockSpec((tm, D_MODEL), lambda i, g, e: (i, 0)), ms, ms, ms,
                  pl.BlockSpec((tm, LANES), lambda i, g, e: (i, 0)),
                  pl.BlockSpec((tm, tm), lambda i, g, e: (0, 0)),
                  w_in, w_in, w_out, row, row],
        out_specs=pl.BlockSpec((tm, D_MODEL), lambda i, g, e: (i, 0)),
        out_shape=jax.ShapeDtypeStruct((N, D_MODEL), F32),
        scratch_shapes=[pltpu.VMEM((tm, D_MODEL), BF16), pltpu.VMEM((tm, D_MODEL), F32),
                        pltpu.VMEM((tm, LANES), F32), pltpu.VMEM((LANES, tm), F32),
                        pltpu.VMEM((LANES, tm), F32), pltpu.VMEM((tm, D_MODEL), BF16),
                        pltpu.VMEM((tm, LANES), F32), pltpu.VMEM((tm, D_MODEL), F32),
                        pltpu.SMEM((N_GROUPS,), I32)],
        compiler_params=_cparams(("arbitrary", "arbitrary", "arbitrary")),
        name="moe_group_ln",
    )(x, sc, sh, g2, gates, ltri, wg_bf, wu_bf, wd_bf, ln_g.reshape(1, D_MODEL), ln_b.reshape(1, D_MODEL))


LOG2_E = 1.4426950408889634
Q_PRESCALE = DSA_HEAD_DIM ** -0.5 * LOG2_E
DSA_IN_PAD = -(-DSA_IN // LANES) * LANES
DSA_QI_OFF = DSA_Q + 2 * DSA_KV
DSA_KW_OFF = DSA_QI_OFF + IDX_HEADS * IDX_DIM


def _rope_tables(pos, rot_dim, period, n_periods):
    half = rot_dim // 2
    inv_freq = 1.0 / (ROPE_THETA ** (jnp.arange(0, rot_dim, 2, dtype=F32) / rot_dim))
    ang = pos.astype(F32)[:, None] * inv_freq[None, :]
    cos, sin = jnp.cos(ang), jnp.sin(ang)
    T = pos.shape[0]
    zeros = jnp.zeros((T, period - 2 * half), F32)
    zh = jnp.zeros((T, half), F32)
    a1 = jnp.concatenate([cos, cos, zeros + 1.0], 1)
    b1 = jnp.concatenate([zh, sin, zeros], 1)
    c1 = jnp.concatenate([-sin, zh, zeros], 1)
    rest = LANES - n_periods * period
    pad1 = jnp.ones((T, rest), F32)
    pad0 = jnp.zeros((T, rest), F32)
    a = jnp.concatenate([a1] * n_periods + [pad1], 1)
    b = jnp.concatenate([b1] * n_periods + [pad0], 1)
    c = jnp.concatenate([c1] * n_periods + [pad0], 1)
    return a, b, c


def _apply_rope(x, a, b, c, half):
    return x * a + _lane_roll(x, half) * b + _lane_roll(x, LANES - half) * c


N_ROPE_TABLES = 9


def _dsa_sections(x_ref, sc_ref, sh_ref, w_ref, tab_refs):
    hb = (x_ref[...] * (1.0 + sc_ref[...]) + sh_ref[...]).astype(BF16)
    ma, mb, mc, ia, ib, ic, ka, kb, kc = (t[...] for t in tab_refs)

    def sec(off, width):
        return jnp.dot(hb, w_ref[:, off:off + width], preferred_element_type=F32)

    def lanes(x, h):
        return x[:, h * LANES:(h + 1) * LANES]

    q = sec(0, DSA_Q)
    q_heads = [_apply_rope(lanes(q, h), ma, mb, mc, ROT_DIM // 2) * Q_PRESCALE for h in range(DSA_HEADS)]
    k = sec(DSA_Q, DSA_KV)
    k = jnp.concatenate([_apply_rope(lanes(k, h), ma, mb, mc, ROT_DIM // 2) for h in range(DSA_KV_HEADS)], 1)
    v = sec(DSA_Q + DSA_KV, DSA_KV)
    qi = sec(DSA_QI_OFF, IDX_HEADS * IDX_DIM)
    qi_pairs = [_apply_rope(lanes(qi, h), ia, ib, ic, IDX_ROT_DIM // 2)
                for h in range(IDX_HEADS * IDX_DIM // LANES)]
    kw = _apply_rope(sec(DSA_KW_OFF, LANES), ka, kb, kc, IDX_ROT_DIM // 2)
    return q_heads, k, v, qi_pairs, kw


def _dsa_proj_kernel(x_ref, sc_ref, sh_ref, w_ref, *rest):
    tabs = rest[:N_ROPE_TABLES]
    q_ref, k_ref, v_ref, qi_ref, kw_ref = rest[N_ROPE_TABLES:]
    q_heads, k, v, qi_pairs, kw = _dsa_sections(x_ref, sc_ref, sh_ref, w_ref, tabs)
    for h, qh in enumerate(q_heads):
        q_ref[:, h * LANES:(h + 1) * LANES] = qh.astype(q_ref.dtype)
    for h, qp in enumerate(qi_pairs):
        qi_ref[:, h * LANES:(h + 1) * LANES] = qp.astype(qi_ref.dtype)
    k_ref[...] = k
    v_ref[...] = v
    kw_ref[...] = kw


def _dsa_proj_prompt_kernel(x_ref, sc_ref, sh_ref, w_ref, *rest, tq):
    tabs = rest[:N_ROPE_TABLES]
    k_ref, v_ref, kw_ref, kbf_ref, kwbf_ref, vt_ref, qt_ref, qit_ref, wit_ref = rest[N_ROPE_TABLES:]
    q_heads, k, v, qi_pairs, kw = _dsa_sections(x_ref, sc_ref, sh_ref, w_ref, tabs)
    tm = k.shape[0]
    k_ref[...] = k
    kbf_ref[...] = k.astype(BF16)
    v_ref[...] = v
    kw_ref[...] = kw
    kwbf_ref[...] = kw.astype(BF16)
    for n in range(DSA_KV_HEADS):
        vt_ref[n, 0] = v[:, n * DSA_HEAD_DIM:(n + 1) * DSA_HEAD_DIM].T.astype(BF16)
    heads_per_pair = LANES // IDX_DIM
    for j in range(tm // tq):
        rows = slice(j * tq, (j + 1) * tq)
        for h, qh in enumerate(q_heads):
            n, g = divmod(h, DSA_GROUP)
            qt_ref[j, n, :, g * tq:(g + 1) * tq] = qh[rows].T.astype(BF16)
        for hp, qp in enumerate(qi_pairs):
            t = qp[rows].T.astype(BF16)
            for s in range(heads_per_pair):
                h = hp * heads_per_pair + s
                qit_ref[j, :IDX_DIM, h * tq:(h + 1) * tq] = t[s * IDX_DIM:(s + 1) * IDX_DIM]
        qit_ref[j, IDX_DIM:, :] = jnp.zeros((LANES - IDX_DIM, IDX_HEADS * tq), BF16)
        wit_ref[j] = kw[rows].T[IDX_DIM:IDX_DIM + IDX_HEADS]


def _dsa_project_call(body, x, sc, sh, w_pad_bf, pos, tm, out_specs, out_shape, name):
    N = x.shape[0]
    G, R, _ = sc.shape
    tpg = (N // G) // tm
    n_tab = pos.shape[0] // tm
    tabs = (_rope_tables(pos, ROT_DIM, DSA_HEAD_DIM, 1)
            + _rope_tables(pos, IDX_ROT_DIM, IDX_DIM, 2)
            + _rope_tables(pos, IDX_ROT_DIM, IDX_DIM, 1))
    tab = pl.BlockSpec((tm, LANES), lambda i: (i % n_tab, 0))
    return pl.pallas_call(
        body,
        grid=(N // tm,),
        in_specs=[pl.BlockSpec((tm, D_MODEL), lambda i: (i, 0)), _mod_spec(R, tpg), _mod_spec(R, tpg),
                  pl.BlockSpec((D_MODEL, DSA_IN_PAD), lambda i: (0, 0))] + [tab] * N_ROPE_TABLES,
        out_specs=out_specs, out_shape=out_shape,
        compiler_params=_cparams(("parallel",)),
        name=name,
    )(x, sc, sh, w_pad_bf, *tabs)


def dsa_project(x, sc, sh, w_pad_bf, pos, tm):
    N = x.shape[0]
    out = lambda w: pl.BlockSpec((tm, w), lambda i: (i, 0))
    return _dsa_project_call(
        _dsa_proj_kernel, x, sc, sh, w_pad_bf, pos, tm,
        [out(DSA_Q), out(DSA_KV), out(DSA_KV), out(IDX_HEADS * IDX_DIM), out(LANES)],
        [jax.ShapeDtypeStruct((N, DSA_Q), BF16), jax.ShapeDtypeStruct((N, DSA_KV), F32),
         jax.ShapeDtypeStruct((N, DSA_KV), F32), jax.ShapeDtypeStruct((N, IDX_HEADS * IDX_DIM), BF16),
         jax.ShapeDtypeStruct((N, LANES), F32)], "dsa_project")


def dsa_project_prompt(x, sc, sh, w_pad_bf, pos, B, T, tm, tq):
    N = B * T
    tiles, n_qt = T // tm, tm // tq
    gq = DSA_GROUP * tq
    out = lambda w: pl.BlockSpec((tm, w), lambda i: (i, 0))
    by_tile = lambda *shape: pl.BlockSpec((None,) + shape,
                                          lambda i: (i // tiles, i % tiles) + (0,) * (len(shape) - 1))
    return _dsa_project_call(
        functools.partial(_dsa_proj_prompt_kernel, tq=tq), x, sc, sh, w_pad_bf, pos, tm,
        [out(DSA_KV), out(DSA_KV), out(LANES), out(DSA_KV), out(LANES),
         pl.BlockSpec((None, DSA_KV_HEADS, 1, DSA_HEAD_DIM, tm), lambda i: (i // tiles, 0, i % tiles, 0, 0)),
         by_tile(n_qt, DSA_KV_HEADS, DSA_HEAD_DIM, gq),
         by_tile(n_qt, LANES, IDX_HEADS * tq),
         by_tile(n_qt, IDX_HEADS, tq)],
        [jax.ShapeDtypeStruct((N, DSA_KV), F32), jax.ShapeDtypeStruct((N, DSA_KV), F32),
         jax.ShapeDtypeStruct((N, LANES), F32), jax.ShapeDtypeStruct((N, DSA_KV), BF16),
         jax.ShapeDtypeStruct((N, LANES), BF16),
         jax.ShapeDtypeStruct((B, DSA_KV_HEADS, tiles, DSA_HEAD_DIM, tm), BF16),
         jax.ShapeDtypeStruct((B, T // tq, DSA_KV_HEADS, DSA_HEAD_DIM, gq), BF16),
         jax.ShapeDtypeStruct((B, T // tq, LANES, IDX_HEADS * tq), BF16),
         jax.ShapeDtypeStruct((B, T // tq, IDX_HEADS, tq), F32)], "dsa_project_prompt")


def _sortable_key(score):
    b = pltpu.bitcast(jnp.where(score == 0.0, 0.0, score), I32)
    return b ^ ((b >> 31) & jnp.int32(0x7FFFFFFF))


KEY_NEG_INF = -2139095041


def _reduce_row_groups(x, op, n_chains=8):
    parts = [x[r:r + SUBLANES] for r in range(0, x.shape[0], SUBLANES)]
    accs = parts[:n_chains]
    for j, part in enumerate(parts[n_chains:]):
        accs[j % len(accs)] = op(accs[j % len(accs)], part)
    while len(accs) > 1:
        accs = [op(a, b) for a, b in zip(accs[0::2], accs[1::2])] + (accs[-1:] if len(accs) % 2 else [])
    return accs[0]


def _sum_row_groups(x):
    return _reduce_row_groups(x, jnp.add)


def _max_row_groups(x):
    return _reduce_row_groups(x, jnp.maximum)


def _kth_largest_key(count_ge, shape, k):
    def body(s, p):
        cand = p + lax.shift_left(jnp.int32(1), 31 - s)
        return jnp.where(count_ge(cand) >= k, cand, p)
    return lax.fori_loop(0, 32, body, jnp.full(shape, INT_MIN, I32))


def _tie_index_bound(count_eq_below, shape, need, n_bits):
    def body(s, m):
        cand = m + lax.shift_left(jnp.int32(1), n_bits - 1 - s)
        return jnp.where(count_eq_below(cand) < need, cand, m)
    return lax.fori_loop(0, n_bits, body, jnp.zeros(shape, I32))


def _dsa_prompt_kernel(ki_ref, qit_ref, wit_ref, k_ref, vt_ref, qt_ref, o_ref,
                       keys_scr, bias_scr, midx_scr, logit_scr, *, tq, kb_size, cb_size, topk, idx_bits):
    i = pl.program_id(1)
    n_kb = ((i + 1) * tq + kb_size - 1) // kb_size
    n_cb = ((i + 1) * tq + cb_size - 1) // cb_size
    q_pos = i * tq + lax.broadcasted_iota(I32, (1, tq), 1)
    heads_per_dot = 2

    @pl.when(i == 0)
    def _():
        keys_scr[...] = jnp.full(keys_scr.shape, KEY_NEG_INF, I32)

    def key_rows(kb):
        return pl.ds(pl.multiple_of(kb * kb_size, kb_size), kb_size)

    def l_index(kb, size=kb_size):
        return kb * size + lax.broadcasted_iota(I32, (size, 1), 0)

    def score_body(kb, carry):
        kib = ki_ref[key_rows(kb), :]
        acc = jnp.zeros((kb_size, tq), F32)
        for hp in range(IDX_HEADS // heads_per_dot):
            s = jnp.dot(kib, qit_ref[:, hp * heads_per_dot * tq:(hp + 1) * heads_per_dot * tq],
                        preferred_element_type=F32)
            for j in range(heads_per_dot):
                h = hp * heads_per_dot + j
                w = wit_ref[h:h + 1, :] * (IDX_HEADS ** -0.5 * IDX_DIM ** -0.5)
                acc = acc + jnp.maximum(s[:, j * tq:(j + 1) * tq], 0.0) * w
        allowed = l_index(kb) <= q_pos
        keys_scr[key_rows(kb), :] = _sortable_key(jnp.where(allowed, acc, -jnp.inf))
        return carry

    lax.fori_loop(0, n_kb, score_body, 0)

    def count(pred_fn):
        def body(cb, acc):
            rows = pl.ds(pl.multiple_of(cb * cb_size, cb_size), cb_size)
            m = jnp.where(pred_fn(keys_scr[rows, :], cb), 1.0, 0.0)
            return acc + _sum_row_groups(m)
        acc = lax.fori_loop(0, n_cb, body, jnp.zeros((SUBLANES, tq), F32))
        return acc.sum(axis=0, keepdims=True)

    thr = _kth_largest_key(lambda c: count(lambda key, cb: key >= c), (1, tq), float(topk))
    n_gt = count(lambda key, cb: key > thr)
    n_ge = count(lambda key, cb: key >= thr)
    need = float(topk) - n_gt
    excess = (n_ge - n_gt > need) & (thr != KEY_NEG_INF)
    midx_scr[...] = jnp.full((1, tq), 2 ** idx_bits, I32)

    @pl.when(jnp.max(jnp.where(excess, 1.0, 0.0)) > 0.5)
    def _():
        m = _tie_index_bound(
            lambda c: count(lambda key, cb: (key == thr) & (l_index(cb, cb_size) < c)),
            (1, tq), need, idx_bits)
        midx_scr[...] = jnp.where(excess, m, 2 ** idx_bits)

    midx = midx_scr[...]

    def bias_body(kb, carry):
        key = keys_scr[key_rows(kb), :]
        l = l_index(kb)
        sel = ((key > thr) | ((key == thr) & (l <= midx))) & (l <= q_pos)
        bias_scr[key_rows(kb), :] = jnp.where(sel, 0.0, NEG_BIG)
        return carry

    lax.fori_loop(0, n_kb, bias_body, 0)

    gq = DSA_GROUP * tq
    heads = range(DSA_KV_HEADS)

    def logit_body(kb, m_run):
        bias = bias_scr[key_rows(kb), :]
        bias = jnp.concatenate([bias] * DSA_GROUP, axis=1)
        new = []
        for n in heads:
            kblk = k_ref[key_rows(kb), n * DSA_HEAD_DIM:(n + 1) * DSA_HEAD_DIM]
            logits = jnp.dot(kblk, qt_ref[n], preferred_element_type=F32) + bias
            logit_scr[n, key_rows(kb), :] = logits
            new.append(jnp.maximum(m_run[n], _max_row_groups(logits)))
        return tuple(new)

    m8 = lax.fori_loop(0, n_kb, logit_body,
                       tuple(jnp.full((SUBLANES, gq), NEG_BIG, F32) for _ in heads))
    m_fin = [jnp.max(m, axis=0, keepdims=True) for m in m8]

    def pv_body(kb, carry):
        new = []
        for n in heads:
            l_run, acc = carry[n]
            p = jnp.exp2(logit_scr[n, key_rows(kb), :] - m_fin[n])
            acc = acc + jnp.dot(vt_ref[n, kb], p.astype(BF16), preferred_element_type=F32)
            new.append((l_run + _sum_row_groups(p), acc))
        return tuple(new)

    fin = lax.fori_loop(0, n_kb, pv_body,
                        tuple((jnp.zeros((SUBLANES, gq), F32), jnp.zeros((DSA_HEAD_DIM, gq), F32))
                              for _ in heads))
    for n in heads:
        l8, acc = fin[n]
        o_t = acc / jnp.sum(l8, axis=0, keepdims=True)
        for g in range(DSA_GROUP):
            h = n * DSA_GROUP + g
            o_ref[:, h * DSA_HEAD_DIM:(h + 1) * DSA_HEAD_DIM] = (
                o_t[:, g * tq:(g + 1) * tq].T.astype(o_ref.dtype))


def dsa_prompt_attend(kw_bf, qi_t, wi_t, k_bf, v_t, q_t, B, T, topk, tq, kb_size):
    nqt = T // tq
    nkb = T // kb_size
    gq = DSA_GROUP * tq
    return pl.pallas_call(
        functools.partial(_dsa_prompt_kernel, tq=tq, kb_size=kb_size, cb_size=min(1024, T), topk=topk,
                          idx_bits=max(1, (T - 1).bit_length())),
        grid=(B, nqt),
        in_specs=[pl.BlockSpec((None, T, LANES), lambda b, i: (b, 0, 0)),
                  pl.BlockSpec((None, None, LANES, IDX_HEADS * tq), lambda b, i: (b, i, 0, 0)),
                  pl.BlockSpec((None, None, IDX_HEADS, tq), lambda b, i: (b, i, 0, 0)),
                  pl.BlockSpec((None, T, DSA_KV), lambda b, i: (b, 0, 0)),
                  pl.BlockSpec((None, DSA_KV_HEADS, nkb, DSA_HEAD_DIM, kb_size),
                               lambda b, i: (b, 0, 0, 0, 0)),
                  pl.BlockSpec((None, None, DSA_KV_HEADS, DSA_HEAD_DIM, gq),
                               lambda b, i: (b, i, 0, 0, 0))],
        out_specs=pl.BlockSpec((tq, DSA_Q), lambda b, i: (b * nqt + i, 0)),
        out_shape=jax.ShapeDtypeStruct((B * T, DSA_Q), BF16),
        scratch_shapes=[pltpu.VMEM((T, tq), I32), pltpu.VMEM((T, tq), F32), pltpu.VMEM((1, tq), I32),
                        pltpu.VMEM((DSA_KV_HEADS, T, gq), F32)],
        compiler_params=_cparams(("arbitrary", "arbitrary")),
        name="dsa_prompt_attend",
    )(kw_bf.reshape(B, T, LANES), qi_t, wi_t, k_bf.reshape(B, T, DSA_KV), v_t, q_t)


def _dsa_sample_score_kernel(pt_ref, qi_ref, wi_ref, kis_ref, *rest, n_pages, page):
    pages, o_ref = rest[:n_pages], rest[n_pages]
    qi = qi_ref[...]
    w = wi_ref[...] * (IDX_HEADS ** -0.5 * IDX_DIM ** -0.5)
    for j in range(n_pages):
        dots = lax.dot_general(qi, pages[j][...].astype(BF16), (((1,), (1,)), ((), ())),
                               preferred_element_type=F32)
        o_ref[:, j * page:(j + 1) * page] = jnp.sum(jnp.maximum(dots, 0.0) * w, axis=0, keepdims=True)
    ki_self = kis_ref[...].astype(BF16).astype(F32)
    d_self = jnp.sum(qi.astype(F32) * ki_self, -1, keepdims=True)
    s_self = jnp.sum(jnp.maximum(d_self, 0.0) * w, axis=0, keepdims=True)
    lane = lax.broadcasted_iota(I32, (1, LANES), 1)
    o_ref[:, n_pages * page:] = jnp.where(lane == 0, s_self, -jnp.inf)


def _page_specs(n_pages, page, width):
    return [pl.BlockSpec((None, page, width), functools.partial(lambda j, b, pt: (pt[b, j], 0, 0), j))
            for j in range(n_pages)]


def dsa_sample_scores(page_table, qi_bf, wi, ki_self, cache_kidx):
    DB, n_pages = page_table.shape
    page = cache_kidx.shape[1]
    lp = n_pages * page + LANES
    per_b = lambda *shape: pl.BlockSpec((None,) + shape, lambda b, pt: (b,) + (0,) * len(shape))
    return pl.pallas_call(
        functools.partial(_dsa_sample_score_kernel, n_pages=n_pages, page=page),
        grid_spec=pltpu.PrefetchScalarGridSpec(
            num_scalar_prefetch=1, grid=(DB,),
            in_specs=[per_b(IDX_HEADS, IDX_DIM), per_b(IDX_HEADS, 1), per_b(1, IDX_DIM)]
            + _page_specs(n_pages, page, IDX_DIM),
            out_specs=per_b(1, lp)),
        out_shape=jax.ShapeDtypeStruct((DB, 1, lp), F32),
        compiler_params=_cparams(("arbitrary",)),
        name="dsa_sample_scores",
    )(page_table, qi_bf.reshape(DB, IDX_HEADS, IDX_DIM), wi.reshape(DB, IDX_HEADS, 1),
      ki_self.reshape(DB, 1, IDX_DIM), *([cache_kidx] * n_pages))


def _dsa_sample_select_kernel(s_ref, o_ref, *, n_keys, topk, idx_bits):
    score = s_ref[...]
    lane = lax.broadcasted_iota(I32, score.shape, 1)
    valid = lane < n_keys
    key = _sortable_key(jnp.where(valid, score, -jnp.inf))

    def count(pred):
        return jnp.sum(jnp.where(pred, 1.0, 0.0), -1, keepdims=True)

    shape = (score.shape[0], 1)
    thr = _kth_largest_key(lambda c: count(key >= c), shape, float(topk))
    n_gt = count(key > thr)
    need = float(topk) - n_gt
    excess = (count(key == thr) > need) & (thr != KEY_NEG_INF)
    m = _tie_index_bound(lambda c: count((key == thr) & (lane < c)), shape, need, idx_bits)
    midx = jnp.where(excess, m, 2 ** idx_bits)
    sel = ((key > thr) | ((key == thr) & (lane <= midx))) & valid
    o_ref[...] = jnp.where(sel, 0.0, NEG_BIG)


def dsa_sample_select(score, n_keys, topk):
    DB, lp = score.shape
    return pl.pallas_call(
        functools.partial(_dsa_sample_select_kernel, n_keys=n_keys, topk=topk,
                          idx_bits=max(1, (lp - 1).bit_length())),
        grid=(1,),
        in_specs=[pl.BlockSpec((DB, lp), lambda i: (0, 0))],
        out_specs=pl.BlockSpec((DB, lp), lambda i: (0, 0)),
        out_shape=jax.ShapeDtypeStruct((DB, lp), F32),
        compiler_params=_cparams(("arbitrary",)),
        name="dsa_sample_select",
    )(score)


def _dsa_sample_attend_kernel(pt_ref, q_ref, bias_ref, ks_ref, vs_ref, *rest, n_pages, page):
    k_pages, v_pages = rest[:n_pages], rest[n_pages:2 * n_pages]
    o_ref, logit_scr = rest[2 * n_pages], rest[2 * n_pages + 1]
    q = q_ref[...]
    tail = n_pages * page
    lane = lax.broadcasted_iota(I32, (DSA_GROUP, LANES), 1)
    for n in range(DSA_KV_HEADS):
        hs = slice(n * DSA_GROUP, (n + 1) * DSA_GROUP)
        ds = slice(n * DSA_HEAD_DIM, (n + 1) * DSA_HEAD_DIM)
        head_rows = pl.ds(n, page, stride=DSA_KV_HEADS)
        qn = q[hs, :]
        for j in range(n_pages):
            lg = lax.dot_general(qn, k_pages[j][head_rows, :].astype(BF16), (((1,), (1,)), ((), ())),
                                 preferred_element_type=F32)
            logit_scr[hs, j * page:(j + 1) * page] = lg + bias_ref[:, j * page:(j + 1) * page]
        k_self = ks_ref[:, ds].astype(BF16).astype(F32)
        lg_self = jnp.sum(qn.astype(F32) * k_self, -1, keepdims=True)
        logit_scr[hs, tail:] = jnp.where(lane == 0, lg_self, 0.0) + bias_ref[:, tail:]
        logits = logit_scr[hs, :]
        m = jnp.max(logits, -1, keepdims=True)
        p = jnp.exp2(logits - m)
        p_bf = (p / jnp.sum(p, -1, keepdims=True)).astype(BF16)
        v_self = vs_ref[:, ds].astype(BF16).astype(F32)
        acc = p_bf[:, tail:].astype(F32)[:, :1] * v_self
        for j in range(n_pages):
            acc = acc + jnp.dot(p_bf[:, j * page:(j + 1) * page], v_pages[j][head_rows, :].astype(BF16),
                                preferred_element_type=F32)
        o_ref[hs, :] = acc


def dsa_sample_attend(page_table, q_bf, bias, k_self, v_self, cache_k, cache_v):
    DB, n_pages = page_table.shape
    rows = cache_k.shape[1]
    page = rows // DSA_KV_HEADS
    lp = n_pages * page + LANES
    per_b = lambda *shape: pl.BlockSpec((None,) + shape, lambda b, pt: (b,) + (0,) * len(shape))
    o = pl.pallas_call(
        functools.partial(_dsa_sample_attend_kernel, n_pages=n_pages, page=page),
        grid_spec=pltpu.PrefetchScalarGridSpec(
            num_scalar_prefetch=1, grid=(DB,),
            in_specs=[per_b(DSA_HEADS, DSA_HEAD_DIM), per_b(1, lp), per_b(1, DSA_KV), per_b(1, DSA_KV)]
            + _page_specs(n_pages, rows, DSA_HEAD_DIM) + _page_specs(n_pages, rows, DSA_HEAD_DIM),
            out_specs=per_b(DSA_HEADS, DSA_HEAD_DIM),
            scratch_shapes=[pltpu.VMEM((DSA_HEADS, lp), F32)]),
        out_shape=jax.ShapeDtypeStruct((DB, DSA_HEADS, DSA_HEAD_DIM), F32),
        compiler_params=_cparams(("arbitrary",)),
        name="dsa_sample_attend",
    )(page_table, q_bf.reshape(DB, DSA_HEADS, DSA_HEAD_DIM), bias.reshape(DB, 1, lp),
      k_self.reshape(DB, 1, DSA_KV), v_self.reshape(DB, 1, DSA_KV),
      *([cache_k] * n_pages), *([cache_v] * n_pages))
    return o.reshape(DB, DSA_Q)


def _retnet_tables(pos):
    inv_freq = jnp.power(RET_ANGLE_BASE, -jnp.linspace(0.0, 1.0, RET_DK // 2, dtype=F32))
    ang = pos.astype(F32)[:, None] * inv_freq[None, :]
    return jnp.cos(ang), jnp.sin(ang)


def _split_mod(mod_l, G, R):
    return [m.reshape(G, R, D_MODEL) for m in jnp.split(mod_l, 6, axis=-1)]


MOE_GROUP_TILE = 1024
MOE_GROUP_CAP = 256


def _channel_mixer(x, sc2, sh2, g2, l, wts, tm):
    gates = router(x, sc2, sh2, wts["w_router"], wts["b_router"], tm)
    args = (x, sc2, sh2, g2, gates, wts["w_gate"][l], wts["w_up"][l], wts["w_down"][l],
            wts["ln2_g"][l], wts["ln2_b"][l])
    rows_per_mod = x.shape[0] // sc2.shape[0]
    if sc2.shape[1] == 1 and rows_per_mod >= 2 * MOE_GROUP_CAP:
        tile = min(MOE_GROUP_TILE, rows_per_mod)
        return moe_group_ln(*args, tile, MOE_GROUP_CAP)
    return moe_ln(*args, tm)


def kernel(x_prompt, x_sample, state_ret, cache_k, cache_v, cache_kidx, page_table,
           c_prompt, c_sample, w_mod, b_mod, ln1_g, ln1_b, ln2_g, ln2_b,
           w_in_ret, gn_ret_g, w_out_ret, w_in_dsa, w_out_dsa,
           w_router, b_router, w_gate, w_up, w_down):
    B, T, _ = x_prompt.shape
    DB = x_sample.shape[0]
    assert x_sample.shape[1] == 1
    n_pages = page_table.shape[1]
    page = cache_k.shape[2]
    past = n_pages * page
    n_pool = cache_k.shape[1]

    wts = dict(w_router=w_router, b_router=b_router, ln2_g=ln2_g, ln2_b=ln2_b,
               w_gate=w_gate.astype(BF16), w_up=w_up.astype(BF16), w_down=w_down.astype(BF16))
    w_mod_bf = w_mod.astype(BF16)
    w_in_ret_bf = w_in_ret[0].astype(BF16)
    w_out_ret_bf = w_out_ret[0].astype(BF16)
    w_in_dsa_bf = jnp.pad(w_in_dsa[0], ((0, 0), (0, DSA_IN_PAD - DSA_IN))).astype(BF16)
    w_out_dsa_bf = w_out_dsa[0].astype(BF16)

    pos_p = jnp.arange(T, dtype=I32)
    pos_s = jnp.full((1,), past, I32)

    tm_p = min(512, T)
    tq = min(128, T)
    xp = x_prompt.reshape(B * T, D_MODEL)
    mod_p = modulation(c_prompt, w_mod_bf, b_mod)
    sh1, sc1, g1, sh2, sc2, g2 = _split_mod(mod_p[0], B, 1)
    proj = mod_proj(xp, sc1, sh1, w_in_ret_bf, min(1024, T), 1536, BF16)
    cos_p, sin_p = _retnet_tables(pos_p)
    gated, ret_p = retention_prompt(proj.reshape(B, T, RET_IN), gn_ret_g[0], cos_p, sin_p, min(256, T))
    xp = out_proj_ln(gated.reshape(B * T, RET_V), w_out_ret_bf, xp, g1, ln1_g[0], ln1_b[0], tm_p)
    xp = _channel_mixer(xp, sc2, sh2, g2, 0, wts, tm_p)

    sh1, sc1, g1, sh2, sc2, g2 = _split_mod(mod_p[1], B, 1)
    k_p, v_p, kw_p, k_bf, kw_bf, v_t, q_t, qi_t, wi_t = dsa_project_prompt(
        xp, sc1, sh1, w_in_dsa_bf, pos_p, B, T, tm_p, tq)
    o_p = dsa_prompt_attend(kw_bf, qi_t, wi_t, k_bf, v_t, q_t, B, T, min(DSA_TOPK, T // 4), tq, tm_p)
    xp = out_proj_ln(o_p, w_out_dsa_bf, xp, g1, ln1_g[1], ln1_b[1], tm_p)
    xp = _channel_mixer(xp, sc2, sh2, g2, 1, wts, tm_p)

    xs = x_sample.reshape(DB, D_MODEL)
    mod_s = modulation(c_sample, w_mod_bf, b_mod)
    sh1, sc1, g1, sh2, sc2, g2 = _split_mod(mod_s[0], 1, DB)
    proj_s = mod_proj(xs, sc1, sh1, w_in_ret_bf, DB, 1536, F32)
    cos_s, sin_s = _retnet_tables(pos_s)
    gated_s, ret_s = retention_sample(proj_s, state_ret[0], gn_ret_g[0], cos_s, sin_s)
    xs = out_proj_ln(gated_s.reshape(DB, RET_V), w_out_ret_bf, xs, g1, ln1_g[0], ln1_b[0], DB)
    xs = _channel_mixer(xs, sc2, sh2, g2, 0, wts, DB)

    sh1, sc1, g1, sh2, sc2, g2 = _split_mod(mod_s[1], 1, DB)
    pos_rows = jnp.full((DB,), past, I32)
    q_s, k_s, v_s, qi_s, kw_s = dsa_project(xs, sc1, sh1, w_in_dsa_bf, pos_rows, DB)
    ki_s = kw_s[:, :IDX_DIM]
    score = dsa_sample_scores(page_table, qi_s, kw_s[:, IDX_DIM:IDX_DIM + IDX_HEADS], ki_s,
                              cache_kidx[0])
    bias = dsa_sample_select(score.reshape(DB, past + LANES), past + 1, min(DSA_TOPK, (past + 1) // 4))
    o_s = dsa_sample_attend(page_table, q_s, bias, k_s, v_s,
                            cache_k[0].reshape(n_pool, page * DSA_KV_HEADS, DSA_HEAD_DIM),
                            cache_v[0].reshape(n_pool, page * DSA_KV_HEADS, DSA_HEAD_DIM))
    xs = out_proj_ln(o_s, w_out_dsa_bf, xs, g1, ln1_g[1], ln1_b[1], DB)
    xs = _channel_mixer(xs, sc2, sh2, g2, 1, wts, DB)

    kv_shape = (DSA_KV_HEADS, DSA_HEAD_DIM)
    return (xp.reshape(B, T, D_MODEL), xs.reshape(DB, 1, D_MODEL),
            ret_p[None], ret_s[None],
            k_p.reshape(1, B, T, *kv_shape), v_p.reshape(1, B, T, *kv_shape),
            kw_p[:, :IDX_DIM].reshape(1, B, T, IDX_DIM),
            k_s.reshape(1, DB, 1, *kv_shape), v_s.reshape(1, DB, 1, *kv_shape),
            ki_s.reshape(1, DB, 1, IDX_DIM))
```

```python
import functools

import jax
import jax.numpy as jnp
from jax import lax
from jax.experimental import pallas as pl
from jax.experimental.pallas import tpu as pltpu

F32 = jnp.float32
BF16 = jnp.bfloat16
I32 = jnp.int32

D_MODEL = 1024
DEPTH = 2
ALPHA = (2.0 * DEPTH) ** 0.25
LN_EPS = 1e-5
GN_EPS = 1e-6

RET_HEADS = 4
RET_DK = 256
RET_DV = 512
RET_QK = RET_HEADS * RET_DK
RET_V = RET_HEADS * RET_DV
RET_IN = 2 * RET_QK + 2 * RET_V
RET_ANGLE_BASE = 10000.0

DSA_HEADS = 8
DSA_KV_HEADS = 2
DSA_HEAD_DIM = 128
DSA_GROUP = DSA_HEADS // DSA_KV_HEADS
DSA_Q = DSA_HEADS * DSA_HEAD_DIM
DSA_KV = DSA_KV_HEADS * DSA_HEAD_DIM
IDX_HEADS = 16
IDX_DIM = 64
DSA_TOPK = 256
ROPE_THETA = 500000.0
ROT_DIM = DSA_HEAD_DIM // 4
IDX_ROT_DIM = IDX_DIM // 4
DSA_IN = DSA_Q + 2 * DSA_KV + IDX_HEADS * IDX_DIM + IDX_DIM + IDX_HEADS

N_EXPERTS = 16
N_GROUPS = 4
EXPERTS_PER_GROUP = N_EXPERTS // N_GROUPS
D_EXPERT = 512
GROUP_LANE0 = N_EXPERTS

LANES = 128
SUBLANES = 8
VMEM_LIMIT = 56 * 1024 * 1024
NEG_BIG = -1e30
INT_MIN = -2147483648


def _cparams(sem):
    return pltpu.CompilerParams(dimension_semantics=sem, vmem_limit_bytes=VMEM_LIMIT)


def _silu(x):
    return x * (1.0 / (1.0 + jnp.exp(-x)))


def _layer_norm(z, g, b):
    mu = jnp.mean(z, -1, keepdims=True)
    d = z - mu
    var = jnp.mean(d * d, -1, keepdims=True)
    return d * lax.rsqrt(var + LN_EPS) * g + b


def _mod_kernel(c_ref, w_ref, b_ref, o_ref):
    a = _silu(c_ref[...]).astype(BF16)
    o_ref[...] = jnp.dot(a, w_ref[...], preferred_element_type=F32) + b_ref[...]


def modulation(c, w_mod_bf, b_mod):
    R = c.shape[0]
    tn = 1536
    return pl.pallas_call(
        _mod_kernel,
        grid=(DEPTH, 6 * D_MODEL // tn),
        in_specs=[pl.BlockSpec((R, D_MODEL), lambda l, j: (0, 0)),
                  pl.BlockSpec((None, D_MODEL, tn), lambda l, j: (l, 0, j)),
                  pl.BlockSpec((None, 1, tn), lambda l, j: (l, 0, j))],
        out_specs=pl.BlockSpec((None, R, tn), lambda l, j: (l, 0, j)),
        out_shape=jax.ShapeDtypeStruct((DEPTH, R, 6 * D_MODEL), F32),
        compiler_params=_cparams(("parallel", "parallel")),
        name="modulation",
    )(c, w_mod_bf, b_mod.reshape(DEPTH, 1, 6 * D_MODEL))


def _proj_kernel(x_ref, sc_ref, sh_ref, w_ref, o_ref, h_scr):
    @pl.when(pl.program_id(1) == 0)
    def _():
        h_scr[...] = (x_ref[...] * (1.0 + sc_ref[...]) + sh_ref[...]).astype(BF16)

    o_ref[...] = jnp.dot(h_scr[...], w_ref[...], preferred_element_type=F32).astype(o_ref.dtype)


def _mod_spec(R, tiles_per_group):
    return pl.BlockSpec((None, R, D_MODEL), lambda i, *_: (i // tiles_per_group, 0, 0))


def mod_proj(x, sc, sh, w_bf, tm, tn, out_dtype):
    N = x.shape[0]
    G, R, _ = sc.shape
    n_out = w_bf.shape[1]
    tpg = (N // G) // tm
    return pl.pallas_call(
        _proj_kernel,
        grid=(N // tm, n_out // tn),
        in_specs=[pl.BlockSpec((tm, D_MODEL), lambda i, j: (i, 0)),
                  _mod_spec(R, tpg), _mod_spec(R, tpg),
                  pl.BlockSpec((D_MODEL, tn), lambda i, j: (0, j))],
        out_specs=pl.BlockSpec((tm, tn), lambda i, j: (i, j)),
        out_shape=jax.ShapeDtypeStruct((N, n_out), out_dtype),
        scratch_shapes=[pltpu.VMEM((tm, D_MODEL), BF16)],
        compiler_params=_cparams(("parallel", "arbitrary")),
        name="mod_proj",
    )(x, sc, sh, w_bf)


def _rot_half(x, cos, sin):
    half = cos.shape[-1]
    x1, x2 = x[:, :half], x[:, half:]
    return jnp.concatenate([x1 * cos - x2 * sin, x1 * sin + x2 * cos], axis=1)


def _group_norm_gate(o, gn, g):
    mu = jnp.mean(o, -1, keepdims=True)
    d = o - mu
    var = jnp.mean(d * d, -1, keepdims=True)
    return d * lax.rsqrt(var + GN_EPS) * gn * _silu(g)


def _ret_prompt_kernel(q_ref, k_ref, v_ref, g_ref, cos_ref, sin_ref, lg_ref, gn_ref,
                       o_ref, s_ref, s_scr, *, chunk):
    c = pl.program_id(2)

    @pl.when(c == 0)
    def _():
        s_scr[...] = jnp.zeros_like(s_scr)

    cos, sin = cos_ref[...], sin_ref[...]
    lg_row = lg_ref[...]
    lg = lg_row[:, :1]
    q = _rot_half(q_ref[...].astype(F32), cos, sin)
    k = _rot_half(k_ref[...].astype(F32), cos, sin) * (RET_DK ** -0.5)
    vb = v_ref[...].astype(BF16)
    qb = q.astype(BF16)

    row = lax.broadcasted_iota(I32, (chunk, chunk), 0)
    col = lax.broadcasted_iota(I32, (chunk, chunk), 1)
    diff = (row - col).astype(F32)
    decay = jnp.where(diff >= 0, jnp.exp(lg_row * jnp.maximum(diff, 0.0)), 0.0)
    idx = lax.broadcasted_iota(I32, (chunk, 1), 0).astype(F32)
    q_dec = jnp.exp(lg * (idx + 1.0))
    k_dec = jnp.exp(lg * (chunk - 1.0 - idx))
    chunk_dec = jnp.exp(lg * chunk)

    s = s_scr[...]
    att = lax.dot_general(qb, k.astype(BF16), (((1,), (1,)), ((), ())),
                          preferred_element_type=F32) * decay
    o = (jnp.dot(att.astype(BF16), vb, preferred_element_type=F32)
         + jnp.dot(qb, s.astype(BF16), preferred_element_type=F32) * q_dec)
    kd_t = (k * k_dec).T.astype(BF16)
    s_new = chunk_dec * s + jnp.dot(kd_t, vb, preferred_element_type=F32)
    s_scr[...] = s_new
    o_ref[...] = _group_norm_gate(o, gn_ref[...], g_ref[...].astype(F32)).astype(o_ref.dtype)

    @pl.when(c == pl.num_programs(2) - 1)
    def _():
        s_ref[...] = s_new


def _log_gamma():
    return jnp.log(1.0 - jnp.power(2.0, -5.0 - jnp.arange(RET_HEADS, dtype=F32)))


def retention_prompt(proj, gn_g, cos, sin, chunk):
    B, T, _ = proj.shape
    lg_tab = jnp.broadcast_to(_log_gamma()[:, None, None], (RET_HEADS, 1, chunk))
    qk_blk = lambda off: pl.BlockSpec((None, chunk, RET_DK), lambda b, h, c: (b, c, off + h))
    v_blk = lambda off: pl.BlockSpec((None, chunk, RET_DV), lambda b, h, c: (b, c, off + h))
    tab = pl.BlockSpec((chunk, RET_DK // 2), lambda b, h, c: (c, 0))
    return pl.pallas_call(
        functools.partial(_ret_prompt_kernel, chunk=chunk),
        grid=(B, RET_HEADS, T // chunk),
        in_specs=[qk_blk(0), qk_blk(RET_QK // RET_DK),
                  v_blk(2 * RET_QK // RET_DV), v_blk((2 * RET_QK + RET_V) // RET_DV),
                  tab, tab,
                  pl.BlockSpec((None, 1, chunk), lambda b, h, c: (h, 0, 0)),
                  pl.BlockSpec((1, RET_DV), lambda b, h, c: (0, h))],
        out_specs=[pl.BlockSpec((None, chunk, RET_DV), lambda b, h, c: (b, c, h)),
                   pl.BlockSpec((None, None, RET_DK, RET_DV), lambda b, h, c: (b, h, 0, 0))],
        out_shape=[jax.ShapeDtypeStruct((B, T, RET_V), BF16),
                   jax.ShapeDtypeStruct((B, RET_HEADS, RET_DK, RET_DV), F32)],
        scratch_shapes=[pltpu.VMEM((RET_DK, RET_DV), F32)],
        compiler_params=_cparams(("parallel", "parallel", "arbitrary")),
        name="retention_prompt",
    )(proj, proj, proj, proj, cos, sin, lg_tab, gn_g.reshape(1, RET_V))


def _ret_sample_kernel(qk_ref, v_ref, g_ref, s0_ref, cos_ref, sin_ref, gam_ref, gn_ref,
                       o_ref, s_ref):
    t = _rot_half(qk_ref[...], cos_ref[...], sin_ref[...])
    row = lax.broadcasted_iota(I32, t.shape, 0)
    t = jnp.where(row >= RET_HEADS, t * (RET_DK ** -0.5), t)
    pad = jnp.zeros((LANES - 2 * RET_HEADS, RET_DK), F32)
    t_t = jnp.concatenate([t, pad], axis=0).T
    for h in range(RET_HEADS):
        qc = t_t[:, h:h + 1]
        kc = t_t[:, RET_HEADS + h:RET_HEADS + h + 1]
        gam = gam_ref[h]
        vh = v_ref[h:h + 1, :]
        s = s0_ref[h]
        qk_dot = jnp.sum(t[h:h + 1, :] * t[RET_HEADS + h:RET_HEADS + h + 1, :], -1, keepdims=True)
        o = qk_dot * vh + gam * jnp.sum(s * qc, axis=0, keepdims=True)
        s_ref[h] = gam * s + kc * vh
        o_ref[h:h + 1, :] = _group_norm_gate(o, gn_ref[h:h + 1, :], g_ref[h:h + 1, :])


def retention_sample(proj, s0, gn_g, cos, sin):
    DB = proj.shape[0]
    qk = proj[:, :2 * RET_QK].reshape(DB, 2 * RET_HEADS, RET_DK)
    v = proj[:, 2 * RET_QK:2 * RET_QK + RET_V].reshape(DB, RET_HEADS, RET_DV)
    g = proj[:, 2 * RET_QK + RET_V:].reshape(DB, RET_HEADS, RET_DV)
    gam = jnp.broadcast_to(jnp.exp(_log_gamma())[:, None, None], (RET_HEADS, 1, RET_DV))
    full = lambda *shape: pl.BlockSpec(shape, lambda b: (0,) * len(shape))
    per_b = lambda *shape: pl.BlockSpec((None,) + shape, lambda b: (b,) + (0,) * len(shape))
    return pl.pallas_call(
        _ret_sample_kernel,
        grid=(DB,),
        in_specs=[per_b(2 * RET_HEADS, RET_DK), per_b(RET_HEADS, RET_DV), per_b(RET_HEADS, RET_DV),
                  per_b(RET_HEADS, RET_DK, RET_DV),
                  full(1, RET_DK // 2), full(1, RET_DK // 2),
                  full(RET_HEADS, 1, RET_DV), full(RET_HEADS, RET_DV)],
        out_specs=[per_b(RET_HEADS, RET_DV), per_b(RET_HEADS, RET_DK, RET_DV)],
        out_shape=[jax.ShapeDtypeStruct((DB, RET_HEADS, RET_DV), F32),
                   jax.ShapeDtypeStruct((DB, RET_HEADS, RET_DK, RET_DV), F32)],
        compiler_params=_cparams(("parallel",)),
        name="retention_sample",
    )(qk, v, g, s0, cos, sin, gam, gn_g.reshape(RET_HEADS, RET_DV))


def _out_ln_kernel(a_ref, w_ref, x_ref, g_ref, lng_ref, lnb_ref, o_ref):
    y = jnp.dot(a_ref[...].astype(BF16), w_ref[...], preferred_element_type=F32)
    o_ref[...] = _layer_norm(ALPHA * x_ref[...] + g_ref[...] * y, lng_ref[...], lnb_ref[...])


def out_proj_ln(a, w_bf, x, gate, ln_g, ln_b, tm):
    N, K = a.shape
    G, R, _ = gate.shape
    tpg = (N // G) // tm
    row = pl.BlockSpec((1, D_MODEL), lambda i: (0, 0))
    return pl.pallas_call(
        _out_ln_kernel,
        grid=(N // tm,),
        in_specs=[pl.BlockSpec((tm, K), lambda i: (i, 0)),
                  pl.BlockSpec((K, D_MODEL), lambda i: (0, 0)),
                  pl.BlockSpec((tm, D_MODEL), lambda i: (i, 0)),
                  _mod_spec(R, tpg), row, row],
        out_specs=pl.BlockSpec((tm, D_MODEL), lambda i: (i, 0)),
        out_shape=jax.ShapeDtypeStruct((N, D_MODEL), F32),
        compiler_params=_cparams(("parallel",)),
        name="out_proj_ln",
    )(a, w_bf, x, gate, ln_g.reshape(1, D_MODEL), ln_b.reshape(1, D_MODEL))


def _lane_roll(x, shift):
    return pltpu.roll(x, shift, axis=1)


def _router_kernel(x_ref, sc_ref, sh_ref, whi_ref, wlo_ref, b_ref, o_ref):
    h = x_ref[...] * (1.0 + sc_ref[...]) + sh_ref[...]
    hi = h.astype(BF16)
    lo = (h - hi.astype(F32)).astype(BF16)
    whi, wlo = whi_ref[...], wlo_ref[...]
    logits = (jnp.dot(hi, whi, preferred_element_type=F32)
              + jnp.dot(hi, wlo, preferred_element_type=F32)
              + jnp.dot(lo, whi, preferred_element_type=F32)) + b_ref[...]
    lane_i = lax.broadcasted_iota(I32, logits.shape, 1)
    lane = lane_i.astype(F32)
    m = jnp.max(logits, -1, keepdims=True)
    e = jnp.exp(logits - m)
    p = e / jnp.sum(e, -1, keepdims=True)
    s1 = p + _lane_roll(p, 1)
    s2 = p + _lane_roll(p, 2)
    s3 = p + _lane_roll(p, 3)
    pair_max = jnp.maximum(jnp.maximum(jnp.maximum(s1, s2), s3),
                           jnp.maximum(jnp.maximum(_lane_roll(s1, 1), _lane_roll(s2, 1)),
                                       _lane_roll(s1, 2)))
    is_last = ((lane_i % EXPERTS_PER_GROUP) == EXPERTS_PER_GROUP - 1) & (lane_i < N_EXPERTS)
    grp_score = jnp.where(is_last, pair_max, -jnp.inf)
    gmax = jnp.max(grp_score, -1, keepdims=True)
    big = float(4 * LANES)
    sel_last = jnp.min(jnp.where(grp_score == gmax, lane, big), -1, keepdims=True)
    in_grp = (lane <= sel_last) & (lane > sel_last - EXPERTS_PER_GROUP)
    masked = jnp.where(in_grp, p, -jnp.inf)
    v1 = jnp.max(masked, -1, keepdims=True)
    i1 = jnp.min(jnp.where(masked == v1, lane, big), -1, keepdims=True)
    masked2 = jnp.where(lane == i1, -jnp.inf, masked)
    v2 = jnp.max(masked2, -1, keepdims=True)
    i2 = jnp.min(jnp.where(masked2 == v2, lane, big), -1, keepdims=True)
    tot = v1 + v2
    grp_hot = lane == (sel_last - (EXPERTS_PER_GROUP - 1)) * (1.0 / EXPERTS_PER_GROUP) + GROUP_LANE0
    o_ref[...] = (jnp.where(lane == i1, v1 / tot, 0.0) + jnp.where(lane == i2, v2 / tot, 0.0)
                  + jnp.where(grp_hot, 1.0, 0.0))


def router(x, sc, sh, w_router, b_router, tm):
    N = x.shape[0]
    G, R, _ = sc.shape
    tpg = (N // G) // tm
    w_pad = jnp.zeros((D_MODEL, LANES), F32).at[:, :N_EXPERTS].set(w_router)
    w_hi = w_pad.astype(BF16)
    w_lo = (w_pad - w_hi.astype(F32)).astype(BF16)
    b_pad = jnp.full((1, LANES), NEG_BIG, F32).at[0, :N_EXPERTS].set(b_router)
    full = lambda r, c: pl.BlockSpec((r, c), lambda i: (0, 0))
    return pl.pallas_call(
        _router_kernel,
        grid=(N // tm,),
        in_specs=[pl.BlockSpec((tm, D_MODEL), lambda i: (i, 0)), _mod_spec(R, tpg), _mod_spec(R, tpg),
                  full(D_MODEL, LANES), full(D_MODEL, LANES), full(1, LANES)],
        out_specs=pl.BlockSpec((tm, LANES), lambda i: (i, 0)),
        out_shape=jax.ShapeDtypeStruct((N, LANES), F32),
        compiler_params=_cparams(("parallel",)),
        name="router",
    )(x, sc, sh, w_hi, w_lo, b_pad)


def _moe_kernel(x_ref, sc_ref, sh_ref, g2_ref, gates_ref, wg_ref, wu_ref, wd_ref, lng_ref, lnb_ref,
                o_ref, h_scr, acc_scr):
    e = pl.program_id(1)

    @pl.when(e == 0)
    def _():
        h_scr[...] = (x_ref[...] * (1.0 + sc_ref[...]) + sh_ref[...]).astype(BF16)
        acc_scr[...] = jnp.zeros_like(acc_scr)

    hb = h_scr[...]
    a = (_silu(jnp.dot(hb, wg_ref[...], preferred_element_type=F32))
         * jnp.dot(hb, wu_ref[...], preferred_element_type=F32))
    y = jnp.dot(a.astype(BF16), wd_ref[...], preferred_element_type=F32)
    gates = gates_ref[...]
    lane = lax.broadcasted_iota(I32, gates.shape, 1)
    gate_e = jnp.sum(jnp.where(lane == e, gates, 0.0), -1, keepdims=True)
    acc_scr[...] += gate_e * y

    @pl.when(e == pl.num_programs(1) - 1)
    def _():
        z = ALPHA * x_ref[...] + g2_ref[...] * acc_scr[...]
        o_ref[...] = _layer_norm(z, lng_ref[...], lnb_ref[...])


def moe_ln(x, sc, sh, g2, gates, l, wg_bf, wu_bf, wd_bf, ln_g, ln_b, tm):
    N = x.shape[0]
    G, R, _ = sc.shape
    tpg = (N // G) // tm
    ms = pl.BlockSpec((None, R, D_MODEL), lambda i, e: (i // tpg, 0, 0))
    row = pl.BlockSpec((1, D_MODEL), lambda i, e: (0, 0))
    return pl.pallas_call(
        _moe_kernel,
        grid=(N // tm, N_EXPERTS),
        in_specs=[pl.BlockSpec((tm, D_MODEL), lambda i, e: (i, 0)), ms, ms, ms,
                  pl.BlockSpec((tm, LANES), lambda i, e: (i, 0)),
                  pl.BlockSpec((None, None, D_MODEL, D_EXPERT), lambda i, e: (l, e, 0, 0)),
                  pl.BlockSpec((None, None, D_MODEL, D_EXPERT), lambda i, e: (l, e, 0, 0)),
                  pl.BlockSpec((None, None, D_EXPERT, D_MODEL), lambda i, e: (l, e, 0, 0)),
                  row, row],
        out_specs=pl.BlockSpec((tm, D_MODEL), lambda i, e: (i, 0)),
        out_shape=jax.ShapeDtypeStruct((N, D_MODEL), F32),
        scratch_shapes=[pltpu.VMEM((tm, D_MODEL), BF16), pltpu.VMEM((tm, D_MODEL), F32)],
        compiler_params=_cparams(("parallel", "arbitrary")),
        name="moe_ln",
    )(x, sc, sh, g2, gates, wg_bf, wu_bf, wd_bf, ln_g.reshape(1, D_MODEL), ln_b.reshape(1, D_MODEL))


def _split3_bf16(x):
    hi = x.astype(BF16)
    r = x - hi.astype(F32)
    mid = r.astype(BF16)
    lo = (r - mid.astype(F32)).astype(BF16)
    return hi, mid, lo


def _moe_group_kernel(x_ref, sc_ref, sh_ref, g2_ref, gates_ref, ltri_ref, wg_ref, wu_ref, wd_ref,
                      lng_ref, lnb_ref, o_ref,
                      h_scr, acc_scr, rank_scr, rank_t_scr, sel_t_scr, xs_scr, gs_scr, yacc_scr, cnt_smem,
                      *, cap):
    g, e = pl.program_id(1), pl.program_id(2)
    tm = x_ref.shape[0]
    lane = lax.broadcasted_iota(I32, (tm, LANES), 1)

    @pl.when((g == 0) & (e == 0))
    def _():
        h_scr[...] = (x_ref[...] * (1.0 + sc_ref[...]) + sh_ref[...]).astype(BF16)
        acc_scr[...] = jnp.zeros_like(acc_scr)
        gates = gates_ref[...]
        sel = jnp.where((lane >= GROUP_LANE0) & (lane < GROUP_LANE0 + N_GROUPS), gates, 0.0)
        rank = jnp.dot(ltri_ref[...], sel.astype(BF16), preferred_element_type=F32)
        rank_scr[...] = rank
        rank_t_scr[...] = rank.T
        sel_t_scr[...] = sel.T
        for gg in range(N_GROUPS):
            cnt_smem[gg] = jnp.sum(jnp.where(lane == GROUP_LANE0 + gg, sel, 0.0)).astype(I32)

    n_chunks = (cnt_smem[g] + cap - 1) // cap
    grp_row = pl.ds(GROUP_LANE0 + g, 1)

    def chunk_rows(c):
        return pl.ds(pl.multiple_of(c * cap, cap), cap)

    @pl.when(e == 0)
    def _():
        expert_lanes = jnp.where(lane < N_EXPERTS, gates_ref[...], 0.0)
        parts = _split3_bf16(expert_lanes)
        g_parts = sum(_lane_roll(part.astype(F32), j * N_EXPERTS)
                      for j, part in enumerate(parts)).astype(BF16)
        rank_row, sel_row = rank_t_scr[grp_row, :], sel_t_scr[grp_row, :]

        def dispatch(c, carry):
            slot = (c * cap + lax.broadcasted_iota(I32, (cap, 1), 0)).astype(F32)
            p = jnp.where((rank_row == slot) & (sel_row > 0.5), 1.0, 0.0).astype(BF16)
            xs_scr[chunk_rows(c), :] = jnp.dot(p, h_scr[...], preferred_element_type=F32).astype(BF16)
            gs_scr[chunk_rows(c), :] = jnp.dot(p, g_parts, preferred_element_type=F32)
            yacc_scr[chunk_rows(c), :] = jnp.zeros((cap, D_MODEL), F32)
            return carry

        lax.fori_loop(0, n_chunks, dispatch, 0)

    def expert(c, carry):
        xs = xs_scr[chunk_rows(c), :]
        a = (_silu(jnp.dot(xs, wg_ref[...], preferred_element_type=F32))
             * jnp.dot(xs, wu_ref[...], preferred_element_type=F32))
        y = jnp.dot(a.astype(BF16), wd_ref[...], preferred_element_type=F32)
        gs = gs_scr[chunk_rows(c), :]
        lane_c = lax.broadcasted_iota(I32, gs.shape, 1)
        is_part = ((lane_c % N_EXPERTS) == g * EXPERTS_PER_GROUP + e) & (lane_c < 3 * N_EXPERTS)
        gate = jnp.sum(jnp.where(is_part, gs, 0.0), -1, keepdims=True)
        yacc_scr[chunk_rows(c), :] += gate * y
        return carry

    lax.fori_loop(0, n_chunks, expert, 0)

    @pl.when(e == EXPERTS_PER_GROUP - 1)
    def _():
        in_grp = lane == GROUP_LANE0 + g
        rank_col = jnp.sum(jnp.where(in_grp, rank_scr[...], 0.0), -1, keepdims=True)
        sel_col = jnp.sum(jnp.where(in_grp, gates_ref[...], 0.0), -1, keepdims=True)

        def combine(c, carry):
            slot = (c * cap + lax.broadcasted_iota(I32, (1, cap), 1)).astype(F32)
            p_t = jnp.where((rank_col == slot) & (sel_col > 0.5), 1.0, 0.0).astype(BF16)
            y_hi, y_mid, _ = _split3_bf16(yacc_scr[chunk_rows(c), :])
            acc_scr[...] += (jnp.dot(p_t, y_hi, preferred_element_type=F32)
                             + jnp.dot(p_t, y_mid, preferred_element_type=F32))
            return carry

        lax.fori_loop(0, n_chunks, combine, 0)

    @pl.when((g == N_GROUPS - 1) & (e == EXPERTS_PER_GROUP - 1))
    def _():
        z = ALPHA * x_ref[...] + g2_ref[...] * acc_scr[...]
        o_ref[...] = _layer_norm(z, lng_ref[...], lnb_ref[...])


def moe_group_ln(x, sc, sh, g2, gates, l, wg_bf, wu_bf, wd_bf, ln_g, ln_b, tm, cap):
    N = x.shape[0]
    G, R, _ = sc.shape
    tpg = (N // G) // tm
    ltri = (jnp.arange(tm)[:, None] > jnp.arange(tm)[None, :]).astype(BF16)
    ms = pl.BlockSpec((None, R, D_MODEL), lambda i, g, e: (i // tpg, 0, 0))
    row = pl.BlockSpec((1, D_MODEL), lambda i, g, e: (0, 0))
    expert = lambda i, g, e: (l, g * EXPERTS_PER_GROUP + e, 0, 0)
    w_in = pl.BlockSpec((None, None, D_MODEL, D_EXPERT), expert)
    w_out = pl.BlockSpec((None, None, D_EXPERT, D_MODEL), expert)
    return pl.pallas_call(
        functools.partial(_moe_group_kernel, cap=cap),
        grid=(N // tm, N_GROUPS, EXPERTS_PER_GROUP),
        in_specs=[pl.BlockSpec((tm, D_MODEL), lambda i, g, e: (i, 0)), ms, ms, ms,
                  pl.BlockSpec((tm, LANES), lambda i, g, e: (i, 0)),
                  pl.BlockSpec((tm, tm), lambda i, g, e: (0, 0)),
                  w_in, w_in, w_out, row, row],
        out_specs=pl.BlockSpec((tm, D_MODEL), lambda i, g, e: (i, 0)),
        out_shape=jax.ShapeDtypeStruct((N, D_MODEL), F32),
        scratch_shapes=[pltpu.VMEM((tm, D_MODEL), BF16), pltpu.VMEM((tm, D_MODEL), F32),
                        pltpu.VMEM((tm, LANES), F32), pltpu.VMEM((LANES, tm), F32),
                        pltpu.VMEM((LANES, tm), F32), pltpu.VMEM((tm, D_MODEL), BF16),
                        pltpu.VMEM((tm, LANES), F32), pltpu.VMEM((tm, D_MODEL), F32),
                        pltpu.SMEM((N_GROUPS,), I32)],
        compiler_params=_cparams(("arbitrary", "arbitrary", "arbitrary")),
        name="moe_group_ln",
    )(x, sc, sh, g2, gates, ltri, wg_bf, wu_bf, wd_bf, ln_g.reshape(1, D_MODEL), ln_b.reshape(1, D_MODEL))


LOG2_E = 1.4426950408889634
Q_PRESCALE = DSA_HEAD_DIM ** -0.5 * LOG2_E
DSA_IN_PAD = -(-DSA_IN // LANES) * LANES
DSA_QI_OFF = DSA_Q + 2 * DSA_KV
DSA_KW_OFF = DSA_QI_OFF + IDX_HEADS * IDX_DIM


def _rope_tables(pos, rot_dim, period, n_periods):
    half = rot_dim // 2
    inv_freq = 1.0 / (ROPE_THETA ** (jnp.arange(0, rot_dim, 2, dtype=F32) / rot_dim))
    ang = pos.astype(F32)[:, None] * inv_freq[None, :]
    cos, sin = jnp.cos(ang), jnp.sin(ang)
    T = pos.shape[0]
    zeros = jnp.zeros((T, period - 2 * half), F32)
    zh = jnp.zeros((T, half), F32)
    a1 = jnp.concatenate([cos, cos, zeros + 1.0], 1)
    b1 = jnp.concatenate([zh, sin, zeros], 1)
    c1 = jnp.concatenate([-sin, zh, zeros], 1)
    rest = LANES - n_periods * period
    pad1 = jnp.ones((T, rest), F32)
    pad0 = jnp.zeros((T, rest), F32)
    a = jnp.concatenate([a1] * n_periods + [pad1], 1)
    b = jnp.concatenate([b1] * n_periods + [pad0], 1)
    c = jnp.concatenate([c1] * n_periods + [pad0], 1)
    return a, b, c


def _apply_rope(x, a, b, c, half):
    return x * a + _lane_roll(x, half) * b + _lane_roll(x, LANES - half) * c


N_ROPE_TABLES = 9


def _dsa_sections(x_ref, sc_ref, sh_ref, w_ref, tab_refs):
    hb = (x_ref[...] * (1.0 + sc_ref[...]) + sh_ref[...]).astype(BF16)
    ma, mb, mc, ia, ib, ic, ka, kb, kc = (t[...] for t in tab_refs)

    def sec(off, width):
        return jnp.dot(hb, w_ref[:, off:off + width], preferred_element_type=F32)

    def lanes(x, h):
        return x[:, h * LANES:(h + 1) * LANES]

    q = sec(0, DSA_Q)
    q_heads = [_apply_rope(lanes(q, h), ma, mb, mc, ROT_DIM // 2) * Q_PRESCALE for h in range(DSA_HEADS)]
    k = sec(DSA_Q, DSA_KV)
    k = jnp.concatenate([_apply_rope(lanes(k, h), ma, mb, mc, ROT_DIM // 2) for h in range(DSA_KV_HEADS)], 1)
    v = sec(DSA_Q + DSA_KV, DSA_KV)
    qi = sec(DSA_QI_OFF, IDX_HEADS * IDX_DIM)
    qi_pairs = [_apply_rope(lanes(qi, h), ia, ib, ic, IDX_ROT_DIM // 2)
                for h in range(IDX_HEADS * IDX_DIM // LANES)]
    kw = _apply_rope(sec(DSA_KW_OFF, LANES), ka, kb, kc, IDX_ROT_DIM // 2)
    return q_heads, k, v, qi_pairs, kw


def _dsa_proj_kernel(x_ref, sc_ref, sh_ref, w_ref, *rest):
    tabs = rest[:N_ROPE_TABLES]
    q_ref, k_ref, v_ref, qi_ref, kw_ref = rest[N_ROPE_TABLES:]
    q_heads, k, v, qi_pairs, kw = _dsa_sections(x_ref, sc_ref, sh_ref, w_ref, tabs)
    for h, qh in enumerate(q_heads):
        q_ref[:, h * LANES:(h + 1) * LANES] = qh.astype(q_ref.dtype)
    for h, qp in enumerate(qi_pairs):
        qi_ref[:, h * LANES:(h + 1) * LANES] = qp.astype(qi_ref.dtype)
    k_ref[...] = k
    v_ref[...] = v
    kw_ref[...] = kw


def _dsa_proj_prompt_kernel(x_ref, sc_ref, sh_ref, w_ref, *rest, tq):
    tabs = rest[:N_ROPE_TABLES]
    k_ref, v_ref, kw_ref, kbf_ref, kwbf_ref, vt_ref, qt_ref, qit_ref, wit_ref = rest[N_ROPE_TABLES:]
    q_heads, k, v, qi_pairs, kw = _dsa_sections(x_ref, sc_ref, sh_ref, w_ref, tabs)
    tm = k.shape[0]
    k_ref[...] = k
    kbf_ref[...] = k.astype(BF16)
    v_ref[...] = v
    kw_ref[...] = kw
    kwbf_ref[...] = kw.astype(BF16)
    for n in range(DSA_KV_HEADS):
        vt_ref[n, 0] = v[:, n * DSA_HEAD_DIM:(n + 1) * DSA_HEAD_DIM].T.astype(BF16)
    heads_per_pair = LANES // IDX_DIM
    for j in range(tm // tq):
        rows = slice(j * tq, (j + 1) * tq)
        for h, qh in enumerate(q_heads):
            n, g = divmod(h, DSA_GROUP)
            qt_ref[j, n, :, g * tq:(g + 1) * tq] = qh[rows].T.astype(BF16)
        for hp, qp in enumerate(qi_pairs):
            t = qp[rows].T.astype(BF16)
            for s in range(heads_per_pair):
                h = hp * heads_per_pair + s
                qit_ref[j, :IDX_DIM, h * tq:(h + 1) * tq] = t[s * IDX_DIM:(s + 1) * IDX_DIM]
        qit_ref[j, IDX_DIM:, :] = jnp.zeros((LANES - IDX_DIM, IDX_HEADS * tq), BF16)
        wit_ref[j] = kw[rows].T[IDX_DIM:IDX_DIM + IDX_HEADS]


def _dsa_project_call(body, x, sc, sh, w_pad_bf, pos, tm, out_specs, out_shape, name):
    N = x.shape[0]
    G, R, _ = sc.shape
    tpg = (N // G) // tm
    n_tab = pos.shape[0] // tm
    tabs = (_rope_tables(pos, ROT_DIM, DSA_HEAD_DIM, 1)
            + _rope_tables(pos, IDX_ROT_DIM, IDX_DIM, 2)
            + _rope_tables(pos, IDX_ROT_DIM, IDX_DIM, 1))
    tab = pl.BlockSpec((tm, LANES), lambda i: (i % n_tab, 0))
    return pl.pallas_call(
        body,
        grid=(N // tm,),
        in_specs=[pl.BlockSpec((tm, D_MODEL), lambda i: (i, 0)), _mod_spec(R, tpg), _mod_spec(R, tpg),
                  pl.BlockSpec((D_MODEL, DSA_IN_PAD), lambda i: (0, 0))] + [tab] * N_ROPE_TABLES,
        out_specs=out_specs, out_shape=out_shape,
        compiler_params=_cparams(("parallel",)),
        name=name,
    )(x, sc, sh, w_pad_bf, *tabs)


def dsa_project(x, sc, sh, w_pad_bf, pos, tm):
    N = x.shape[0]
    out = lambda w: pl.BlockSpec((tm, w), lambda i: (i, 0))
    return _dsa_project_call(
        _dsa_proj_kernel, x, sc, sh, w_pad_bf, pos, tm,
        [out(DSA_Q), out(DSA_KV), out(DSA_KV), out(IDX_HEADS * IDX_DIM), out(LANES)],
        [jax.ShapeDtypeStruct((N, DSA_Q), BF16), jax.ShapeDtypeStruct((N, DSA_KV), F32),
         jax.ShapeDtypeStruct((N, DSA_KV), F32), jax.ShapeDtypeStruct((N, IDX_HEADS * IDX_DIM), BF16),
         jax.ShapeDtypeStruct((N, LANES), F32)], "dsa_project")


def dsa_project_prompt(x, sc, sh, w_pad_bf, pos, B, T, tm, tq):
    N = B * T
    tiles, n_qt = T // tm, tm // tq
    gq = DSA_GROUP * tq
    out = lambda w: pl.BlockSpec((tm, w), lambda i: (i, 0))
    by_tile = lambda *shape: pl.BlockSpec((None,) + shape,
                                          lambda i: (i // tiles, i % tiles) + (0,) * (len(shape) - 1))
    return _dsa_project_call(
        functools.partial(_dsa_proj_prompt_kernel, tq=tq), x, sc, sh, w_pad_bf, pos, tm,
        [out(DSA_KV), out(DSA_KV), out(LANES), out(DSA_KV), out(LANES),
         pl.BlockSpec((None, DSA_KV_HEADS, 1, DSA_HEAD_DIM, tm), lambda i: (i // tiles, 0, i % tiles, 0, 0)),
         by_tile(n_qt, DSA_KV_HEADS, DSA_HEAD_DIM, gq),
         by_tile(n_qt, LANES, IDX_HEADS * tq),
         by_tile(n_qt, IDX_HEADS, tq)],
        [jax.ShapeDtypeStruct((N, DSA_KV), F32), jax.ShapeDtypeStruct((N, DSA_KV), F32),
         jax.ShapeDtypeStruct((N, LANES), F32), jax.ShapeDtypeStruct((N, DSA_KV), BF16),
         jax.ShapeDtypeStruct((N, LANES), BF16),
         jax.ShapeDtypeStruct((B, DSA_KV_HEADS, tiles, DSA_HEAD_DIM, tm), BF16),
         jax.ShapeDtypeStruct((B, T // tq, DSA_KV_HEADS, DSA_HEAD_DIM, gq), BF16),
         jax.ShapeDtypeStruct((B, T // tq, LANES, IDX_HEADS * tq), BF16),
         jax.ShapeDtypeStruct((B, T // tq, IDX_HEADS, tq), F32)], "dsa_project_prompt")


def _sortable_key(score):
    b = pltpu.bitcast(jnp.where(score == 0.0, 0.0, score), I32)
    return b ^ ((b >> 31) & jnp.int32(0x7FFFFFFF))


KEY_NEG_INF = -2139095041
BISECT_FIRST = 20
BISECT_STAGE = 4


def _reduce_row_groups(x, op, n_chains=8):
    parts = [x[r:r + SUBLANES] for r in range(0, x.shape[0], SUBLANES)]
    accs = parts[:n_chains]
    for j, part in enumerate(parts[n_chains:]):
        accs[j % len(accs)] = op(accs[j % len(accs)], part)
    while len(accs) > 1:
        accs = [op(a, b) for a, b in zip(accs[0::2], accs[1::2])] + (accs[-1:] if len(accs) % 2 else [])
    return accs[0]


def _sum_row_groups(x):
    return _reduce_row_groups(x, jnp.add)


def _max_row_groups(x):
    return _reduce_row_groups(x, jnp.maximum)


def _kth_largest_key(count_ge, shape, k):
    def body(s, p):
        cand = p + lax.shift_left(jnp.int32(1), 31 - s)
        return jnp.where(count_ge(cand) >= k, cand, p)
    return lax.fori_loop(0, 32, body, jnp.full(shape, INT_MIN, I32))


def _tie_index_bound(count_eq_below, shape, need, n_bits):
    def body(s, m):
        cand = m + lax.shift_left(jnp.int32(1), n_bits - 1 - s)
        return jnp.where(count_eq_below(cand) < need, cand, m)
    return lax.fori_loop(0, n_bits, body, jnp.zeros(shape, I32))


def _dsa_prompt_kernel(ki_ref, qit_ref, wit_ref, k_ref, vt_ref, qt_ref, o_ref,
                       keys_scr, bias_scr, midx_scr, logit_scr, thr_scr, nge_scr,
                       *, tq, kb_size, cb_size, topk, idx_bits):
    i = pl.program_id(1)
    n_kb = ((i + 1) * tq + kb_size - 1) // kb_size
    n_cb = ((i + 1) * tq + cb_size - 1) // cb_size
    q_pos = i * tq + lax.broadcasted_iota(I32, (1, tq), 1)
    heads_per_dot = 2

    @pl.when(i == 0)
    def _():
        keys_scr[...] = jnp.full(keys_scr.shape, KEY_NEG_INF, I32)

    def key_rows(kb):
        return pl.ds(pl.multiple_of(kb * kb_size, kb_size), kb_size)

    def l_index(kb, size=kb_size):
        return kb * size + lax.broadcasted_iota(I32, (size, 1), 0)

    def score_body(kb, carry):
        kib = ki_ref[key_rows(kb), :]
        acc = jnp.zeros((kb_size, tq), F32)
        for hp in range(IDX_HEADS // heads_per_dot):
            s = jnp.dot(kib, qit_ref[:, hp * heads_per_dot * tq:(hp + 1) * heads_per_dot * tq],
                        preferred_element_type=F32)
            for j in range(heads_per_dot):
                h = hp * heads_per_dot + j
                w = wit_ref[h:h + 1, :] * (IDX_HEADS ** -0.5 * IDX_DIM ** -0.5)
                acc = acc + jnp.maximum(s[:, j * tq:(j + 1) * tq], 0.0) * w
        allowed = l_index(kb) <= q_pos
        keys_scr[key_rows(kb), :] = _sortable_key(jnp.where(allowed, acc, -jnp.inf))
        return carry

    lax.fori_loop(0, n_kb, score_body, 0)

    def count(pred_fn):
        def body(cb, acc):
            rows = pl.ds(pl.multiple_of(cb * cb_size, cb_size), cb_size)
            m = jnp.where(pred_fn(keys_scr[rows, :], cb), 1.0, 0.0)
            return acc + _sum_row_groups(m)
        acc = lax.fori_loop(0, n_cb, body, jnp.zeros((SUBLANES, tq), F32))
        return acc.sum(axis=0, keepdims=True)

    k_f = float(topk)

    def bisect(s_lo, s_hi):
        def body(s, carry):
            p, n_p = carry
            cand = p + lax.shift_left(jnp.int32(1), 31 - s)
            n_c = count(lambda key, cb: key >= cand)
            take = n_c >= k_f
            return jnp.where(take, cand, p), jnp.where(take, n_c, n_p)
        p, n_p = lax.fori_loop(s_lo, s_hi, body, (thr_scr[...], nge_scr[...]))
        thr_scr[...] = p
        nge_scr[...] = n_p

    thr_scr[...] = jnp.full((1, tq), INT_MIN, I32)
    nge_scr[...] = jnp.full((1, tq), float(2 ** 30), F32)
    bisect(0, BISECT_FIRST)
    for s0 in range(BISECT_FIRST, 32, BISECT_STAGE):
        @pl.when(jnp.max(jnp.where(nge_scr[...] != k_f, 1.0, 0.0)) > 0.5)
        def _():
            bisect(s0, min(32, s0 + BISECT_STAGE))

    thr = thr_scr[...]
    n_ge = nge_scr[...]
    n_gt = count(lambda key, cb: key > thr)
    need = k_f - n_gt
    excess = (n_ge - n_gt > need) & (thr != KEY_NEG_INF)
    midx_scr[...] = jnp.full((1, tq), 2 ** idx_bits, I32)

    @pl.when(jnp.max(jnp.where(excess, 1.0, 0.0)) > 0.5)
    def _():
        m = _tie_index_bound(
            lambda c: count(lambda key, cb: (key == thr) & (l_index(cb, cb_size) < c)),
            (1, tq), need, idx_bits)
        midx_scr[...] = jnp.where(excess, m, 2 ** idx_bits)

    midx = midx_scr[...]

    def bias_body(kb, carry):
        key = keys_scr[key_rows(kb), :]
        l = l_index(kb)
        sel = ((key > thr) | ((key == thr) & (l <= midx))) & (l <= q_pos)
        bias_scr[key_rows(kb), :] = jnp.where(sel, 0.0, NEG_BIG)
        return carry

    lax.fori_loop(0, n_kb, bias_body, 0)

    gq = DSA_GROUP * tq
    heads = range(DSA_KV_HEADS)

    def logits_of(kb):
        bias = bias_scr[key_rows(kb), :]
        bias = jnp.concatenate([bias] * DSA_GROUP, axis=1)
        blk_max = []
        for n in heads:
            kblk = k_ref[key_rows(kb), n * DSA_HEAD_DIM:(n + 1) * DSA_HEAD_DIM]
            logits = jnp.dot(kblk, qt_ref[n], preferred_element_type=F32) + bias
            logit_scr[n, key_rows(kb), :] = logits
            blk_max.append(_max_row_groups(logits))
        return tuple(blk_max)

    def absorb(kb, blk_max, state):
        new = []
        for n in heads:
            m_run, l_run, acc = state[n]
            m_new = jnp.maximum(m_run, jnp.max(blk_max[n], axis=0, keepdims=True))
            alpha = jnp.exp2(m_run - m_new)
            p = jnp.exp2(logit_scr[n, key_rows(kb), :] - m_new)
            acc = acc * alpha + jnp.dot(vt_ref[n, kb], p.astype(BF16), preferred_element_type=F32)
            new.append((m_new, l_run * alpha + _sum_row_groups(p), acc))
        return tuple(new)

    def att_body(kb, carry):
        blk_max, state = carry
        state = absorb(kb - 1, blk_max, state)
        return logits_of(kb), state

    state0 = tuple((jnp.full((1, gq), NEG_BIG, F32), jnp.zeros((SUBLANES, gq), F32),
                    jnp.zeros((DSA_HEAD_DIM, gq), F32)) for _ in heads)
    last_max, state = lax.fori_loop(1, n_kb, att_body, (logits_of(0), state0))
    fin = absorb(n_kb - 1, last_max, state)
    for n in heads:
        _, l8, acc = fin[n]
        o_t = acc / jnp.sum(l8, axis=0, keepdims=True)
        for g in range(DSA_GROUP):
            h = n * DSA_GROUP + g
            o_ref[:, h * DSA_HEAD_DIM:(h + 1) * DSA_HEAD_DIM] = (
                o_t[:, g * tq:(g + 1) * tq].T.astype(o_ref.dtype))


def dsa_prompt_attend(kw_bf, qi_t, wi_t, k_bf, v_t, q_t, B, T, topk, tq, kb_size):
    nqt = T // tq
    nkb = T // kb_size
    gq = DSA_GROUP * tq
    return pl.pallas_call(
        functools.partial(_dsa_prompt_kernel, tq=tq, kb_size=kb_size, cb_size=min(512, T), topk=topk,
                          idx_bits=max(1, (T - 1).bit_length())),
        grid=(B, nqt),
        in_specs=[pl.BlockSpec((None, T, LANES), lambda b, i: (b, 0, 0)),
                  pl.BlockSpec((None, None, LANES, IDX_HEADS * tq), lambda b, i: (b, i, 0, 0)),
                  pl.BlockSpec((None, None, IDX_HEADS, tq), lambda b, i: (b, i, 0, 0)),
                  pl.BlockSpec((None, T, DSA_KV), lambda b, i: (b, 0, 0)),
                  pl.BlockSpec((None, DSA_KV_HEADS, nkb, DSA_HEAD_DIM, kb_size),
                               lambda b, i: (b, 0, 0, 0, 0)),
                  pl.BlockSpec((None, None, DSA_KV_HEADS, DSA_HEAD_DIM, gq),
                               lambda b, i: (b, i, 0, 0, 0))],
        out_specs=pl.BlockSpec((tq, DSA_Q), lambda b, i: (b * nqt + i, 0)),
        out_shape=jax.ShapeDtypeStruct((B * T, DSA_Q), BF16),
        scratch_shapes=[pltpu.VMEM((T, tq), I32), pltpu.VMEM((T, tq), F32), pltpu.VMEM((1, tq), I32),
                        pltpu.VMEM((DSA_KV_HEADS, T, gq), F32),
                        pltpu.VMEM((1, tq), I32), pltpu.VMEM((1, tq), F32)],
        compiler_params=_cparams(("arbitrary", "arbitrary")),
        name="dsa_prompt_attend",
    )(kw_bf.reshape(B, T, LANES), qi_t, wi_t, k_bf.reshape(B, T, DSA_KV), v_t, q_t)


def _dsa_sample_score_kernel(pt_ref, qi_ref, wi_ref, kis_ref, *rest, n_pages, page):
    pages, o_ref = rest[:n_pages], rest[n_pages]
    qi = qi_ref[...]
    w = wi_ref[...] * (IDX_HEADS ** -0.5 * IDX_DIM ** -0.5)
    for j in range(n_pages):
        dots = jnp.dot(qi, pages[j][...].astype(BF16), preferred_element_type=F32)
        o_ref[:, j * page:(j + 1) * page] = jnp.sum(jnp.maximum(dots, 0.0) * w, axis=0, keepdims=True)
    ki_self = kis_ref[...].astype(BF16).astype(F32)
    d_self = jnp.sum(qi.astype(F32) * ki_self, -1, keepdims=True)
    s_self = jnp.sum(jnp.maximum(d_self, 0.0) * w, axis=0, keepdims=True)
    lane = lax.broadcasted_iota(I32, (1, LANES), 1)
    o_ref[:, n_pages * page:] = jnp.where(lane == 0, s_self, -jnp.inf)


def _page_specs(n_pages, page, width):
    return [pl.BlockSpec((None, page, width), functools.partial(lambda j, b, pt: (pt[b, j], 0, 0), j))
            for j in range(n_pages)]


def dsa_sample_scores(page_table, qi_bf, wi, ki_self, cache_kidx_t):
    DB, n_pages = page_table.shape
    page = cache_kidx_t.shape[2]
    lp = n_pages * page + LANES
    per_b = lambda *shape: pl.BlockSpec((None,) + shape, lambda b, pt: (b,) + (0,) * len(shape))
    return pl.pallas_call(
        functools.partial(_dsa_sample_score_kernel, n_pages=n_pages, page=page),
        grid_spec=pltpu.PrefetchScalarGridSpec(
            num_scalar_prefetch=1, grid=(DB,),
            in_specs=[per_b(IDX_HEADS, IDX_DIM), per_b(IDX_HEADS, 1), per_b(1, IDX_DIM)]
            + _page_specs(n_pages, IDX_DIM, page),
            out_specs=per_b(1, lp)),
        out_shape=jax.ShapeDtypeStruct((DB, 1, lp), F32),
        compiler_params=_cparams(("arbitrary",)),
        name="dsa_sample_scores",
    )(page_table, qi_bf.reshape(DB, IDX_HEADS, IDX_DIM), wi.reshape(DB, IDX_HEADS, 1),
      ki_self.reshape(DB, 1, IDX_DIM), *([cache_kidx_t] * n_pages))


def _dsa_sample_select_kernel(s_ref, o_ref, *, n_keys, topk, idx_bits):
    score = s_ref[...]
    lane = lax.broadcasted_iota(I32, score.shape, 1)
    valid = lane < n_keys
    key = _sortable_key(jnp.where(valid, score, -jnp.inf))

    def count(pred):
        return jnp.sum(jnp.where(pred, 1.0, 0.0), -1, keepdims=True)

    shape = (score.shape[0], 1)
    thr = _kth_largest_key(lambda c: count(key >= c), shape, float(topk))
    n_gt = count(key > thr)
    need = float(topk) - n_gt
    excess = (count(key == thr) > need) & (thr != KEY_NEG_INF)
    m = _tie_index_bound(lambda c: count((key == thr) & (lane < c)), shape, need, idx_bits)
    midx = jnp.where(excess, m, 2 ** idx_bits)
    sel = ((key > thr) | ((key == thr) & (lane <= midx))) & valid
    o_ref[...] = jnp.where(sel, 0.0, NEG_BIG)


def dsa_sample_select(score, n_keys, topk):
    DB, lp = score.shape
    return pl.pallas_call(
        functools.partial(_dsa_sample_select_kernel, n_keys=n_keys, topk=topk,
                          idx_bits=max(1, (lp - 1).bit_length())),
        grid=(1,),
        in_specs=[pl.BlockSpec((DB, lp), lambda i: (0, 0))],
        out_specs=pl.BlockSpec((DB, lp), lambda i: (0, 0)),
        out_shape=jax.ShapeDtypeStruct((DB, lp), F32),
        compiler_params=_cparams(("arbitrary",)),
        name="dsa_sample_select",
    )(score)


def _dsa_sample_attend_kernel(pt_ref, q_ref, bias_ref, ks_ref, vs_ref, *rest, n_pages, page):
    k_pages, v_pages = rest[:n_pages], rest[n_pages:2 * n_pages]
    o_ref, logit_scr = rest[2 * n_pages], rest[2 * n_pages + 1]
    q = q_ref[...]
    tail = n_pages * page
    lane = lax.broadcasted_iota(I32, (DSA_GROUP, LANES), 1)
    for n in range(DSA_KV_HEADS):
        hs = slice(n * DSA_GROUP, (n + 1) * DSA_GROUP)
        ds = slice(n * DSA_HEAD_DIM, (n + 1) * DSA_HEAD_DIM)
        head_rows = pl.ds(n, page, stride=DSA_KV_HEADS)
        qn = q[hs, :]
        for j in range(n_pages):
            lg = lax.dot_general(qn, k_pages[j][head_rows, :].astype(BF16), (((1,), (1,)), ((), ())),
                                 preferred_element_type=F32)
            logit_scr[hs, j * page:(j + 1) * page] = lg + bias_ref[:, j * page:(j + 1) * page]
        k_self = ks_ref[:, ds].astype(BF16).astype(F32)
        lg_self = jnp.sum(qn.astype(F32) * k_self, -1, keepdims=True)
        logit_scr[hs, tail:] = jnp.where(lane == 0, lg_self, 0.0) + bias_ref[:, tail:]
        logits = logit_scr[hs, :]
        m = jnp.max(logits, -1, keepdims=True)
        p = jnp.exp2(logits - m)
        p_bf = (p / jnp.sum(p, -1, keepdims=True)).astype(BF16)
        v_self = vs_ref[:, ds].astype(BF16).astype(F32)
        acc = p_bf[:, tail:].astype(F32)[:, :1] * v_self
        for j in range(n_pages):
            acc = acc + jnp.dot(p_bf[:, j * page:(j + 1) * page], v_pages[j][head_rows, :].astype(BF16),
                                preferred_element_type=F32)
        o_ref[hs, :] = acc


def dsa_sample_attend(page_table, q_bf, bias, k_self, v_self, cache_k, cache_v):
    DB, n_pages = page_table.shape
    rows = cache_k.shape[1]
    page = rows // DSA_KV_HEADS
    lp = n_pages * page + LANES
    per_b = lambda *shape: pl.BlockSpec((None,) + shape, lambda b, pt: (b,) + (0,) * len(shape))
    o = pl.pallas_call(
        functools.partial(_dsa_sample_attend_kernel, n_pages=n_pages, page=page),
        grid_spec=pltpu.PrefetchScalarGridSpec(
            num_scalar_prefetch=1, grid=(DB,),
            in_specs=[per_b(DSA_HEADS, DSA_HEAD_DIM), per_b(1, lp), per_b(1, DSA_KV), per_b(1, DSA_KV)]
            + _page_specs(n_pages, rows, DSA_HEAD_DIM) + _page_specs(n_pages, rows, DSA_HEAD_DIM),
            out_specs=per_b(DSA_HEADS, DSA_HEAD_DIM),
            scratch_shapes=[pltpu.VMEM((DSA_HEADS, lp), F32)]),
        out_shape=jax.ShapeDtypeStruct((DB, DSA_HEADS, DSA_HEAD_DIM), F32),
        compiler_params=_cparams(("arbitrary",)),
        name="dsa_sample_attend",
    )(page_table, q_bf.reshape(DB, DSA_HEADS, DSA_HEAD_DIM), bias.reshape(DB, 1, lp),
      k_self.reshape(DB, 1, DSA_KV), v_self.reshape(DB, 1, DSA_KV),
      *([cache_k] * n_pages), *([cache_v] * n_pages))
    return o.reshape(DB, DSA_Q)


def _retnet_tables(pos):
    inv_freq = jnp.power(RET_ANGLE_BASE, -jnp.linspace(0.0, 1.0, RET_DK // 2, dtype=F32))
    ang = pos.astype(F32)[:, None] * inv_freq[None, :]
    return jnp.cos(ang), jnp.sin(ang)


def _split_mod(mod_l, G, R):
    return [m.reshape(G, R, D_MODEL) for m in jnp.split(mod_l, 6, axis=-1)]


MOE_GROUP_TILE = 1024
MOE_GROUP_CAP = 256


def _channel_mixer(x, sc2, sh2, g2, l, wts, tm):
    gates = router(x, sc2, sh2, wts["w_router"], wts["b_router"], tm)
    args = (x, sc2, sh2, g2, gates, l, wts["w_gate"], wts["w_up"], wts["w_down"],
            wts["ln2_g"][l], wts["ln2_b"][l])
    rows_per_mod = x.shape[0] // sc2.shape[0]
    if sc2.shape[1] == 1 and rows_per_mod >= 2 * MOE_GROUP_CAP:
        tile = min(MOE_GROUP_TILE, rows_per_mod)
        return moe_group_ln(*args, tile, MOE_GROUP_CAP)
    return moe_ln(*args, tm)


def kernel(x_prompt, x_sample, state_ret, cache_k, cache_v, cache_kidx, page_table,
           c_prompt, c_sample, w_mod, b_mod, ln1_g, ln1_b, ln2_g, ln2_b,
           w_in_ret, gn_ret_g, w_out_ret, w_in_dsa, w_out_dsa,
           w_router, b_router, w_gate, w_up, w_down):
    B, T, _ = x_prompt.shape
    DB = x_sample.shape[0]
    assert x_sample.shape[1] == 1
    n_pages = page_table.shape[1]
    page = cache_k.shape[2]
    past = n_pages * page
    n_pool = cache_k.shape[1]

    wts = dict(w_router=w_router, b_router=b_router, ln2_g=ln2_g, ln2_b=ln2_b,
               w_gate=w_gate.astype(BF16), w_up=w_up.astype(BF16), w_down=w_down.astype(BF16))
    w_mod_bf = w_mod.astype(BF16)
    w_in_ret_bf = w_in_ret[0].astype(BF16)
    w_out_ret_bf = w_out_ret[0].astype(BF16)
    w_in_dsa_bf = jnp.pad(w_in_dsa[0], ((0, 0), (0, DSA_IN_PAD - DSA_IN))).astype(BF16)
    w_out_dsa_bf = w_out_dsa[0].astype(BF16)

    pos_p = jnp.arange(T, dtype=I32)
    pos_s = jnp.full((1,), past, I32)

    tm_p = min(512, T)
    tq = min(128, T)
    xp = x_prompt.reshape(B * T, D_MODEL)
    mod_p = modulation(c_prompt, w_mod_bf, b_mod)
    sh1, sc1, g1, sh2, sc2, g2 = _split_mod(mod_p[0], B, 1)
    proj = mod_proj(xp, sc1, sh1, w_in_ret_bf, min(1024, T), 1536, BF16)
    cos_p, sin_p = _retnet_tables(pos_p)
    gated, ret_p = retention_prompt(proj.reshape(B, T, RET_IN), gn_ret_g[0], cos_p, sin_p, min(256, T))
    xp = out_proj_ln(gated.reshape(B * T, RET_V), w_out_ret_bf, xp, g1, ln1_g[0], ln1_b[0], tm_p)
    xp = _channel_mixer(xp, sc2, sh2, g2, 0, wts, tm_p)

    sh1, sc1, g1, sh2, sc2, g2 = _split_mod(mod_p[1], B, 1)
    k_p, v_p, kw_p, k_bf, kw_bf, v_t, q_t, qi_t, wi_t = dsa_project_prompt(
        xp, sc1, sh1, w_in_dsa_bf, pos_p, B, T, tm_p, tq)
    o_p = dsa_prompt_attend(kw_bf, qi_t, wi_t, k_bf, v_t, q_t, B, T, min(DSA_TOPK, T // 4), tq, tm_p)
    xp = out_proj_ln(o_p, w_out_dsa_bf, xp, g1, ln1_g[1], ln1_b[1], tm_p)
    xp = _channel_mixer(xp, sc2, sh2, g2, 1, wts, tm_p)

    xs = x_sample.reshape(DB, D_MODEL)
    mod_s = modulation(c_sample, w_mod_bf, b_mod)
    sh1, sc1, g1, sh2, sc2, g2 = _split_mod(mod_s[0], 1, DB)
    proj_s = mod_proj(xs, sc1, sh1, w_in_ret_bf, DB, 1536, F32)
    cos_s, sin_s = _retnet_tables(pos_s)
    gated_s, ret_s = retention_sample(proj_s, state_ret[0], gn_ret_g[0], cos_s, sin_s)
    xs = out_proj_ln(gated_s.reshape(DB, RET_V), w_out_ret_bf, xs, g1, ln1_g[0], ln1_b[0], DB)
    xs = _channel_mixer(xs, sc2, sh2, g2, 0, wts, DB)

    sh1, sc1, g1, sh2, sc2, g2 = _split_mod(mod_s[1], 1, DB)
    pos_rows = jnp.full((DB,), past, I32)
    q_s, k_s, v_s, qi_s, kw_s = dsa_project(xs, sc1, sh1, w_in_dsa_bf, pos_rows, DB)
    ki_s = kw_s[:, :IDX_DIM]
    score = dsa_sample_scores(page_table, qi_s, kw_s[:, IDX_DIM:IDX_DIM + IDX_HEADS], ki_s,
                              jnp.swapaxes(cache_kidx[0], 1, 2))
    bias = dsa_sample_select(score.reshape(DB, past + LANES), past + 1, min(DSA_TOPK, (past + 1) // 4))
    o_s = dsa_sample_attend(page_table, q_s, bias, k_s, v_s,
                            cache_k[0].reshape(n_pool, page * DSA_KV_HEADS, DSA_HEAD_DIM),
                            cache_v[0].reshape(n_pool, page * DSA_KV_HEADS, DSA_HEAD_DIM))
    xs = out_proj_ln(o_s, w_out_dsa_bf, xs, g1, ln1_g[1], ln1_b[1], DB)
    xs = _channel_mixer(xs, sc2, sh2, g2, 1, wts, DB)

    kv_shape = (DSA_KV_HEADS, DSA_HEAD_DIM)
    return (xp.reshape(B, T, D_MODEL), xs.reshape(DB, 1, D_MODEL),
            ret_p[None], ret_s[None],
            k_p.reshape(1, B, T, *kv_shape), v_p.reshape(1, B, T, *kv_shape),
            kw_p[:, :IDX_DIM].reshape(1, B, T, IDX_DIM),
            k_s.reshape(1, DB, 1, *kv_shape), v_s.reshape(1, DB, 1, *kv_shape),
            ki_s.reshape(1, DB, 1, IDX_DIM))
```

```python
import functools

import jax
import jax.numpy as jnp
from jax import lax
from jax.experimental import pallas as pl
from jax.experimental.pallas import tpu as pltpu

F32 = jnp.float32
BF16 = jnp.bfloat16
I32 = jnp.int32

D_MODEL = 1024
DEPTH = 2
ALPHA = (2.0 * DEPTH) ** 0.25
LN_EPS = 1e-5
GN_EPS = 1e-6

RET_HEADS = 4
RET_DK = 256
RET_DV = 512
RET_QK = RET_HEADS * RET_DK
RET_V = RET_HEADS * RET_DV
RET_IN = 2 * RET_QK + 2 * RET_V
RET_ANGLE_BASE = 10000.0

DSA_HEADS = 8
DSA_KV_HEADS = 2
DSA_HEAD_DIM = 128
DSA_GROUP = DSA_HEADS // DSA_KV_HEADS
DSA_Q = DSA_HEADS * DSA_HEAD_DIM
DSA_KV = DSA_KV_HEADS * DSA_HEAD_DIM
IDX_HEADS = 16
IDX_DIM = 64
DSA_TOPK = 256
ROPE_THETA = 500000.0
ROT_DIM = DSA_HEAD_DIM // 4
IDX_ROT_DIM = IDX_DIM // 4
DSA_IN = DSA_Q + 2 * DSA_KV + IDX_HEADS * IDX_DIM + IDX_DIM + IDX_HEADS

N_EXPERTS = 16
N_GROUPS = 4
EXPERTS_PER_GROUP = N_EXPERTS // N_GROUPS
D_EXPERT = 512
GROUP_LANE0 = N_EXPERTS

LANES = 128
SUBLANES = 8
VMEM_LIMIT = 56 * 1024 * 1024
NEG_BIG = -1e30
INT_MIN = -2147483648


def _cparams(sem):
    return pltpu.CompilerParams(dimension_semantics=sem, vmem_limit_bytes=VMEM_LIMIT)


def _silu(x):
    return x * (1.0 / (1.0 + jnp.exp(-x)))


def _layer_norm(z, g, b):
    mu = jnp.mean(z, -1, keepdims=True)
    d = z - mu
    var = jnp.mean(d * d, -1, keepdims=True)
    return d * lax.rsqrt(var + LN_EPS) * g + b


def _mod_kernel(c_ref, w_ref, b_ref, o_ref):
    a = _silu(c_ref[...]).astype(BF16)
    o_ref[...] = jnp.dot(a, w_ref[...], preferred_element_type=F32) + b_ref[...]


def modulation(c, w_mod_bf, b_mod):
    R = c.shape[0]
    tn = 1536
    return pl.pallas_call(
        _mod_kernel,
        grid=(DEPTH, 6 * D_MODEL // tn),
        in_specs=[pl.BlockSpec((R, D_MODEL), lambda l, j: (0, 0)),
                  pl.BlockSpec((None, D_MODEL, tn), lambda l, j: (l, 0, j)),
                  pl.BlockSpec((None, 1, tn), lambda l, j: (l, 0, j))],
        out_specs=pl.BlockSpec((None, R, tn), lambda l, j: (l, 0, j)),
        out_shape=jax.ShapeDtypeStruct((DEPTH, R, 6 * D_MODEL), F32),
        compiler_params=_cparams(("parallel", "parallel")),
        name="modulation",
    )(c, w_mod_bf, b_mod.reshape(DEPTH, 1, 6 * D_MODEL))


def _proj_kernel(x_ref, sc_ref, sh_ref, w_ref, o_ref, h_scr):
    @pl.when(pl.program_id(1) == 0)
    def _():
        h_scr[...] = (x_ref[...] * (1.0 + sc_ref[...]) + sh_ref[...]).astype(BF16)

    o_ref[...] = jnp.dot(h_scr[...], w_ref[...], preferred_element_type=F32).astype(o_ref.dtype)


def _mod_spec(R, tiles_per_group):
    return pl.BlockSpec((None, R, D_MODEL), lambda i, *_: (i // tiles_per_group, 0, 0))


def mod_proj(x, sc, sh, w_bf, tm, tn, out_dtype):
    N = x.shape[0]
    G, R, _ = sc.shape
    n_out = w_bf.shape[1]
    tpg = (N // G) // tm
    return pl.pallas_call(
        _proj_kernel,
        grid=(N // tm, n_out // tn),
        in_specs=[pl.BlockSpec((tm, D_MODEL), lambda i, j: (i, 0)),
                  _mod_spec(R, tpg), _mod_spec(R, tpg),
                  pl.BlockSpec((D_MODEL, tn), lambda i, j: (0, j))],
        out_specs=pl.BlockSpec((tm, tn), lambda i, j: (i, j)),
        out_shape=jax.ShapeDtypeStruct((N, n_out), out_dtype),
        scratch_shapes=[pltpu.VMEM((tm, D_MODEL), BF16)],
        compiler_params=_cparams(("parallel", "arbitrary")),
        name="mod_proj",
    )(x, sc, sh, w_bf)


def _rot_half(x, cos, sin):
    half = cos.shape[-1]
    x1, x2 = x[:, :half], x[:, half:]
    return jnp.concatenate([x1 * cos - x2 * sin, x1 * sin + x2 * cos], axis=1)


def _group_norm_gate(o, gn, g):
    mu = jnp.mean(o, -1, keepdims=True)
    d = o - mu
    var = jnp.mean(d * d, -1, keepdims=True)
    return d * lax.rsqrt(var + GN_EPS) * gn * _silu(g)


def _ret_prompt_kernel(q_ref, k_ref, v_ref, g_ref, cos_ref, sin_ref, lg_ref, gn_ref,
                       o_ref, s_ref, s_scr, *, chunk):
    c = pl.program_id(2)

    @pl.when(c == 0)
    def _():
        s_scr[...] = jnp.zeros_like(s_scr)

    cos, sin = cos_ref[...], sin_ref[...]
    lg_row = lg_ref[...]
    lg = lg_row[:, :1]
    q = _rot_half(q_ref[...].astype(F32), cos, sin)
    k = _rot_half(k_ref[...].astype(F32), cos, sin) * (RET_DK ** -0.5)
    vb = v_ref[...].astype(BF16)
    qb = q.astype(BF16)

    row = lax.broadcasted_iota(I32, (chunk, chunk), 0)
    col = lax.broadcasted_iota(I32, (chunk, chunk), 1)
    diff = (row - col).astype(F32)
    decay = jnp.where(diff >= 0, jnp.exp(lg_row * jnp.maximum(diff, 0.0)), 0.0)
    idx = lax.broadcasted_iota(I32, (chunk, 1), 0).astype(F32)
    q_dec = jnp.exp(lg * (idx + 1.0))
    k_dec = jnp.exp(lg * (chunk - 1.0 - idx))
    chunk_dec = jnp.exp(lg * chunk)

    s = s_scr[...]
    att = lax.dot_general(qb, k.astype(BF16), (((1,), (1,)), ((), ())),
                          preferred_element_type=F32) * decay
    o = (jnp.dot(att.astype(BF16), vb, preferred_element_type=F32)
         + jnp.dot(qb, s.astype(BF16), preferred_element_type=F32) * q_dec)
    kd_t = (k * k_dec).T.astype(BF16)
    s_new = chunk_dec * s + jnp.dot(kd_t, vb, preferred_element_type=F32)
    s_scr[...] = s_new
    o_ref[...] = _group_norm_gate(o, gn_ref[...], g_ref[...].astype(F32)).astype(o_ref.dtype)

    @pl.when(c == pl.num_programs(2) - 1)
    def _():
        s_ref[...] = s_new


def _log_gamma():
    return jnp.log(1.0 - jnp.power(2.0, -5.0 - jnp.arange(RET_HEADS, dtype=F32)))


def retention_prompt(proj, gn_g, cos, sin, chunk):
    B, T, _ = proj.shape
    lg_tab = jnp.broadcast_to(_log_gamma()[:, None, None], (RET_HEADS, 1, chunk))
    qk_blk = lambda off: pl.BlockSpec((None, chunk, RET_DK), lambda b, h, c: (b, c, off + h))
    v_blk = lambda off: pl.BlockSpec((None, chunk, RET_DV), lambda b, h, c: (b, c, off + h))
    tab = pl.BlockSpec((chunk, RET_DK // 2), lambda b, h, c: (c, 0))
    return pl.pallas_call(
        functools.partial(_ret_prompt_kernel, chunk=chunk),
        grid=(B, RET_HEADS, T // chunk),
        in_specs=[qk_blk(0), qk_blk(RET_QK // RET_DK),
                  v_blk(2 * RET_QK // RET_DV), v_blk((2 * RET_QK + RET_V) // RET_DV),
                  tab, tab,
                  pl.BlockSpec((None, 1, chunk), lambda b, h, c: (h, 0, 0)),
                  pl.BlockSpec((1, RET_DV), lambda b, h, c: (0, h))],
        out_specs=[pl.BlockSpec((None, chunk, RET_DV), lambda b, h, c: (b, c, h)),
                   pl.BlockSpec((None, None, RET_DK, RET_DV), lambda b, h, c: (b, h, 0, 0))],
        out_shape=[jax.ShapeDtypeStruct((B, T, RET_V), BF16),
                   jax.ShapeDtypeStruct((B, RET_HEADS, RET_DK, RET_DV), F32)],
        scratch_shapes=[pltpu.VMEM((RET_DK, RET_DV), F32)],
        compiler_params=_cparams(("parallel", "parallel", "arbitrary")),
        name="retention_prompt",
    )(proj, proj, proj, proj, cos, sin, lg_tab, gn_g.reshape(1, RET_V))


def _ret_sample_kernel(qk_ref, v_ref, g_ref, s0_ref, cos_ref, sin_ref, gam_ref, gn_ref,
                       o_ref, s_ref):
    t = _rot_half(qk_ref[...], cos_ref[...], sin_ref[...])
    row = lax.broadcasted_iota(I32, t.shape, 0)
    t = jnp.where(row >= RET_HEADS, t * (RET_DK ** -0.5), t)
    pad = jnp.zeros((LANES - 2 * RET_HEADS, RET_DK), F32)
    t_t = jnp.concatenate([t, pad], axis=0).T
    for h in range(RET_HEADS):
        qc = t_t[:, h:h + 1]
        kc = t_t[:, RET_HEADS + h:RET_HEADS + h + 1]
        gam = gam_ref[h]
        vh = v_ref[h:h + 1, :]
        s = s0_ref[h]
        qk_dot = jnp.sum(t[h:h + 1, :] * t[RET_HEADS + h:RET_HEADS + h + 1, :], -1, keepdims=True)
        o = qk_dot * vh + gam * jnp.sum(s * qc, axis=0, keepdims=True)
        s_ref[h] = gam * s + kc * vh
        o_ref[h:h + 1, :] = _group_norm_gate(o, gn_ref[h:h + 1, :], g_ref[h:h + 1, :])


def retention_sample(proj, s0, gn_g, cos, sin):
    DB = proj.shape[0]
    qk = proj[:, :2 * RET_QK].reshape(DB, 2 * RET_HEADS, RET_DK)
    v = proj[:, 2 * RET_QK:2 * RET_QK + RET_V].reshape(DB, RET_HEADS, RET_DV)
    g = proj[:, 2 * RET_QK + RET_V:].reshape(DB, RET_HEADS, RET_DV)
    gam = jnp.broadcast_to(jnp.exp(_log_gamma())[:, None, None], (RET_HEADS, 1, RET_DV))
    full = lambda *shape: pl.BlockSpec(shape, lambda b: (0,) * len(shape))
    per_b = lambda *shape: pl.BlockSpec((None,) + shape, lambda b: (b,) + (0,) * len(shape))
    return pl.pallas_call(
        _ret_sample_kernel,
        grid=(DB,),
        in_specs=[per_b(2 * RET_HEADS, RET_DK), per_b(RET_HEADS, RET_DV), per_b(RET_HEADS, RET_DV),
                  per_b(RET_HEADS, RET_DK, RET_DV),
                  full(1, RET_DK // 2), full(1, RET_DK // 2),
                  full(RET_HEADS, 1, RET_DV), full(RET_HEADS, RET_DV)],
        out_specs=[per_b(RET_HEADS, RET_DV), per_b(RET_HEADS, RET_DK, RET_DV)],
        out_shape=[jax.ShapeDtypeStruct((DB, RET_HEADS, RET_DV), F32),
                   jax.ShapeDtypeStruct((DB, RET_HEADS, RET_DK, RET_DV), F32)],
        compiler_params=_cparams(("parallel",)),
        name="retention_sample",
    )(qk, v, g, s0, cos, sin, gam, gn_g.reshape(RET_HEADS, RET_DV))


def _out_ln_kernel(a_ref, w_ref, x_ref, g_ref, lng_ref, lnb_ref, o_ref):
    y = jnp.dot(a_ref[...].astype(BF16), w_ref[...], preferred_element_type=F32)
    o_ref[...] = _layer_norm(ALPHA * x_ref[...] + g_ref[...] * y, lng_ref[...], lnb_ref[...])


def out_proj_ln(a, w_bf, x, gate, ln_g, ln_b, tm):
    N, K = a.shape
    G, R, _ = gate.shape
    tpg = (N // G) // tm
    row = pl.BlockSpec((1, D_MODEL), lambda i: (0, 0))
    return pl.pallas_call(
        _out_ln_kernel,
        grid=(N // tm,),
        in_specs=[pl.BlockSpec((tm, K), lambda i: (i, 0)),
                  pl.BlockSpec((K, D_MODEL), lambda i: (0, 0)),
                  pl.BlockSpec((tm, D_MODEL), lambda i: (i, 0)),
                  _mod_spec(R, tpg), row, row],
        out_specs=pl.BlockSpec((tm, D_MODEL), lambda i: (i, 0)),
        out_shape=jax.ShapeDtypeStruct((N, D_MODEL), F32),
        compiler_params=_cparams(("parallel",)),
        name="out_proj_ln",
    )(a, w_bf, x, gate, ln_g.reshape(1, D_MODEL), ln_b.reshape(1, D_MODEL))


def _lane_roll(x, shift):
    return pltpu.roll(x, shift, axis=1)


def _router_kernel(x_ref, sc_ref, sh_ref, whi_ref, wlo_ref, b_ref, o_ref):
    h = x_ref[...] * (1.0 + sc_ref[...]) + sh_ref[...]
    hi = h.astype(BF16)
    lo = (h - hi.astype(F32)).astype(BF16)
    whi, wlo = whi_ref[...], wlo_ref[...]
    logits = (jnp.dot(hi, whi, preferred_element_type=F32)
              + jnp.dot(hi, wlo, preferred_element_type=F32)
              + jnp.dot(lo, whi, preferred_element_type=F32)) + b_ref[...]
    lane_i = lax.broadcasted_iota(I32, logits.shape, 1)
    lane = lane_i.astype(F32)
    m = jnp.max(logits, -1, keepdims=True)
    e = jnp.exp(logits - m)
    p = e / jnp.sum(e, -1, keepdims=True)
    s1 = p + _lane_roll(p, 1)
    s2 = p + _lane_roll(p, 2)
    s3 = p + _lane_roll(p, 3)
    pair_max = jnp.maximum(jnp.maximum(jnp.maximum(s1, s2), s3),
                           jnp.maximum(jnp.maximum(_lane_roll(s1, 1), _lane_roll(s2, 1)),
                                       _lane_roll(s1, 2)))
    is_last = ((lane_i % EXPERTS_PER_GROUP) == EXPERTS_PER_GROUP - 1) & (lane_i < N_EXPERTS)
    grp_score = jnp.where(is_last, pair_max, -jnp.inf)
    gmax = jnp.max(grp_score, -1, keepdims=True)
    big = float(4 * LANES)
    sel_last = jnp.min(jnp.where(grp_score == gmax, lane, big), -1, keepdims=True)
    in_grp = (lane <= sel_last) & (lane > sel_last - EXPERTS_PER_GROUP)
    masked = jnp.where(in_grp, p, -jnp.inf)
    v1 = jnp.max(masked, -1, keepdims=True)
    i1 = jnp.min(jnp.where(masked == v1, lane, big), -1, keepdims=True)
    masked2 = jnp.where(lane == i1, -jnp.inf, masked)
    v2 = jnp.max(masked2, -1, keepdims=True)
    i2 = jnp.min(jnp.where(masked2 == v2, lane, big), -1, keepdims=True)
    tot = v1 + v2
    grp_hot = lane == (sel_last - (EXPERTS_PER_GROUP - 1)) * (1.0 / EXPERTS_PER_GROUP) + GROUP_LANE0
    o_ref[...] = (jnp.where(lane == i1, v1 / tot, 0.0) + jnp.where(lane == i2, v2 / tot, 0.0)
                  + jnp.where(grp_hot, 1.0, 0.0))


def router(x, sc, sh, w_router, b_router, tm):
    N = x.shape[0]
    G, R, _ = sc.shape
    tpg = (N // G) // tm
    w_pad = jnp.zeros((D_MODEL, LANES), F32).at[:, :N_EXPERTS].set(w_router)
    w_hi = w_pad.astype(BF16)
    w_lo = (w_pad - w_hi.astype(F32)).astype(BF16)
    b_pad = jnp.full((1, LANES), NEG_BIG, F32).at[0, :N_EXPERTS].set(b_router)
    full = lambda r, c: pl.BlockSpec((r, c), lambda i: (0, 0))
    return pl.pallas_call(
        _router_kernel,
        grid=(N // tm,),
        in_specs=[pl.BlockSpec((tm, D_MODEL), lambda i: (i, 0)), _mod_spec(R, tpg), _mod_spec(R, tpg),
                  full(D_MODEL, LANES), full(D_MODEL, LANES), full(1, LANES)],
        out_specs=pl.BlockSpec((tm, LANES), lambda i: (i, 0)),
        out_shape=jax.ShapeDtypeStruct((N, LANES), F32),
        compiler_params=_cparams(("parallel",)),
        name="router",
    )(x, sc, sh, w_hi, w_lo, b_pad)


def _moe_kernel(x_ref, sc_ref, sh_ref, g2_ref, gates_ref, wg_ref, wu_ref, wd_ref, lng_ref, lnb_ref,
                o_ref, h_scr, acc_scr):
    e = pl.program_id(1)

    @pl.when(e == 0)
    def _():
        h_scr[...] = (x_ref[...] * (1.0 + sc_ref[...]) + sh_ref[...]).astype(BF16)
        acc_scr[...] = jnp.zeros_like(acc_scr)

    hb = h_scr[...]
    a = (_silu(jnp.dot(hb, wg_ref[...], preferred_element_type=F32))
         * jnp.dot(hb, wu_ref[...], preferred_element_type=F32))
    y = jnp.dot(a.astype(BF16), wd_ref[...], preferred_element_type=F32)
    gates = gates_ref[...]
    lane = lax.broadcasted_iota(I32, gates.shape, 1)
    gate_e = jnp.sum(jnp.where(lane == e, gates, 0.0), -1, keepdims=True)
    acc_scr[...] += gate_e * y

    @pl.when(e == pl.num_programs(1) - 1)
    def _():
        z = ALPHA * x_ref[...] + g2_ref[...] * acc_scr[...]
        o_ref[...] = _layer_norm(z, lng_ref[...], lnb_ref[...])


def moe_ln(x, sc, sh, g2, gates, l, wg_bf, wu_bf, wd_bf, ln_g, ln_b, tm):
    N = x.shape[0]
    G, R, _ = sc.shape
    tpg = (N // G) // tm
    ms = pl.BlockSpec((None, R, D_MODEL), lambda i, e: (i // tpg, 0, 0))
    row = pl.BlockSpec((1, D_MODEL), lambda i, e: (0, 0))
    return pl.pallas_call(
        _moe_kernel,
        grid=(N // tm, N_EXPERTS),
        in_specs=[pl.BlockSpec((tm, D_MODEL), lambda i, e: (i, 0)), ms, ms, ms,
                  pl.BlockSpec((tm, LANES), lambda i, e: (i, 0)),
                  pl.BlockSpec((None, None, D_MODEL, D_EXPERT), lambda i, e: (l, e, 0, 0)),
                  pl.BlockSpec((None, None, D_MODEL, D_EXPERT), lambda i, e: (l, e, 0, 0)),
                  pl.BlockSpec((None, None, D_EXPERT, D_MODEL), lambda i, e: (l, e, 0, 0)),
                  row, row],
        out_specs=pl.BlockSpec((tm, D_MODEL), lambda i, e: (i, 0)),
        out_shape=jax.ShapeDtypeStruct((N, D_MODEL), F32),
        scratch_shapes=[pltpu.VMEM((tm, D_MODEL), BF16), pltpu.VMEM((tm, D_MODEL), F32)],
        compiler_params=_cparams(("parallel", "arbitrary")),
        name="moe_ln",
    )(x, sc, sh, g2, gates, wg_bf, wu_bf, wd_bf, ln_g.reshape(1, D_MODEL), ln_b.reshape(1, D_MODEL))


def _split3_bf16(x):
    hi = x.astype(BF16)
    r = x - hi.astype(F32)
    mid = r.astype(BF16)
    lo = (r - mid.astype(F32)).astype(BF16)
    return hi, mid, lo


def _moe_group_kernel(x_ref, sc_ref, sh_ref, g2_ref, gates_ref, ltri_ref, wg_ref, wu_ref, wd_ref,
                      lng_ref, lnb_ref, o_ref,
                      h_scr, acc_scr, rank_scr, rank_t_scr, sel_t_scr, xs_scr, gs_scr, yacc_scr, cnt_smem,
                      *, cap):
    g, e = pl.program_id(1), pl.program_id(2)
    tm = x_ref.shape[0]
    lane = lax.broadcasted_iota(I32, (tm, LANES), 1)

    @pl.when((g == 0) & (e == 0))
    def _():
        h_scr[...] = (x_ref[...] * (1.0 + sc_ref[...]) + sh_ref[...]).astype(BF16)
        acc_scr[...] = jnp.zeros_like(acc_scr)
        gates = gates_ref[...]
        sel = jnp.where((lane >= GROUP_LANE0) & (lane < GROUP_LANE0 + N_GROUPS), gates, 0.0)
        rank = jnp.dot(ltri_ref[...], sel.astype(BF16), preferred_element_type=F32)
        rank_scr[...] = rank
        rank_t_scr[...] = rank.T
        sel_t_scr[...] = sel.T
        for gg in range(N_GROUPS):
            cnt_smem[gg] = jnp.sum(jnp.where(lane == GROUP_LANE0 + gg, sel, 0.0)).astype(I32)

    half = cap // 2
    cnt = cnt_smem[g]
    rem = cnt % cap
    use_tail = (rem > 0) & (rem <= half)
    n_full = cnt // cap + jnp.where(rem > half, 1, 0)
    tail_start = pl.multiple_of(n_full * cap, half)
    grp_row = pl.ds(GROUP_LANE0 + g, 1)

    def for_chunks(fn):
        def body(c, carry):
            fn(pl.multiple_of(c * cap, cap), cap)
            return carry
        lax.fori_loop(0, n_full, body, 0)

        @pl.when(use_tail)
        def _():
            fn(tail_start, half)

    @pl.when(e == 0)
    def _():
        expert_lanes = jnp.where(lane < N_EXPERTS, gates_ref[...], 0.0)
        parts = _split3_bf16(expert_lanes)
        g_parts = sum(_lane_roll(part.astype(F32), j * N_EXPERTS)
                      for j, part in enumerate(parts)).astype(BF16)
        rank_row, sel_row = rank_t_scr[grp_row, :], sel_t_scr[grp_row, :]

        def dispatch(start, size):
            rows = pl.ds(start, size)
            slot = (start + lax.broadcasted_iota(I32, (size, 1), 0)).astype(F32)
            p = jnp.where((rank_row == slot) & (sel_row > 0.5), 1.0, 0.0).astype(BF16)
            xs_scr[rows, :] = jnp.dot(p, h_scr[...], preferred_element_type=F32).astype(BF16)
            gs_scr[rows, :] = jnp.dot(p, g_parts, preferred_element_type=F32)
            yacc_scr[rows, :] = jnp.zeros((size, D_MODEL), F32)

        for_chunks(dispatch)

    def expert(start, size):
        rows = pl.ds(start, size)
        xs = xs_scr[rows, :]
        a = (_silu(jnp.dot(xs, wg_ref[...], preferred_element_type=F32))
             * jnp.dot(xs, wu_ref[...], preferred_element_type=F32))
        y = jnp.dot(a.astype(BF16), wd_ref[...], preferred_element_type=F32)
        gs = gs_scr[rows, :]
        lane_c = lax.broadcasted_iota(I32, gs.shape, 1)
        is_part = ((lane_c % N_EXPERTS) == g * EXPERTS_PER_GROUP + e) & (lane_c < 3 * N_EXPERTS)
        gate = jnp.sum(jnp.where(is_part, gs, 0.0), -1, keepdims=True)
        yacc_scr[rows, :] += gate * y

    for_chunks(expert)

    @pl.when(e == EXPERTS_PER_GROUP - 1)
    def _():
        in_grp = lane == GROUP_LANE0 + g
        rank_col = jnp.sum(jnp.where(in_grp, rank_scr[...], 0.0), -1, keepdims=True)
        sel_col = jnp.sum(jnp.where(in_grp, gates_ref[...], 0.0), -1, keepdims=True)

        def combine(start, size):
            slot = (start + lax.broadcasted_iota(I32, (1, size), 1)).astype(F32)
            p_t = jnp.where((rank_col == slot) & (sel_col > 0.5), 1.0, 0.0).astype(BF16)
            y_hi, y_mid, _ = _split3_bf16(yacc_scr[pl.ds(start, size), :])
            acc_scr[...] += (jnp.dot(p_t, y_hi, preferred_element_type=F32)
                             + jnp.dot(p_t, y_mid, preferred_element_type=F32))

        for_chunks(combine)

    @pl.when((g == N_GROUPS - 1) & (e == EXPERTS_PER_GROUP - 1))
    def _():
        z = ALPHA * x_ref[...] + g2_ref[...] * acc_scr[...]
        o_ref[...] = _layer_norm(z, lng_ref[...], lnb_ref[...])


def moe_group_ln(x, sc, sh, g2, gates, l, wg_bf, wu_bf, wd_bf, ln_g, ln_b, tm, cap):
    N = x.shape[0]
    G, R, _ = sc.shape
    tpg = (N // G) // tm
    ltri = (jnp.arange(tm)[:, None] > jnp.arange(tm)[None, :]).astype(BF16)
    ms = pl.BlockSpec((None, R, D_MODEL), lambda i, g, e: (i // tpg, 0, 0))
    row = pl.BlockSpec((1, D_MODEL), lambda i, g, e: (0, 0))
    expert = lambda i, g, e: (l, g * EXPERTS_PER_GROUP + e, 0, 0)
    w_in = pl.BlockSpec((None, None, D_MODEL, D_EXPERT), expert)
    w_out = pl.BlockSpec((None, None, D_EXPERT, D_MODEL), expert)
    return pl.pallas_call(
        functools.partial(_moe_group_kernel, cap=cap),
        grid=(N // tm, N_GROUPS, EXPERTS_PER_GROUP),
        in_specs=[pl.BlockSpec((tm, D_MODEL), lambda i, g, e: (i, 0)), ms, ms, ms,
                  pl.BlockSpec((tm, LANES), lambda i, g, e: (i, 0)),
                  pl.BlockSpec((tm, tm), lambda i, g, e: (0, 0)),
                  w_in, w_in, w_out, row, row],
        out_specs=pl.BlockSpec((tm, D_MODEL), lambda i, g, e: (i, 0)),
        out_shape=jax.ShapeDtypeStruct((N, D_MODEL), F32),
        scratch_shapes=[pltpu.VMEM((tm, D_MODEL), BF16), pltpu.VMEM((tm, D_MODEL), F32),
                        pltpu.VMEM((tm, LANES), F32), pltpu.VMEM((LANES, tm), F32),
                        pltpu.VMEM((LANES, tm), F32), pltpu.VMEM((tm, D_MODEL), BF16),
                        pltpu.VMEM((tm, LANES), F32), pltpu.VMEM((tm, D_MODEL), F32),
                        pltpu.SMEM((N_GROUPS,), I32)],
        compiler_params=_cparams(("arbitrary", "arbitrary", "arbitrary")),
        name="moe_group_ln",
    )(x, sc, sh, g2, gates, ltri, wg_bf, wu_bf, wd_bf, ln_g.reshape(1, D_MODEL), ln_b.reshape(1, D_MODEL))


LOG2_E = 1.4426950408889634
Q_PRESCALE = DSA_HEAD_DIM ** -0.5 * LOG2_E
DSA_IN_PAD = -(-DSA_IN // LANES) * LANES
DSA_QI_OFF = DSA_Q + 2 * DSA_KV
DSA_KW_OFF = DSA_QI_OFF + IDX_HEADS * IDX_DIM


def _rope_tables(pos, rot_dim, period, n_periods):
    half = rot_dim // 2
    inv_freq = 1.0 / (ROPE_THETA ** (jnp.arange(0, rot_dim, 2, dtype=F32) / rot_dim))
    ang = pos.astype(F32)[:, None] * inv_freq[None, :]
    cos, sin = jnp.cos(ang), jnp.sin(ang)
    T = pos.shape[0]
    zeros = jnp.zeros((T, period - 2 * half), F32)
    zh = jnp.zeros((T, half), F32)
    a1 = jnp.concatenate([cos, cos, zeros + 1.0], 1)
    b1 = jnp.concatenate([zh, sin, zeros], 1)
    c1 = jnp.concatenate([-sin, zh, zeros], 1)
    rest = LANES - n_periods * period
    pad1 = jnp.ones((T, rest), F32)
    pad0 = jnp.zeros((T, rest), F32)
    a = jnp.concatenate([a1] * n_periods + [pad1], 1)
    b = jnp.concatenate([b1] * n_periods + [pad0], 1)
    c = jnp.concatenate([c1] * n_periods + [pad0], 1)
    return a, b, c


def _apply_rope(x, a, b, c, half):
    return x * a + _lane_roll(x, half) * b + _lane_roll(x, LANES - half) * c


N_ROPE_TABLES = 9


def _dsa_sections(x_ref, sc_ref, sh_ref, w_ref, tab_refs):
    hb = (x_ref[...] * (1.0 + sc_ref[...]) + sh_ref[...]).astype(BF16)
    ma, mb, mc, ia, ib, ic, ka, kb, kc = (t[...] for t in tab_refs)

    def sec(off, width):
        return jnp.dot(hb, w_ref[:, off:off + width], preferred_element_type=F32)

    def lanes(x, h):
        return x[:, h * LANES:(h + 1) * LANES]

    q = sec(0, DSA_Q)
    q_heads = [_apply_rope(lanes(q, h), ma, mb, mc, ROT_DIM // 2) * Q_PRESCALE for h in range(DSA_HEADS)]
    k = sec(DSA_Q, DSA_KV)
    k = jnp.concatenate([_apply_rope(lanes(k, h), ma, mb, mc, ROT_DIM // 2) for h in range(DSA_KV_HEADS)], 1)
    v = sec(DSA_Q + DSA_KV, DSA_KV)
    qi = sec(DSA_QI_OFF, IDX_HEADS * IDX_DIM)
    qi_pairs = [_apply_rope(lanes(qi, h), ia, ib, ic, IDX_ROT_DIM // 2)
                for h in range(IDX_HEADS * IDX_DIM // LANES)]
    kw = _apply_rope(sec(DSA_KW_OFF, LANES), ka, kb, kc, IDX_ROT_DIM // 2)
    return q_heads, k, v, qi_pairs, kw


def _dsa_proj_kernel(x_ref, sc_ref, sh_ref, w_ref, *rest):
    tabs = rest[:N_ROPE_TABLES]
    q_ref, k_ref, v_ref, qi_ref, kw_ref = rest[N_ROPE_TABLES:]
    q_heads, k, v, qi_pairs, kw = _dsa_sections(x_ref, sc_ref, sh_ref, w_ref, tabs)
    for h, qh in enumerate(q_heads):
        q_ref[:, h * LANES:(h + 1) * LANES] = qh.astype(q_ref.dtype)
    for h, qp in enumerate(qi_pairs):
        qi_ref[:, h * LANES:(h + 1) * LANES] = qp.astype(qi_ref.dtype)
    k_ref[...] = k
    v_ref[...] = v
    kw_ref[...] = kw


def _dsa_proj_prompt_kernel(x_ref, sc_ref, sh_ref, w_ref, *rest, tq):
    tabs = rest[:N_ROPE_TABLES]
    k_ref, v_ref, kw_ref, kbf_ref, kwbf_ref, vt_ref, qt_ref, qit_ref, wit_ref = rest[N_ROPE_TABLES:]
    q_heads, k, v, qi_pairs, kw = _dsa_sections(x_ref, sc_ref, sh_ref, w_ref, tabs)
    tm = k.shape[0]
    k_ref[...] = k
    kbf_ref[...] = k.astype(BF16)
    v_ref[...] = v
    kw_ref[...] = kw
    kwbf_ref[...] = kw.astype(BF16)
    for n in range(DSA_KV_HEADS):
        vt_ref[n, 0] = v[:, n * DSA_HEAD_DIM:(n + 1) * DSA_HEAD_DIM].T.astype(BF16)
    heads_per_pair = LANES // IDX_DIM
    for j in range(tm // tq):
        rows = slice(j * tq, (j + 1) * tq)
        for h, qh in enumerate(q_heads):
            n, g = divmod(h, DSA_GROUP)
            qt_ref[j, n, :, g * tq:(g + 1) * tq] = qh[rows].T.astype(BF16)
        for hp, qp in enumerate(qi_pairs):
            t = qp[rows].T.astype(BF16)
            for s in range(heads_per_pair):
                h = hp * heads_per_pair + s
                qit_ref[j, :IDX_DIM, h * tq:(h + 1) * tq] = t[s * IDX_DIM:(s + 1) * IDX_DIM]
        qit_ref[j, IDX_DIM:, :] = jnp.zeros((LANES - IDX_DIM, IDX_HEADS * tq), BF16)
        wit_ref[j] = kw[rows].T[IDX_DIM:IDX_DIM + IDX_HEADS]


def _dsa_project_call(body, x, sc, sh, w_pad_bf, pos, tm, out_specs, out_shape, name):
    N = x.shape[0]
    G, R, _ = sc.shape
    tpg = (N // G) // tm
    n_tab = pos.shape[0] // tm
    tabs = (_rope_tables(pos, ROT_DIM, DSA_HEAD_DIM, 1)
            + _rope_tables(pos, IDX_ROT_DIM, IDX_DIM, 2)
            + _rope_tables(pos, IDX_ROT_DIM, IDX_DIM, 1))
    tab = pl.BlockSpec((tm, LANES), lambda i: (i % n_tab, 0))
    return pl.pallas_call(
        body,
        grid=(N // tm,),
        in_specs=[pl.BlockSpec((tm, D_MODEL), lambda i: (i, 0)), _mod_spec(R, tpg), _mod_spec(R, tpg),
                  pl.BlockSpec((D_MODEL, DSA_IN_PAD), lambda i: (0, 0))] + [tab] * N_ROPE_TABLES,
        out_specs=out_specs, out_shape=out_shape,
        compiler_params=_cparams(("parallel",)),
        name=name,
    )(x, sc, sh, w_pad_bf, *tabs)


def dsa_project(x, sc, sh, w_pad_bf, pos, tm):
    N = x.shape[0]
    out = lambda w: pl.BlockSpec((tm, w), lambda i: (i, 0))
    return _dsa_project_call(
        _dsa_proj_kernel, x, sc, sh, w_pad_bf, pos, tm,
        [out(DSA_Q), out(DSA_KV), out(DSA_KV), out(IDX_HEADS * IDX_DIM), out(LANES)],
        [jax.ShapeDtypeStruct((N, DSA_Q), BF16), jax.ShapeDtypeStruct((N, DSA_KV), F32),
         jax.ShapeDtypeStruct((N, DSA_KV), F32), jax.ShapeDtypeStruct((N, IDX_HEADS * IDX_DIM), BF16),
         jax.ShapeDtypeStruct((N, LANES), F32)], "dsa_project")


def dsa_project_prompt(x, sc, sh, w_pad_bf, pos, B, T, tm, tq):
    N = B * T
    tiles, n_qt = T // tm, tm // tq
    gq = DSA_GROUP * tq
    out = lambda w: pl.BlockSpec((tm, w), lambda i: (i, 0))
    by_tile = lambda *shape: pl.BlockSpec((None,) + shape,
                                          lambda i: (i // tiles, i % tiles) + (0,) * (len(shape) - 1))
    return _dsa_project_call(
        functools.partial(_dsa_proj_prompt_kernel, tq=tq), x, sc, sh, w_pad_bf, pos, tm,
        [out(DSA_KV), out(DSA_KV), out(LANES), out(DSA_KV), out(LANES),
         pl.BlockSpec((None, DSA_KV_HEADS, 1, DSA_HEAD_DIM, tm), lambda i: (i // tiles, 0, i % tiles, 0, 0)),
         by_tile(n_qt, DSA_KV_HEADS, DSA_HEAD_DIM, gq),
         by_tile(n_qt, LANES, IDX_HEADS * tq),
         by_tile(n_qt, IDX_HEADS, tq)],
        [jax.ShapeDtypeStruct((N, DSA_KV), F32), jax.ShapeDtypeStruct((N, DSA_KV), F32),
         jax.ShapeDtypeStruct((N, LANES), F32), jax.ShapeDtypeStruct((N, DSA_KV), BF16),
         jax.ShapeDtypeStruct((N, LANES), BF16),
         jax.ShapeDtypeStruct((B, DSA_KV_HEADS, tiles, DSA_HEAD_DIM, tm), BF16),
         jax.ShapeDtypeStruct((B, T // tq, DSA_KV_HEADS, DSA_HEAD_DIM, gq), BF16),
         jax.ShapeDtypeStruct((B, T // tq, LANES, IDX_HEADS * tq), BF16),
         jax.ShapeDtypeStruct((B, T // tq, IDX_HEADS, tq), F32)], "dsa_project_prompt")


CODE_NEG_INF = -2139095041
BISECT_FIRST = 20
BISECT_STAGE = 4


def _threshold_of_code(code):
    bits = code ^ ((code >> 31) & jnp.int32(0x7FFFFFFF))
    return jnp.where(code < CODE_NEG_INF, -jnp.inf, pltpu.bitcast(bits, F32))


def _reduce_row_groups(x, op, n_chains=8):
    parts = [x[r:r + SUBLANES] for r in range(0, x.shape[0], SUBLANES)]
    accs = parts[:n_chains]
    for j, part in enumerate(parts[n_chains:]):
        accs[j % len(accs)] = op(accs[j % len(accs)], part)
    while len(accs) > 1:
        accs = [op(a, b) for a, b in zip(accs[0::2], accs[1::2])] + (accs[-1:] if len(accs) % 2 else [])
    return accs[0]


def _sum_row_groups(x):
    return _reduce_row_groups(x, jnp.add)


def _max_row_groups(x):
    return _reduce_row_groups(x, jnp.maximum)


def _kth_largest_threshold(count_ge, shape, k):
    def body(s, p):
        cand = p + lax.shift_left(jnp.int32(1), 31 - s)
        return jnp.where(count_ge(_threshold_of_code(cand)) >= k, cand, p)
    return _threshold_of_code(lax.fori_loop(0, 32, body, jnp.full(shape, INT_MIN, I32)))


def _tie_index_bound(count_eq_below, shape, need, n_bits):
    def body(s, m):
        cand = m + lax.shift_left(jnp.int32(1), n_bits - 1 - s)
        return jnp.where(count_eq_below(cand) < need, cand, m)
    return lax.fori_loop(0, n_bits, body, jnp.zeros(shape, I32))


def _dsa_prompt_kernel(ki_ref, qit_ref, wit_ref, k_ref, vt_ref, qt_ref, o_ref,
                       score_scr, bias_scr, midx_scr, logit_scr, thr_scr, nge_scr,
                       *, tq, kb_size, cb_size, topk, idx_bits):
    i = pl.program_id(1)
    n_kb = ((i + 1) * tq + kb_size - 1) // kb_size
    n_cb = ((i + 1) * tq + cb_size - 1) // cb_size
    q_pos = i * tq + lax.broadcasted_iota(I32, (1, tq), 1)
    heads_per_dot = 2

    @pl.when(i == 0)
    def _():
        score_scr[...] = jnp.full(score_scr.shape, -jnp.inf, F32)

    def key_rows(kb):
        return pl.ds(pl.multiple_of(kb * kb_size, kb_size), kb_size)

    def l_index(kb, size=kb_size):
        return kb * size + lax.broadcasted_iota(I32, (size, 1), 0)

    def score_body(kb, carry):
        kib = ki_ref[key_rows(kb), :]
        acc = jnp.zeros((kb_size, tq), F32)
        for hp in range(IDX_HEADS // heads_per_dot):
            s = jnp.dot(kib, qit_ref[:, hp * heads_per_dot * tq:(hp + 1) * heads_per_dot * tq],
                        preferred_element_type=F32)
            for j in range(heads_per_dot):
                h = hp * heads_per_dot + j
                w = wit_ref[h:h + 1, :] * (IDX_HEADS ** -0.5 * IDX_DIM ** -0.5)
                acc = acc + jnp.maximum(s[:, j * tq:(j + 1) * tq], 0.0) * w
        allowed = l_index(kb) <= q_pos
        score_scr[key_rows(kb), :] = jnp.where(allowed, acc, -jnp.inf)
        return carry

    lax.fori_loop(0, n_kb, score_body, 0)

    def count(pred_fn):
        def body(cb, acc):
            rows = pl.ds(pl.multiple_of(cb * cb_size, cb_size), cb_size)
            m = jnp.where(pred_fn(score_scr[rows, :], cb), 1.0, 0.0)
            return acc + _sum_row_groups(m)
        acc = lax.fori_loop(0, n_cb, body, jnp.zeros((SUBLANES, tq), F32))
        return acc.sum(axis=0, keepdims=True)

    k_f = float(topk)

    def bisect(s_lo, s_hi):
        def body(s, carry):
            p, n_p = carry
            cand = p + lax.shift_left(jnp.int32(1), 31 - s)
            cand_thr = _threshold_of_code(cand)
            n_c = count(lambda score, cb: score >= cand_thr)
            take = n_c >= k_f
            return jnp.where(take, cand, p), jnp.where(take, n_c, n_p)
        p, n_p = lax.fori_loop(s_lo, s_hi, body, (thr_scr[...], nge_scr[...]))
        thr_scr[...] = p
        nge_scr[...] = n_p

    thr_scr[...] = jnp.full((1, tq), INT_MIN, I32)
    nge_scr[...] = jnp.full((1, tq), float(2 ** 30), F32)
    bisect(0, BISECT_FIRST)
    for s0 in range(BISECT_FIRST, 32, BISECT_STAGE):
        @pl.when(jnp.max(jnp.where(nge_scr[...] != k_f, 1.0, 0.0)) > 0.5)
        def _():
            bisect(s0, min(32, s0 + BISECT_STAGE))

    thr = _threshold_of_code(thr_scr[...])
    n_ge = nge_scr[...]
    n_gt = count(lambda score, cb: score > thr)
    need = k_f - n_gt
    excess = (n_ge - n_gt > need) & (thr > -jnp.inf)
    midx_scr[...] = jnp.full((1, tq), 2 ** idx_bits, I32)

    @pl.when(jnp.max(jnp.where(excess, 1.0, 0.0)) > 0.5)
    def _():
        m = _tie_index_bound(
            lambda c: count(lambda score, cb: (score == thr) & (l_index(cb, cb_size) < c)),
            (1, tq), need, idx_bits)
        midx_scr[...] = jnp.where(excess, m, 2 ** idx_bits)

    midx = midx_scr[...]

    def bias_body(kb, carry):
        score = score_scr[key_rows(kb), :]
        l = l_index(kb)
        sel = ((score > thr) | ((score == thr) & (l <= midx))) & (l <= q_pos)
        bias_scr[key_rows(kb), :] = jnp.where(sel, 0.0, NEG_BIG)
        return carry

    lax.fori_loop(0, n_kb, bias_body, 0)

    gq = DSA_GROUP * tq
    heads = range(DSA_KV_HEADS)

    def logits_of(kb):
        bias = bias_scr[key_rows(kb), :]
        bias = jnp.concatenate([bias] * DSA_GROUP, axis=1)
        blk_max = []
        for n in heads:
            kblk = k_ref[key_rows(kb), n * DSA_HEAD_DIM:(n + 1) * DSA_HEAD_DIM]
            logits = jnp.dot(kblk, qt_ref[n], preferred_element_type=F32) + bias
            logit_scr[n, key_rows(kb), :] = logits
            blk_max.append(_max_row_groups(logits))
        return tuple(blk_max)

    def absorb(kb, blk_max, state):
        new = []
        for n in heads:
            m_run, l_run, acc = state[n]
            m_new = jnp.maximum(m_run, jnp.max(blk_max[n], axis=0, keepdims=True))
            alpha = jnp.exp2(m_run - m_new)
            p = jnp.exp2(logit_scr[n, key_rows(kb), :] - m_new)
            acc = acc * alpha + jnp.dot(vt_ref[n, kb], p.astype(BF16), preferred_element_type=F32)
            new.append((m_new, l_run * alpha + _sum_row_groups(p), acc))
        return tuple(new)

    def att_body(kb, carry):
        blk_max, state = carry
        state = absorb(kb - 1, blk_max, state)
        return logits_of(kb), state

    state0 = tuple((jnp.full((1, gq), NEG_BIG, F32), jnp.zeros((SUBLANES, gq), F32),
                    jnp.zeros((DSA_HEAD_DIM, gq), F32)) for _ in heads)
    last_max, state = lax.fori_loop(1, n_kb, att_body, (logits_of(0), state0))
    fin = absorb(n_kb - 1, last_max, state)
    for n in heads:
        _, l8, acc = fin[n]
        o_t = acc / jnp.sum(l8, axis=0, keepdims=True)
        for g in range(DSA_GROUP):
            h = n * DSA_GROUP + g
            o_ref[:, h * DSA_HEAD_DIM:(h + 1) * DSA_HEAD_DIM] = (
                o_t[:, g * tq:(g + 1) * tq].T.astype(o_ref.dtype))


def dsa_prompt_attend(kw_bf, qi_t, wi_t, k_bf, v_t, q_t, B, T, topk, tq, kb_size):
    nqt = T // tq
    nkb = T // kb_size
    gq = DSA_GROUP * tq
    return pl.pallas_call(
        functools.partial(_dsa_prompt_kernel, tq=tq, kb_size=kb_size, cb_size=min(512, T), topk=topk,
                          idx_bits=max(1, (T - 1).bit_length())),
        grid=(B, nqt),
        in_specs=[pl.BlockSpec((None, T, LANES), lambda b, i: (b, 0, 0)),
                  pl.BlockSpec((None, None, LANES, IDX_HEADS * tq), lambda b, i: (b, i, 0, 0)),
                  pl.BlockSpec((None, None, IDX_HEADS, tq), lambda b, i: (b, i, 0, 0)),
                  pl.BlockSpec((None, T, DSA_KV), lambda b, i: (b, 0, 0)),
                  pl.BlockSpec((None, DSA_KV_HEADS, nkb, DSA_HEAD_DIM, kb_size),
                               lambda b, i: (b, 0, 0, 0, 0)),
                  pl.BlockSpec((None, None, DSA_KV_HEADS, DSA_HEAD_DIM, gq),
                               lambda b, i: (b, i, 0, 0, 0))],
        out_specs=pl.BlockSpec((tq, DSA_Q), lambda b, i: (b * nqt + i, 0)),
        out_shape=jax.ShapeDtypeStruct((B * T, DSA_Q), BF16),
        scratch_shapes=[pltpu.VMEM((T, tq), F32), pltpu.VMEM((T, tq), F32), pltpu.VMEM((1, tq), I32),
                        pltpu.VMEM((DSA_KV_HEADS, T, gq), F32),
                        pltpu.VMEM((1, tq), I32), pltpu.VMEM((1, tq), F32)],
        compiler_params=_cparams(("arbitrary", "arbitrary")),
        name="dsa_prompt_attend",
    )(kw_bf.reshape(B, T, LANES), qi_t, wi_t, k_bf.reshape(B, T, DSA_KV), v_t, q_t)


def _dsa_sample_score_kernel(pt_ref, qi_ref, wi_ref, kis_ref, *rest, n_pages, page):
    pages, o_ref = rest[:n_pages], rest[n_pages]
    qi = qi_ref[...]
    w = wi_ref[...] * (IDX_HEADS ** -0.5 * IDX_DIM ** -0.5)
    for j in range(n_pages):
        dots = jnp.dot(qi, pages[j][...].astype(BF16), preferred_element_type=F32)
        o_ref[:, j * page:(j + 1) * page] = jnp.sum(jnp.maximum(dots, 0.0) * w, axis=0, keepdims=True)
    ki_self = kis_ref[...].astype(BF16).astype(F32)
    d_self = jnp.sum(qi.astype(F32) * ki_self, -1, keepdims=True)
    s_self = jnp.sum(jnp.maximum(d_self, 0.0) * w, axis=0, keepdims=True)
    lane = lax.broadcasted_iota(I32, (1, LANES), 1)
    o_ref[:, n_pages * page:] = jnp.where(lane == 0, s_self, -jnp.inf)


def _page_specs(n_pages, page, width):
    return [pl.BlockSpec((None, page, width), functools.partial(lambda j, b, pt: (pt[b, j], 0, 0), j))
            for j in range(n_pages)]


def dsa_sample_scores(page_table, qi_bf, wi, ki_self, cache_kidx_t):
    DB, n_pages = page_table.shape
    page = cache_kidx_t.shape[2]
    lp = n_pages * page + LANES
    per_b = lambda *shape: pl.BlockSpec((None,) + shape, lambda b, pt: (b,) + (0,) * len(shape))
    return pl.pallas_call(
        functools.partial(_dsa_sample_score_kernel, n_pages=n_pages, page=page),
        grid_spec=pltpu.PrefetchScalarGridSpec(
            num_scalar_prefetch=1, grid=(DB,),
            in_specs=[per_b(IDX_HEADS, IDX_DIM), per_b(IDX_HEADS, 1), per_b(1, IDX_DIM)]
            + _page_specs(n_pages, IDX_DIM, page),
            out_specs=per_b(1, lp)),
        out_shape=jax.ShapeDtypeStruct((DB, 1, lp), F32),
        compiler_params=_cparams(("arbitrary",)),
        name="dsa_sample_scores",
    )(page_table, qi_bf.reshape(DB, IDX_HEADS, IDX_DIM), wi.reshape(DB, IDX_HEADS, 1),
      ki_self.reshape(DB, 1, IDX_DIM), *([cache_kidx_t] * n_pages))


def _dsa_sample_select_kernel(s_ref, o_ref, *, n_keys, topk, idx_bits):
    score = s_ref[...]
    lane = lax.broadcasted_iota(I32, score.shape, 1)
    valid = lane < n_keys
    score = jnp.where(valid, score, -jnp.inf)

    def count(pred):
        return jnp.sum(jnp.where(pred, 1.0, 0.0), -1, keepdims=True)

    shape = (score.shape[0], 1)
    thr = _kth_largest_threshold(lambda t: count(score >= t), shape, float(topk))
    n_gt = count(score > thr)
    need = float(topk) - n_gt
    excess = (count(score == thr) > need) & (thr > -jnp.inf)
    m = _tie_index_bound(lambda c: count((score == thr) & (lane < c)), shape, need, idx_bits)
    midx = jnp.where(excess, m, 2 ** idx_bits)
    sel = ((score > thr) | ((score == thr) & (lane <= midx))) & valid
    o_ref[...] = jnp.where(sel, 0.0, NEG_BIG)


def dsa_sample_select(score, n_keys, topk):
    DB, lp = score.shape
    return pl.pallas_call(
        functools.partial(_dsa_sample_select_kernel, n_keys=n_keys, topk=topk,
                          idx_bits=max(1, (lp - 1).bit_length())),
        grid=(1,),
        in_specs=[pl.BlockSpec((DB, lp), lambda i: (0, 0))],
        out_specs=pl.BlockSpec((DB, lp), lambda i: (0, 0)),
        out_shape=jax.ShapeDtypeStruct((DB, lp), F32),
        compiler_params=_cparams(("arbitrary",)),
        name="dsa_sample_select",
    )(score)


def _dsa_sample_attend_kernel(pt_ref, q_ref, bias_ref, ks_ref, vs_ref, *rest, n_pages, page):
    k_pages, v_pages = rest[:n_pages], rest[n_pages:2 * n_pages]
    o_ref, logit_scr = rest[2 * n_pages], rest[2 * n_pages + 1]
    q = q_ref[...]
    tail = n_pages * page
    lane = lax.broadcasted_iota(I32, (DSA_GROUP, LANES), 1)
    for n in range(DSA_KV_HEADS):
        hs = slice(n * DSA_GROUP, (n + 1) * DSA_GROUP)
        ds = slice(n * DSA_HEAD_DIM, (n + 1) * DSA_HEAD_DIM)
        head_rows = pl.ds(n, page, stride=DSA_KV_HEADS)
        qn = q[hs, :]
        for j in range(n_pages):
            lg = lax.dot_general(qn, k_pages[j][head_rows, :].astype(BF16), (((1,), (1,)), ((), ())),
                                 preferred_element_type=F32)
            logit_scr[hs, j * page:(j + 1) * page] = lg + bias_ref[:, j * page:(j + 1) * page]
        k_self = ks_ref[:, ds].astype(BF16).astype(F32)
        lg_self = jnp.sum(qn.astype(F32) * k_self, -1, keepdims=True)
        logit_scr[hs, tail:] = jnp.where(lane == 0, lg_self, 0.0) + bias_ref[:, tail:]
        logits = logit_scr[hs, :]
        m = jnp.max(logits, -1, keepdims=True)
        p = jnp.exp2(logits - m)
        p_bf = (p / jnp.sum(p, -1, keepdims=True)).astype(BF16)
        v_self = vs_ref[:, ds].astype(BF16).astype(F32)
        acc = p_bf[:, tail:].astype(F32)[:, :1] * v_self
        for j in range(n_pages):
            acc = acc + jnp.dot(p_bf[:, j * page:(j + 1) * page], v_pages[j][head_rows, :].astype(BF16),
                                preferred_element_type=F32)
        o_ref[hs, :] = acc


def dsa_sample_attend(page_table, q_bf, bias, k_self, v_self, cache_k, cache_v):
    DB, n_pages = page_table.shape
    rows = cache_k.shape[1]
    page = rows // DSA_KV_HEADS
    lp = n_pages * page + LANES
    per_b = lambda *shape: pl.BlockSpec((None,) + shape, lambda b, pt: (b,) + (0,) * len(shape))
    o = pl.pallas_call(
        functools.partial(_dsa_sample_attend_kernel, n_pages=n_pages, page=page),
        grid_spec=pltpu.PrefetchScalarGridSpec(
            num_scalar_prefetch=1, grid=(DB,),
            in_specs=[per_b(DSA_HEADS, DSA_HEAD_DIM), per_b(1, lp), per_b(1, DSA_KV), per_b(1, DSA_KV)]
            + _page_specs(n_pages, rows, DSA_HEAD_DIM) + _page_specs(n_pages, rows, DSA_HEAD_DIM),
            out_specs=per_b(DSA_HEADS, DSA_HEAD_DIM),
            scratch_shapes=[pltpu.VMEM((DSA_HEADS, lp), F32)]),
        out_shape=jax.ShapeDtypeStruct((DB, DSA_HEADS, DSA_HEAD_DIM), F32),
        compiler_params=_cparams(("arbitrary",)),
        name="dsa_sample_attend",
    )(page_table, q_bf.reshape(DB, DSA_HEADS, DSA_HEAD_DIM), bias.reshape(DB, 1, lp),
      k_self.reshape(DB, 1, DSA_KV), v_self.reshape(DB, 1, DSA_KV),
      *([cache_k] * n_pages), *([cache_v] * n_pages))
    return o.reshape(DB, DSA_Q)


def _retnet_tables(pos):
    inv_freq = jnp.power(RET_ANGLE_BASE, -jnp.linspace(0.0, 1.0, RET_DK // 2, dtype=F32))
    ang = pos.astype(F32)[:, None] * inv_freq[None, :]
    return jnp.cos(ang), jnp.sin(ang)


def _split_mod(mod_l, G, R):
    return [m.reshape(G, R, D_MODEL) for m in jnp.split(mod_l, 6, axis=-1)]


MOE_GROUP_TILE = 1024
MOE_GROUP_CAP = 256


def _channel_mixer(x, sc2, sh2, g2, l, wts, tm):
    gates = router(x, sc2, sh2, wts["w_router"], wts["b_router"], tm)
    args = (x, sc2, sh2, g2, gates, l, wts["w_gate"], wts["w_up"], wts["w_down"],
            wts["ln2_g"][l], wts["ln2_b"][l])
    rows_per_mod = x.shape[0] // sc2.shape[0]
    if sc2.shape[1] == 1 and rows_per_mod >= 2 * MOE_GROUP_CAP:
        tile = min(MOE_GROUP_TILE, rows_per_mod)
        return moe_group_ln(*args, tile, MOE_GROUP_CAP)
    return moe_ln(*args, tm)


def kernel(x_prompt, x_sample, state_ret, cache_k, cache_v, cache_kidx, page_table,
           c_prompt, c_sample, w_mod, b_mod, ln1_g, ln1_b, ln2_g, ln2_b,
           w_in_ret, gn_ret_g, w_out_ret, w_in_dsa, w_out_dsa,
           w_router, b_router, w_gate, w_up, w_down):
    B, T, _ = x_prompt.shape
    DB = x_sample.shape[0]
    assert x_sample.shape[1] == 1
    n_pages = page_table.shape[1]
    page = cache_k.shape[2]
    past = n_pages * page
    n_pool = cache_k.shape[1]

    wts = dict(w_router=w_router, b_router=b_router, ln2_g=ln2_g, ln2_b=ln2_b,
               w_gate=w_gate.astype(BF16), w_up=w_up.astype(BF16), w_down=w_down.astype(BF16))
    w_mod_bf = w_mod.astype(BF16)
    w_in_ret_bf = w_in_ret[0].astype(BF16)
    w_out_ret_bf = w_out_ret[0].astype(BF16)
    w_in_dsa_bf = jnp.pad(w_in_dsa[0], ((0, 0), (0, DSA_IN_PAD - DSA_IN))).astype(BF16)
    w_out_dsa_bf = w_out_dsa[0].astype(BF16)

    pos_p = jnp.arange(T, dtype=I32)
    pos_s = jnp.full((1,), past, I32)

    tm_p = min(512, T)
    tq = min(128, T)
    xp = x_prompt.reshape(B * T, D_MODEL)
    mod_p = modulation(c_prompt, w_mod_bf, b_mod)
    sh1, sc1, g1, sh2, sc2, g2 = _split_mod(mod_p[0], B, 1)
    proj = mod_proj(xp, sc1, sh1, w_in_ret_bf, min(1024, T), 1536, BF16)
    cos_p, sin_p = _retnet_tables(pos_p)
    gated, ret_p = retention_prompt(proj.reshape(B, T, RET_IN), gn_ret_g[0], cos_p, sin_p, min(256, T))
    xp = out_proj_ln(gated.reshape(B * T, RET_V), w_out_ret_bf, xp, g1, ln1_g[0], ln1_b[0], tm_p)
    xp = _channel_mixer(xp, sc2, sh2, g2, 0, wts, tm_p)

    sh1, sc1, g1, sh2, sc2, g2 = _split_mod(mod_p[1], B, 1)
    k_p, v_p, kw_p, k_bf, kw_bf, v_t, q_t, qi_t, wi_t = dsa_project_prompt(
        xp, sc1, sh1, w_in_dsa_bf, pos_p, B, T, tm_p, tq)
    o_p = dsa_prompt_attend(kw_bf, qi_t, wi_t, k_bf, v_t, q_t, B, T, min(DSA_TOPK, T // 4), tq, tm_p)
    xp = out_proj_ln(o_p, w_out_dsa_bf, xp, g1, ln1_g[1], ln1_b[1], tm_p)
    xp = _channel_mixer(xp, sc2, sh2, g2, 1, wts, tm_p)

    xs = x_sample.reshape(DB, D_MODEL)
    mod_s = modulation(c_sample, w_mod_bf, b_mod)
    sh1, sc1, g1, sh2, sc2, g2 = _split_mod(mod_s[0], 1, DB)
    proj_s = mod_proj(xs, sc1, sh1, w_in_ret_bf, DB, 1536, F32)
    cos_s, sin_s = _retnet_tables(pos_s)
    gated_s, ret_s = retention_sample(proj_s, state_ret[0], gn_ret_g[0], cos_s, sin_s)
    xs = out_proj_ln(gated_s.reshape(DB, RET_V), w_out_ret_bf, xs, g1, ln1_g[0], ln1_b[0], DB)
    xs = _channel_mixer(xs, sc2, sh2, g2, 0, wts, DB)

    sh1, sc1, g1, sh2, sc2, g2 = _split_mod(mod_s[1], 1, DB)
    pos_rows = jnp.full((DB,), past, I32)
    q_s, k_s, v_s, qi_s, kw_s = dsa_project(xs, sc1, sh1, w_in_dsa_bf, pos_rows, DB)
    ki_s = kw_s[:, :IDX_DIM]
    score = dsa_sample_scores(page_table, qi_s, kw_s[:, IDX_DIM:IDX_DIM + IDX_HEADS], ki_s,
                              jnp.swapaxes(cache_kidx[0], 1, 2))
    bias = dsa_sample_select(score.reshape(DB, past + LANES), past + 1, min(DSA_TOPK, (past + 1) // 4))
    o_s = dsa_sample_attend(page_table, q_s, bias, k_s, v_s,
                            cache_k[0].reshape(n_pool, page * DSA_KV_HEADS, DSA_HEAD_DIM),
                            cache_v[0].reshape(n_pool, page * DSA_KV_HEADS, DSA_HEAD_DIM))
    xs = out_proj_ln(o_s, w_out_dsa_bf, xs, g1, ln1_g[1], ln1_b[1], DB)
    xs = _channel_mixer(xs, sc2, sh2, g2, 1, wts, DB)

    kv_shape = (DSA_KV_HEADS, DSA_HEAD_DIM)
    return (xp.reshape(B, T, D_MODEL), xs.reshape(DB, 1, D_MODEL),
            ret_p[None], ret_s[None],
            k_p.reshape(1, B, T, *kv_shape), v_p.reshape(1, B, T, *kv_shape),
            kw_p[:, :IDX_DIM].reshape(1, B, T, IDX_DIM),
            k_s.reshape(1, DB, 1, *kv_shape), v_s.reshape(1, DB, 1, *kv_shape),
            ki_s.reshape(1, DB, 1, IDX_DIM))
```

```python
import functools

import jax
import jax.numpy as jnp
from jax import lax
from jax.experimental import pallas as pl
from jax.experimental.pallas import tpu as pltpu

F32 = jnp.float32
BF16 = jnp.bfloat16
I32 = jnp.int32

D_MODEL = 1024
DEPTH = 2
ALPHA = (2.0 * DEPTH) ** 0.25
LN_EPS = 1e-5
GN_EPS = 1e-6

RET_HEADS = 4
RET_DK = 256
RET_DV = 512
RET_QK = RET_HEADS * RET_DK
RET_V = RET_HEADS * RET_DV
RET_IN = 2 * RET_QK + 2 * RET_V
RET_ANGLE_BASE = 10000.0

DSA_HEADS = 8
DSA_KV_HEADS = 2
DSA_HEAD_DIM = 128
DSA_GROUP = DSA_HEADS // DSA_KV_HEADS
DSA_Q = DSA_HEADS * DSA_HEAD_DIM
DSA_KV = DSA_KV_HEADS * DSA_HEAD_DIM
IDX_HEADS = 16
IDX_DIM = 64
DSA_TOPK = 256
ROPE_THETA = 500000.0
ROT_DIM = DSA_HEAD_DIM // 4
IDX_ROT_DIM = IDX_DIM // 4
DSA_IN = DSA_Q + 2 * DSA_KV + IDX_HEADS * IDX_DIM + IDX_DIM + IDX_HEADS

N_EXPERTS = 16
N_GROUPS = 4
EXPERTS_PER_GROUP = N_EXPERTS // N_GROUPS
D_EXPERT = 512
GROUP_LANE0 = N_EXPERTS

LANES = 128
SUBLANES = 8
VMEM_LIMIT = 56 * 1024 * 1024
NEG_BIG = -1e30
INT_MIN = -2147483648


def _cparams(sem):
    return pltpu.CompilerParams(dimension_semantics=sem, vmem_limit_bytes=VMEM_LIMIT)


def _silu(x):
    return x * (1.0 / (1.0 + jnp.exp(-x)))


def _layer_norm(z, g, b):
    mu = jnp.mean(z, -1, keepdims=True)
    d = z - mu
    var = jnp.mean(d * d, -1, keepdims=True)
    return d * lax.rsqrt(var + LN_EPS) * g + b


def _mod_kernel(c_ref, w_ref, b_ref, o_ref):
    a = _silu(c_ref[...]).astype(BF16)
    o_ref[...] = jnp.dot(a, w_ref[...], preferred_element_type=F32) + b_ref[...]


def modulation(c, w_mod_bf, b_mod):
    R = c.shape[0]
    tn = 1536
    return pl.pallas_call(
        _mod_kernel,
        grid=(DEPTH, 6 * D_MODEL // tn),
        in_specs=[pl.BlockSpec((R, D_MODEL), lambda l, j: (0, 0)),
                  pl.BlockSpec((None, D_MODEL, tn), lambda l, j: (l, 0, j)),
                  pl.BlockSpec((None, 1, tn), lambda l, j: (l, 0, j))],
        out_specs=pl.BlockSpec((None, R, tn), lambda l, j: (l, 0, j)),
        out_shape=jax.ShapeDtypeStruct((DEPTH, R, 6 * D_MODEL), F32),
        compiler_params=_cparams(("parallel", "parallel")),
        name="modulation",
    )(c, w_mod_bf, b_mod.reshape(DEPTH, 1, 6 * D_MODEL))


def _proj_kernel(x_ref, sc_ref, sh_ref, w_ref, o_ref, h_scr):
    @pl.when(pl.program_id(1) == 0)
    def _():
        h_scr[...] = (x_ref[...] * (1.0 + sc_ref[...]) + sh_ref[...]).astype(BF16)

    o_ref[...] = jnp.dot(h_scr[...], w_ref[...], preferred_element_type=F32).astype(o_ref.dtype)


def _mod_spec(R, tiles_per_group):
    return pl.BlockSpec((None, R, D_MODEL), lambda i, *_: (i // tiles_per_group, 0, 0))


def mod_proj(x, sc, sh, w_bf, tm, tn, out_dtype):
    N = x.shape[0]
    G, R, _ = sc.shape
    n_out = w_bf.shape[1]
    tpg = (N // G) // tm
    return pl.pallas_call(
        _proj_kernel,
        grid=(N // tm, n_out // tn),
        in_specs=[pl.BlockSpec((tm, D_MODEL), lambda i, j: (i, 0)),
                  _mod_spec(R, tpg), _mod_spec(R, tpg),
                  pl.BlockSpec((D_MODEL, tn), lambda i, j: (0, j))],
        out_specs=pl.BlockSpec((tm, tn), lambda i, j: (i, j)),
        out_shape=jax.ShapeDtypeStruct((N, n_out), out_dtype),
        scratch_shapes=[pltpu.VMEM((tm, D_MODEL), BF16)],
        compiler_params=_cparams(("parallel", "arbitrary")),
        name="mod_proj",
    )(x, sc, sh, w_bf)


def _rot_half(x, cos, sin):
    half = cos.shape[-1]
    x1, x2 = x[:, :half], x[:, half:]
    return jnp.concatenate([x1 * cos - x2 * sin, x1 * sin + x2 * cos], axis=1)


def _group_norm_gate(o, gn, g):
    mu = jnp.mean(o, -1, keepdims=True)
    d = o - mu
    var = jnp.mean(d * d, -1, keepdims=True)
    return d * lax.rsqrt(var + GN_EPS) * gn * _silu(g)


def _ret_prompt_kernel(q_ref, k_ref, v_ref, g_ref, cos_ref, sin_ref, lg_ref, gn_ref,
                       o_ref, s_ref, s_scr, *, chunk):
    c = pl.program_id(2)

    @pl.when(c == 0)
    def _():
        s_scr[...] = jnp.zeros_like(s_scr)

    cos, sin = cos_ref[...], sin_ref[...]
    lg_row = lg_ref[...]
    lg = lg_row[:, :1]
    q = _rot_half(q_ref[...].astype(F32), cos, sin)
    k = _rot_half(k_ref[...].astype(F32), cos, sin) * (RET_DK ** -0.5)
    vb = v_ref[...].astype(BF16)
    qb = q.astype(BF16)

    row = lax.broadcasted_iota(I32, (chunk, chunk), 0)
    col = lax.broadcasted_iota(I32, (chunk, chunk), 1)
    diff = (row - col).astype(F32)
    decay = jnp.where(diff >= 0, jnp.exp(lg_row * jnp.maximum(diff, 0.0)), 0.0)
    idx = lax.broadcasted_iota(I32, (chunk, 1), 0).astype(F32)
    q_dec = jnp.exp(lg * (idx + 1.0))
    k_dec = jnp.exp(lg * (chunk - 1.0 - idx))
    chunk_dec = jnp.exp(lg * chunk)

    s = s_scr[...]
    att = lax.dot_general(qb, k.astype(BF16), (((1,), (1,)), ((), ())),
                          preferred_element_type=F32) * decay
    o = (jnp.dot(att.astype(BF16), vb, preferred_element_type=F32)
         + jnp.dot(qb, s.astype(BF16), preferred_element_type=F32) * q_dec)
    kd_t = (k * k_dec).T.astype(BF16)
    s_new = chunk_dec * s + jnp.dot(kd_t, vb, preferred_element_type=F32)
    s_scr[...] = s_new
    o_ref[...] = _group_norm_gate(o, gn_ref[...], g_ref[...].astype(F32)).astype(o_ref.dtype)

    @pl.when(c == pl.num_programs(2) - 1)
    def _():
        s_ref[...] = s_new


def _log_gamma():
    return jnp.log(1.0 - jnp.power(2.0, -5.0 - jnp.arange(RET_HEADS, dtype=F32)))


def retention_prompt(proj, gn_g, cos, sin, chunk):
    B, T, _ = proj.shape
    lg_tab = jnp.broadcast_to(_log_gamma()[:, None, None], (RET_HEADS, 1, chunk))
    qk_blk = lambda off: pl.BlockSpec((None, chunk, RET_DK), lambda b, h, c: (b, c, off + h))
    v_blk = lambda off: pl.BlockSpec((None, chunk, RET_DV), lambda b, h, c: (b, c, off + h))
    tab = pl.BlockSpec((chunk, RET_DK // 2), lambda b, h, c: (c, 0))
    return pl.pallas_call(
        functools.partial(_ret_prompt_kernel, chunk=chunk),
        grid=(B, RET_HEADS, T // chunk),
        in_specs=[qk_blk(0), qk_blk(RET_QK // RET_DK),
                  v_blk(2 * RET_QK // RET_DV), v_blk((2 * RET_QK + RET_V) // RET_DV),
                  tab, tab,
                  pl.BlockSpec((None, 1, chunk), lambda b, h, c: (h, 0, 0)),
                  pl.BlockSpec((1, RET_DV), lambda b, h, c: (0, h))],
        out_specs=[pl.BlockSpec((None, chunk, RET_DV), lambda b, h, c: (b, c, h)),
                   pl.BlockSpec((None, None, RET_DK, RET_DV), lambda b, h, c: (b, h, 0, 0))],
        out_shape=[jax.ShapeDtypeStruct((B, T, RET_V), BF16),
                   jax.ShapeDtypeStruct((B, RET_HEADS, RET_DK, RET_DV), F32)],
        scratch_shapes=[pltpu.VMEM((RET_DK, RET_DV), F32)],
        compiler_params=_cparams(("parallel", "parallel", "arbitrary")),
        name="retention_prompt",
    )(proj, proj, proj, proj, cos, sin, lg_tab, gn_g.reshape(1, RET_V))


def _ret_sample_kernel(qk_ref, v_ref, g_ref, s0_ref, cos_ref, sin_ref, gam_ref, gn_ref,
                       o_ref, s_ref):
    t = _rot_half(qk_ref[...], cos_ref[...], sin_ref[...])
    row = lax.broadcasted_iota(I32, t.shape, 0)
    t = jnp.where(row >= RET_HEADS, t * (RET_DK ** -0.5), t)
    pad = jnp.zeros((LANES - 2 * RET_HEADS, RET_DK), F32)
    t_t = jnp.concatenate([t, pad], axis=0).T
    for h in range(RET_HEADS):
        qc = t_t[:, h:h + 1]
        kc = t_t[:, RET_HEADS + h:RET_HEADS + h + 1]
        gam = gam_ref[h]
        vh = v_ref[h:h + 1, :]
        s = s0_ref[h]
        qk_dot = jnp.sum(t[h:h + 1, :] * t[RET_HEADS + h:RET_HEADS + h + 1, :], -1, keepdims=True)
        o = qk_dot * vh + gam * jnp.sum(s * qc, axis=0, keepdims=True)
        s_ref[h] = gam * s + kc * vh
        o_ref[h:h + 1, :] = _group_norm_gate(o, gn_ref[h:h + 1, :], g_ref[h:h + 1, :])


def retention_sample(proj, s0, gn_g, cos, sin):
    DB = proj.shape[0]
    qk = proj[:, :2 * RET_QK].reshape(DB, 2 * RET_HEADS, RET_DK)
    v = proj[:, 2 * RET_QK:2 * RET_QK + RET_V].reshape(DB, RET_HEADS, RET_DV)
    g = proj[:, 2 * RET_QK + RET_V:].reshape(DB, RET_HEADS, RET_DV)
    gam = jnp.broadcast_to(jnp.exp(_log_gamma())[:, None, None], (RET_HEADS, 1, RET_DV))
    full = lambda *shape: pl.BlockSpec(shape, lambda b: (0,) * len(shape))
    per_b = lambda *shape: pl.BlockSpec((None,) + shape, lambda b: (b,) + (0,) * len(shape))
    return pl.pallas_call(
        _ret_sample_kernel,
        grid=(DB,),
        in_specs=[per_b(2 * RET_HEADS, RET_DK), per_b(RET_HEADS, RET_DV), per_b(RET_HEADS, RET_DV),
                  per_b(RET_HEADS, RET_DK, RET_DV),
                  full(1, RET_DK // 2), full(1, RET_DK // 2),
                  full(RET_HEADS, 1, RET_DV), full(RET_HEADS, RET_DV)],
        out_specs=[per_b(RET_HEADS, RET_DV), per_b(RET_HEADS, RET_DK, RET_DV)],
        out_shape=[jax.ShapeDtypeStruct((DB, RET_HEADS, RET_DV), F32),
                   jax.ShapeDtypeStruct((DB, RET_HEADS, RET_DK, RET_DV), F32)],
        compiler_params=_cparams(("parallel",)),
        name="retention_sample",
    )(qk, v, g, s0, cos, sin, gam, gn_g.reshape(RET_HEADS, RET_DV))


def _out_ln_kernel(a_ref, w_ref, x_ref, g_ref, lng_ref, lnb_ref, o_ref):
    y = jnp.dot(a_ref[...].astype(BF16), w_ref[...], preferred_element_type=F32)
    o_ref[...] = _layer_norm(ALPHA * x_ref[...] + g_ref[...] * y, lng_ref[...], lnb_ref[...])


def out_proj_ln(a, w_bf, x, gate, ln_g, ln_b, tm):
    N, K = a.shape
    G, R, _ = gate.shape
    tpg = (N // G) // tm
    row = pl.BlockSpec((1, D_MODEL), lambda i: (0, 0))
    return pl.pallas_call(
        _out_ln_kernel,
        grid=(N // tm,),
        in_specs=[pl.BlockSpec((tm, K), lambda i: (i, 0)),
                  pl.BlockSpec((K, D_MODEL), lambda i: (0, 0)),
                  pl.BlockSpec((tm, D_MODEL), lambda i: (i, 0)),
                  _mod_spec(R, tpg), row, row],
        out_specs=pl.BlockSpec((tm, D_MODEL), lambda i: (i, 0)),
        out_shape=jax.ShapeDtypeStruct((N, D_MODEL), F32),
        compiler_params=_cparams(("parallel",)),
        name="out_proj_ln",
    )(a, w_bf, x, gate, ln_g.reshape(1, D_MODEL), ln_b.reshape(1, D_MODEL))


def _lane_roll(x, shift):
    return pltpu.roll(x, shift, axis=1)


def _router_kernel(x_ref, sc_ref, sh_ref, whi_ref, wlo_ref, b_ref, o_ref):
    h = x_ref[...] * (1.0 + sc_ref[...]) + sh_ref[...]
    hi = h.astype(BF16)
    lo = (h - hi.astype(F32)).astype(BF16)
    whi, wlo = whi_ref[...], wlo_ref[...]
    logits = (jnp.dot(hi, whi, preferred_element_type=F32)
              + jnp.dot(hi, wlo, preferred_element_type=F32)
              + jnp.dot(lo, whi, preferred_element_type=F32)) + b_ref[...]
    lane_i = lax.broadcasted_iota(I32, logits.shape, 1)
    lane = lane_i.astype(F32)
    m = jnp.max(logits, -1, keepdims=True)
    e = jnp.exp(logits - m)
    p = e / jnp.sum(e, -1, keepdims=True)
    s1 = p + _lane_roll(p, 1)
    s2 = p + _lane_roll(p, 2)
    s3 = p + _lane_roll(p, 3)
    pair_max = jnp.maximum(jnp.maximum(jnp.maximum(s1, s2), s3),
                           jnp.maximum(jnp.maximum(_lane_roll(s1, 1), _lane_roll(s2, 1)),
                                       _lane_roll(s1, 2)))
    is_last = ((lane_i % EXPERTS_PER_GROUP) == EXPERTS_PER_GROUP - 1) & (lane_i < N_EXPERTS)
    grp_score = jnp.where(is_last, pair_max, -jnp.inf)
    gmax = jnp.max(grp_score, -1, keepdims=True)
    big = float(4 * LANES)
    sel_last = jnp.min(jnp.where(grp_score == gmax, lane, big), -1, keepdims=True)
    in_grp = (lane <= sel_last) & (lane > sel_last - EXPERTS_PER_GROUP)
    masked = jnp.where(in_grp, p, -jnp.inf)
    v1 = jnp.max(masked, -1, keepdims=True)
    i1 = jnp.min(jnp.where(masked == v1, lane, big), -1, keepdims=True)
    masked2 = jnp.where(lane == i1, -jnp.inf, masked)
    v2 = jnp.max(masked2, -1, keepdims=True)
    i2 = jnp.min(jnp.where(masked2 == v2, lane, big), -1, keepdims=True)
    tot = v1 + v2
    grp_hot = lane == (sel_last - (EXPERTS_PER_GROUP - 1)) * (1.0 / EXPERTS_PER_GROUP) + GROUP_LANE0
    o_ref[...] = (jnp.where(lane == i1, v1 / tot, 0.0) + jnp.where(lane == i2, v2 / tot, 0.0)
                  + jnp.where(grp_hot, 1.0, 0.0))


def router(x, sc, sh, w_router, b_router, tm):
    N = x.shape[0]
    G, R, _ = sc.shape
    tpg = (N // G) // tm
    w_pad = jnp.zeros((D_MODEL, LANES), F32).at[:, :N_EXPERTS].set(w_router)
    w_hi = w_pad.astype(BF16)
    w_lo = (w_pad - w_hi.astype(F32)).astype(BF16)
    b_pad = jnp.full((1, LANES), NEG_BIG, F32).at[0, :N_EXPERTS].set(b_router)
    full = lambda r, c: pl.BlockSpec((r, c), lambda i: (0, 0))
    return pl.pallas_call(
        _router_kernel,
        grid=(N // tm,),
        in_specs=[pl.BlockSpec((tm, D_MODEL), lambda i: (i, 0)), _mod_spec(R, tpg), _mod_spec(R, tpg),
                  full(D_MODEL, LANES), full(D_MODEL, LANES), full(1, LANES)],
        out_specs=pl.BlockSpec((tm, LANES), lambda i: (i, 0)),
        out_shape=jax.ShapeDtypeStruct((N, LANES), F32),
        compiler_params=_cparams(("parallel",)),
        name="router",
    )(x, sc, sh, w_hi, w_lo, b_pad)


def _moe_kernel(x_ref, sc_ref, sh_ref, g2_ref, gates_ref, wg_ref, wu_ref, wd_ref, lng_ref, lnb_ref,
                o_ref, h_scr, acc_scr):
    e = pl.program_id(1)

    @pl.when(e == 0)
    def _():
        h_scr[...] = (x_ref[...] * (1.0 + sc_ref[...]) + sh_ref[...]).astype(BF16)
        acc_scr[...] = jnp.zeros_like(acc_scr)

    hb = h_scr[...]
    a = (_silu(jnp.dot(hb, wg_ref[...], preferred_element_type=F32))
         * jnp.dot(hb, wu_ref[...], preferred_element_type=F32))
    y = jnp.dot(a.astype(BF16), wd_ref[...], preferred_element_type=F32)
    gates = gates_ref[...]
    lane = lax.broadcasted_iota(I32, gates.shape, 1)
    gate_e = jnp.sum(jnp.where(lane == e, gates, 0.0), -1, keepdims=True)
    acc_scr[...] += gate_e * y

    @pl.when(e == pl.num_programs(1) - 1)
    def _():
        z = ALPHA * x_ref[...] + g2_ref[...] * acc_scr[...]
        o_ref[...] = _layer_norm(z, lng_ref[...], lnb_ref[...])


def moe_ln(x, sc, sh, g2, gates, l, wg_bf, wu_bf, wd_bf, ln_g, ln_b, tm):
    N = x.shape[0]
    G, R, _ = sc.shape
    tpg = (N // G) // tm
    ms = pl.BlockSpec((None, R, D_MODEL), lambda i, e: (i // tpg, 0, 0))
    row = pl.BlockSpec((1, D_MODEL), lambda i, e: (0, 0))
    return pl.pallas_call(
        _moe_kernel,
        grid=(N // tm, N_EXPERTS),
        in_specs=[pl.BlockSpec((tm, D_MODEL), lambda i, e: (i, 0)), ms, ms, ms,
                  pl.BlockSpec((tm, LANES), lambda i, e: (i, 0)),
                  pl.BlockSpec((None, None, D_MODEL, D_EXPERT), lambda i, e: (l, e, 0, 0)),
                  pl.BlockSpec((None, None, D_MODEL, D_EXPERT), lambda i, e: (l, e, 0, 0)),
                  pl.BlockSpec((None, None, D_EXPERT, D_MODEL), lambda i, e: (l, e, 0, 0)),
                  row, row],
        out_specs=pl.BlockSpec((tm, D_MODEL), lambda i, e: (i, 0)),
        out_shape=jax.ShapeDtypeStruct((N, D_MODEL), F32),
        scratch_shapes=[pltpu.VMEM((tm, D_MODEL), BF16), pltpu.VMEM((tm, D_MODEL), F32)],
        compiler_params=_cparams(("parallel", "arbitrary")),
        name="moe_ln",
    )(x, sc, sh, g2, gates, wg_bf, wu_bf, wd_bf, ln_g.reshape(1, D_MODEL), ln_b.reshape(1, D_MODEL))


def _split3_bf16(x):
    hi = x.astype(BF16)
    r = x - hi.astype(F32)
    mid = r.astype(BF16)
    lo = (r - mid.astype(F32)).astype(BF16)
    return hi, mid, lo


def _moe_group_kernel(x_ref, sc_ref, sh_ref, g2_ref, gates_ref, ltri_ref, wg_ref, wu_ref, wd_ref,
                      lng_ref, lnb_ref, o_ref,
                      h_scr, acc_scr, rank_scr, rank_t_scr, sel_t_scr, xs_scr, gs_scr, yacc_scr, cnt_smem,
                      *, cap):
    g, e = pl.program_id(1), pl.program_id(2)
    tm = x_ref.shape[0]
    lane = lax.broadcasted_iota(I32, (tm, LANES), 1)

    @pl.when((g == 0) & (e == 0))
    def _():
        h_scr[...] = (x_ref[...] * (1.0 + sc_ref[...]) + sh_ref[...]).astype(BF16)
        acc_scr[...] = jnp.zeros_like(acc_scr)
        gates = gates_ref[...]
        sel = jnp.where((lane >= GROUP_LANE0) & (lane < GROUP_LANE0 + N_GROUPS), gates, 0.0)
        rank = jnp.dot(ltri_ref[...], sel.astype(BF16), preferred_element_type=F32)
        rank_scr[...] = rank
        rank_t_scr[...] = rank.T
        sel_t_scr[...] = sel.T
        for gg in range(N_GROUPS):
            cnt_smem[gg] = jnp.sum(jnp.where(lane == GROUP_LANE0 + gg, sel, 0.0)).astype(I32)

    half = cap // 2
    cnt = cnt_smem[g]
    rem = cnt % cap
    use_tail = (rem > 0) & (rem <= half)
    n_full = cnt // cap + jnp.where(rem > half, 1, 0)
    tail_start = pl.multiple_of(n_full * cap, half)
    grp_row = pl.ds(GROUP_LANE0 + g, 1)

    def for_chunks(fn):
        def body(c, carry):
            fn(pl.multiple_of(c * cap, cap), cap)
            return carry
        lax.fori_loop(0, n_full, body, 0)

        @pl.when(use_tail)
        def _():
            fn(tail_start, half)

    @pl.when(e == 0)
    def _():
        expert_lanes = jnp.where(lane < N_EXPERTS, gates_ref[...], 0.0)
        parts = _split3_bf16(expert_lanes)
        g_parts = sum(_lane_roll(part.astype(F32), j * N_EXPERTS)
                      for j, part in enumerate(parts)).astype(BF16)
        rank_row, sel_row = rank_t_scr[grp_row, :], sel_t_scr[grp_row, :]

        def dispatch(start, size):
            rows = pl.ds(start, size)
            slot = (start + lax.broadcasted_iota(I32, (size, 1), 0)).astype(F32)
            p = jnp.where((rank_row == slot) & (sel_row > 0.5), 1.0, 0.0).astype(BF16)
            xs_scr[rows, :] = jnp.dot(p, h_scr[...], preferred_element_type=F32).astype(BF16)
            gs_scr[rows, :] = jnp.dot(p, g_parts, preferred_element_type=F32)
            yacc_scr[rows, :] = jnp.zeros((size, D_MODEL), F32)

        for_chunks(dispatch)

    def expert(start, size):
        rows = pl.ds(start, size)
        xs = xs_scr[rows, :]
        a = (_silu(jnp.dot(xs, wg_ref[...], preferred_element_type=F32))
             * jnp.dot(xs, wu_ref[...], preferred_element_type=F32))
        y = jnp.dot(a.astype(BF16), wd_ref[...], preferred_element_type=F32)
        gs = gs_scr[rows, :]
        lane_c = lax.broadcasted_iota(I32, gs.shape, 1)
        is_part = ((lane_c % N_EXPERTS) == g * EXPERTS_PER_GROUP + e) & (lane_c < 3 * N_EXPERTS)
        gate = jnp.sum(jnp.where(is_part, gs, 0.0), -1, keepdims=True)
        yacc_scr[rows, :] += gate * y

    for_chunks(expert)

    @pl.when(e == EXPERTS_PER_GROUP - 1)
    def _():
        in_grp = lane == GROUP_LANE0 + g
        rank_col = jnp.sum(jnp.where(in_grp, rank_scr[...], 0.0), -1, keepdims=True)
        sel_col = jnp.sum(jnp.where(in_grp, gates_ref[...], 0.0), -1, keepdims=True)

        def combine(start, size):
            slot = (start + lax.broadcasted_iota(I32, (1, size), 1)).astype(F32)
            p_t = jnp.where((rank_col == slot) & (sel_col > 0.5), 1.0, 0.0).astype(BF16)
            y_hi, y_mid, _ = _split3_bf16(yacc_scr[pl.ds(start, size), :])
            acc_scr[...] += (jnp.dot(p_t, y_hi, preferred_element_type=F32)
                             + jnp.dot(p_t, y_mid, preferred_element_type=F32))

        for_chunks(combine)

    @pl.when((g == N_GROUPS - 1) & (e == EXPERTS_PER_GROUP - 1))
    def _():
        z = ALPHA * x_ref[...] + g2_ref[...] * acc_scr[...]
        o_ref[...] = _layer_norm(z, lng_ref[...], lnb_ref[...])


def moe_group_ln(x, sc, sh, g2, gates, l, wg_bf, wu_bf, wd_bf, ln_g, ln_b, tm, cap):
    N = x.shape[0]
    G, R, _ = sc.shape
    tpg = (N // G) // tm
    ltri = (jnp.arange(tm)[:, None] > jnp.arange(tm)[None, :]).astype(BF16)
    ms = pl.BlockSpec((None, R, D_MODEL), lambda i, g, e: (i // tpg, 0, 0))
    row = pl.BlockSpec((1, D_MODEL), lambda i, g, e: (0, 0))
    expert = lambda i, g, e: (l, g * EXPERTS_PER_GROUP + e, 0, 0)
    w_in = pl.BlockSpec((None, None, D_MODEL, D_EXPERT), expert)
    w_out = pl.BlockSpec((None, None, D_EXPERT, D_MODEL), expert)
    return pl.pallas_call(
        functools.partial(_moe_group_kernel, cap=cap),
        grid=(N // tm, N_GROUPS, EXPERTS_PER_GROUP),
        in_specs=[pl.BlockSpec((tm, D_MODEL), lambda i, g, e: (i, 0)), ms, ms, ms,
                  pl.BlockSpec((tm, LANES), lambda i, g, e: (i, 0)),
                  pl.BlockSpec((tm, tm), lambda i, g, e: (0, 0)),
                  w_in, w_in, w_out, row, row],
        out_specs=pl.BlockSpec((tm, D_MODEL), lambda i, g, e: (i, 0)),
        out_shape=jax.ShapeDtypeStruct((N, D_MODEL), F32),
        scratch_shapes=[pltpu.VMEM((tm, D_MODEL), BF16), pltpu.VMEM((tm, D_MODEL), F32),
                        pltpu.VMEM((tm, LANES), F32), pltpu.VMEM((LANES, tm), F32),
                        pltpu.VMEM((LANES, tm), F32), pltpu.VMEM((tm, D_MODEL), BF16),
                        pltpu.VMEM((tm, LANES), F32), pltpu.VMEM((tm, D_MODEL), F32),
                        pltpu.SMEM((N_GROUPS,), I32)],
        compiler_params=_cparams(("arbitrary", "arbitrary", "arbitrary")),
        name="moe_group_ln",
    )(x, sc, sh, g2, gates, ltri, wg_bf, wu_bf, wd_bf, ln_g.reshape(1, D_MODEL), ln_b.reshape(1, D_MODEL))


LOG2_E = 1.4426950408889634
Q_PRESCALE = DSA_HEAD_DIM ** -0.5 * LOG2_E
DSA_IN_PAD = -(-DSA_IN // LANES) * LANES
DSA_QI_OFF = DSA_Q + 2 * DSA_KV
DSA_KW_OFF = DSA_QI_OFF + IDX_HEADS * IDX_DIM


def _rope_tables(pos, rot_dim, period, n_periods):
    half = rot_dim // 2
    inv_freq = 1.0 / (ROPE_THETA ** (jnp.arange(0, rot_dim, 2, dtype=F32) / rot_dim))
    ang = pos.astype(F32)[:, None] * inv_freq[None, :]
    cos, sin = jnp.cos(ang), jnp.sin(ang)
    T = pos.shape[0]
    zeros = jnp.zeros((T, period - 2 * half), F32)
    zh = jnp.zeros((T, half), F32)
    a1 = jnp.concatenate([cos, cos, zeros + 1.0], 1)
    b1 = jnp.concatenate([zh, sin, zeros], 1)
    c1 = jnp.concatenate([-sin, zh, zeros], 1)
    rest = LANES - n_periods * period
    pad1 = jnp.ones((T, rest), F32)
    pad0 = jnp.zeros((T, rest), F32)
    a = jnp.concatenate([a1] * n_periods + [pad1], 1)
    b = jnp.concatenate([b1] * n_periods + [pad0], 1)
    c = jnp.concatenate([c1] * n_periods + [pad0], 1)
    return a, b, c


def _apply_rope(x, a, b, c, half):
    return x * a + _lane_roll(x, half) * b + _lane_roll(x, LANES - half) * c


N_ROPE_TABLES = 9


def _dsa_sections(x_ref, sc_ref, sh_ref, w_ref, tab_refs):
    hb = (x_ref[...] * (1.0 + sc_ref[...]) + sh_ref[...]).astype(BF16)
    ma, mb, mc, ia, ib, ic, ka, kb, kc = (t[...] for t in tab_refs)

    def sec(off, width):
        return jnp.dot(hb, w_ref[:, off:off + width], preferred_element_type=F32)

    def lanes(x, h):
        return x[:, h * LANES:(h + 1) * LANES]

    q = sec(0, DSA_Q)
    q_heads = [_apply_rope(lanes(q, h), ma, mb, mc, ROT_DIM // 2) * Q_PRESCALE for h in range(DSA_HEADS)]
    k = sec(DSA_Q, DSA_KV)
    k = jnp.concatenate([_apply_rope(lanes(k, h), ma, mb, mc, ROT_DIM // 2) for h in range(DSA_KV_HEADS)], 1)
    v = sec(DSA_Q + DSA_KV, DSA_KV)
    qi = sec(DSA_QI_OFF, IDX_HEADS * IDX_DIM)
    qi_pairs = [_apply_rope(lanes(qi, h), ia, ib, ic, IDX_ROT_DIM // 2)
                for h in range(IDX_HEADS * IDX_DIM // LANES)]
    kw = _apply_rope(sec(DSA_KW_OFF, LANES), ka, kb, kc, IDX_ROT_DIM // 2)
    return q_heads, k, v, qi_pairs, kw


def _dsa_proj_kernel(x_ref, sc_ref, sh_ref, w_ref, *rest):
    tabs = rest[:N_ROPE_TABLES]
    q_ref, k_ref, v_ref, qi_ref, kw_ref = rest[N_ROPE_TABLES:]
    q_heads, k, v, qi_pairs, kw = _dsa_sections(x_ref, sc_ref, sh_ref, w_ref, tabs)
    for h, qh in enumerate(q_heads):
        q_ref[:, h * LANES:(h + 1) * LANES] = qh.astype(q_ref.dtype)
    for h, qp in enumerate(qi_pairs):
        qi_ref[:, h * LANES:(h + 1) * LANES] = qp.astype(qi_ref.dtype)
    k_ref[...] = k
    v_ref[...] = v
    kw_ref[...] = kw


def _dsa_proj_prompt_kernel(x_ref, sc_ref, sh_ref, w_ref, *rest, tq):
    tabs = rest[:N_ROPE_TABLES]
    k_ref, v_ref, kw_ref, kbf_ref, kwbf_ref, vt_ref, qt_ref, qit_ref, wit_ref = rest[N_ROPE_TABLES:]
    q_heads, k, v, qi_pairs, kw = _dsa_sections(x_ref, sc_ref, sh_ref, w_ref, tabs)
    tm = k.shape[0]
    k_ref[...] = k
    kbf_ref[...] = k.astype(BF16)
    v_ref[...] = v
    kw_ref[...] = kw
    kwbf_ref[...] = kw.astype(BF16)
    for n in range(DSA_KV_HEADS):
        vt_ref[n, 0] = v[:, n * DSA_HEAD_DIM:(n + 1) * DSA_HEAD_DIM].T.astype(BF16)
    heads_per_pair = LANES // IDX_DIM
    for j in range(tm // tq):
        rows = slice(j * tq, (j + 1) * tq)
        for h, qh in enumerate(q_heads):
            n, g = divmod(h, DSA_GROUP)
            qt_ref[j, n, :, g * tq:(g + 1) * tq] = qh[rows].T.astype(BF16)
        for hp, qp in enumerate(qi_pairs):
            t = qp[rows].T.astype(BF16)
            for s in range(heads_per_pair):
                h = hp * heads_per_pair + s
                qit_ref[j, :IDX_DIM, h * tq:(h + 1) * tq] = t[s * IDX_DIM:(s + 1) * IDX_DIM]
        qit_ref[j, IDX_DIM:, :] = jnp.zeros((LANES - IDX_DIM, IDX_HEADS * tq), BF16)
        wit_ref[j] = kw[rows].T[IDX_DIM:IDX_DIM + IDX_HEADS]


def _dsa_project_call(body, x, sc, sh, w_pad_bf, pos, tm, out_specs, out_shape, name):
    N = x.shape[0]
    G, R, _ = sc.shape
    tpg = (N // G) // tm
    n_tab = pos.shape[0] // tm
    tabs = (_rope_tables(pos, ROT_DIM, DSA_HEAD_DIM, 1)
            + _rope_tables(pos, IDX_ROT_DIM, IDX_DIM, 2)
            + _rope_tables(pos, IDX_ROT_DIM, IDX_DIM, 1))
    tab = pl.BlockSpec((tm, LANES), lambda i: (i % n_tab, 0))
    return pl.pallas_call(
        body,
        grid=(N // tm,),
        in_specs=[pl.BlockSpec((tm, D_MODEL), lambda i: (i, 0)), _mod_spec(R, tpg), _mod_spec(R, tpg),
                  pl.BlockSpec((D_MODEL, DSA_IN_PAD), lambda i: (0, 0))] + [tab] * N_ROPE_TABLES,
        out_specs=out_specs, out_shape=out_shape,
        compiler_params=_cparams(("parallel",)),
        name=name,
    )(x, sc, sh, w_pad_bf, *tabs)


def dsa_project(x, sc, sh, w_pad_bf, pos, tm):
    N = x.shape[0]
    out = lambda w: pl.BlockSpec((tm, w), lambda i: (i, 0))
    return _dsa_project_call(
        _dsa_proj_kernel, x, sc, sh, w_pad_bf, pos, tm,
        [out(DSA_Q), out(DSA_KV), out(DSA_KV), out(IDX_HEADS * IDX_DIM), out(LANES)],
        [jax.ShapeDtypeStruct((N, DSA_Q), BF16), jax.ShapeDtypeStruct((N, DSA_KV), F32),
         jax.ShapeDtypeStruct((N, DSA_KV), F32), jax.ShapeDtypeStruct((N, IDX_HEADS * IDX_DIM), BF16),
         jax.ShapeDtypeStruct((N, LANES), F32)], "dsa_project")


def dsa_project_prompt(x, sc, sh, w_pad_bf, pos, B, T, tm, tq):
    N = B * T
    tiles, n_qt = T // tm, tm // tq
    gq = DSA_GROUP * tq
    out = lambda w: pl.BlockSpec((tm, w), lambda i: (i, 0))
    by_tile = lambda *shape: pl.BlockSpec((None,) + shape,
                                          lambda i: (i // tiles, i % tiles) + (0,) * (len(shape) - 1))
    return _dsa_project_call(
        functools.partial(_dsa_proj_prompt_kernel, tq=tq), x, sc, sh, w_pad_bf, pos, tm,
        [out(DSA_KV), out(DSA_KV), out(LANES), out(DSA_KV), out(LANES),
         pl.BlockSpec((None, DSA_KV_HEADS, 1, DSA_HEAD_DIM, tm), lambda i: (i // tiles, 0, i % tiles, 0, 0)),
         by_tile(n_qt, DSA_KV_HEADS, DSA_HEAD_DIM, gq),
         by_tile(n_qt, LANES, IDX_HEADS * tq),
         by_tile(n_qt, IDX_HEADS, tq)],
        [jax.ShapeDtypeStruct((N, DSA_KV), F32), jax.ShapeDtypeStruct((N, DSA_KV), F32),
         jax.ShapeDtypeStruct((N, LANES), F32), jax.ShapeDtypeStruct((N, DSA_KV), BF16),
         jax.ShapeDtypeStruct((N, LANES), BF16),
         jax.ShapeDtypeStruct((B, DSA_KV_HEADS, tiles, DSA_HEAD_DIM, tm), BF16),
         jax.ShapeDtypeStruct((B, T // tq, DSA_KV_HEADS, DSA_HEAD_DIM, gq), BF16),
         jax.ShapeDtypeStruct((B, T // tq, LANES, IDX_HEADS * tq), BF16),
         jax.ShapeDtypeStruct((B, T // tq, IDX_HEADS, tq), F32)], "dsa_project_prompt")


CODE_NEG_INF = -2139095041
BISECT_FIRST = 20
BISECT_STAGE = 4


def _threshold_of_code(code):
    bits = code ^ ((code >> 31) & jnp.int32(0x7FFFFFFF))
    return jnp.where(code < CODE_NEG_INF, -jnp.inf, pltpu.bitcast(bits, F32))


def _reduce_row_groups(x, op, n_chains=8):
    parts = [x[r:r + SUBLANES] for r in range(0, x.shape[0], SUBLANES)]
    accs = parts[:n_chains]
    for j, part in enumerate(parts[n_chains:]):
        accs[j % len(accs)] = op(accs[j % len(accs)], part)
    while len(accs) > 1:
        accs = [op(a, b) for a, b in zip(accs[0::2], accs[1::2])] + (accs[-1:] if len(accs) % 2 else [])
    return accs[0]


def _sum_row_groups(x):
    return _reduce_row_groups(x, jnp.add)


def _max_row_groups(x):
    return _reduce_row_groups(x, jnp.maximum)


def _kth_largest_threshold(count_ge, shape, k):
    def body(s, p):
        cand = p + lax.shift_left(jnp.int32(1), 31 - s)
        return jnp.where(count_ge(_threshold_of_code(cand)) >= k, cand, p)
    return _threshold_of_code(lax.fori_loop(0, 32, body, jnp.full(shape, INT_MIN, I32)))


def _tie_index_bound(count_eq_below, shape, need, n_bits):
    def body(s, m):
        cand = m + lax.shift_left(jnp.int32(1), n_bits - 1 - s)
        return jnp.where(count_eq_below(cand) < need, cand, m)
    return lax.fori_loop(0, n_bits, body, jnp.zeros(shape, I32))


def _dsa_prompt_kernel(ki_ref, qit_ref, wit_ref, k_ref, vt_ref, qt_ref, o_ref,
                       score_scr, bias_scr, midx_scr, logit_scr, thr_scr, nge_scr,
                       *, tq, kb_size, cb_size, topk, idx_bits):
    i = pl.program_id(1)
    n_kb = ((i + 1) * tq + kb_size - 1) // kb_size
    n_cb = ((i + 1) * tq + cb_size - 1) // cb_size
    q_pos = i * tq + lax.broadcasted_iota(I32, (1, tq), 1)
    heads_per_dot = 2

    @pl.when(i == 0)
    def _():
        score_scr[...] = jnp.full(score_scr.shape, -jnp.inf, F32)

    def key_rows(kb):
        return pl.ds(pl.multiple_of(kb * kb_size, kb_size), kb_size)

    def l_index(kb, size=kb_size):
        return kb * size + lax.broadcasted_iota(I32, (size, 1), 0)

    def score_body(kb, carry):
        kib = ki_ref[key_rows(kb), :]
        acc = jnp.zeros((kb_size, tq), F32)
        for hp in range(IDX_HEADS // heads_per_dot):
            s = jnp.dot(kib, qit_ref[:, hp * heads_per_dot * tq:(hp + 1) * heads_per_dot * tq],
                        preferred_element_type=F32)
            for j in range(heads_per_dot):
                h = hp * heads_per_dot + j
                w = wit_ref[h:h + 1, :] * (IDX_HEADS ** -0.5 * IDX_DIM ** -0.5)
                acc = acc + jnp.maximum(s[:, j * tq:(j + 1) * tq], 0.0) * w
        allowed = l_index(kb) <= q_pos
        score_scr[key_rows(kb), :] = jnp.where(allowed, acc, -jnp.inf)
        return carry

    lax.fori_loop(0, n_kb, score_body, 0)

    def count(pred_fn):
        def body(cb, acc):
            rows = pl.ds(pl.multiple_of(cb * cb_size, cb_size), cb_size)
            m = jnp.where(pred_fn(score_scr[rows, :], cb), 1.0, 0.0)
            return acc + _sum_row_groups(m)
        acc = lax.fori_loop(0, n_cb, body, jnp.zeros((SUBLANES, tq), F32))
        return acc.sum(axis=0, keepdims=True)

    k_f = float(topk)

    def bisect(s_lo, s_hi):
        def body(s, carry):
            p, n_p = carry
            cand = p + lax.shift_left(jnp.int32(1), 31 - s)
            cand_thr = _threshold_of_code(cand)
            n_c = count(lambda score, cb: score >= cand_thr)
            take = n_c >= k_f
            return jnp.where(take, cand, p), jnp.where(take, n_c, n_p)
        p, n_p = lax.fori_loop(s_lo, s_hi, body, (thr_scr[...], nge_scr[...]))
        thr_scr[...] = p
        nge_scr[...] = n_p

    thr_scr[...] = jnp.full((1, tq), INT_MIN, I32)
    nge_scr[...] = jnp.full((1, tq), float(2 ** 30), F32)
    bisect(0, BISECT_FIRST)
    for s0 in range(BISECT_FIRST, 32, BISECT_STAGE):
        @pl.when(jnp.max(jnp.where(nge_scr[...] != k_f, 1.0, 0.0)) > 0.5)
        def _():
            bisect(s0, min(32, s0 + BISECT_STAGE))

    thr = _threshold_of_code(thr_scr[...])
    n_ge = nge_scr[...]
    n_gt = count(lambda score, cb: score > thr)
    need = k_f - n_gt
    excess = (n_ge - n_gt > need) & (thr > -jnp.inf)
    midx_scr[...] = jnp.full((1, tq), 2 ** idx_bits, I32)

    @pl.when(jnp.max(jnp.where(excess, 1.0, 0.0)) > 0.5)
    def _():
        m = _tie_index_bound(
            lambda c: count(lambda score, cb: (score == thr) & (l_index(cb, cb_size) < c)),
            (1, tq), need, idx_bits)
        midx_scr[...] = jnp.where(excess, m, 2 ** idx_bits)

    midx = midx_scr[...]

    def bias_body(kb, carry):
        score = score_scr[key_rows(kb), :]
        l = l_index(kb)
        sel = ((score > thr) | ((score == thr) & (l <= midx))) & (l <= q_pos)
        bias_scr[key_rows(kb), :] = jnp.where(sel, 0.0, NEG_BIG)
        return carry

    lax.fori_loop(0, n_kb, bias_body, 0)

    gq = DSA_GROUP * tq
    heads = range(DSA_KV_HEADS)

    def logits_of(kb):
        bias = bias_scr[key_rows(kb), :]
        bias = jnp.concatenate([bias] * DSA_GROUP, axis=1)
        blk_max = []
        for n in heads:
            kblk = k_ref[key_rows(kb), n * DSA_HEAD_DIM:(n + 1) * DSA_HEAD_DIM]
            logits = jnp.dot(kblk, qt_ref[n], preferred_element_type=F32) + bias
            logit_scr[n, key_rows(kb), :] = logits
            blk_max.append(_max_row_groups(logits))
        return tuple(blk_max)

    def absorb(kb, blk_max, state):
        new = []
        for n in heads:
            m_run, l_run, acc = state[n]
            m_new = jnp.maximum(m_run, jnp.max(blk_max[n], axis=0, keepdims=True))
            alpha = jnp.exp2(m_run - m_new)
            p = jnp.exp2(logit_scr[n, key_rows(kb), :] - m_new)
            acc = acc * alpha + jnp.dot(vt_ref[n, kb], p.astype(BF16), preferred_element_type=F32)
            new.append((m_new, l_run * alpha + _sum_row_groups(p), acc))
        return tuple(new)

    def att_body(kb, carry):
        blk_max, state = carry
        state = absorb(kb - 1, blk_max, state)
        return logits_of(kb), state

    state0 = tuple((jnp.full((1, gq), NEG_BIG, F32), jnp.zeros((SUBLANES, gq), F32),
                    jnp.zeros((DSA_HEAD_DIM, gq), F32)) for _ in heads)
    last_max, state = lax.fori_loop(1, n_kb, att_body, (logits_of(0), state0))
    fin = absorb(n_kb - 1, last_max, state)
    for n in heads:
        _, l8, acc = fin[n]
        o_t = acc / jnp.sum(l8, axis=0, keepdims=True)
        for g in range(DSA_GROUP):
            h = n * DSA_GROUP + g
            o_ref[:, h * DSA_HEAD_DIM:(h + 1) * DSA_HEAD_DIM] = (
                o_t[:, g * tq:(g + 1) * tq].T.astype(o_ref.dtype))


def dsa_prompt_attend(kw_bf, qi_t, wi_t, k_bf, v_t, q_t, B, T, topk, tq, kb_size):
    nqt = T // tq
    nkb = T // kb_size
    gq = DSA_GROUP * tq
    return pl.pallas_call(
        functools.partial(_dsa_prompt_kernel, tq=tq, kb_size=kb_size, cb_size=min(512, T), topk=topk,
                          idx_bits=max(1, (T - 1).bit_length())),
        grid=(B, nqt),
        in_specs=[pl.BlockSpec((None, T, LANES), lambda b, i: (b, 0, 0)),
                  pl.BlockSpec((None, None, LANES, IDX_HEADS * tq), lambda b, i: (b, i, 0, 0)),
                  pl.BlockSpec((None, None, IDX_HEADS, tq), lambda b, i: (b, i, 0, 0)),
                  pl.BlockSpec((None, T, DSA_KV), lambda b, i: (b, 0, 0)),
                  pl.BlockSpec((None, DSA_KV_HEADS, nkb, DSA_HEAD_DIM, kb_size),
                               lambda b, i: (b, 0, 0, 0, 0)),
                  pl.BlockSpec((None, None, DSA_KV_HEADS, DSA_HEAD_DIM, gq),
                               lambda b, i: (b, i, 0, 0, 0))],
        out_specs=pl.BlockSpec((tq, DSA_Q), lambda b, i: (b * nqt + i, 0)),
        out_shape=jax.ShapeDtypeStruct((B * T, DSA_Q), BF16),
        scratch_shapes=[pltpu.VMEM((T, tq), F32), pltpu.VMEM((T, tq), F32), pltpu.VMEM((1, tq), I32),
                        pltpu.VMEM((DSA_KV_HEADS, T, gq), F32),
                        pltpu.VMEM((1, tq), I32), pltpu.VMEM((1, tq), F32)],
        compiler_params=_cparams(("arbitrary", "arbitrary")),
        name="dsa_prompt_attend",
    )(kw_bf.reshape(B, T, LANES), qi_t, wi_t, k_bf.reshape(B, T, DSA_KV), v_t, q_t)


def _dsa_sample_score_kernel(pt_ref, qi_ref, wi_ref, kis_ref, kidx_hbm, o_ref, buf, sem, *, n_pages, page):
    slot = _fetch_pages_double_buffered(pt_ref, (kidx_hbm,), (buf,), sem, n_pages)
    pages = [buf.at[slot, j] for j in range(n_pages)]
    qi = qi_ref[...]
    w = wi_ref[...] * (IDX_HEADS ** -0.5 * IDX_DIM ** -0.5)
    for j in range(n_pages):
        dots = jnp.dot(qi, pages[j][...].astype(BF16), preferred_element_type=F32)
        o_ref[:, j * page:(j + 1) * page] = jnp.sum(jnp.maximum(dots, 0.0) * w, axis=0, keepdims=True)
    ki_self = kis_ref[...].astype(BF16).astype(F32)
    d_self = jnp.sum(qi.astype(F32) * ki_self, -1, keepdims=True)
    s_self = jnp.sum(jnp.maximum(d_self, 0.0) * w, axis=0, keepdims=True)
    lane = lax.broadcasted_iota(I32, (1, LANES), 1)
    o_ref[:, n_pages * page:] = jnp.where(lane == 0, s_self, -jnp.inf)


def dsa_sample_scores(page_table, qi_bf, wi, ki_self, cache_kidx_t):
    DB, n_pages = page_table.shape
    page = cache_kidx_t.shape[2]
    lp = n_pages * page + LANES
    per_b = lambda *shape: pl.BlockSpec((None,) + shape, lambda b, pt: (b,) + (0,) * len(shape))
    return pl.pallas_call(
        functools.partial(_dsa_sample_score_kernel, n_pages=n_pages, page=page),
        grid_spec=pltpu.PrefetchScalarGridSpec(
            num_scalar_prefetch=1, grid=(DB,),
            in_specs=[per_b(IDX_HEADS, IDX_DIM), per_b(IDX_HEADS, 1), per_b(1, IDX_DIM),
                      pl.BlockSpec(memory_space=pl.ANY)],
            out_specs=per_b(1, lp),
            scratch_shapes=[pltpu.VMEM((2, n_pages, IDX_DIM, page), F32), pltpu.SemaphoreType.DMA((2,))]),
        out_shape=jax.ShapeDtypeStruct((DB, 1, lp), F32),
        compiler_params=_cparams(("arbitrary",)),
        name="dsa_sample_scores",
    )(page_table, qi_bf.reshape(DB, IDX_HEADS, IDX_DIM), wi.reshape(DB, IDX_HEADS, 1),
      ki_self.reshape(DB, 1, IDX_DIM), cache_kidx_t)


def _dsa_sample_select_kernel(s_ref, o_ref, *, n_keys, topk, idx_bits):
    score = s_ref[...]
    lane = lax.broadcasted_iota(I32, score.shape, 1)
    valid = lane < n_keys
    score = jnp.where(valid, score, -jnp.inf)

    def count(pred):
        return jnp.sum(jnp.where(pred, 1.0, 0.0), -1, keepdims=True)

    shape = (score.shape[0], 1)
    thr = _kth_largest_threshold(lambda t: count(score >= t), shape, float(topk))
    n_gt = count(score > thr)
    need = float(topk) - n_gt
    excess = (count(score == thr) > need) & (thr > -jnp.inf)
    m = _tie_index_bound(lambda c: count((score == thr) & (lane < c)), shape, need, idx_bits)
    midx = jnp.where(excess, m, 2 ** idx_bits)
    sel = ((score > thr) | ((score == thr) & (lane <= midx))) & valid
    o_ref[...] = jnp.where(sel, 0.0, NEG_BIG)


def dsa_sample_select(score, n_keys, topk):
    DB, lp = score.shape
    return pl.pallas_call(
        functools.partial(_dsa_sample_select_kernel, n_keys=n_keys, topk=topk,
                          idx_bits=max(1, (lp - 1).bit_length())),
        grid=(1,),
        in_specs=[pl.BlockSpec((DB, lp), lambda i: (0, 0))],
        out_specs=pl.BlockSpec((DB, lp), lambda i: (0, 0)),
        out_shape=jax.ShapeDtypeStruct((DB, lp), F32),
        compiler_params=_cparams(("arbitrary",)),
        name="dsa_sample_select",
    )(score)


def _page_copies(pt_ref, b, slot, hbm_refs, bufs, sem, n_pages):
    return [pltpu.make_async_copy(hbm.at[pt_ref[b, j]], buf.at[slot, j], sem.at[slot])
            for hbm, buf in zip(hbm_refs, bufs) for j in range(n_pages)]


def _fetch_pages_double_buffered(pt_ref, hbm_refs, bufs, sem, n_pages):
    b = pl.program_id(0)
    slot = b % 2

    @pl.when(b == 0)
    def _():
        for c in _page_copies(pt_ref, 0, 0, hbm_refs, bufs, sem, n_pages):
            c.start()

    @pl.when(b + 1 < pl.num_programs(0))
    def _():
        for c in _page_copies(pt_ref, b + 1, 1 - slot, hbm_refs, bufs, sem, n_pages):
            c.start()

    for c in _page_copies(pt_ref, b, slot, hbm_refs, bufs, sem, n_pages):
        c.wait()
    return slot


def _dsa_sample_attend_kernel(pt_ref, q_ref, bias_ref, ks_ref, vs_ref, k_hbm, v_hbm, o_ref,
                              kbuf, vbuf, sem, logit_scr, *, n_pages, page):
    slot = _fetch_pages_double_buffered(pt_ref, (k_hbm, v_hbm), (kbuf, vbuf), sem, n_pages)
    k_pages = [kbuf.at[slot, j] for j in range(n_pages)]
    v_pages = [vbuf.at[slot, j] for j in range(n_pages)]
    q = q_ref[...]
    tail = n_pages * page
    lane = lax.broadcasted_iota(I32, (DSA_GROUP, LANES), 1)
    for n in range(DSA_KV_HEADS):
        hs = slice(n * DSA_GROUP, (n + 1) * DSA_GROUP)
        ds = slice(n * DSA_HEAD_DIM, (n + 1) * DSA_HEAD_DIM)
        head_rows = pl.ds(n, page, stride=DSA_KV_HEADS)
        qn = q[hs, :]
        for j in range(n_pages):
            lg = lax.dot_general(qn, k_pages[j][head_rows, :].astype(BF16), (((1,), (1,)), ((), ())),
                                 preferred_element_type=F32)
            logit_scr[hs, j * page:(j + 1) * page] = lg + bias_ref[:, j * page:(j + 1) * page]
        k_self = ks_ref[:, ds].astype(BF16).astype(F32)
        lg_self = jnp.sum(qn.astype(F32) * k_self, -1, keepdims=True)
        logit_scr[hs, tail:] = jnp.where(lane == 0, lg_self, 0.0) + bias_ref[:, tail:]
        logits = logit_scr[hs, :]
        m = jnp.max(logits, -1, keepdims=True)
        p = jnp.exp2(logits - m)
        p_bf = (p / jnp.sum(p, -1, keepdims=True)).astype(BF16)
        v_self = vs_ref[:, ds].astype(BF16).astype(F32)
        acc = p_bf[:, tail:].astype(F32)[:, :1] * v_self
        for j in range(n_pages):
            acc = acc + jnp.dot(p_bf[:, j * page:(j + 1) * page], v_pages[j][head_rows, :].astype(BF16),
                                preferred_element_type=F32)
        o_ref[hs, :] = acc


def dsa_sample_attend(page_table, q_bf, bias, k_self, v_self, cache_k, cache_v):
    DB, n_pages = page_table.shape
    rows = cache_k.shape[1]
    page = rows // DSA_KV_HEADS
    lp = n_pages * page + LANES
    per_b = lambda *shape: pl.BlockSpec((None,) + shape, lambda b, pt: (b,) + (0,) * len(shape))
    o = pl.pallas_call(
        functools.partial(_dsa_sample_attend_kernel, n_pages=n_pages, page=page),
        grid_spec=pltpu.PrefetchScalarGridSpec(
            num_scalar_prefetch=1, grid=(DB,),
            in_specs=[per_b(DSA_HEADS, DSA_HEAD_DIM), per_b(1, lp), per_b(1, DSA_KV), per_b(1, DSA_KV),
                      pl.BlockSpec(memory_space=pl.ANY), pl.BlockSpec(memory_space=pl.ANY)],
            out_specs=per_b(DSA_HEADS, DSA_HEAD_DIM),
            scratch_shapes=[pltpu.VMEM((2, n_pages, rows, DSA_HEAD_DIM), F32),
                            pltpu.VMEM((2, n_pages, rows, DSA_HEAD_DIM), F32),
                            pltpu.SemaphoreType.DMA((2,)),
                            pltpu.VMEM((DSA_HEADS, lp), F32)]),
        out_shape=jax.ShapeDtypeStruct((DB, DSA_HEADS, DSA_HEAD_DIM), F32),
        compiler_params=_cparams(("arbitrary",)),
        name="dsa_sample_attend",
    )(page_table, q_bf.reshape(DB, DSA_HEADS, DSA_HEAD_DIM), bias.reshape(DB, 1, lp),
      k_self.reshape(DB, 1, DSA_KV), v_self.reshape(DB, 1, DSA_KV), cache_k, cache_v)
    return o.reshape(DB, DSA_Q)


def _retnet_tables(pos):
    inv_freq = jnp.power(RET_ANGLE_BASE, -jnp.linspace(0.0, 1.0, RET_DK // 2, dtype=F32))
    ang = pos.astype(F32)[:, None] * inv_freq[None, :]
    return jnp.cos(ang), jnp.sin(ang)


def _split_mod(mod_l, G, R):
    return [m.reshape(G, R, D_MODEL) for m in jnp.split(mod_l, 6, axis=-1)]


MOE_GROUP_TILE = 1024
MOE_GROUP_CAP = 256


def _channel_mixer(x, sc2, sh2, g2, l, wts, tm):
    gates = router(x, sc2, sh2, wts["w_router"], wts["b_router"], tm)
    args = (x, sc2, sh2, g2, gates, l, wts["w_gate"], wts["w_up"], wts["w_down"],
            wts["ln2_g"][l], wts["ln2_b"][l])
    rows_per_mod = x.shape[0] // sc2.shape[0]
    if sc2.shape[1] == 1 and rows_per_mod >= 2 * MOE_GROUP_CAP:
        tile = min(MOE_GROUP_TILE, rows_per_mod)
        return moe_group_ln(*args, tile, MOE_GROUP_CAP)
    return moe_ln(*args, tm)


def kernel(x_prompt, x_sample, state_ret, cache_k, cache_v, cache_kidx, page_table,
           c_prompt, c_sample, w_mod, b_mod, ln1_g, ln1_b, ln2_g, ln2_b,
           w_in_ret, gn_ret_g, w_out_ret, w_in_dsa, w_out_dsa,
           w_router, b_router, w_gate, w_up, w_down):
    B, T, _ = x_prompt.shape
    DB = x_sample.shape[0]
    assert x_sample.shape[1] == 1
    n_pages = page_table.shape[1]
    page = cache_k.shape[2]
    past = n_pages * page
    n_pool = cache_k.shape[1]

    wts = dict(w_router=w_router, b_router=b_router, ln2_g=ln2_g, ln2_b=ln2_b,
               w_gate=w_gate.astype(BF16), w_up=w_up.astype(BF16), w_down=w_down.astype(BF16))
    w_mod_bf = w_mod.astype(BF16)
    w_in_ret_bf = w_in_ret[0].astype(BF16)
    w_out_ret_bf = w_out_ret[0].astype(BF16)
    w_in_dsa_bf = jnp.pad(w_in_dsa[0], ((0, 0), (0, DSA_IN_PAD - DSA_IN))).astype(BF16)
    w_out_dsa_bf = w_out_dsa[0].astype(BF16)

    pos_p = jnp.arange(T, dtype=I32)
    pos_s = jnp.full((1,), past, I32)

    tm_p = min(512, T)
    tq = min(128, T)
    xp = x_prompt.reshape(B * T, D_MODEL)
    mod_p = modulation(c_prompt, w_mod_bf, b_mod)
    sh1, sc1, g1, sh2, sc2, g2 = _split_mod(mod_p[0], B, 1)
    proj = mod_proj(xp, sc1, sh1, w_in_ret_bf, min(1024, T), 1536, BF16)
    cos_p, sin_p = _retnet_tables(pos_p)
    gated, ret_p = retention_prompt(proj.reshape(B, T, RET_IN), gn_ret_g[0], cos_p, sin_p, min(256, T))
    xp = out_proj_ln(gated.reshape(B * T, RET_V), w_out_ret_bf, xp, g1, ln1_g[0], ln1_b[0], tm_p)
    xp = _channel_mixer(xp, sc2, sh2, g2, 0, wts, tm_p)

    sh1, sc1, g1, sh2, sc2, g2 = _split_mod(mod_p[1], B, 1)
    k_p, v_p, kw_p, k_bf, kw_bf, v_t, q_t, qi_t, wi_t = dsa_project_prompt(
        xp, sc1, sh1, w_in_dsa_bf, pos_p, B, T, tm_p, tq)
    o_p = dsa_prompt_attend(kw_bf, qi_t, wi_t, k_bf, v_t, q_t, B, T, min(DSA_TOPK, T // 4), tq, tm_p)
    xp = out_proj_ln(o_p, w_out_dsa_bf, xp, g1, ln1_g[1], ln1_b[1], tm_p)
    xp = _channel_mixer(xp, sc2, sh2, g2, 1, wts, tm_p)

    xs = x_sample.reshape(DB, D_MODEL)
    mod_s = modulation(c_sample, w_mod_bf, b_mod)
    sh1, sc1, g1, sh2, sc2, g2 = _split_mod(mod_s[0], 1, DB)
    proj_s = mod_proj(xs, sc1, sh1, w_in_ret_bf, DB, 1536, F32)
    cos_s, sin_s = _retnet_tables(pos_s)
    gated_s, ret_s = retention_sample(proj_s, state_ret[0], gn_ret_g[0], cos_s, sin_s)
    xs = out_proj_ln(gated_s.reshape(DB, RET_V), w_out_ret_bf, xs, g1, ln1_g[0], ln1_b[0], DB)
    xs = _channel_mixer(xs, sc2, sh2, g2, 0, wts, DB)

    sh1, sc1, g1, sh2, sc2, g2 = _split_mod(mod_s[1], 1, DB)
    pos_rows = jnp.full((DB,), past, I32)
    q_s, k_s, v_s, qi_s, kw_s = dsa_project(xs, sc1, sh1, w_in_dsa_bf, pos_rows, DB)
    ki_s = kw_s[:, :IDX_DIM]
    score = dsa_sample_scores(page_table, qi_s, kw_s[:, IDX_DIM:IDX_DIM + IDX_HEADS], ki_s,
                              jnp.swapaxes(cache_kidx[0], 1, 2))
    bias = dsa_sample_select(score.reshape(DB, past + LANES), past + 1, min(DSA_TOPK, (past + 1) // 4))
    o_s = dsa_sample_attend(page_table, q_s, bias, k_s, v_s,
                            cache_k[0].reshape(n_pool, page * DSA_KV_HEADS, DSA_HEAD_DIM),
                            cache_v[0].reshape(n_pool, page * DSA_KV_HEADS, DSA_HEAD_DIM))
    xs = out_proj_ln(o_s, w_out_dsa_bf, xs, g1, ln1_g[1], ln1_b[1], DB)
    xs = _channel_mixer(xs, sc2, sh2, g2, 1, wts, DB)

    kv_shape = (DSA_KV_HEADS, DSA_HEAD_DIM)
    return (xp.reshape(B, T, D_MODEL), xs.reshape(DB, 1, D_MODEL),
            ret_p[None], ret_s[None],
            k_p.reshape(1, B, T, *kv_shape), v_p.reshape(1, B, T, *kv_shape),
            kw_p[:, :IDX_DIM].reshape(1, B, T, IDX_DIM),
            k_s.reshape(1, DB, 1, *kv_shape), v_s.reshape(1, DB, 1, *kv_shape),
            ki_s.reshape(1, DB, 1, IDX_DIM))
```

```python
import functools

import jax
import jax.numpy as jnp
from jax import lax
from jax.experimental import pallas as pl
from jax.experimental.pallas import tpu as pltpu

F32 = jnp.float32
BF16 = jnp.bfloat16
I32 = jnp.int32

D_MODEL = 1024
DEPTH = 2
ALPHA = (2.0 * DEPTH) ** 0.25
LN_EPS = 1e-5
GN_EPS = 1e-6

RET_HEADS = 4
RET_DK = 256
RET_DV = 512
RET_QK = RET_HEADS * RET_DK
RET_V = RET_HEADS * RET_DV
RET_IN = 2 * RET_QK + 2 * RET_V
RET_ANGLE_BASE = 10000.0

DSA_HEADS = 8
DSA_KV_HEADS = 2
DSA_HEAD_DIM = 128
DSA_GROUP = DSA_HEADS // DSA_KV_HEADS
DSA_Q = DSA_HEADS * DSA_HEAD_DIM
DSA_KV = DSA_KV_HEADS * DSA_HEAD_DIM
IDX_HEADS = 16
IDX_DIM = 64
DSA_TOPK = 256
ROPE_THETA = 500000.0
ROT_DIM = DSA_HEAD_DIM // 4
IDX_ROT_DIM = IDX_DIM // 4
DSA_IN = DSA_Q + 2 * DSA_KV + IDX_HEADS * IDX_DIM + IDX_DIM + IDX_HEADS

N_EXPERTS = 16
N_GROUPS = 4
EXPERTS_PER_GROUP = N_EXPERTS // N_GROUPS
D_EXPERT = 512
GROUP_LANE0 = N_EXPERTS

LANES = 128
SUBLANES = 8
VMEM_LIMIT = 56 * 1024 * 1024
NEG_BIG = -1e30
INT_MIN = -2147483648


def _cparams(sem):
    return pltpu.CompilerParams(dimension_semantics=sem, vmem_limit_bytes=VMEM_LIMIT)


def _silu(x):
    return x * (1.0 / (1.0 + jnp.exp(-x)))


def _layer_norm(z, g, b):
    mu = jnp.mean(z, -1, keepdims=True)
    d = z - mu
    var = jnp.mean(d * d, -1, keepdims=True)
    return d * lax.rsqrt(var + LN_EPS) * g + b


def _mod_kernel(c_ref, w_ref, b_ref, o_ref):
    a = _silu(c_ref[...]).astype(BF16)
    o_ref[...] = jnp.dot(a, w_ref[...], preferred_element_type=F32) + b_ref[...]


def modulation(c, w_mod_bf, b_mod):
    R = c.shape[0]
    tn = 1536
    return pl.pallas_call(
        _mod_kernel,
        grid=(DEPTH, 6 * D_MODEL // tn),
        in_specs=[pl.BlockSpec((R, D_MODEL), lambda l, j: (0, 0)),
                  pl.BlockSpec((None, D_MODEL, tn), lambda l, j: (l, 0, j)),
                  pl.BlockSpec((None, 1, tn), lambda l, j: (l, 0, j))],
        out_specs=pl.BlockSpec((None, R, tn), lambda l, j: (l, 0, j)),
        out_shape=jax.ShapeDtypeStruct((DEPTH, R, 6 * D_MODEL), F32),
        compiler_params=_cparams(("parallel", "parallel")),
        name="modulation",
    )(c, w_mod_bf, b_mod.reshape(DEPTH, 1, 6 * D_MODEL))


def _proj_kernel(x_ref, sc_ref, sh_ref, w_ref, o_ref, h_scr):
    @pl.when(pl.program_id(1) == 0)
    def _():
        h_scr[...] = (x_ref[...] * (1.0 + sc_ref[...]) + sh_ref[...]).astype(BF16)

    o_ref[...] = jnp.dot(h_scr[...], w_ref[...], preferred_element_type=F32).astype(o_ref.dtype)


def _mod_spec(R, tiles_per_group):
    return pl.BlockSpec((None, R, D_MODEL), lambda i, *_: (i // tiles_per_group, 0, 0))


def mod_proj(x, sc, sh, w_bf, tm, tn, out_dtype):
    N = x.shape[0]
    G, R, _ = sc.shape
    n_out = w_bf.shape[1]
    tpg = (N // G) // tm
    return pl.pallas_call(
        _proj_kernel,
        grid=(N // tm, n_out // tn),
        in_specs=[pl.BlockSpec((tm, D_MODEL), lambda i, j: (i, 0)),
                  _mod_spec(R, tpg), _mod_spec(R, tpg),
                  pl.BlockSpec((D_MODEL, tn), lambda i, j: (0, j))],
        out_specs=pl.BlockSpec((tm, tn), lambda i, j: (i, j)),
        out_shape=jax.ShapeDtypeStruct((N, n_out), out_dtype),
        scratch_shapes=[pltpu.VMEM((tm, D_MODEL), BF16)],
        compiler_params=_cparams(("parallel", "arbitrary")),
        name="mod_proj",
    )(x, sc, sh, w_bf)


def _rot_half(x, cos, sin):
    half = cos.shape[-1]
    x1, x2 = x[:, :half], x[:, half:]
    return jnp.concatenate([x1 * cos - x2 * sin, x1 * sin + x2 * cos], axis=1)


def _group_norm_gate(o, gn, g):
    mu = jnp.mean(o, -1, keepdims=True)
    d = o - mu
    var = jnp.mean(d * d, -1, keepdims=True)
    return d * lax.rsqrt(var + GN_EPS) * gn * _silu(g)


def _ret_prompt_kernel(q_ref, k_ref, v_ref, g_ref, cos_ref, sin_ref, lg_ref, gn_ref,
                       o_ref, s_ref, s_scr, *, chunk):
    c = pl.program_id(2)

    @pl.when(c == 0)
    def _():
        s_scr[...] = jnp.zeros_like(s_scr)

    cos, sin = cos_ref[...], sin_ref[...]
    lg_row = lg_ref[...]
    lg = lg_row[:, :1]
    q = _rot_half(q_ref[...].astype(F32), cos, sin)
    k = _rot_half(k_ref[...].astype(F32), cos, sin) * (RET_DK ** -0.5)
    vb = v_ref[...].astype(BF16)
    qb = q.astype(BF16)

    row = lax.broadcasted_iota(I32, (chunk, chunk), 0)
    col = lax.broadcasted_iota(I32, (chunk, chunk), 1)
    diff = (row - col).astype(F32)
    decay = jnp.where(diff >= 0, jnp.exp(lg_row * jnp.maximum(diff, 0.0)), 0.0)
    idx = lax.broadcasted_iota(I32, (chunk, 1), 0).astype(F32)
    q_dec = jnp.exp(lg * (idx + 1.0))
    k_dec = jnp.exp(lg * (chunk - 1.0 - idx))
    chunk_dec = jnp.exp(lg * chunk)

    s = s_scr[...]
    att = lax.dot_general(qb, k.astype(BF16), (((1,), (1,)), ((), ())),
                          preferred_element_type=F32) * decay
    o = (jnp.dot(att.astype(BF16), vb, preferred_element_type=F32)
         + jnp.dot(qb, s.astype(BF16), preferred_element_type=F32) * q_dec)
    kd_t = (k * k_dec).T.astype(BF16)
    s_new = chunk_dec * s + jnp.dot(kd_t, vb, preferred_element_type=F32)
    s_scr[...] = s_new
    o_ref[...] = _group_norm_gate(o, gn_ref[...], g_ref[...].astype(F32)).astype(o_ref.dtype)

    @pl.when(c == pl.num_programs(2) - 1)
    def _():
        s_ref[...] = s_new


def _log_gamma():
    return jnp.log(1.0 - jnp.power(2.0, -5.0 - jnp.arange(RET_HEADS, dtype=F32)))


def retention_prompt(proj, gn_g, cos, sin, chunk):
    B, T, _ = proj.shape
    lg_tab = jnp.broadcast_to(_log_gamma()[:, None, None], (RET_HEADS, 1, chunk))
    qk_blk = lambda off: pl.BlockSpec((None, chunk, RET_DK), lambda b, h, c: (b, c, off + h))
    v_blk = lambda off: pl.BlockSpec((None, chunk, RET_DV), lambda b, h, c: (b, c, off + h))
    tab = pl.BlockSpec((chunk, RET_DK // 2), lambda b, h, c: (c, 0))
    return pl.pallas_call(
        functools.partial(_ret_prompt_kernel, chunk=chunk),
        grid=(B, RET_HEADS, T // chunk),
        in_specs=[qk_blk(0), qk_blk(RET_QK // RET_DK),
                  v_blk(2 * RET_QK // RET_DV), v_blk((2 * RET_QK + RET_V) // RET_DV),
                  tab, tab,
                  pl.BlockSpec((None, 1, chunk), lambda b, h, c: (h, 0, 0)),
                  pl.BlockSpec((1, RET_DV), lambda b, h, c: (0, h))],
        out_specs=[pl.BlockSpec((None, chunk, RET_DV), lambda b, h, c: (b, c, h)),
                   pl.BlockSpec((None, None, RET_DK, RET_DV), lambda b, h, c: (b, h, 0, 0))],
        out_shape=[jax.ShapeDtypeStruct((B, T, RET_V), BF16),
                   jax.ShapeDtypeStruct((B, RET_HEADS, RET_DK, RET_DV), F32)],
        scratch_shapes=[pltpu.VMEM((RET_DK, RET_DV), F32)],
        compiler_params=_cparams(("parallel", "parallel", "arbitrary")),
        name="retention_prompt",
    )(proj, proj, proj, proj, cos, sin, lg_tab, gn_g.reshape(1, RET_V))


def _ret_sample_kernel(qk_ref, v_ref, g_ref, s0_ref, cos_ref, sin_ref, gam_ref, gn_ref,
                       o_ref, s_ref):
    t = _rot_half(qk_ref[...], cos_ref[...], sin_ref[...])
    row = lax.broadcasted_iota(I32, t.shape, 0)
    t = jnp.where(row >= RET_HEADS, t * (RET_DK ** -0.5), t)
    pad = jnp.zeros((LANES - 2 * RET_HEADS, RET_DK), F32)
    t_t = jnp.concatenate([t, pad], axis=0).T
    for h in range(RET_HEADS):
        qc = t_t[:, h:h + 1]
        kc = t_t[:, RET_HEADS + h:RET_HEADS + h + 1]
        gam = gam_ref[h]
        vh = v_ref[h:h + 1, :]
        s = s0_ref[h]
        qk_dot = jnp.sum(t[h:h + 1, :] * t[RET_HEADS + h:RET_HEADS + h + 1, :], -1, keepdims=True)
        o = qk_dot * vh + gam * jnp.sum(s * qc, axis=0, keepdims=True)
        s_ref[h] = gam * s + kc * vh
        o_ref[h:h + 1, :] = _group_norm_gate(o, gn_ref[h:h + 1, :], g_ref[h:h + 1, :])


def retention_sample(proj, s0, gn_g, cos, sin):
    DB = proj.shape[0]
    qk = proj[:, :2 * RET_QK].reshape(DB, 2 * RET_HEADS, RET_DK)
    v = proj[:, 2 * RET_QK:2 * RET_QK + RET_V].reshape(DB, RET_HEADS, RET_DV)
    g = proj[:, 2 * RET_QK + RET_V:].reshape(DB, RET_HEADS, RET_DV)
    gam = jnp.broadcast_to(jnp.exp(_log_gamma())[:, None, None], (RET_HEADS, 1, RET_DV))
    full = lambda *shape: pl.BlockSpec(shape, lambda b: (0,) * len(shape))
    per_b = lambda *shape: pl.BlockSpec((None,) + shape, lambda b: (b,) + (0,) * len(shape))
    return pl.pallas_call(
        _ret_sample_kernel,
        grid=(DB,),
        in_specs=[per_b(2 * RET_HEADS, RET_DK), per_b(RET_HEADS, RET_DV), per_b(RET_HEADS, RET_DV),
                  per_b(RET_HEADS, RET_DK, RET_DV),
                  full(1, RET_DK // 2), full(1, RET_DK // 2),
                  full(RET_HEADS, 1, RET_DV), full(RET_HEADS, RET_DV)],
        out_specs=[per_b(RET_HEADS, RET_DV), per_b(RET_HEADS, RET_DK, RET_DV)],
        out_shape=[jax.ShapeDtypeStruct((DB, RET_HEADS, RET_DV), F32),
                   jax.ShapeDtypeStruct((DB, RET_HEADS, RET_DK, RET_DV), F32)],
        compiler_params=_cparams(("parallel",)),
        name="retention_sample",
    )(qk, v, g, s0, cos, sin, gam, gn_g.reshape(RET_HEADS, RET_DV))


def _out_ln_kernel(a_ref, w_ref, x_ref, g_ref, lng_ref, lnb_ref, o_ref):
    y = jnp.dot(a_ref[...].astype(BF16), w_ref[...], preferred_element_type=F32)
    o_ref[...] = _layer_norm(ALPHA * x_ref[...] + g_ref[...] * y, lng_ref[...], lnb_ref[...])


def out_proj_ln(a, w_bf, x, gate, ln_g, ln_b, tm):
    N, K = a.shape
    G, R, _ = gate.shape
    tpg = (N // G) // tm
    row = pl.BlockSpec((1, D_MODEL), lambda i: (0, 0))
    return pl.pallas_call(
        _out_ln_kernel,
        grid=(N // tm,),
        in_specs=[pl.BlockSpec((tm, K), lambda i: (i, 0)),
                  pl.BlockSpec((K, D_MODEL), lambda i: (0, 0)),
                  pl.BlockSpec((tm, D_MODEL), lambda i: (i, 0)),
                  _mod_spec(R, tpg), row, row],
        out_specs=pl.BlockSpec((tm, D_MODEL), lambda i: (i, 0)),
        out_shape=jax.ShapeDtypeStruct((N, D_MODEL), F32),
        compiler_params=_cparams(("parallel",)),
        name="out_proj_ln",
    )(a, w_bf, x, gate, ln_g.reshape(1, D_MODEL), ln_b.reshape(1, D_MODEL))


def _lane_roll(x, shift):
    return pltpu.roll(x, shift, axis=1)


def _router_kernel(x_ref, sc_ref, sh_ref, whi_ref, wlo_ref, b_ref, o_ref, gt_scr):
    h = x_ref[...] * (1.0 + sc_ref[...]) + sh_ref[...]
    hi = h.astype(BF16)
    lo = (h - hi.astype(F32)).astype(BF16)
    whi, wlo = whi_ref[...], wlo_ref[...]
    logits = (jnp.dot(hi, whi, preferred_element_type=F32)
              + jnp.dot(hi, wlo, preferred_element_type=F32)
              + jnp.dot(lo, whi, preferred_element_type=F32)) + b_ref[...]
    lt = logits.T
    l_rows = [lt[e:e + 1, :] for e in range(N_EXPERTS)]
    m = functools.reduce(jnp.maximum, l_rows)
    ex = [jnp.exp(l - m) for l in l_rows]
    denom = functools.reduce(jnp.add, ex)
    p = [x / denom for x in ex]

    def first_max(vals):
        best, idx = vals[0], jnp.zeros_like(vals[0])
        for j, val in enumerate(vals[1:], 1):
            take = val > best
            idx = jnp.where(take, float(j), idx)
            best = jnp.where(take, val, best)
        return best, idx

    grp_score = []
    for g in range(N_GROUPS):
        a, b, c, d = p[g * EXPERTS_PER_GROUP:(g + 1) * EXPERTS_PER_GROUP]
        pairs = [a + b, a + c, a + d, b + c, b + d, c + d]
        grp_score.append(functools.reduce(jnp.maximum, pairs))
    _, g_sel = first_max(grp_score)
    neg_inf = jnp.full_like(m, -jnp.inf)
    masked = [jnp.where(g_sel == float(e // EXPERTS_PER_GROUP), p[e], neg_inf) for e in range(N_EXPERTS)]
    v1, i1 = first_max(masked)
    masked2 = [jnp.where(i1 == float(e), neg_inf, masked[e]) for e in range(N_EXPERTS)]
    v2, i2 = first_max(masked2)
    tot = v1 + v2
    w1, w2 = v1 / tot, v2 / tot
    gt_scr[...] = jnp.zeros_like(gt_scr)
    for e in range(N_EXPERTS):
        gt_scr[e:e + 1, :] = jnp.where(i1 == float(e), w1, 0.0) + jnp.where(i2 == float(e), w2, 0.0)
    for g in range(N_GROUPS):
        gt_scr[GROUP_LANE0 + g:GROUP_LANE0 + g + 1, :] = jnp.where(g_sel == float(g), 1.0, 0.0)
    o_ref[...] = gt_scr[...].T


def router(x, sc, sh, w_router, b_router, tm):
    N = x.shape[0]
    G, R, _ = sc.shape
    tpg = (N // G) // tm
    w_pad = jnp.zeros((D_MODEL, LANES), F32).at[:, :N_EXPERTS].set(w_router)
    w_hi = w_pad.astype(BF16)
    w_lo = (w_pad - w_hi.astype(F32)).astype(BF16)
    b_pad = jnp.full((1, LANES), NEG_BIG, F32).at[0, :N_EXPERTS].set(b_router)
    full = lambda r, c: pl.BlockSpec((r, c), lambda i: (0, 0))
    return pl.pallas_call(
        _router_kernel,
        grid=(N // tm,),
        in_specs=[pl.BlockSpec((tm, D_MODEL), lambda i: (i, 0)), _mod_spec(R, tpg), _mod_spec(R, tpg),
                  full(D_MODEL, LANES), full(D_MODEL, LANES), full(1, LANES)],
        out_specs=pl.BlockSpec((tm, LANES), lambda i: (i, 0)),
        out_shape=jax.ShapeDtypeStruct((N, LANES), F32),
        scratch_shapes=[pltpu.VMEM((LANES, tm), F32)],
        compiler_params=_cparams(("parallel",)),
        name="router",
    )(x, sc, sh, w_hi, w_lo, b_pad)


def _moe_kernel(x_ref, sc_ref, sh_ref, g2_ref, gates_ref, wg_ref, wu_ref, wd_ref, lng_ref, lnb_ref,
                o_ref, h_scr, acc_scr):
    e = pl.program_id(1)

    @pl.when(e == 0)
    def _():
        h_scr[...] = (x_ref[...] * (1.0 + sc_ref[...]) + sh_ref[...]).astype(BF16)
        acc_scr[...] = jnp.zeros_like(acc_scr)

    hb = h_scr[...]
    a = (_silu(jnp.dot(hb, wg_ref[...], preferred_element_type=F32))
         * jnp.dot(hb, wu_ref[...], preferred_element_type=F32))
    y = jnp.dot(a.astype(BF16), wd_ref[...], preferred_element_type=F32)
    gates = gates_ref[...]
    lane = lax.broadcasted_iota(I32, gates.shape, 1)
    gate_e = jnp.sum(jnp.where(lane == e, gates, 0.0), -1, keepdims=True)
    acc_scr[...] += gate_e * y

    @pl.when(e == pl.num_programs(1) - 1)
    def _():
        z = ALPHA * x_ref[...] + g2_ref[...] * acc_scr[...]
        o_ref[...] = _layer_norm(z, lng_ref[...], lnb_ref[...])


def moe_ln(x, sc, sh, g2, gates, l, wg_bf, wu_bf, wd_bf, ln_g, ln_b, tm):
    N = x.shape[0]
    G, R, _ = sc.shape
    tpg = (N // G) // tm
    ms = pl.BlockSpec((None, R, D_MODEL), lambda i, e: (i // tpg, 0, 0))
    row = pl.BlockSpec((1, D_MODEL), lambda i, e: (0, 0))
    return pl.pallas_call(
        _moe_kernel,
        grid=(N // tm, N_EXPERTS),
        in_specs=[pl.BlockSpec((tm, D_MODEL), lambda i, e: (i, 0)), ms, ms, ms,
                  pl.BlockSpec((tm, LANES), lambda i, e: (i, 0)),
                  pl.BlockSpec((None, None, D_MODEL, D_EXPERT), lambda i, e: (l, e, 0, 0)),
                  pl.BlockSpec((None, None, D_MODEL, D_EXPERT), lambda i, e: (l, e, 0, 0)),
                  pl.BlockSpec((None, None, D_EXPERT, D_MODEL), lambda i, e: (l, e, 0, 0)),
                  row, row],
        out_specs=pl.BlockSpec((tm, D_MODEL), lambda i, e: (i, 0)),
        out_shape=jax.ShapeDtypeStruct((N, D_MODEL), F32),
        scratch_shapes=[pltpu.VMEM((tm, D_MODEL), BF16), pltpu.VMEM((tm, D_MODEL), F32)],
        compiler_params=_cparams(("parallel", "arbitrary")),
        name="moe_ln",
    )(x, sc, sh, g2, gates, wg_bf, wu_bf, wd_bf, ln_g.reshape(1, D_MODEL), ln_b.reshape(1, D_MODEL))


def _split3_bf16(x):
    hi = x.astype(BF16)
    r = x - hi.astype(F32)
    mid = r.astype(BF16)
    lo = (r - mid.astype(F32)).astype(BF16)
    return hi, mid, lo


def _moe_group_kernel(x_ref, sc_ref, sh_ref, g2_ref, gates_ref, ltri_ref, wg_ref, wu_ref, wd_ref,
                      lng_ref, lnb_ref, o_ref,
                      h_scr, acc_scr, rank_scr, rank_t_scr, sel_t_scr, xs_scr, gs_scr, yacc_scr, cnt_smem,
                      *, cap):
    g, e = pl.program_id(1), pl.program_id(2)
    tm = x_ref.shape[0]
    lane = lax.broadcasted_iota(I32, (tm, LANES), 1)

    @pl.when((g == 0) & (e == 0))
    def _():
        h_scr[...] = (x_ref[...] * (1.0 + sc_ref[...]) + sh_ref[...]).astype(BF16)
        acc_scr[...] = jnp.zeros_like(acc_scr)
        gates = gates_ref[...]
        sel = jnp.where((lane >= GROUP_LANE0) & (lane < GROUP_LANE0 + N_GROUPS), gates, 0.0)
        rank = jnp.dot(ltri_ref[...], sel.astype(BF16), preferred_element_type=F32)
        rank_scr[...] = rank
        rank_t_scr[...] = rank.T
        sel_t_scr[...] = sel.T
        for gg in range(N_GROUPS):
            cnt_smem[gg] = jnp.sum(jnp.where(lane == GROUP_LANE0 + gg, sel, 0.0)).astype(I32)

    half = cap // 2
    cnt = cnt_smem[g]
    rem = cnt % cap
    use_tail = (rem > 0) & (rem <= half)
    n_full = cnt // cap + jnp.where(rem > half, 1, 0)
    tail_start = pl.multiple_of(n_full * cap, half)
    grp_row = pl.ds(GROUP_LANE0 + g, 1)

    def for_chunks(fn):
        def body(c, carry):
            fn(pl.multiple_of(c * cap, cap), cap)
            return carry
        lax.fori_loop(0, n_full, body, 0)

        @pl.when(use_tail)
        def _():
            fn(tail_start, half)

    @pl.when(e == 0)
    def _():
        expert_lanes = jnp.where(lane < N_EXPERTS, gates_ref[...], 0.0)
        parts = _split3_bf16(expert_lanes)
        g_parts = sum(_lane_roll(part.astype(F32), j * N_EXPERTS)
                      for j, part in enumerate(parts)).astype(BF16)
        rank_row, sel_row = rank_t_scr[grp_row, :], sel_t_scr[grp_row, :]

        def dispatch(start, size):
            rows = pl.ds(start, size)
            slot = (start + lax.broadcasted_iota(I32, (size, 1), 0)).astype(F32)
            p = jnp.where((rank_row == slot) & (sel_row > 0.5), 1.0, 0.0).astype(BF16)
            xs_scr[rows, :] = jnp.dot(p, h_scr[...], preferred_element_type=F32).astype(BF16)
            gs_scr[rows, :] = jnp.dot(p, g_parts, preferred_element_type=F32)
            yacc_scr[rows, :] = jnp.zeros((size, D_MODEL), F32)

        for_chunks(dispatch)

    def expert(start, size):
        rows = pl.ds(start, size)
        xs = xs_scr[rows, :]
        a = (_silu(jnp.dot(xs, wg_ref[...], preferred_element_type=F32))
             * jnp.dot(xs, wu_ref[...], preferred_element_type=F32))
        y = jnp.dot(a.astype(BF16), wd_ref[...], preferred_element_type=F32)
        gs = gs_scr[rows, :]
        lane_c = lax.broadcasted_iota(I32, gs.shape, 1)
        is_part = ((lane_c % N_EXPERTS) == g * EXPERTS_PER_GROUP + e) & (lane_c < 3 * N_EXPERTS)
        gate = jnp.sum(jnp.where(is_part, gs, 0.0), -1, keepdims=True)
        yacc_scr[rows, :] += gate * y

    for_chunks(expert)

    @pl.when(e == EXPERTS_PER_GROUP - 1)
    def _():
        in_grp = lane == GROUP_LANE0 + g
        rank_col = jnp.sum(jnp.where(in_grp, rank_scr[...], 0.0), -1, keepdims=True)
        sel_col = jnp.sum(jnp.where(in_grp, gates_ref[...], 0.0), -1, keepdims=True)

        def combine(start, size):
            slot = (start + lax.broadcasted_iota(I32, (1, size), 1)).astype(F32)
            p_t = jnp.where((rank_col == slot) & (sel_col > 0.5), 1.0, 0.0).astype(BF16)
            y_hi, y_mid, _ = _split3_bf16(yacc_scr[pl.ds(start, size), :])
            acc_scr[...] += (jnp.dot(p_t, y_hi, preferred_element_type=F32)
                             + jnp.dot(p_t, y_mid, preferred_element_type=F32))

        for_chunks(combine)

    @pl.when((g == N_GROUPS - 1) & (e == EXPERTS_PER_GROUP - 1))
    def _():
        z = ALPHA * x_ref[...] + g2_ref[...] * acc_scr[...]
        o_ref[...] = _layer_norm(z, lng_ref[...], lnb_ref[...])


def moe_group_ln(x, sc, sh, g2, gates, l, wg_bf, wu_bf, wd_bf, ln_g, ln_b, tm, cap):
    N = x.shape[0]
    G, R, _ = sc.shape
    tpg = (N // G) // tm
    ltri = (jnp.arange(tm)[:, None] > jnp.arange(tm)[None, :]).astype(BF16)
    ms = pl.BlockSpec((None, R, D_MODEL), lambda i, g, e: (i // tpg, 0, 0))
    row = pl.BlockSpec((1, D_MODEL), lambda i, g, e: (0, 0))
    expert = lambda i, g, e: (l, g * EXPERTS_PER_GROUP + e, 0, 0)
    w_in = pl.BlockSpec((None, None, D_MODEL, D_EXPERT), expert)
    w_out = pl.BlockSpec((None, None, D_EXPERT, D_MODEL), expert)
    return pl.pallas_call(
        functools.partial(_moe_group_kernel, cap=cap),
        grid=(N // tm, N_GROUPS, EXPERTS_PER_GROUP),
        in_specs=[pl.BlockSpec((tm, D_MODEL), lambda i, g, e: (i, 0)), ms, ms, ms,
                  pl.BlockSpec((tm, LANES), lambda i, g, e: (i, 0)),
                  pl.BlockSpec((tm, tm), lambda i, g, e: (0, 0)),
                  w_in, w_in, w_out, row, row],
        out_specs=pl.BlockSpec((tm, D_MODEL), lambda i, g, e: (i, 0)),
        out_shape=jax.ShapeDtypeStruct((N, D_MODEL), F32),
        scratch_shapes=[pltpu.VMEM((tm, D_MODEL), BF16), pltpu.VMEM((tm, D_MODEL), F32),
                        pltpu.VMEM((tm, LANES), F32), pltpu.VMEM((LANES, tm), F32),
                        pltpu.VMEM((LANES, tm), F32), pltpu.VMEM((tm, D_MODEL), BF16),
                        pltpu.VMEM((tm, LANES), F32), pltpu.VMEM((tm, D_MODEL), F32),
                        pltpu.SMEM((N_GROUPS,), I32)],
        compiler_params=_cparams(("arbitrary", "arbitrary", "arbitrary")),
        name="moe_group_ln",
    )(x, sc, sh, g2, gates, ltri, wg_bf, wu_bf, wd_bf, ln_g.reshape(1, D_MODEL), ln_b.reshape(1, D_MODEL))


LOG2_E = 1.4426950408889634
Q_PRESCALE = DSA_HEAD_DIM ** -0.5 * LOG2_E
DSA_IN_PAD = -(-DSA_IN // LANES) * LANES
DSA_QI_OFF = DSA_Q + 2 * DSA_KV
DSA_KW_OFF = DSA_QI_OFF + IDX_HEADS * IDX_DIM


def _rope_tables(pos, rot_dim, period, n_periods):
    half = rot_dim // 2
    inv_freq = 1.0 / (ROPE_THETA ** (jnp.arange(0, rot_dim, 2, dtype=F32) / rot_dim))
    ang = pos.astype(F32)[:, None] * inv_freq[None, :]
    cos, sin = jnp.cos(ang), jnp.sin(ang)
    T = pos.shape[0]
    zeros = jnp.zeros((T, period - 2 * half), F32)
    zh = jnp.zeros((T, half), F32)
    a1 = jnp.concatenate([cos, cos, zeros + 1.0], 1)
    b1 = jnp.concatenate([zh, sin, zeros], 1)
    c1 = jnp.concatenate([-sin, zh, zeros], 1)
    rest = LANES - n_periods * period
    pad1 = jnp.ones((T, rest), F32)
    pad0 = jnp.zeros((T, rest), F32)
    a = jnp.concatenate([a1] * n_periods + [pad1], 1)
    b = jnp.concatenate([b1] * n_periods + [pad0], 1)
    c = jnp.concatenate([c1] * n_periods + [pad0], 1)
    return a, b, c


def _apply_rope(x, a, b, c, half):
    return x * a + _lane_roll(x, half) * b + _lane_roll(x, LANES - half) * c


N_ROPE_TABLES = 9


def _dsa_sections(x_ref, sc_ref, sh_ref, w_ref, tab_refs):
    hb = (x_ref[...] * (1.0 + sc_ref[...]) + sh_ref[...]).astype(BF16)
    ma, mb, mc, ia, ib, ic, ka, kb, kc = (t[...] for t in tab_refs)

    def sec(off, width):
        return jnp.dot(hb, w_ref[:, off:off + width], preferred_element_type=F32)

    def lanes(x, h):
        return x[:, h * LANES:(h + 1) * LANES]

    q = sec(0, DSA_Q)
    q_heads = [_apply_rope(lanes(q, h), ma, mb, mc, ROT_DIM // 2) * Q_PRESCALE for h in range(DSA_HEADS)]
    k = sec(DSA_Q, DSA_KV)
    k = jnp.concatenate([_apply_rope(lanes(k, h), ma, mb, mc, ROT_DIM // 2) for h in range(DSA_KV_HEADS)], 1)
    v = sec(DSA_Q + DSA_KV, DSA_KV)
    qi = sec(DSA_QI_OFF, IDX_HEADS * IDX_DIM)
    qi_pairs = [_apply_rope(lanes(qi, h), ia, ib, ic, IDX_ROT_DIM // 2)
                for h in range(IDX_HEADS * IDX_DIM // LANES)]
    kw = _apply_rope(sec(DSA_KW_OFF, LANES), ka, kb, kc, IDX_ROT_DIM // 2)
    return q_heads, k, v, qi_pairs, kw


def _dsa_proj_kernel(x_ref, sc_ref, sh_ref, w_ref, *rest):
    tabs = rest[:N_ROPE_TABLES]
    q_ref, k_ref, v_ref, qi_ref, kw_ref = rest[N_ROPE_TABLES:]
    q_heads, k, v, qi_pairs, kw = _dsa_sections(x_ref, sc_ref, sh_ref, w_ref, tabs)
    for h, qh in enumerate(q_heads):
        q_ref[:, h * LANES:(h + 1) * LANES] = qh.astype(q_ref.dtype)
    for h, qp in enumerate(qi_pairs):
        qi_ref[:, h * LANES:(h + 1) * LANES] = qp.astype(qi_ref.dtype)
    k_ref[...] = k
    v_ref[...] = v
    kw_ref[...] = kw


def _dsa_proj_prompt_kernel(x_ref, sc_ref, sh_ref, w_ref, *rest, tq):
    tabs = rest[:N_ROPE_TABLES]
    k_ref, v_ref, kw_ref, kbf_ref, kwbf_ref, vt_ref, qt_ref, qit_ref, wit_ref = rest[N_ROPE_TABLES:]
    q_heads, k, v, qi_pairs, kw = _dsa_sections(x_ref, sc_ref, sh_ref, w_ref, tabs)
    tm = k.shape[0]
    k_ref[...] = k
    kbf_ref[...] = k.astype(BF16)
    v_ref[...] = v
    kw_ref[...] = kw
    kwbf_ref[...] = kw.astype(BF16)
    for n in range(DSA_KV_HEADS):
        vt_ref[n, 0] = v[:, n * DSA_HEAD_DIM:(n + 1) * DSA_HEAD_DIM].T.astype(BF16)
    heads_per_pair = LANES // IDX_DIM
    for j in range(tm // tq):
        rows = slice(j * tq, (j + 1) * tq)
        for h, qh in enumerate(q_heads):
            n, g = divmod(h, DSA_GROUP)
            qt_ref[j, n, :, g * tq:(g + 1) * tq] = qh[rows].T.astype(BF16)
        for hp, qp in enumerate(qi_pairs):
            t = qp[rows].T.astype(BF16)
            for s in range(heads_per_pair):
                h = hp * heads_per_pair + s
                qit_ref[j, :IDX_DIM, h * tq:(h + 1) * tq] = t[s * IDX_DIM:(s + 1) * IDX_DIM]
        qit_ref[j, IDX_DIM:, :] = jnp.zeros((LANES - IDX_DIM, IDX_HEADS * tq), BF16)
        wit_ref[j] = kw[rows].T[IDX_DIM:IDX_DIM + IDX_HEADS]


def _dsa_project_call(body, x, sc, sh, w_pad_bf, pos, tm, out_specs, out_shape, name):
    N = x.shape[0]
    G, R, _ = sc.shape
    tpg = (N // G) // tm
    n_tab = pos.shape[0] // tm
    tabs = (_rope_tables(pos, ROT_DIM, DSA_HEAD_DIM, 1)
            + _rope_tables(pos, IDX_ROT_DIM, IDX_DIM, 2)
            + _rope_tables(pos, IDX_ROT_DIM, IDX_DIM, 1))
    tab = pl.BlockSpec((tm, LANES), lambda i: (i % n_tab, 0))
    return pl.pallas_call(
        body,
        grid=(N // tm,),
        in_specs=[pl.BlockSpec((tm, D_MODEL), lambda i: (i, 0)), _mod_spec(R, tpg), _mod_spec(R, tpg),
                  pl.BlockSpec((D_MODEL, DSA_IN_PAD), lambda i: (0, 0))] + [tab] * N_ROPE_TABLES,
        out_specs=out_specs, out_shape=out_shape,
        compiler_params=_cparams(("parallel",)),
        name=name,
    )(x, sc, sh, w_pad_bf, *tabs)


def dsa_project(x, sc, sh, w_pad_bf, pos, tm):
    N = x.shape[0]
    out = lambda w: pl.BlockSpec((tm, w), lambda i: (i, 0))
    return _dsa_project_call(
        _dsa_proj_kernel, x, sc, sh, w_pad_bf, pos, tm,
        [out(DSA_Q), out(DSA_KV), out(DSA_KV), out(IDX_HEADS * IDX_DIM), out(LANES)],
        [jax.ShapeDtypeStruct((N, DSA_Q), BF16), jax.ShapeDtypeStruct((N, DSA_KV), F32),
         jax.ShapeDtypeStruct((N, DSA_KV), F32), jax.ShapeDtypeStruct((N, IDX_HEADS * IDX_DIM), BF16),
         jax.ShapeDtypeStruct((N, LANES), F32)], "dsa_project")


def dsa_project_prompt(x, sc, sh, w_pad_bf, pos, B, T, tm, tq):
    N = B * T
    tiles, n_qt = T // tm, tm // tq
    gq = DSA_GROUP * tq
    out = lambda w: pl.BlockSpec((tm, w), lambda i: (i, 0))
    by_tile = lambda *shape: pl.BlockSpec((None,) + shape,
                                          lambda i: (i // tiles, i % tiles) + (0,) * (len(shape) - 1))
    return _dsa_project_call(
        functools.partial(_dsa_proj_prompt_kernel, tq=tq), x, sc, sh, w_pad_bf, pos, tm,
        [out(DSA_KV), out(DSA_KV), out(LANES), out(DSA_KV), out(LANES),
         pl.BlockSpec((None, DSA_KV_HEADS, 1, DSA_HEAD_DIM, tm), lambda i: (i // tiles, 0, i % tiles, 0, 0)),
         by_tile(n_qt, DSA_KV_HEADS, DSA_HEAD_DIM, gq),
         by_tile(n_qt, LANES, IDX_HEADS * tq),
         by_tile(n_qt, IDX_HEADS, tq)],
        [jax.ShapeDtypeStruct((N, DSA_KV), F32), jax.ShapeDtypeStruct((N, DSA_KV), F32),
         jax.ShapeDtypeStruct((N, LANES), F32), jax.ShapeDtypeStruct((N, DSA_KV), BF16),
         jax.ShapeDtypeStruct((N, LANES), BF16),
         jax.ShapeDtypeStruct((B, DSA_KV_HEADS, tiles, DSA_HEAD_DIM, tm), BF16),
         jax.ShapeDtypeStruct((B, T // tq, DSA_KV_HEADS, DSA_HEAD_DIM, gq), BF16),
         jax.ShapeDtypeStruct((B, T // tq, LANES, IDX_HEADS * tq), BF16),
         jax.ShapeDtypeStruct((B, T // tq, IDX_HEADS, tq), F32)], "dsa_project_prompt")


CODE_NEG_INF = -2139095041
BISECT_FIRST = 22
BISECT_STAGE = 2


def _threshold_of_code(code):
    bits = code ^ ((code >> 31) & jnp.int32(0x7FFFFFFF))
    return jnp.where(code < CODE_NEG_INF, -jnp.inf, pltpu.bitcast(bits, F32))


def _reduce_row_groups(x, op, n_chains=8):
    parts = [x[r:r + SUBLANES] for r in range(0, x.shape[0], SUBLANES)]
    accs = parts[:n_chains]
    for j, part in enumerate(parts[n_chains:]):
        accs[j % len(accs)] = op(accs[j % len(accs)], part)
    while len(accs) > 1:
        accs = [op(a, b) for a, b in zip(accs[0::2], accs[1::2])] + (accs[-1:] if len(accs) % 2 else [])
    return accs[0]


def _sum_row_groups(x):
    return _reduce_row_groups(x, jnp.add)


def _max_row_groups(x):
    return _reduce_row_groups(x, jnp.maximum)


def _kth_largest_threshold(count_ge, shape, k):
    def body(s, p):
        cand = p + lax.shift_left(jnp.int32(1), 31 - s)
        return jnp.where(count_ge(_threshold_of_code(cand)) >= k, cand, p)
    return _threshold_of_code(lax.fori_loop(0, 32, body, jnp.full(shape, INT_MIN, I32)))


def _tie_index_bound(count_eq_below, shape, need, n_bits):
    def body(s, m):
        cand = m + lax.shift_left(jnp.int32(1), n_bits - 1 - s)
        return jnp.where(count_eq_below(cand) < need, cand, m)
    return lax.fori_loop(0, n_bits, body, jnp.zeros(shape, I32))


def _dsa_prompt_kernel(ki_ref, qit_ref, wit_ref, k_ref, vt_ref, qt_ref, o_ref,
                       score_scr, bias_scr, midx_scr, logit_scr, thr_scr, nge_scr,
                       *, tq, kb_size, cb_size, topk, idx_bits):
    i = pl.program_id(1)
    n_kb = ((i + 1) * tq + kb_size - 1) // kb_size
    n_cb = ((i + 1) * tq + cb_size - 1) // cb_size
    q_pos = i * tq + lax.broadcasted_iota(I32, (1, tq), 1)
    heads_per_dot = 2

    @pl.when(i == 0)
    def _():
        score_scr[...] = jnp.full(score_scr.shape, -jnp.inf, F32)

    def key_rows(kb):
        return pl.ds(pl.multiple_of(kb * kb_size, kb_size), kb_size)

    def l_index(kb, size=kb_size):
        return kb * size + lax.broadcasted_iota(I32, (size, 1), 0)

    def score_body(kb, carry):
        kib = ki_ref[key_rows(kb), :]
        acc = jnp.zeros((kb_size, tq), F32)
        for hp in range(IDX_HEADS // heads_per_dot):
            s = jnp.dot(kib, qit_ref[:, hp * heads_per_dot * tq:(hp + 1) * heads_per_dot * tq],
                        preferred_element_type=F32)
            for j in range(heads_per_dot):
                h = hp * heads_per_dot + j
                w = wit_ref[h:h + 1, :] * (IDX_HEADS ** -0.5 * IDX_DIM ** -0.5)
                acc = acc + jnp.maximum(s[:, j * tq:(j + 1) * tq], 0.0) * w
        allowed = l_index(kb) <= q_pos
        score_scr[key_rows(kb), :] = jnp.where(allowed, acc, -jnp.inf)
        return carry

    lax.fori_loop(0, n_kb, score_body, 0)

    def count(pred_fn):
        def body(cb, acc):
            rows = pl.ds(pl.multiple_of(cb * cb_size, cb_size), cb_size)
            m = jnp.where(pred_fn(score_scr[rows, :], cb), 1.0, 0.0)
            return acc + _sum_row_groups(m)
        acc = lax.fori_loop(0, n_cb, body, jnp.zeros((SUBLANES, tq), F32))
        return acc.sum(axis=0, keepdims=True)

    k_f = float(topk)

    def bisect(s_lo, s_hi):
        def body(s, carry):
            p, n_p = carry
            cand = p + lax.shift_left(jnp.int32(1), 31 - s)
            cand_thr = _threshold_of_code(cand)
            n_c = count(lambda score, cb: score >= cand_thr)
            take = n_c >= k_f
            return jnp.where(take, cand, p), jnp.where(take, n_c, n_p)
        p, n_p = lax.fori_loop(s_lo, s_hi, body, (thr_scr[...], nge_scr[...]))
        thr_scr[...] = p
        nge_scr[...] = n_p

    thr_scr[...] = jnp.full((1, tq), INT_MIN, I32)
    nge_scr[...] = jnp.full((1, tq), float(2 ** 30), F32)
    bisect(0, BISECT_FIRST)
    for s0 in range(BISECT_FIRST, 32, BISECT_STAGE):
        @pl.when(jnp.max(jnp.where(nge_scr[...] != k_f, 1.0, 0.0)) > 0.5)
        def _():
            bisect(s0, min(32, s0 + BISECT_STAGE))

    thr = _threshold_of_code(thr_scr[...])
    n_ge = nge_scr[...]
    n_gt = count(lambda score, cb: score > thr)
    need = k_f - n_gt
    excess = (n_ge - n_gt > need) & (thr > -jnp.inf)
    midx_scr[...] = jnp.full((1, tq), 2 ** idx_bits, I32)

    @pl.when(jnp.max(jnp.where(excess, 1.0, 0.0)) > 0.5)
    def _():
        m = _tie_index_bound(
            lambda c: count(lambda score, cb: (score == thr) & (l_index(cb, cb_size) < c)),
            (1, tq), need, idx_bits)
        midx_scr[...] = jnp.where(excess, m, 2 ** idx_bits)

    midx = midx_scr[...]

    def bias_body(kb, carry):
        score = score_scr[key_rows(kb), :]
        l = l_index(kb)
        sel = ((score > thr) | ((score == thr) & (l <= midx))) & (l <= q_pos)
        bias_scr[key_rows(kb), :] = jnp.where(sel, 0.0, NEG_BIG)
        return carry

    lax.fori_loop(0, n_kb, bias_body, 0)

    gq = DSA_GROUP * tq
    heads = range(DSA_KV_HEADS)

    def logits_of(kb):
        bias = bias_scr[key_rows(kb), :]
        bias = jnp.concatenate([bias] * DSA_GROUP, axis=1)
        blk_max = []
        for n in heads:
            kblk = k_ref[key_rows(kb), n * DSA_HEAD_DIM:(n + 1) * DSA_HEAD_DIM]
            logits = jnp.dot(kblk, qt_ref[n], preferred_element_type=F32) + bias
            logit_scr[n, key_rows(kb), :] = logits
            blk_max.append(_max_row_groups(logits))
        return tuple(blk_max)

    def absorb(kb, blk_max, state):
        new = []
        for n in heads:
            m_run, l_run, acc = state[n]
            m_new = jnp.maximum(m_run, jnp.max(blk_max[n], axis=0, keepdims=True))
            alpha = jnp.exp2(m_run - m_new)
            p = jnp.exp2(logit_scr[n, key_rows(kb), :] - m_new)
            acc = acc * alpha + jnp.dot(vt_ref[n, kb], p.astype(BF16), preferred_element_type=F32)
            new.append((m_new, l_run * alpha + _sum_row_groups(p), acc))
        return tuple(new)

    def att_body(kb, carry):
        blk_max, state = carry
        state = absorb(kb - 1, blk_max, state)
        return logits_of(kb), state

    state0 = tuple((jnp.full((1, gq), NEG_BIG, F32), jnp.zeros((SUBLANES, gq), F32),
                    jnp.zeros((DSA_HEAD_DIM, gq), F32)) for _ in heads)
    last_max, state = lax.fori_loop(1, n_kb, att_body, (logits_of(0), state0))
    fin = absorb(n_kb - 1, last_max, state)
    for n in heads:
        _, l8, acc = fin[n]
        o_t = acc / jnp.sum(l8, axis=0, keepdims=True)
        for g in range(DSA_GROUP):
            h = n * DSA_GROUP + g
            o_ref[:, h * DSA_HEAD_DIM:(h + 1) * DSA_HEAD_DIM] = (
                o_t[:, g * tq:(g + 1) * tq].T.astype(o_ref.dtype))


def dsa_prompt_attend(kw_bf, qi_t, wi_t, k_bf, v_t, q_t, B, T, topk, tq, kb_size):
    nqt = T // tq
    nkb = T // kb_size
    gq = DSA_GROUP * tq
    return pl.pallas_call(
        functools.partial(_dsa_prompt_kernel, tq=tq, kb_size=kb_size, cb_size=min(512, T), topk=topk,
                          idx_bits=max(1, (T - 1).bit_length())),
        grid=(B, nqt),
        in_specs=[pl.BlockSpec((None, T, LANES), lambda b, i: (b, 0, 0)),
                  pl.BlockSpec((None, None, LANES, IDX_HEADS * tq), lambda b, i: (b, i, 0, 0)),
                  pl.BlockSpec((None, None, IDX_HEADS, tq), lambda b, i: (b, i, 0, 0)),
                  pl.BlockSpec((None, T, DSA_KV), lambda b, i: (b, 0, 0)),
                  pl.BlockSpec((None, DSA_KV_HEADS, nkb, DSA_HEAD_DIM, kb_size),
                               lambda b, i: (b, 0, 0, 0, 0)),
                  pl.BlockSpec((None, None, DSA_KV_HEADS, DSA_HEAD_DIM, gq),
                               lambda b, i: (b, i, 0, 0, 0))],
        out_specs=pl.BlockSpec((tq, DSA_Q), lambda b, i: (b * nqt + i, 0)),
        out_shape=jax.ShapeDtypeStruct((B * T, DSA_Q), BF16),
        scratch_shapes=[pltpu.VMEM((T, tq), F32), pltpu.VMEM((T, tq), F32), pltpu.VMEM((1, tq), I32),
                        pltpu.VMEM((DSA_KV_HEADS, T, gq), F32),
                        pltpu.VMEM((1, tq), I32), pltpu.VMEM((1, tq), F32)],
        compiler_params=_cparams(("arbitrary", "arbitrary")),
        name="dsa_prompt_attend",
    )(kw_bf.reshape(B, T, LANES), qi_t, wi_t, k_bf.reshape(B, T, DSA_KV), v_t, q_t)


def _dsa_sample_score_kernel(pt_ref, qi_ref, wi_ref, kis_ref, kidx_hbm, o_ref, buf, sem, *, n_pages, page):
    slot = _fetch_pages_ahead(pt_ref, (kidx_hbm,), (buf,), sem, n_pages)
    pages = [buf.at[slot, j] for j in range(n_pages)]
    qi = qi_ref[...]
    w = wi_ref[...] * (IDX_HEADS ** -0.5 * IDX_DIM ** -0.5)
    for j in range(n_pages):
        dots = jnp.dot(qi, pages[j][...].astype(BF16), preferred_element_type=F32)
        o_ref[:, j * page:(j + 1) * page] = jnp.sum(jnp.maximum(dots, 0.0) * w, axis=0, keepdims=True)
    ki_self = kis_ref[...].astype(BF16).astype(F32)
    d_self = jnp.sum(qi.astype(F32) * ki_self, -1, keepdims=True)
    s_self = jnp.sum(jnp.maximum(d_self, 0.0) * w, axis=0, keepdims=True)
    lane = lax.broadcasted_iota(I32, (1, LANES), 1)
    o_ref[:, n_pages * page:] = jnp.where(lane == 0, s_self, -jnp.inf)


def dsa_sample_scores(page_table, qi_bf, wi, ki_self, cache_kidx_t):
    DB, n_pages = page_table.shape
    page = cache_kidx_t.shape[2]
    lp = n_pages * page + LANES
    per_b = lambda *shape: pl.BlockSpec((None,) + shape, lambda b, pt: (b,) + (0,) * len(shape))
    return pl.pallas_call(
        functools.partial(_dsa_sample_score_kernel, n_pages=n_pages, page=page),
        grid_spec=pltpu.PrefetchScalarGridSpec(
            num_scalar_prefetch=1, grid=(DB,),
            in_specs=[per_b(IDX_HEADS, IDX_DIM), per_b(IDX_HEADS, 1), per_b(1, IDX_DIM),
                      pl.BlockSpec(memory_space=pl.ANY)],
            out_specs=per_b(1, lp),
            scratch_shapes=[pltpu.VMEM((SCORE_PAGE_SLOTS, n_pages, IDX_DIM, page), F32),
                            pltpu.SemaphoreType.DMA((SCORE_PAGE_SLOTS,))]),
        out_shape=jax.ShapeDtypeStruct((DB, 1, lp), F32),
        compiler_params=_cparams(("arbitrary",)),
        name="dsa_sample_scores",
    )(page_table, qi_bf.reshape(DB, IDX_HEADS, IDX_DIM), wi.reshape(DB, IDX_HEADS, 1),
      ki_self.reshape(DB, 1, IDX_DIM), cache_kidx_t)


def _dsa_sample_select_kernel(s_ref, o_ref, *, n_keys, topk, idx_bits):
    score = s_ref[...]
    lane = lax.broadcasted_iota(I32, score.shape, 1)
    valid = lane < n_keys
    score = jnp.where(valid, score, -jnp.inf)

    def count(pred):
        return jnp.sum(jnp.where(pred, 1.0, 0.0), -1, keepdims=True)

    shape = (score.shape[0], 1)
    thr = _kth_largest_threshold(lambda t: count(score >= t), shape, float(topk))
    n_gt = count(score > thr)
    need = float(topk) - n_gt
    excess = (count(score == thr) > need) & (thr > -jnp.inf)
    m = _tie_index_bound(lambda c: count((score == thr) & (lane < c)), shape, need, idx_bits)
    midx = jnp.where(excess, m, 2 ** idx_bits)
    sel = ((score > thr) | ((score == thr) & (lane <= midx))) & valid
    o_ref[...] = jnp.where(sel, 0.0, NEG_BIG)


def dsa_sample_select(score, n_keys, topk):
    DB, lp = score.shape
    return pl.pallas_call(
        functools.partial(_dsa_sample_select_kernel, n_keys=n_keys, topk=topk,
                          idx_bits=max(1, (lp - 1).bit_length())),
        grid=(1,),
        in_specs=[pl.BlockSpec((DB, lp), lambda i: (0, 0))],
        out_specs=pl.BlockSpec((DB, lp), lambda i: (0, 0)),
        out_shape=jax.ShapeDtypeStruct((DB, lp), F32),
        compiler_params=_cparams(("arbitrary",)),
        name="dsa_sample_select",
    )(score)


SCORE_PAGE_SLOTS = 4
ATTEND_PAGE_SLOTS = 3


def _page_copies(pt_ref, b, slot, hbm_refs, bufs, sem, n_pages):
    return [pltpu.make_async_copy(hbm.at[pt_ref[b, j]], buf.at[slot, j], sem.at[slot])
            for hbm, buf in zip(hbm_refs, bufs) for j in range(n_pages)]


def _fetch_pages_ahead(pt_ref, hbm_refs, bufs, sem, n_pages):
    n_slots = bufs[0].shape[0]
    ahead = n_slots - 1
    b = pl.program_id(0)
    n_steps = pl.num_programs(0)

    @pl.when(b == 0)
    def _():
        for s in range(ahead):
            @pl.when(s < n_steps)
            def _():
                for c in _page_copies(pt_ref, s, s, hbm_refs, bufs, sem, n_pages):
                    c.start()

    @pl.when(b + ahead < n_steps)
    def _():
        for c in _page_copies(pt_ref, b + ahead, (b + ahead) % n_slots, hbm_refs, bufs, sem, n_pages):
            c.start()

    slot = b % n_slots
    for c in _page_copies(pt_ref, b, slot, hbm_refs, bufs, sem, n_pages):
        c.wait()
    return slot


def _dsa_sample_attend_kernel(pt_ref, q_ref, bias_ref, ks_ref, vs_ref, k_hbm, v_hbm, o_ref,
                              kbuf, vbuf, sem, logit_scr, *, n_pages, page):
    slot = _fetch_pages_ahead(pt_ref, (k_hbm, v_hbm), (kbuf, vbuf), sem, n_pages)
    k_pages = [kbuf.at[slot, j] for j in range(n_pages)]
    v_pages = [vbuf.at[slot, j] for j in range(n_pages)]
    q = q_ref[...]
    tail = n_pages * page
    lane = lax.broadcasted_iota(I32, (DSA_GROUP, LANES), 1)
    for n in range(DSA_KV_HEADS):
        hs = slice(n * DSA_GROUP, (n + 1) * DSA_GROUP)
        ds = slice(n * DSA_HEAD_DIM, (n + 1) * DSA_HEAD_DIM)
        head_rows = pl.ds(n, page, stride=DSA_KV_HEADS)
        qn = q[hs, :]
        for j in range(n_pages):
            lg = lax.dot_general(qn, k_pages[j][head_rows, :].astype(BF16), (((1,), (1,)), ((), ())),
                                 preferred_element_type=F32)
            logit_scr[hs, j * page:(j + 1) * page] = lg + bias_ref[:, j * page:(j + 1) * page]
        k_self = ks_ref[:, ds].astype(BF16).astype(F32)
        lg_self = jnp.sum(qn.astype(F32) * k_self, -1, keepdims=True)
        logit_scr[hs, tail:] = jnp.where(lane == 0, lg_self, 0.0) + bias_ref[:, tail:]
        logits = logit_scr[hs, :]
        m = jnp.max(logits, -1, keepdims=True)
        p = jnp.exp2(logits - m)
        p_bf = (p / jnp.sum(p, -1, keepdims=True)).astype(BF16)
        v_self = vs_ref[:, ds].astype(BF16).astype(F32)
        acc = p_bf[:, tail:].astype(F32)[:, :1] * v_self
        for j in range(n_pages):
            acc = acc + jnp.dot(p_bf[:, j * page:(j + 1) * page], v_pages[j][head_rows, :].astype(BF16),
                                preferred_element_type=F32)
        o_ref[hs, :] = acc


def dsa_sample_attend(page_table, q_bf, bias, k_self, v_self, cache_k, cache_v):
    DB, n_pages = page_table.shape
    rows = cache_k.shape[1]
    page = rows // DSA_KV_HEADS
    lp = n_pages * page + LANES
    per_b = lambda *shape: pl.BlockSpec((None,) + shape, lambda b, pt: (b,) + (0,) * len(shape))
    o = pl.pallas_call(
        functools.partial(_dsa_sample_attend_kernel, n_pages=n_pages, page=page),
        grid_spec=pltpu.PrefetchScalarGridSpec(
            num_scalar_prefetch=1, grid=(DB,),
            in_specs=[per_b(DSA_HEADS, DSA_HEAD_DIM), per_b(1, lp), per_b(1, DSA_KV), per_b(1, DSA_KV),
                      pl.BlockSpec(memory_space=pl.ANY), pl.BlockSpec(memory_space=pl.ANY)],
            out_specs=per_b(DSA_HEADS, DSA_HEAD_DIM),
            scratch_shapes=[pltpu.VMEM((ATTEND_PAGE_SLOTS, n_pages, rows, DSA_HEAD_DIM), F32),
                            pltpu.VMEM((ATTEND_PAGE_SLOTS, n_pages, rows, DSA_HEAD_DIM), F32),
                            pltpu.SemaphoreType.DMA((ATTEND_PAGE_SLOTS,)),
                            pltpu.VMEM((DSA_HEADS, lp), F32)]),
        out_shape=jax.ShapeDtypeStruct((DB, DSA_HEADS, DSA_HEAD_DIM), F32),
        compiler_params=_cparams(("arbitrary",)),
        name="dsa_sample_attend",
    )(page_table, q_bf.reshape(DB, DSA_HEADS, DSA_HEAD_DIM), bias.reshape(DB, 1, lp),
      k_self.reshape(DB, 1, DSA_KV), v_self.reshape(DB, 1, DSA_KV), cache_k, cache_v)
    return o.reshape(DB, DSA_Q)


def _retnet_tables(pos):
    inv_freq = jnp.power(RET_ANGLE_BASE, -jnp.linspace(0.0, 1.0, RET_DK // 2, dtype=F32))
    ang = pos.astype(F32)[:, None] * inv_freq[None, :]
    return jnp.cos(ang), jnp.sin(ang)


def _split_mod(mod_l, G, R):
    return [m.reshape(G, R, D_MODEL) for m in jnp.split(mod_l, 6, axis=-1)]


MOE_GROUP_TILE = 1024
MOE_GROUP_CAP = 256


def _channel_mixer(x, sc2, sh2, g2, l, wts, tm):
    gates = router(x, sc2, sh2, wts["w_router"], wts["b_router"], tm)
    args = (x, sc2, sh2, g2, gates, l, wts["w_gate"], wts["w_up"], wts["w_down"],
            wts["ln2_g"][l], wts["ln2_b"][l])
    rows_per_mod = x.shape[0] // sc2.shape[0]
    if sc2.shape[1] == 1 and rows_per_mod >= 2 * MOE_GROUP_CAP:
        tile = min(MOE_GROUP_TILE, rows_per_mod)
        return moe_group_ln(*args, tile, MOE_GROUP_CAP)
    return moe_ln(*args, tm)


def kernel(x_prompt, x_sample, state_ret, cache_k, cache_v, cache_kidx, page_table,
           c_prompt, c_sample, w_mod, b_mod, ln1_g, ln1_b, ln2_g, ln2_b,
           w_in_ret, gn_ret_g, w_out_ret, w_in_dsa, w_out_dsa,
           w_router, b_router, w_gate, w_up, w_down):
    B, T, _ = x_prompt.shape
    DB = x_sample.shape[0]
    assert x_sample.shape[1] == 1
    n_pages = page_table.shape[1]
    page = cache_k.shape[2]
    past = n_pages * page
    n_pool = cache_k.shape[1]

    wts = dict(w_router=w_router, b_router=b_router, ln2_g=ln2_g, ln2_b=ln2_b,
               w_gate=w_gate.astype(BF16), w_up=w_up.astype(BF16), w_down=w_down.astype(BF16))
    w_mod_bf = w_mod.astype(BF16)
    w_in_ret_bf = w_in_ret[0].astype(BF16)
    w_out_ret_bf = w_out_ret[0].astype(BF16)
    w_in_dsa_bf = jnp.pad(w_in_dsa[0], ((0, 0), (0, DSA_IN_PAD - DSA_IN))).astype(BF16)
    w_out_dsa_bf = w_out_dsa[0].astype(BF16)

    pos_p = jnp.arange(T, dtype=I32)
    pos_s = jnp.full((1,), past, I32)

    tm_p = min(512, T)
    tq = min(128, T)
    xp = x_prompt.reshape(B * T, D_MODEL)
    mod_p = modulation(c_prompt, w_mod_bf, b_mod)
    sh1, sc1, g1, sh2, sc2, g2 = _split_mod(mod_p[0], B, 1)
    proj = mod_proj(xp, sc1, sh1, w_in_ret_bf, min(1024, T), 1536, BF16)
    cos_p, sin_p = _retnet_tables(pos_p)
    gated, ret_p = retention_prompt(proj.reshape(B, T, RET_IN), gn_ret_g[0], cos_p, sin_p, min(256, T))
    xp = out_proj_ln(gated.reshape(B * T, RET_V), w_out_ret_bf, xp, g1, ln1_g[0], ln1_b[0], tm_p)
    xp = _channel_mixer(xp, sc2, sh2, g2, 0, wts, tm_p)

    sh1, sc1, g1, sh2, sc2, g2 = _split_mod(mod_p[1], B, 1)
    k_p, v_p, kw_p, k_bf, kw_bf, v_t, q_t, qi_t, wi_t = dsa_project_prompt(
        xp, sc1, sh1, w_in_dsa_bf, pos_p, B, T, tm_p, tq)
    o_p = dsa_prompt_attend(kw_bf, qi_t, wi_t, k_bf, v_t, q_t, B, T, min(DSA_TOPK, T // 4), tq, tm_p)
    xp = out_proj_ln(o_p, w_out_dsa_bf, xp, g1, ln1_g[1], ln1_b[1], tm_p)
    xp = _channel_mixer(xp, sc2, sh2, g2, 1, wts, tm_p)

    xs = x_sample.reshape(DB, D_MODEL)
    mod_s = modulation(c_sample, w_mod_bf, b_mod)
    sh1, sc1, g1, sh2, sc2, g2 = _split_mod(mod_s[0], 1, DB)
    proj_s = mod_proj(xs, sc1, sh1, w_in_ret_bf, DB, 1536, F32)
    cos_s, sin_s = _retnet_tables(pos_s)
    gated_s, ret_s = retention_sample(proj_s, state_ret[0], gn_ret_g[0], cos_s, sin_s)
    xs = out_proj_ln(gated_s.reshape(DB, RET_V), w_out_ret_bf, xs, g1, ln1_g[0], ln1_b[0], DB)
    xs = _channel_mixer(xs, sc2, sh2, g2, 0, wts, DB)

    sh1, sc1, g1, sh2, sc2, g2 = _split_mod(mod_s[1], 1, DB)
    pos_rows = jnp.full((DB,), past, I32)
    q_s, k_s, v_s, qi_s, kw_s = dsa_project(xs, sc1, sh1, w_in_dsa_bf, pos_rows, DB)
    ki_s = kw_s[:, :IDX_DIM]
    score = dsa_sample_scores(page_table, qi_s, kw_s[:, IDX_DIM:IDX_DIM + IDX_HEADS], ki_s,
                              jnp.swapaxes(cache_kidx[0], 1, 2))
    bias = dsa_sample_select(score.reshape(DB, past + LANES), past + 1, min(DSA_TOPK, (past + 1) // 4))
    o_s = dsa_sample_attend(page_table, q_s, bias, k_s, v_s,
                            cache_k[0].reshape(n_pool, page * DSA_KV_HEADS, DSA_HEAD_DIM),
                            cache_v[0].reshape(n_pool, page * DSA_KV_HEADS, DSA_HEAD_DIM))
    xs = out_proj_ln(o_s, w_out_dsa_bf, xs, g1, ln1_g[1], ln1_b[1], DB)
    xs = _channel_mixer(xs, sc2, sh2, g2, 1, wts, DB)

    kv_shape = (DSA_KV_HEADS, DSA_HEAD_DIM)
    return (xp.reshape(B, T, D_MODEL), xs.reshape(DB, 1, D_MODEL),
            ret_p[None], ret_s[None],
            k_p.reshape(1, B, T, *kv_shape), v_p.reshape(1, B, T, *kv_shape),
            kw_p[:, :IDX_DIM].reshape(1, B, T, IDX_DIM),
            k_s.reshape(1, DB, 1, *kv_shape), v_s.reshape(1, DB, 1, *kv_shape),
            ki_s.reshape(1, DB, 1, IDX_DIM))
```

```python
import functools

import jax
import jax.numpy as jnp
from jax import lax
from jax.experimental import pallas as pl
from jax.experimental.pallas import tpu as pltpu

F32 = jnp.float32
BF16 = jnp.bfloat16
I32 = jnp.int32

D_MODEL = 1024
DEPTH = 2
ALPHA = (2.0 * DEPTH) ** 0.25
LN_EPS = 1e-5
GN_EPS = 1e-6

RET_HEADS = 4
RET_DK = 256
RET_DV = 512
RET_QK = RET_HEADS * RET_DK
RET_V = RET_HEADS * RET_DV
RET_IN = 2 * RET_QK + 2 * RET_V
RET_ANGLE_BASE = 10000.0

DSA_HEADS = 8
DSA_KV_HEADS = 2
DSA_HEAD_DIM = 128
DSA_GROUP = DSA_HEADS // DSA_KV_HEADS
DSA_Q = DSA_HEADS * DSA_HEAD_DIM
DSA_KV = DSA_KV_HEADS * DSA_HEAD_DIM
IDX_HEADS = 16
IDX_DIM = 64
DSA_TOPK = 256
ROPE_THETA = 500000.0
ROT_DIM = DSA_HEAD_DIM // 4
IDX_ROT_DIM = IDX_DIM // 4
DSA_IN = DSA_Q + 2 * DSA_KV + IDX_HEADS * IDX_DIM + IDX_DIM + IDX_HEADS

N_EXPERTS = 16
N_GROUPS = 4
EXPERTS_PER_GROUP = N_EXPERTS // N_GROUPS
D_EXPERT = 512
GROUP_LANE0 = N_EXPERTS

LANES = 128
SUBLANES = 8
VMEM_LIMIT = 56 * 1024 * 1024
NEG_BIG = -1e30
INT_MIN = -2147483648


def _cparams(sem):
    return pltpu.CompilerParams(dimension_semantics=sem, vmem_limit_bytes=VMEM_LIMIT)


def _silu(x):
    return x * (1.0 / (1.0 + jnp.exp(-x)))


def _layer_norm(z, g, b):
    mu = jnp.mean(z, -1, keepdims=True)
    d = z - mu
    var = jnp.mean(d * d, -1, keepdims=True)
    return d * lax.rsqrt(var + LN_EPS) * g + b


def _mod_kernel(c_ref, w_ref, b_ref, o_ref):
    a = _silu(c_ref[...]).astype(BF16)
    o_ref[...] = jnp.dot(a, w_ref[...], preferred_element_type=F32) + b_ref[...]


def modulation(c, w_mod_bf, b_mod):
    R = c.shape[0]
    tn = 1536
    return pl.pallas_call(
        _mod_kernel,
        grid=(DEPTH, 6 * D_MODEL // tn),
        in_specs=[pl.BlockSpec((R, D_MODEL), lambda l, j: (0, 0)),
                  pl.BlockSpec((None, D_MODEL, tn), lambda l, j: (l, 0, j)),
                  pl.BlockSpec((None, 1, tn), lambda l, j: (l, 0, j))],
        out_specs=pl.BlockSpec((None, R, tn), lambda l, j: (l, 0, j)),
        out_shape=jax.ShapeDtypeStruct((DEPTH, R, 6 * D_MODEL), F32),
        compiler_params=_cparams(("parallel", "parallel")),
        name="modulation",
    )(c, w_mod_bf, b_mod.reshape(DEPTH, 1, 6 * D_MODEL))


def _proj_kernel(x_ref, sc_ref, sh_ref, w_ref, o_ref, h_scr):
    @pl.when(pl.program_id(1) == 0)
    def _():
        h_scr[...] = (x_ref[...] * (1.0 + sc_ref[...]) + sh_ref[...]).astype(BF16)

    o_ref[...] = jnp.dot(h_scr[...], w_ref[...], preferred_element_type=F32).astype(o_ref.dtype)


def _mod_spec(R, tiles_per_group):
    return pl.BlockSpec((None, R, D_MODEL), lambda i, *_: (i // tiles_per_group, 0, 0))


def mod_proj(x, sc, sh, w_bf, tm, tn, out_dtype):
    N = x.shape[0]
    G, R, _ = sc.shape
    n_out = w_bf.shape[1]
    tpg = (N // G) // tm
    return pl.pallas_call(
        _proj_kernel,
        grid=(N // tm, n_out // tn),
        in_specs=[pl.BlockSpec((tm, D_MODEL), lambda i, j: (i, 0)),
                  _mod_spec(R, tpg), _mod_spec(R, tpg),
                  pl.BlockSpec((D_MODEL, tn), lambda i, j: (0, j))],
        out_specs=pl.BlockSpec((tm, tn), lambda i, j: (i, j)),
        out_shape=jax.ShapeDtypeStruct((N, n_out), out_dtype),
        scratch_shapes=[pltpu.VMEM((tm, D_MODEL), BF16)],
        compiler_params=_cparams(("parallel", "arbitrary")),
        name="mod_proj",
    )(x, sc, sh, w_bf)


def _rot_half(x, cos, sin):
    half = cos.shape[-1]
    x1, x2 = x[:, :half], x[:, half:]
    return jnp.concatenate([x1 * cos - x2 * sin, x1 * sin + x2 * cos], axis=1)


def _group_norm_gate(o, gn, g):
    mu = jnp.mean(o, -1, keepdims=True)
    d = o - mu
    var = jnp.mean(d * d, -1, keepdims=True)
    return d * lax.rsqrt(var + GN_EPS) * gn * _silu(g)


def _ret_prompt_kernel(q_ref, k_ref, v_ref, g_ref, cos_ref, sin_ref, lg_ref, gn_ref,
                       o_ref, s_ref, s_scr, *, chunk):
    c = pl.program_id(2)

    @pl.when(c == 0)
    def _():
        s_scr[...] = jnp.zeros_like(s_scr)

    cos, sin = cos_ref[...], sin_ref[...]
    lg_row = lg_ref[...]
    lg = lg_row[:, :1]
    q = _rot_half(q_ref[...].astype(F32), cos, sin)
    k = _rot_half(k_ref[...].astype(F32), cos, sin) * (RET_DK ** -0.5)
    vb = v_ref[...].astype(BF16)
    qb = q.astype(BF16)

    row = lax.broadcasted_iota(I32, (chunk, chunk), 0)
    col = lax.broadcasted_iota(I32, (chunk, chunk), 1)
    diff = (row - col).astype(F32)
    decay = jnp.where(diff >= 0, jnp.exp(lg_row * jnp.maximum(diff, 0.0)), 0.0)
    idx = lax.broadcasted_iota(I32, (chunk, 1), 0).astype(F32)
    q_dec = jnp.exp(lg * (idx + 1.0))
    k_dec = jnp.exp(lg * (chunk - 1.0 - idx))
    chunk_dec = jnp.exp(lg * chunk)

    s = s_scr[...]
    att = lax.dot_general(qb, k.astype(BF16), (((1,), (1,)), ((), ())),
                          preferred_element_type=F32) * decay
    o = (jnp.dot(att.astype(BF16), vb, preferred_element_type=F32)
         + jnp.dot(qb, s.astype(BF16), preferred_element_type=F32) * q_dec)
    kd_t = (k * k_dec).T.astype(BF16)
    s_new = chunk_dec * s + jnp.dot(kd_t, vb, preferred_element_type=F32)
    s_scr[...] = s_new
    o_ref[...] = _group_norm_gate(o, gn_ref[...], g_ref[...].astype(F32)).astype(o_ref.dtype)

    @pl.when(c == pl.num_programs(2) - 1)
    def _():
        s_ref[...] = s_new


def _log_gamma():
    return jnp.log(1.0 - jnp.power(2.0, -5.0 - jnp.arange(RET_HEADS, dtype=F32)))


def retention_prompt(proj, gn_g, cos, sin, chunk):
    B, T, _ = proj.shape
    lg_tab = jnp.broadcast_to(_log_gamma()[:, None, None], (RET_HEADS, 1, chunk))
    qk_blk = lambda off: pl.BlockSpec((None, chunk, RET_DK), lambda b, h, c: (b, c, off + h))
    v_blk = lambda off: pl.BlockSpec((None, chunk, RET_DV), lambda b, h, c: (b, c, off + h))
    tab = pl.BlockSpec((chunk, RET_DK // 2), lambda b, h, c: (c, 0))
    return pl.pallas_call(
        functools.partial(_ret_prompt_kernel, chunk=chunk),
        grid=(B, RET_HEADS, T // chunk),
        in_specs=[qk_blk(0), qk_blk(RET_QK // RET_DK),
                  v_blk(2 * RET_QK // RET_DV), v_blk((2 * RET_QK + RET_V) // RET_DV),
                  tab, tab,
                  pl.BlockSpec((None, 1, chunk), lambda b, h, c: (h, 0, 0)),
                  pl.BlockSpec((1, RET_DV), lambda b, h, c: (0, h))],
        out_specs=[pl.BlockSpec((None, chunk, RET_DV), lambda b, h, c: (b, c, h)),
                   pl.BlockSpec((None, None, RET_DK, RET_DV), lambda b, h, c: (b, h, 0, 0))],
        out_shape=[jax.ShapeDtypeStruct((B, T, RET_V), BF16),
                   jax.ShapeDtypeStruct((B, RET_HEADS, RET_DK, RET_DV), F32)],
        scratch_shapes=[pltpu.VMEM((RET_DK, RET_DV), F32)],
        compiler_params=_cparams(("parallel", "parallel", "arbitrary")),
        name="retention_prompt",
    )(proj, proj, proj, proj, cos, sin, lg_tab, gn_g.reshape(1, RET_V))


def _ret_sample_kernel(qk_ref, v_ref, g_ref, s0_ref, cos_ref, sin_ref, gam_ref, gn_ref,
                       o_ref, s_ref):
    t = _rot_half(qk_ref[...], cos_ref[...], sin_ref[...])
    row = lax.broadcasted_iota(I32, t.shape, 0)
    t = jnp.where(row >= RET_HEADS, t * (RET_DK ** -0.5), t)
    pad = jnp.zeros((LANES - 2 * RET_HEADS, RET_DK), F32)
    t_t = jnp.concatenate([t, pad], axis=0).T
    for h in range(RET_HEADS):
        qc = t_t[:, h:h + 1]
        kc = t_t[:, RET_HEADS + h:RET_HEADS + h + 1]
        gam = gam_ref[h]
        vh = v_ref[h:h + 1, :]
        s = s0_ref[h]
        qk_dot = jnp.sum(t[h:h + 1, :] * t[RET_HEADS + h:RET_HEADS + h + 1, :], -1, keepdims=True)
        o = qk_dot * vh + gam * jnp.sum(s * qc, axis=0, keepdims=True)
        s_ref[h] = gam * s + kc * vh
        o_ref[h:h + 1, :] = _group_norm_gate(o, gn_ref[h:h + 1, :], g_ref[h:h + 1, :])


def retention_sample(proj, s0, gn_g, cos, sin):
    DB = proj.shape[0]
    qk = proj[:, :2 * RET_QK].reshape(DB, 2 * RET_HEADS, RET_DK)
    v = proj[:, 2 * RET_QK:2 * RET_QK + RET_V].reshape(DB, RET_HEADS, RET_DV)
    g = proj[:, 2 * RET_QK + RET_V:].reshape(DB, RET_HEADS, RET_DV)
    gam = jnp.broadcast_to(jnp.exp(_log_gamma())[:, None, None], (RET_HEADS, 1, RET_DV))
    full = lambda *shape: pl.BlockSpec(shape, lambda b: (0,) * len(shape))
    per_b = lambda *shape: pl.BlockSpec((None,) + shape, lambda b: (b,) + (0,) * len(shape))
    return pl.pallas_call(
        _ret_sample_kernel,
        grid=(DB,),
        in_specs=[per_b(2 * RET_HEADS, RET_DK), per_b(RET_HEADS, RET_DV), per_b(RET_HEADS, RET_DV),
                  per_b(RET_HEADS, RET_DK, RET_DV),
                  full(1, RET_DK // 2), full(1, RET_DK // 2),
                  full(RET_HEADS, 1, RET_DV), full(RET_HEADS, RET_DV)],
        out_specs=[per_b(RET_HEADS, RET_DV), per_b(RET_HEADS, RET_DK, RET_DV)],
        out_shape=[jax.ShapeDtypeStruct((DB, RET_HEADS, RET_DV), F32),
                   jax.ShapeDtypeStruct((DB, RET_HEADS, RET_DK, RET_DV), F32)],
        compiler_params=_cparams(("parallel",)),
        name="retention_sample",
    )(qk, v, g, s0, cos, sin, gam, gn_g.reshape(RET_HEADS, RET_DV))


def _out_ln_route_kernel(a_ref, w_ref, x_ref, g_ref, lng_ref, lnb_ref, sc2_ref, sh2_ref,
                         whi_ref, wlo_ref, b_ref, o_ref, gates_ref, gt_scr):
    y = jnp.dot(a_ref[...].astype(BF16), w_ref[...], preferred_element_type=F32)
    x1 = _layer_norm(ALPHA * x_ref[...] + g_ref[...] * y, lng_ref[...], lnb_ref[...])
    o_ref[...] = x1
    h2 = x1 * (1.0 + sc2_ref[...]) + sh2_ref[...]
    gates_ref[...] = _route(h2, whi_ref[...], wlo_ref[...], b_ref[...], gt_scr)


def out_proj_ln_route(a, w_bf, x, gate, ln_g, ln_b, sc2, sh2, router_w, tm):
    N, K = a.shape
    G, R, _ = gate.shape
    tpg = (N // G) // tm
    row = pl.BlockSpec((1, D_MODEL), lambda i: (0, 0))
    full = lambda r, c: pl.BlockSpec((r, c), lambda i: (0, 0))
    ms = _mod_spec(R, tpg)
    return pl.pallas_call(
        _out_ln_route_kernel,
        grid=(N // tm,),
        in_specs=[pl.BlockSpec((tm, K), lambda i: (i, 0)),
                  pl.BlockSpec((K, D_MODEL), lambda i: (0, 0)),
                  pl.BlockSpec((tm, D_MODEL), lambda i: (i, 0)),
                  ms, row, row, ms, ms,
                  full(D_MODEL, LANES), full(D_MODEL, LANES), full(1, LANES)],
        out_specs=[pl.BlockSpec((tm, D_MODEL), lambda i: (i, 0)),
                   pl.BlockSpec((tm, LANES), lambda i: (i, 0))],
        out_shape=[jax.ShapeDtypeStruct((N, D_MODEL), F32), jax.ShapeDtypeStruct((N, LANES), F32)],
        scratch_shapes=[pltpu.VMEM((LANES, tm), F32)],
        compiler_params=_cparams(("parallel",)),
        name="out_proj_ln_route",
    )(a, w_bf, x, gate, ln_g.reshape(1, D_MODEL), ln_b.reshape(1, D_MODEL), sc2, sh2, *router_w)


def _lane_roll(x, shift):
    return pltpu.roll(x, shift, axis=1)


def _route(h, whi, wlo, b, gt_scr):
    hi = h.astype(BF16)
    lo = (h - hi.astype(F32)).astype(BF16)
    logits = (jnp.dot(hi, whi, preferred_element_type=F32)
              + jnp.dot(hi, wlo, preferred_element_type=F32)
              + jnp.dot(lo, whi, preferred_element_type=F32)) + b
    lt = logits.T
    l_rows = [lt[e:e + 1, :] for e in range(N_EXPERTS)]
    m = functools.reduce(jnp.maximum, l_rows)
    ex = [jnp.exp(l - m) for l in l_rows]
    denom = functools.reduce(jnp.add, ex)
    p = [x / denom for x in ex]

    def first_max(vals):
        best, idx = vals[0], jnp.zeros_like(vals[0])
        for j, val in enumerate(vals[1:], 1):
            take = val > best
            idx = jnp.where(take, float(j), idx)
            best = jnp.where(take, val, best)
        return best, idx

    grp_score = []
    for g in range(N_GROUPS):
        a, b, c, d = p[g * EXPERTS_PER_GROUP:(g + 1) * EXPERTS_PER_GROUP]
        pairs = [a + b, a + c, a + d, b + c, b + d, c + d]
        grp_score.append(functools.reduce(jnp.maximum, pairs))
    _, g_sel = first_max(grp_score)
    neg_inf = jnp.full_like(m, -jnp.inf)
    masked = [jnp.where(g_sel == float(e // EXPERTS_PER_GROUP), p[e], neg_inf) for e in range(N_EXPERTS)]
    v1, i1 = first_max(masked)
    masked2 = [jnp.where(i1 == float(e), neg_inf, masked[e]) for e in range(N_EXPERTS)]
    v2, i2 = first_max(masked2)
    tot = v1 + v2
    w1, w2 = v1 / tot, v2 / tot
    gt_scr[...] = jnp.zeros_like(gt_scr)
    for e in range(N_EXPERTS):
        gt_scr[e:e + 1, :] = jnp.where(i1 == float(e), w1, 0.0) + jnp.where(i2 == float(e), w2, 0.0)
    for g in range(N_GROUPS):
        gt_scr[GROUP_LANE0 + g:GROUP_LANE0 + g + 1, :] = jnp.where(g_sel == float(g), 1.0, 0.0)
    return gt_scr[...].T


def router_weights(w_router, b_router):
    w_pad = jnp.zeros((D_MODEL, LANES), F32).at[:, :N_EXPERTS].set(w_router)
    w_hi = w_pad.astype(BF16)
    w_lo = (w_pad - w_hi.astype(F32)).astype(BF16)
    b_pad = jnp.zeros((1, LANES), F32).at[0, :N_EXPERTS].set(b_router)
    return w_hi, w_lo, b_pad


def _moe_kernel(x_ref, sc_ref, sh_ref, g2_ref, gates_ref, wg_ref, wu_ref, wd_ref, lng_ref, lnb_ref,
                o_ref, h_scr, acc_scr):
    e = pl.program_id(1)

    @pl.when(e == 0)
    def _():
        h_scr[...] = (x_ref[...] * (1.0 + sc_ref[...]) + sh_ref[...]).astype(BF16)
        acc_scr[...] = jnp.zeros_like(acc_scr)

    hb = h_scr[...]
    a = (_silu(jnp.dot(hb, wg_ref[...], preferred_element_type=F32))
         * jnp.dot(hb, wu_ref[...], preferred_element_type=F32))
    y = jnp.dot(a.astype(BF16), wd_ref[...], preferred_element_type=F32)
    gates = gates_ref[...]
    lane = lax.broadcasted_iota(I32, gates.shape, 1)
    gate_e = jnp.sum(jnp.where(lane == e, gates, 0.0), -1, keepdims=True)
    acc_scr[...] += gate_e * y

    @pl.when(e == pl.num_programs(1) - 1)
    def _():
        z = ALPHA * x_ref[...] + g2_ref[...] * acc_scr[...]
        o_ref[...] = _layer_norm(z, lng_ref[...], lnb_ref[...])


def moe_ln(x, sc, sh, g2, gates, l, wg_bf, wu_bf, wd_bf, ln_g, ln_b, tm):
    N = x.shape[0]
    G, R, _ = sc.shape
    tpg = (N // G) // tm
    ms = pl.BlockSpec((None, R, D_MODEL), lambda i, e: (i // tpg, 0, 0))
    row = pl.BlockSpec((1, D_MODEL), lambda i, e: (0, 0))
    return pl.pallas_call(
        _moe_kernel,
        grid=(N // tm, N_EXPERTS),
        in_specs=[pl.BlockSpec((tm, D_MODEL), lambda i, e: (i, 0)), ms, ms, ms,
                  pl.BlockSpec((tm, LANES), lambda i, e: (i, 0)),
                  pl.BlockSpec((None, None, D_MODEL, D_EXPERT), lambda i, e: (l, e, 0, 0)),
                  pl.BlockSpec((None, None, D_MODEL, D_EXPERT), lambda i, e: (l, e, 0, 0)),
                  pl.BlockSpec((None, None, D_EXPERT, D_MODEL), lambda i, e: (l, e, 0, 0)),
                  row, row],
        out_specs=pl.BlockSpec((tm, D_MODEL), lambda i, e: (i, 0)),
        out_shape=jax.ShapeDtypeStruct((N, D_MODEL), F32),
        scratch_shapes=[pltpu.VMEM((tm, D_MODEL), BF16), pltpu.VMEM((tm, D_MODEL), F32)],
        compiler_params=_cparams(("parallel", "arbitrary")),
        name="moe_ln",
    )(x, sc, sh, g2, gates, wg_bf, wu_bf, wd_bf, ln_g.reshape(1, D_MODEL), ln_b.reshape(1, D_MODEL))


def _split3_bf16(x):
    hi = x.astype(BF16)
    r = x - hi.astype(F32)
    mid = r.astype(BF16)
    lo = (r - mid.astype(F32)).astype(BF16)
    return hi, mid, lo


def _moe_group_kernel(x_ref, sc_ref, sh_ref, g2_ref, gates_ref, ltri_ref, wg_ref, wu_ref, wd_ref,
                      lng_ref, lnb_ref, o_ref,
                      h_scr, acc_scr, rank_scr, rank_t_scr, sel_t_scr, xs_scr, gs_scr, yacc_scr, cnt_smem,
                      *, cap):
    g, e = pl.program_id(1), pl.program_id(2)
    tm = x_ref.shape[0]
    lane = lax.broadcasted_iota(I32, (tm, LANES), 1)

    @pl.when((g == 0) & (e == 0))
    def _():
        h_scr[...] = (x_ref[...] * (1.0 + sc_ref[...]) + sh_ref[...]).astype(BF16)
        acc_scr[...] = jnp.zeros_like(acc_scr)
        gates = gates_ref[...]
        sel = jnp.where((lane >= GROUP_LANE0) & (lane < GROUP_LANE0 + N_GROUPS), gates, 0.0)
        rank = jnp.dot(ltri_ref[...], sel.astype(BF16), preferred_element_type=F32)
        rank_scr[...] = rank
        rank_t_scr[...] = rank.T
        sel_t_scr[...] = sel.T
        for gg in range(N_GROUPS):
            cnt_smem[gg] = jnp.sum(jnp.where(lane == GROUP_LANE0 + gg, sel, 0.0)).astype(I32)

    half = cap // 2
    cnt = cnt_smem[g]
    rem = cnt % cap
    use_tail = (rem > 0) & (rem <= half)
    n_full = cnt // cap + jnp.where(rem > half, 1, 0)
    tail_start = pl.multiple_of(n_full * cap, half)
    grp_row = pl.ds(GROUP_LANE0 + g, 1)

    def for_chunks(fn):
        def body(c, carry):
            fn(pl.multiple_of(c * cap, cap), cap)
            return carry
        lax.fori_loop(0, n_full, body, 0)

        @pl.when(use_tail)
        def _():
            fn(tail_start, half)

    @pl.when(e == 0)
    def _():
        expert_lanes = jnp.where(lane < N_EXPERTS, gates_ref[...], 0.0)
        parts = _split3_bf16(expert_lanes)
        g_parts = sum(_lane_roll(part.astype(F32), j * N_EXPERTS)
                      for j, part in enumerate(parts)).astype(BF16)
        rank_row, sel_row = rank_t_scr[grp_row, :], sel_t_scr[grp_row, :]

        def dispatch(start, size):
            rows = pl.ds(start, size)
            slot = (start + lax.broadcasted_iota(I32, (size, 1), 0)).astype(F32)
            p = jnp.where((rank_row == slot) & (sel_row > 0.5), 1.0, 0.0).astype(BF16)
            xs_scr[rows, :] = jnp.dot(p, h_scr[...], preferred_element_type=F32).astype(BF16)
            gs_scr[rows, :] = jnp.dot(p, g_parts, preferred_element_type=F32)
            yacc_scr[rows, :] = jnp.zeros((size, D_MODEL), F32)

        for_chunks(dispatch)

    def expert(start, size):
        rows = pl.ds(start, size)
        xs = xs_scr[rows, :]
        a = (_silu(jnp.dot(xs, wg_ref[...], preferred_element_type=F32))
             * jnp.dot(xs, wu_ref[...], preferred_element_type=F32))
        y = jnp.dot(a.astype(BF16), wd_ref[...], preferred_element_type=F32)
        gs = gs_scr[rows, :]
        lane_c = lax.broadcasted_iota(I32, gs.shape, 1)
        is_part = ((lane_c % N_EXPERTS) == g * EXPERTS_PER_GROUP + e) & (lane_c < 3 * N_EXPERTS)
        gate = jnp.sum(jnp.where(is_part, gs, 0.0), -1, keepdims=True)
        yacc_scr[rows, :] += gate * y

    for_chunks(expert)

    @pl.when(e == EXPERTS_PER_GROUP - 1)
    def _():
        in_grp = lane == GROUP_LANE0 + g
        rank_col = jnp.sum(jnp.where(in_grp, rank_scr[...], 0.0), -1, keepdims=True)
        sel_col = jnp.sum(jnp.where(in_grp, gates_ref[...], 0.0), -1, keepdims=True)

        def combine(start, size):
            slot = (start + lax.broadcasted_iota(I32, (1, size), 1)).astype(F32)
            p_t = jnp.where((rank_col == slot) & (sel_col > 0.5), 1.0, 0.0).astype(BF16)
            y_hi, y_mid, _ = _split3_bf16(yacc_scr[pl.ds(start, size), :])
            acc_scr[...] += (jnp.dot(p_t, y_hi, preferred_element_type=F32)
                             + jnp.dot(p_t, y_mid, preferred_element_type=F32))

        for_chunks(combine)

    @pl.when((g == N_GROUPS - 1) & (e == EXPERTS_PER_GROUP - 1))
    def _():
        z = ALPHA * x_ref[...] + g2_ref[...] * acc_scr[...]
        o_ref[...] = _layer_norm(z, lng_ref[...], lnb_ref[...])


def moe_group_ln(x, sc, sh, g2, gates, l, wg_bf, wu_bf, wd_bf, ln_g, ln_b, tm, cap):
    N = x.shape[0]
    G, R, _ = sc.shape
    tpg = (N // G) // tm
    ltri = (jnp.arange(tm)[:, None] > jnp.arange(tm)[None, :]).astype(BF16)
    ms = pl.BlockSpec((None, R, D_MODEL), lambda i, g, e: (i // tpg, 0, 0))
    row = pl.BlockSpec((1, D_MODEL), lambda i, g, e: (0, 0))
    expert = lambda i, g, e: (l, g * EXPERTS_PER_GROUP + e, 0, 0)
    w_in = pl.BlockSpec((None, None, D_MODEL, D_EXPERT), expert)
    w_out = pl.BlockSpec((None, None, D_EXPERT, D_MODEL), expert)
    return pl.pallas_call(
        functools.partial(_moe_group_kernel, cap=cap),
        grid=(N // tm, N_GROUPS, EXPERTS_PER_GROUP),
        in_specs=[pl.BlockSpec((tm, D_MODEL), lambda i, g, e: (i, 0)), ms, ms, ms,
                  pl.BlockSpec((tm, LANES), lambda i, g, e: (i, 0)),
                  pl.BlockSpec((tm, tm), lambda i, g, e: (0, 0)),
                  w_in, w_in, w_out, row, row],
        out_specs=pl.BlockSpec((tm, D_MODEL), lambda i, g, e: (i, 0)),
        out_shape=jax.ShapeDtypeStruct((N, D_MODEL), F32),
        scratch_shapes=[pltpu.VMEM((tm, D_MODEL), BF16), pltpu.VMEM((tm, D_MODEL), F32),
                        pltpu.VMEM((tm, LANES), F32), pltpu.VMEM((LANES, tm), F32),
                        pltpu.VMEM((LANES, tm), F32), pltpu.VMEM((tm, D_MODEL), BF16),
                        pltpu.VMEM((tm, LANES), F32), pltpu.VMEM((tm, D_MODEL), F32),
                        pltpu.SMEM((N_GROUPS,), I32)],
        compiler_params=_cparams(("arbitrary", "arbitrary", "arbitrary")),
        name="moe_group_ln",
    )(x, sc, sh, g2, gates, ltri, wg_bf, wu_bf, wd_bf, ln_g.reshape(1, D_MODEL), ln_b.reshape(1, D_MODEL))


LOG2_E = 1.4426950408889634
Q_PRESCALE = DSA_HEAD_DIM ** -0.5 * LOG2_E
DSA_IN_PAD = -(-DSA_IN // LANES) * LANES
DSA_QI_OFF = DSA_Q + 2 * DSA_KV
DSA_KW_OFF = DSA_QI_OFF + IDX_HEADS * IDX_DIM


def _rope_tables(pos, rot_dim, period, n_periods):
    half = rot_dim // 2
    inv_freq = 1.0 / (ROPE_THETA ** (jnp.arange(0, rot_dim, 2, dtype=F32) / rot_dim))
    ang = pos.astype(F32)[:, None] * inv_freq[None, :]
    cos, sin = jnp.cos(ang), jnp.sin(ang)
    T = pos.shape[0]
    zeros = jnp.zeros((T, period - 2 * half), F32)
    zh = jnp.zeros((T, half), F32)
    a1 = jnp.concatenate([cos, cos, zeros + 1.0], 1)
    b1 = jnp.concatenate([zh, sin, zeros], 1)
    c1 = jnp.concatenate([-sin, zh, zeros], 1)
    rest = LANES - n_periods * period
    pad1 = jnp.ones((T, rest), F32)
    pad0 = jnp.zeros((T, rest), F32)
    a = jnp.concatenate([a1] * n_periods + [pad1], 1)
    b = jnp.concatenate([b1] * n_periods + [pad0], 1)
    c = jnp.concatenate([c1] * n_periods + [pad0], 1)
    return a, b, c


def _apply_rope(x, a, b, c, half):
    return x * a + _lane_roll(x, half) * b + _lane_roll(x, LANES - half) * c


N_ROPE_TABLES = 9


def _dsa_sections(x_ref, sc_ref, sh_ref, w_ref, tab_refs):
    hb = (x_ref[...] * (1.0 + sc_ref[...]) + sh_ref[...]).astype(BF16)
    ma, mb, mc, ia, ib, ic, ka, kb, kc = (t[...] for t in tab_refs)

    def sec(off, width):
        return jnp.dot(hb, w_ref[:, off:off + width], preferred_element_type=F32)

    def lanes(x, h):
        return x[:, h * LANES:(h + 1) * LANES]

    q = sec(0, DSA_Q)
    q_heads = [_apply_rope(lanes(q, h), ma, mb, mc, ROT_DIM // 2) * Q_PRESCALE for h in range(DSA_HEADS)]
    k = sec(DSA_Q, DSA_KV)
    k = jnp.concatenate([_apply_rope(lanes(k, h), ma, mb, mc, ROT_DIM // 2) for h in range(DSA_KV_HEADS)], 1)
    v = sec(DSA_Q + DSA_KV, DSA_KV)
    qi = sec(DSA_QI_OFF, IDX_HEADS * IDX_DIM)
    qi_pairs = [_apply_rope(lanes(qi, h), ia, ib, ic, IDX_ROT_DIM // 2)
                for h in range(IDX_HEADS * IDX_DIM // LANES)]
    kw = _apply_rope(sec(DSA_KW_OFF, LANES), ka, kb, kc, IDX_ROT_DIM // 2)
    return q_heads, k, v, qi_pairs, kw


def _dsa_proj_kernel(x_ref, sc_ref, sh_ref, w_ref, *rest):
    tabs = rest[:N_ROPE_TABLES]
    q_ref, k_ref, v_ref, qi_ref, kw_ref = rest[N_ROPE_TABLES:]
    q_heads, k, v, qi_pairs, kw = _dsa_sections(x_ref, sc_ref, sh_ref, w_ref, tabs)
    for h, qh in enumerate(q_heads):
        q_ref[:, h * LANES:(h + 1) * LANES] = qh.astype(q_ref.dtype)
    for h, qp in enumerate(qi_pairs):
        qi_ref[:, h * LANES:(h + 1) * LANES] = qp.astype(qi_ref.dtype)
    k_ref[...] = k
    v_ref[...] = v
    kw_ref[...] = kw


def _dsa_proj_prompt_kernel(x_ref, sc_ref, sh_ref, w_ref, *rest, tq):
    tabs = rest[:N_ROPE_TABLES]
    k_ref, v_ref, kw_ref, kbf_ref, kwbf_ref, vt_ref, qt_ref, qit_ref, wit_ref = rest[N_ROPE_TABLES:]
    q_heads, k, v, qi_pairs, kw = _dsa_sections(x_ref, sc_ref, sh_ref, w_ref, tabs)
    tm = k.shape[0]
    k_ref[...] = k
    kbf_ref[...] = k.astype(BF16)
    v_ref[...] = v
    kw_ref[...] = kw
    kwbf_ref[...] = kw.astype(BF16)
    for n in range(DSA_KV_HEADS):
        vt_ref[n, 0] = v[:, n * DSA_HEAD_DIM:(n + 1) * DSA_HEAD_DIM].T.astype(BF16)
    heads_per_pair = LANES // IDX_DIM
    for j in range(tm // tq):
        rows = slice(j * tq, (j + 1) * tq)
        for h, qh in enumerate(q_heads):
            n, g = divmod(h, DSA_GROUP)
            qt_ref[j, n, :, g * tq:(g + 1) * tq] = qh[rows].T.astype(BF16)
        for hp, qp in enumerate(qi_pairs):
            t = qp[rows].T.astype(BF16)
            for s in range(heads_per_pair):
                h = hp * heads_per_pair + s
                qit_ref[j, :IDX_DIM, h * tq:(h + 1) * tq] = t[s * IDX_DIM:(s + 1) * IDX_DIM]
        qit_ref[j, IDX_DIM:, :] = jnp.zeros((LANES - IDX_DIM, IDX_HEADS * tq), BF16)
        wit_ref[j] = kw[rows].T[IDX_DIM:IDX_DIM + IDX_HEADS]


def _dsa_project_call(body, x, sc, sh, w_pad_bf, pos, tm, out_specs, out_shape, name):
    N = x.shape[0]
    G, R, _ = sc.shape
    tpg = (N // G) // tm
    n_tab = pos.shape[0] // tm
    tabs = (_rope_tables(pos, ROT_DIM, DSA_HEAD_DIM, 1)
            + _rope_tables(pos, IDX_ROT_DIM, IDX_DIM, 2)
            + _rope_tables(pos, IDX_ROT_DIM, IDX_DIM, 1))
    tab = pl.BlockSpec((tm, LANES), lambda i: (i % n_tab, 0))
    return pl.pallas_call(
        body,
        grid=(N // tm,),
        in_specs=[pl.BlockSpec((tm, D_MODEL), lambda i: (i, 0)), _mod_spec(R, tpg), _mod_spec(R, tpg),
                  pl.BlockSpec((D_MODEL, DSA_IN_PAD), lambda i: (0, 0))] + [tab] * N_ROPE_TABLES,
        out_specs=out_specs, out_shape=out_shape,
        compiler_params=_cparams(("parallel",)),
        name=name,
    )(x, sc, sh, w_pad_bf, *tabs)


def dsa_project(x, sc, sh, w_pad_bf, pos, tm):
    N = x.shape[0]
    out = lambda w: pl.BlockSpec((tm, w), lambda i: (i, 0))
    return _dsa_project_call(
        _dsa_proj_kernel, x, sc, sh, w_pad_bf, pos, tm,
        [out(DSA_Q), out(DSA_KV), out(DSA_KV), out(IDX_HEADS * IDX_DIM), out(LANES)],
        [jax.ShapeDtypeStruct((N, DSA_Q), BF16), jax.ShapeDtypeStruct((N, DSA_KV), F32),
         jax.ShapeDtypeStruct((N, DSA_KV), F32), jax.ShapeDtypeStruct((N, IDX_HEADS * IDX_DIM), BF16),
         jax.ShapeDtypeStruct((N, LANES), F32)], "dsa_project")


def dsa_project_prompt(x, sc, sh, w_pad_bf, pos, B, T, tm, tq):
    N = B * T
    tiles, n_qt = T // tm, tm // tq
    gq = DSA_GROUP * tq
    out = lambda w: pl.BlockSpec((tm, w), lambda i: (i, 0))
    by_tile = lambda *shape: pl.BlockSpec((None,) + shape,
                                          lambda i: (i // tiles, i % tiles) + (0,) * (len(shape) - 1))
    return _dsa_project_call(
        functools.partial(_dsa_proj_prompt_kernel, tq=tq), x, sc, sh, w_pad_bf, pos, tm,
        [out(DSA_KV), out(DSA_KV), out(LANES), out(DSA_KV), out(LANES),
         pl.BlockSpec((None, DSA_KV_HEADS, 1, DSA_HEAD_DIM, tm), lambda i: (i // tiles, 0, i % tiles, 0, 0)),
         by_tile(n_qt, DSA_KV_HEADS, DSA_HEAD_DIM, gq),
         by_tile(n_qt, LANES, IDX_HEADS * tq),
         by_tile(n_qt, IDX_HEADS, tq)],
        [jax.ShapeDtypeStruct((N, DSA_KV), F32), jax.ShapeDtypeStruct((N, DSA_KV), F32),
         jax.ShapeDtypeStruct((N, LANES), F32), jax.ShapeDtypeStruct((N, DSA_KV), BF16),
         jax.ShapeDtypeStruct((N, LANES), BF16),
         jax.ShapeDtypeStruct((B, DSA_KV_HEADS, tiles, DSA_HEAD_DIM, tm), BF16),
         jax.ShapeDtypeStruct((B, T // tq, DSA_KV_HEADS, DSA_HEAD_DIM, gq), BF16),
         jax.ShapeDtypeStruct((B, T // tq, LANES, IDX_HEADS * tq), BF16),
         jax.ShapeDtypeStruct((B, T // tq, IDX_HEADS, tq), F32)], "dsa_project_prompt")


CODE_NEG_INF = -2139095041
BISECT_FIRST = 22
BISECT_STAGE = 2


def _threshold_of_code(code):
    bits = code ^ ((code >> 31) & jnp.int32(0x7FFFFFFF))
    return jnp.where(code < CODE_NEG_INF, -jnp.inf, pltpu.bitcast(bits, F32))


def _reduce_row_groups(x, op, n_chains=8):
    parts = [x[r:r + SUBLANES] for r in range(0, x.shape[0], SUBLANES)]
    accs = parts[:n_chains]
    for j, part in enumerate(parts[n_chains:]):
        accs[j % len(accs)] = op(accs[j % len(accs)], part)
    while len(accs) > 1:
        accs = [op(a, b) for a, b in zip(accs[0::2], accs[1::2])] + (accs[-1:] if len(accs) % 2 else [])
    return accs[0]


def _sum_row_groups(x):
    return _reduce_row_groups(x, jnp.add)


def _max_row_groups(x):
    return _reduce_row_groups(x, jnp.maximum)


def _kth_largest_threshold(count_ge, shape, k):
    def body(s, p):
        cand = p + lax.shift_left(jnp.int32(1), 31 - s)
        return jnp.where(count_ge(_threshold_of_code(cand)) >= k, cand, p)
    return _threshold_of_code(lax.fori_loop(0, 32, body, jnp.full(shape, INT_MIN, I32)))


def _tie_index_bound(count_eq_below, shape, need, n_bits):
    def body(s, m):
        cand = m + lax.shift_left(jnp.int32(1), n_bits - 1 - s)
        return jnp.where(count_eq_below(cand) < need, cand, m)
    return lax.fori_loop(0, n_bits, body, jnp.zeros(shape, I32))


def _dsa_prompt_kernel(ki_ref, qit_ref, wit_ref, k_ref, vt_ref, qt_ref, o_ref,
                       score_scr, bias_scr, midx_scr, logit_scr, thr_scr, nge_scr,
                       *, tq, kb_size, cb_size, topk, idx_bits):
    i = pl.program_id(1)
    n_kb = ((i + 1) * tq + kb_size - 1) // kb_size
    n_cb = ((i + 1) * tq + cb_size - 1) // cb_size
    q_pos = i * tq + lax.broadcasted_iota(I32, (1, tq), 1)
    heads_per_dot = 2

    @pl.when(i == 0)
    def _():
        score_scr[...] = jnp.full(score_scr.shape, -jnp.inf, F32)

    def key_rows(kb):
        return pl.ds(pl.multiple_of(kb * kb_size, kb_size), kb_size)

    def l_index(kb, size=kb_size):
        return kb * size + lax.broadcasted_iota(I32, (size, 1), 0)

    def score_body(kb, carry):
        kib = ki_ref[key_rows(kb), :]
        acc = jnp.zeros((kb_size, tq), F32)
        for hp in range(IDX_HEADS // heads_per_dot):
            s = jnp.dot(kib, qit_ref[:, hp * heads_per_dot * tq:(hp + 1) * heads_per_dot * tq],
                        preferred_element_type=F32)
            for j in range(heads_per_dot):
                h = hp * heads_per_dot + j
                w = wit_ref[h:h + 1, :] * (IDX_HEADS ** -0.5 * IDX_DIM ** -0.5)
                acc = acc + jnp.maximum(s[:, j * tq:(j + 1) * tq], 0.0) * w
        allowed = l_index(kb) <= q_pos
        score_scr[key_rows(kb), :] = jnp.where(allowed, acc, -jnp.inf)
        return carry

    lax.fori_loop(0, n_kb, score_body, 0)

    def count(pred_fn):
        def body(cb, acc):
            rows = pl.ds(pl.multiple_of(cb * cb_size, cb_size), cb_size)
            m = jnp.where(pred_fn(score_scr[rows, :], cb), 1.0, 0.0)
            return acc + _sum_row_groups(m)
        acc = lax.fori_loop(0, n_cb, body, jnp.zeros((SUBLANES, tq), F32))
        return acc.sum(axis=0, keepdims=True)

    k_f = float(topk)

    def bisect(s_lo, s_hi):
        def body(s, carry):
            p, n_p = carry
            cand = p + lax.shift_left(jnp.int32(1), 31 - s)
            cand_thr = _threshold_of_code(cand)
            n_c = count(lambda score, cb: score >= cand_thr)
            take = n_c >= k_f
            return jnp.where(take, cand, p), jnp.where(take, n_c, n_p)
        p, n_p = lax.fori_loop(s_lo, s_hi, body, (thr_scr[...], nge_scr[...]))
        thr_scr[...] = p
        nge_scr[...] = n_p

    thr_scr[...] = jnp.full((1, tq), INT_MIN, I32)
    nge_scr[...] = jnp.full((1, tq), float(2 ** 30), F32)
    bisect(0, BISECT_FIRST)
    for s0 in range(BISECT_FIRST, 32, BISECT_STAGE):
        @pl.when(jnp.max(jnp.where(nge_scr[...] != k_f, 1.0, 0.0)) > 0.5)
        def _():
            bisect(s0, min(32, s0 + BISECT_STAGE))

    thr = _threshold_of_code(thr_scr[...])
    n_ge = nge_scr[...]
    n_gt = count(lambda score, cb: score > thr)
    need = k_f - n_gt
    excess = (n_ge - n_gt > need) & (thr > -jnp.inf)
    midx_scr[...] = jnp.full((1, tq), 2 ** idx_bits, I32)

    @pl.when(jnp.max(jnp.where(excess, 1.0, 0.0)) > 0.5)
    def _():
        m = _tie_index_bound(
            lambda c: count(lambda score, cb: (score == thr) & (l_index(cb, cb_size) < c)),
            (1, tq), need, idx_bits)
        midx_scr[...] = jnp.where(excess, m, 2 ** idx_bits)

    midx = midx_scr[...]

    def bias_body(kb, carry):
        score = score_scr[key_rows(kb), :]
        l = l_index(kb)
        sel = ((score > thr) | ((score == thr) & (l <= midx))) & (l <= q_pos)
        bias_scr[key_rows(kb), :] = jnp.where(sel, 0.0, NEG_BIG)
        return carry

    lax.fori_loop(0, n_kb, bias_body, 0)

    gq = DSA_GROUP * tq
    heads = range(DSA_KV_HEADS)

    def logits_of(kb):
        bias = bias_scr[key_rows(kb), :]
        bias = jnp.concatenate([bias] * DSA_GROUP, axis=1)
        blk_max = []
        for n in heads:
            kblk = k_ref[key_rows(kb), n * DSA_HEAD_DIM:(n + 1) * DSA_HEAD_DIM]
            logits = jnp.dot(kblk, qt_ref[n], preferred_element_type=F32) + bias
            logit_scr[n, key_rows(kb), :] = logits
            blk_max.append(_max_row_groups(logits))
        return tuple(blk_max)

    def absorb(kb, blk_max, state):
        new = []
        for n in heads:
            m_run, l_run, acc = state[n]
            m_new = jnp.maximum(m_run, jnp.max(blk_max[n], axis=0, keepdims=True))
            alpha = jnp.exp2(m_run - m_new)
            p = jnp.exp2(logit_scr[n, key_rows(kb), :] - m_new)
            acc = acc * alpha + jnp.dot(vt_ref[n, kb], p.astype(BF16), preferred_element_type=F32)
            new.append((m_new, l_run * alpha + _sum_row_groups(p), acc))
        return tuple(new)

    def att_body(kb, carry):
        blk_max, state = carry
        state = absorb(kb - 1, blk_max, state)
        return logits_of(kb), state

    state0 = tuple((jnp.full((1, gq), NEG_BIG, F32), jnp.zeros((SUBLANES, gq), F32),
                    jnp.zeros((DSA_HEAD_DIM, gq), F32)) for _ in heads)
    last_max, state = lax.fori_loop(1, n_kb, att_body, (logits_of(0), state0))
    fin = absorb(n_kb - 1, last_max, state)
    for n in heads:
        _, l8, acc = fin[n]
        o_t = acc / jnp.sum(l8, axis=0, keepdims=True)
        for g in range(DSA_GROUP):
            h = n * DSA_GROUP + g
            o_ref[:, h * DSA_HEAD_DIM:(h + 1) * DSA_HEAD_DIM] = (
                o_t[:, g * tq:(g + 1) * tq].T.astype(o_ref.dtype))


def dsa_prompt_attend(kw_bf, qi_t, wi_t, k_bf, v_t, q_t, B, T, topk, tq, kb_size):
    nqt = T // tq
    nkb = T // kb_size
    gq = DSA_GROUP * tq
    return pl.pallas_call(
        functools.partial(_dsa_prompt_kernel, tq=tq, kb_size=kb_size, cb_size=min(512, T), topk=topk,
                          idx_bits=max(1, (T - 1).bit_length())),
        grid=(B, nqt),
        in_specs=[pl.BlockSpec((None, T, LANES), lambda b, i: (b, 0, 0)),
                  pl.BlockSpec((None, None, LANES, IDX_HEADS * tq), lambda b, i: (b, i, 0, 0)),
                  pl.BlockSpec((None, None, IDX_HEADS, tq), lambda b, i: (b, i, 0, 0)),
                  pl.BlockSpec((None, T, DSA_KV), lambda b, i: (b, 0, 0)),
                  pl.BlockSpec((None, DSA_KV_HEADS, nkb, DSA_HEAD_DIM, kb_size),
                               lambda b, i: (b, 0, 0, 0, 0)),
                  pl.BlockSpec((None, None, DSA_KV_HEADS, DSA_HEAD_DIM, gq),
                               lambda b, i: (b, i, 0, 0, 0))],
        out_specs=pl.BlockSpec((tq, DSA_Q), lambda b, i: (b * nqt + i, 0)),
        out_shape=jax.ShapeDtypeStruct((B * T, DSA_Q), BF16),
        scratch_shapes=[pltpu.VMEM((T, tq), F32), pltpu.VMEM((T, tq), F32), pltpu.VMEM((1, tq), I32),
                        pltpu.VMEM((DSA_KV_HEADS, T, gq), F32),
                        pltpu.VMEM((1, tq), I32), pltpu.VMEM((1, tq), F32)],
        compiler_params=_cparams(("arbitrary", "arbitrary")),
        name="dsa_prompt_attend",
    )(kw_bf.reshape(B, T, LANES), qi_t, wi_t, k_bf.reshape(B, T, DSA_KV), v_t, q_t)


def _dsa_sample_score_kernel(pt_ref, qi_ref, wi_ref, kis_ref, kidx_hbm, o_ref, buf, sem, *, n_pages, page):
    slot = _fetch_pages_ahead(pt_ref, (kidx_hbm,), (buf,), sem, n_pages)
    pages = [buf.at[slot, j] for j in range(n_pages)]
    qi = qi_ref[...]
    w = wi_ref[...] * (IDX_HEADS ** -0.5 * IDX_DIM ** -0.5)
    for j in range(n_pages):
        dots = jnp.dot(qi, pages[j][...].astype(BF16), preferred_element_type=F32)
        o_ref[:, j * page:(j + 1) * page] = jnp.sum(jnp.maximum(dots, 0.0) * w, axis=0, keepdims=True)
    ki_self = kis_ref[...].astype(BF16).astype(F32)
    d_self = jnp.sum(qi.astype(F32) * ki_self, -1, keepdims=True)
    s_self = jnp.sum(jnp.maximum(d_self, 0.0) * w, axis=0, keepdims=True)
    lane = lax.broadcasted_iota(I32, (1, LANES), 1)
    o_ref[:, n_pages * page:] = jnp.where(lane == 0, s_self, -jnp.inf)


def dsa_sample_scores(page_table, qi_bf, wi, ki_self, cache_kidx_t):
    DB, n_pages = page_table.shape
    page = cache_kidx_t.shape[2]
    lp = n_pages * page + LANES
    per_b = lambda *shape: pl.BlockSpec((None,) + shape, lambda b, pt: (b,) + (0,) * len(shape))
    return pl.pallas_call(
        functools.partial(_dsa_sample_score_kernel, n_pages=n_pages, page=page),
        grid_spec=pltpu.PrefetchScalarGridSpec(
            num_scalar_prefetch=1, grid=(DB,),
            in_specs=[per_b(IDX_HEADS, IDX_DIM), per_b(IDX_HEADS, 1), per_b(1, IDX_DIM),
                      pl.BlockSpec(memory_space=pl.ANY)],
            out_specs=per_b(1, lp),
            scratch_shapes=[pltpu.VMEM((SCORE_PAGE_SLOTS, n_pages, IDX_DIM, page), F32),
                            pltpu.SemaphoreType.DMA((SCORE_PAGE_SLOTS,))]),
        out_shape=jax.ShapeDtypeStruct((DB, 1, lp), F32),
        compiler_params=_cparams(("arbitrary",)),
        name="dsa_sample_scores",
    )(page_table, qi_bf.reshape(DB, IDX_HEADS, IDX_DIM), wi.reshape(DB, IDX_HEADS, 1),
      ki_self.reshape(DB, 1, IDX_DIM), cache_kidx_t)


def _dsa_sample_select_kernel(s_ref, o_ref, *, n_keys, topk, idx_bits):
    score = s_ref[...]
    lane = lax.broadcasted_iota(I32, score.shape, 1)
    valid = lane < n_keys
    score = jnp.where(valid, score, -jnp.inf)

    def count(pred):
        return jnp.sum(jnp.where(pred, 1.0, 0.0), -1, keepdims=True)

    shape = (score.shape[0], 1)
    thr = _kth_largest_threshold(lambda t: count(score >= t), shape, float(topk))
    n_gt = count(score > thr)
    need = float(topk) - n_gt
    excess = (count(score == thr) > need) & (thr > -jnp.inf)
    m = _tie_index_bound(lambda c: count((score == thr) & (lane < c)), shape, need, idx_bits)
    midx = jnp.where(excess, m, 2 ** idx_bits)
    sel = ((score > thr) | ((score == thr) & (lane <= midx))) & valid
    o_ref[...] = jnp.where(sel, 0.0, NEG_BIG)


def dsa_sample_select(score, n_keys, topk):
    DB, lp = score.shape
    return pl.pallas_call(
        functools.partial(_dsa_sample_select_kernel, n_keys=n_keys, topk=topk,
                          idx_bits=max(1, (lp - 1).bit_length())),
        grid=(1,),
        in_specs=[pl.BlockSpec((DB, lp), lambda i: (0, 0))],
        out_specs=pl.BlockSpec((DB, lp), lambda i: (0, 0)),
        out_shape=jax.ShapeDtypeStruct((DB, lp), F32),
        compiler_params=_cparams(("arbitrary",)),
        name="dsa_sample_select",
    )(score)


SCORE_PAGE_SLOTS = 4
ATTEND_PAGE_SLOTS = 3


def _page_copies(pt_ref, b, slot, hbm_refs, bufs, sem, n_pages):
    return [pltpu.make_async_copy(hbm.at[pt_ref[b, j]], buf.at[slot, j], sem.at[slot])
            for hbm, buf in zip(hbm_refs, bufs) for j in range(n_pages)]


def _fetch_pages_ahead(pt_ref, hbm_refs, bufs, sem, n_pages):
    n_slots = bufs[0].shape[0]
    ahead = n_slots - 1
    b = pl.program_id(0)
    n_steps = pl.num_programs(0)

    @pl.when(b == 0)
    def _():
        for s in range(ahead):
            @pl.when(s < n_steps)
            def _():
                for c in _page_copies(pt_ref, s, s, hbm_refs, bufs, sem, n_pages):
                    c.start()

    @pl.when(b + ahead < n_steps)
    def _():
        for c in _page_copies(pt_ref, b + ahead, (b + ahead) % n_slots, hbm_refs, bufs, sem, n_pages):
            c.start()

    slot = b % n_slots
    for c in _page_copies(pt_ref, b, slot, hbm_refs, bufs, sem, n_pages):
        c.wait()
    return slot


def _dsa_sample_attend_kernel(pt_ref, q_ref, bias_ref, ks_ref, vs_ref, k_hbm, v_hbm, o_ref,
                              kbuf, vbuf, sem, logit_scr, *, n_pages, page):
    slot = _fetch_pages_ahead(pt_ref, (k_hbm, v_hbm), (kbuf, vbuf), sem, n_pages)
    k_pages = [kbuf.at[slot, j] for j in range(n_pages)]
    v_pages = [vbuf.at[slot, j] for j in range(n_pages)]
    q = q_ref[...]
    tail = n_pages * page
    lane = lax.broadcasted_iota(I32, (DSA_GROUP, LANES), 1)
    for n in range(DSA_KV_HEADS):
        hs = slice(n * DSA_GROUP, (n + 1) * DSA_GROUP)
        ds = slice(n * DSA_HEAD_DIM, (n + 1) * DSA_HEAD_DIM)
        head_rows = pl.ds(n, page, stride=DSA_KV_HEADS)
        qn = q[hs, :]
        for j in range(n_pages):
            lg = lax.dot_general(qn, k_pages[j][head_rows, :].astype(BF16), (((1,), (1,)), ((), ())),
                                 preferred_element_type=F32)
            logit_scr[hs, j * page:(j + 1) * page] = lg + bias_ref[:, j * page:(j + 1) * page]
        k_self = ks_ref[:, ds].astype(BF16).astype(F32)
        lg_self = jnp.sum(qn.astype(F32) * k_self, -1, keepdims=True)
        logit_scr[hs, tail:] = jnp.where(lane == 0, lg_self, 0.0) + bias_ref[:, tail:]
        logits = logit_scr[hs, :]
        m = jnp.max(logits, -1, keepdims=True)
        p = jnp.exp2(logits - m)
        p_bf = (p / jnp.sum(p, -1, keepdims=True)).astype(BF16)
        v_self = vs_ref[:, ds].astype(BF16).astype(F32)
        acc = p_bf[:, tail:].astype(F32)[:, :1] * v_self
        for j in range(n_pages):
            acc = acc + jnp.dot(p_bf[:, j * page:(j + 1) * page], v_pages[j][head_rows, :].astype(BF16),
                                preferred_element_type=F32)
        o_ref[hs, :] = acc


def dsa_sample_attend(page_table, q_bf, bias, k_self, v_self, cache_k, cache_v):
    DB, n_pages = page_table.shape
    rows = cache_k.shape[1]
    page = rows // DSA_KV_HEADS
    lp = n_pages * page + LANES
    per_b = lambda *shape: pl.BlockSpec((None,) + shape, lambda b, pt: (b,) + (0,) * len(shape))
    o = pl.pallas_call(
        functools.partial(_dsa_sample_attend_kernel, n_pages=n_pages, page=page),
        grid_spec=pltpu.PrefetchScalarGridSpec(
            num_scalar_prefetch=1, grid=(DB,),
            in_specs=[per_b(DSA_HEADS, DSA_HEAD_DIM), per_b(1, lp), per_b(1, DSA_KV), per_b(1, DSA_KV),
                      pl.BlockSpec(memory_space=pl.ANY), pl.BlockSpec(memory_space=pl.ANY)],
            out_specs=per_b(DSA_HEADS, DSA_HEAD_DIM),
            scratch_shapes=[pltpu.VMEM((ATTEND_PAGE_SLOTS, n_pages, rows, DSA_HEAD_DIM), F32),
                            pltpu.VMEM((ATTEND_PAGE_SLOTS, n_pages, rows, DSA_HEAD_DIM), F32),
                            pltpu.SemaphoreType.DMA((ATTEND_PAGE_SLOTS,)),
                            pltpu.VMEM((DSA_HEADS, lp), F32)]),
        out_shape=jax.ShapeDtypeStruct((DB, DSA_HEADS, DSA_HEAD_DIM), F32),
        compiler_params=_cparams(("arbitrary",)),
        name="dsa_sample_attend",
    )(page_table, q_bf.reshape(DB, DSA_HEADS, DSA_HEAD_DIM), bias.reshape(DB, 1, lp),
      k_self.reshape(DB, 1, DSA_KV), v_self.reshape(DB, 1, DSA_KV), cache_k, cache_v)
    return o.reshape(DB, DSA_Q)


def _retnet_tables(pos):
    inv_freq = jnp.power(RET_ANGLE_BASE, -jnp.linspace(0.0, 1.0, RET_DK // 2, dtype=F32))
    ang = pos.astype(F32)[:, None] * inv_freq[None, :]
    return jnp.cos(ang), jnp.sin(ang)


def _split_mod(mod_l, G, R):
    return [m.reshape(G, R, D_MODEL) for m in jnp.split(mod_l, 6, axis=-1)]


MOE_GROUP_TILE = 1024
MOE_GROUP_CAP = 256


def _finish_layer(a, w_out_bf, x, g1, sc2, sh2, g2, l, wts, tm):
    x, gates = out_proj_ln_route(a, w_out_bf, x, g1, wts["ln1_g"][l], wts["ln1_b"][l], sc2, sh2,
                                 wts["router"], tm)
    args = (x, sc2, sh2, g2, gates, l, wts["w_gate"], wts["w_up"], wts["w_down"],
            wts["ln2_g"][l], wts["ln2_b"][l])
    rows_per_mod = x.shape[0] // sc2.shape[0]
    if sc2.shape[1] == 1 and rows_per_mod >= 2 * MOE_GROUP_CAP:
        tile = min(MOE_GROUP_TILE, rows_per_mod)
        return moe_group_ln(*args, tile, MOE_GROUP_CAP)
    return moe_ln(*args, tm)


def kernel(x_prompt, x_sample, state_ret, cache_k, cache_v, cache_kidx, page_table,
           c_prompt, c_sample, w_mod, b_mod, ln1_g, ln1_b, ln2_g, ln2_b,
           w_in_ret, gn_ret_g, w_out_ret, w_in_dsa, w_out_dsa,
           w_router, b_router, w_gate, w_up, w_down):
    B, T, _ = x_prompt.shape
    DB = x_sample.shape[0]
    assert x_sample.shape[1] == 1
    n_pages = page_table.shape[1]
    page = cache_k.shape[2]
    past = n_pages * page
    n_pool = cache_k.shape[1]

    wts = dict(router=router_weights(w_router, b_router), ln1_g=ln1_g, ln1_b=ln1_b, ln2_g=ln2_g, ln2_b=ln2_b,
               w_gate=w_gate.astype(BF16), w_up=w_up.astype(BF16), w_down=w_down.astype(BF16))
    w_mod_bf = w_mod.astype(BF16)
    w_in_ret_bf = w_in_ret[0].astype(BF16)
    w_out_ret_bf = w_out_ret[0].astype(BF16)
    w_in_dsa_bf = jnp.pad(w_in_dsa[0], ((0, 0), (0, DSA_IN_PAD - DSA_IN))).astype(BF16)
    w_out_dsa_bf = w_out_dsa[0].astype(BF16)

    pos_p = jnp.arange(T, dtype=I32)
    pos_s = jnp.full((1,), past, I32)

    tm_p = min(512, T)
    tq = min(128, T)
    xp = x_prompt.reshape(B * T, D_MODEL)
    mod_p = modulation(c_prompt, w_mod_bf, b_mod)
    sh1, sc1, g1, sh2, sc2, g2 = _split_mod(mod_p[0], B, 1)
    proj = mod_proj(xp, sc1, sh1, w_in_ret_bf, min(1024, T), 1536, BF16)
    cos_p, sin_p = _retnet_tables(pos_p)
    gated, ret_p = retention_prompt(proj.reshape(B, T, RET_IN), gn_ret_g[0], cos_p, sin_p, min(256, T))
    xp = _finish_layer(gated.reshape(B * T, RET_V), w_out_ret_bf, xp, g1, sc2, sh2, g2, 0, wts, tm_p)

    sh1, sc1, g1, sh2, sc2, g2 = _split_mod(mod_p[1], B, 1)
    k_p, v_p, kw_p, k_bf, kw_bf, v_t, q_t, qi_t, wi_t = dsa_project_prompt(
        xp, sc1, sh1, w_in_dsa_bf, pos_p, B, T, tm_p, tq)
    o_p = dsa_prompt_attend(kw_bf, qi_t, wi_t, k_bf, v_t, q_t, B, T, min(DSA_TOPK, T // 4), tq, tm_p)
    xp = _finish_layer(o_p, w_out_dsa_bf, xp, g1, sc2, sh2, g2, 1, wts, tm_p)

    xs = x_sample.reshape(DB, D_MODEL)
    mod_s = modulation(c_sample, w_mod_bf, b_mod)
    sh1, sc1, g1, sh2, sc2, g2 = _split_mod(mod_s[0], 1, DB)
    proj_s = mod_proj(xs, sc1, sh1, w_in_ret_bf, DB, 1536, F32)
    cos_s, sin_s = _retnet_tables(pos_s)
    gated_s, ret_s = retention_sample(proj_s, state_ret[0], gn_ret_g[0], cos_s, sin_s)
    xs = _finish_layer(gated_s.reshape(DB, RET_V), w_out_ret_bf, xs, g1, sc2, sh2, g2, 0, wts, DB)

    sh1, sc1, g1, sh2, sc2, g2 = _split_mod(mod_s[1], 1, DB)
    pos_rows = jnp.full((DB,), past, I32)
    q_s, k_s, v_s, qi_s, kw_s = dsa_project(xs, sc1, sh1, w_in_dsa_bf, pos_rows, DB)
    ki_s = kw_s[:, :IDX_DIM]
    score = dsa_sample_scores(page_table, qi_s, kw_s[:, IDX_DIM:IDX_DIM + IDX_HEADS], ki_s,
                              jnp.swapaxes(cache_kidx[0], 1, 2))
    bias = dsa_sample_select(score.reshape(DB, past + LANES), past + 1, min(DSA_TOPK, (past + 1) // 4))
    o_s = dsa_sample_attend(page_table, q_s, bias, k_s, v_s,
                            cache_k[0].reshape(n_pool, page * DSA_KV_HEADS, DSA_HEAD_DIM),
                            cache_v[0].reshape(n_pool, page * DSA_KV_HEADS, DSA_HEAD_DIM))
    xs = _finish_layer(o_s, w_out_dsa_bf, xs, g1, sc2, sh2, g2, 1, wts, DB)

    kv_shape = (DSA_KV_HEADS, DSA_HEAD_DIM)
    return (xp.reshape(B, T, D_MODEL), xs.reshape(DB, 1, D_MODEL),
            ret_p[None], ret_s[None],
            k_p.reshape(1, B, T, *kv_shape), v_p.reshape(1, B, T, *kv_shape),
            kw_p[:, :IDX_DIM].reshape(1, B, T, IDX_DIM),
            k_s.reshape(1, DB, 1, *kv_shape), v_s.reshape(1, DB, 1, *kv_shape),
            ki_s.reshape(1, DB, 1, IDX_DIM))
```

```python
import functools

import jax
import jax.numpy as jnp
from jax import lax
from jax.experimental import pallas as pl
from jax.experimental.pallas import tpu as pltpu

F32 = jnp.float32
BF16 = jnp.bfloat16
I32 = jnp.int32

D_MODEL = 1024
DEPTH = 2
ALPHA = (2.0 * DEPTH) ** 0.25
LN_EPS = 1e-5
GN_EPS = 1e-6

RET_HEADS = 4
RET_DK = 256
RET_DV = 512
RET_QK = RET_HEADS * RET_DK
RET_V = RET_HEADS * RET_DV
RET_IN = 2 * RET_QK + 2 * RET_V
RET_ANGLE_BASE = 10000.0

DSA_HEADS = 8
DSA_KV_HEADS = 2
DSA_HEAD_DIM = 128
DSA_GROUP = DSA_HEADS // DSA_KV_HEADS
DSA_Q = DSA_HEADS * DSA_HEAD_DIM
DSA_KV = DSA_KV_HEADS * DSA_HEAD_DIM
IDX_HEADS = 16
IDX_DIM = 64
DSA_TOPK = 256
ROPE_THETA = 500000.0
ROT_DIM = DSA_HEAD_DIM // 4
IDX_ROT_DIM = IDX_DIM // 4
DSA_IN = DSA_Q + 2 * DSA_KV + IDX_HEADS * IDX_DIM + IDX_DIM + IDX_HEADS

N_EXPERTS = 16
N_GROUPS = 4
EXPERTS_PER_GROUP = N_EXPERTS // N_GROUPS
D_EXPERT = 512
GROUP_LANE0 = N_EXPERTS

LANES = 128
SUBLANES = 8
VMEM_LIMIT = 56 * 1024 * 1024
NEG_BIG = -1e30
INT_MIN = -2147483648


def _cparams(sem):
    return pltpu.CompilerParams(dimension_semantics=sem, vmem_limit_bytes=VMEM_LIMIT)


def _silu(x):
    return x * (1.0 / (1.0 + jnp.exp(-x)))


def _layer_norm(z, g, b):
    mu = jnp.mean(z, -1, keepdims=True)
    d = z - mu
    var = jnp.mean(d * d, -1, keepdims=True)
    return d * lax.rsqrt(var + LN_EPS) * g + b


def _mod_kernel(c_ref, w_ref, b_ref, o_ref):
    a = _silu(c_ref[...]).astype(BF16)
    o_ref[...] = jnp.dot(a, w_ref[...], preferred_element_type=F32) + b_ref[...]


def modulation(c, w_mod_bf, b_mod):
    R = c.shape[0]
    tn = 1536
    return pl.pallas_call(
        _mod_kernel,
        grid=(DEPTH, 6 * D_MODEL // tn),
        in_specs=[pl.BlockSpec((R, D_MODEL), lambda l, j: (0, 0)),
                  pl.BlockSpec((None, D_MODEL, tn), lambda l, j: (l, 0, j)),
                  pl.BlockSpec((None, 1, tn), lambda l, j: (l, 0, j))],
        out_specs=pl.BlockSpec((None, R, tn), lambda l, j: (l, 0, j)),
        out_shape=jax.ShapeDtypeStruct((DEPTH, R, 6 * D_MODEL), F32),
        compiler_params=_cparams(("parallel", "parallel")),
        name="modulation",
    )(c, w_mod_bf, b_mod.reshape(DEPTH, 1, 6 * D_MODEL))


def _proj_kernel(x_ref, sc_ref, sh_ref, w_ref, o_ref, h_scr):
    @pl.when(pl.program_id(1) == 0)
    def _():
        h_scr[...] = (x_ref[...] * (1.0 + sc_ref[...]) + sh_ref[...]).astype(BF16)

    o_ref[...] = jnp.dot(h_scr[...], w_ref[...], preferred_element_type=F32).astype(o_ref.dtype)


def _mod_spec(R, tiles_per_group):
    return pl.BlockSpec((None, R, D_MODEL), lambda i, *_: (i // tiles_per_group, 0, 0))


def mod_proj(x, sc, sh, w_bf, tm, tn, out_dtype):
    N = x.shape[0]
    G, R, _ = sc.shape
    n_out = w_bf.shape[1]
    tpg = (N // G) // tm
    return pl.pallas_call(
        _proj_kernel,
        grid=(N // tm, n_out // tn),
        in_specs=[pl.BlockSpec((tm, D_MODEL), lambda i, j: (i, 0)),
                  _mod_spec(R, tpg), _mod_spec(R, tpg),
                  pl.BlockSpec((D_MODEL, tn), lambda i, j: (0, j))],
        out_specs=pl.BlockSpec((tm, tn), lambda i, j: (i, j)),
        out_shape=jax.ShapeDtypeStruct((N, n_out), out_dtype),
        scratch_shapes=[pltpu.VMEM((tm, D_MODEL), BF16)],
        compiler_params=_cparams(("parallel", "arbitrary")),
        name="mod_proj",
    )(x, sc, sh, w_bf)


def _rot_half(x, cos, sin):
    half = cos.shape[-1]
    x1, x2 = x[:, :half], x[:, half:]
    return jnp.concatenate([x1 * cos - x2 * sin, x1 * sin + x2 * cos], axis=1)


def _group_norm_gate(o, gn, g):
    mu = jnp.mean(o, -1, keepdims=True)
    d = o - mu
    var = jnp.mean(d * d, -1, keepdims=True)
    return d * lax.rsqrt(var + GN_EPS) * gn * _silu(g)


def _ret_prompt_kernel(q_ref, k_ref, v_ref, g_ref, cos_ref, sin_ref, lg_ref, gn_ref,
                       o_ref, s_ref, s_scr, decay_scr, *, chunk):
    c = pl.program_id(2)
    lg_row = lg_ref[...]
    lg = lg_row[:, :1]

    @pl.when(c == 0)
    def _():
        s_scr[...] = jnp.zeros_like(s_scr)
        row = lax.broadcasted_iota(I32, (chunk, chunk), 0)
        col = lax.broadcasted_iota(I32, (chunk, chunk), 1)
        diff = (row - col).astype(F32)
        decay_scr[...] = jnp.where(diff >= 0, jnp.exp(lg_row * jnp.maximum(diff, 0.0)), 0.0)

    cos, sin = cos_ref[...], sin_ref[...]
    q = _rot_half(q_ref[...].astype(F32), cos, sin)
    k = _rot_half(k_ref[...].astype(F32), cos, sin) * (RET_DK ** -0.5)
    vb = v_ref[...].astype(BF16)
    qb = q.astype(BF16)

    decay = decay_scr[...]
    idx = lax.broadcasted_iota(I32, (chunk, 1), 0).astype(F32)
    q_dec = jnp.exp(lg * (idx + 1.0))
    k_dec = jnp.exp(lg * (chunk - 1.0 - idx))
    chunk_dec = jnp.exp(lg * chunk)

    s = s_scr[...]
    att = lax.dot_general(qb, k.astype(BF16), (((1,), (1,)), ((), ())),
                          preferred_element_type=F32) * decay
    o = (jnp.dot(att.astype(BF16), vb, preferred_element_type=F32)
         + jnp.dot(qb, s.astype(BF16), preferred_element_type=F32) * q_dec)
    kd_t = (k * k_dec).T.astype(BF16)
    s_new = chunk_dec * s + jnp.dot(kd_t, vb, preferred_element_type=F32)
    s_scr[...] = s_new
    o_ref[...] = _group_norm_gate(o, gn_ref[...], g_ref[...].astype(F32)).astype(o_ref.dtype)

    @pl.when(c == pl.num_programs(2) - 1)
    def _():
        s_ref[...] = s_new


def _log_gamma():
    return jnp.log(1.0 - jnp.power(2.0, -5.0 - jnp.arange(RET_HEADS, dtype=F32)))


def retention_prompt(proj, gn_g, cos, sin, chunk):
    B, T, _ = proj.shape
    lg_tab = jnp.broadcast_to(_log_gamma()[:, None, None], (RET_HEADS, 1, chunk))
    qk_blk = lambda off: pl.BlockSpec((None, chunk, RET_DK), lambda b, h, c: (b, c, off + h))
    v_blk = lambda off: pl.BlockSpec((None, chunk, RET_DV), lambda b, h, c: (b, c, off + h))
    tab = pl.BlockSpec((chunk, RET_DK // 2), lambda b, h, c: (c, 0))
    return pl.pallas_call(
        functools.partial(_ret_prompt_kernel, chunk=chunk),
        grid=(B, RET_HEADS, T // chunk),
        in_specs=[qk_blk(0), qk_blk(RET_QK // RET_DK),
                  v_blk(2 * RET_QK // RET_DV), v_blk((2 * RET_QK + RET_V) // RET_DV),
                  tab, tab,
                  pl.BlockSpec((None, 1, chunk), lambda b, h, c: (h, 0, 0)),
                  pl.BlockSpec((1, RET_DV), lambda b, h, c: (0, h))],
        out_specs=[pl.BlockSpec((None, chunk, RET_DV), lambda b, h, c: (b, c, h)),
                   pl.BlockSpec((None, None, RET_DK, RET_DV), lambda b, h, c: (b, h, 0, 0))],
        out_shape=[jax.ShapeDtypeStruct((B, T, RET_V), BF16),
                   jax.ShapeDtypeStruct((B, RET_HEADS, RET_DK, RET_DV), F32)],
        scratch_shapes=[pltpu.VMEM((RET_DK, RET_DV), F32), pltpu.VMEM((chunk, chunk), F32)],
        compiler_params=_cparams(("parallel", "parallel", "arbitrary")),
        name="retention_prompt",
    )(proj, proj, proj, proj, cos, sin, lg_tab, gn_g.reshape(1, RET_V))


def _ret_sample_kernel(qk_ref, v_ref, g_ref, s0_ref, cos_ref, sin_ref, gam_ref, gn_ref,
                       o_ref, s_ref):
    t = _rot_half(qk_ref[...], cos_ref[...], sin_ref[...])
    row = lax.broadcasted_iota(I32, t.shape, 0)
    t = jnp.where(row >= RET_HEADS, t * (RET_DK ** -0.5), t)
    pad = jnp.zeros((LANES - 2 * RET_HEADS, RET_DK), F32)
    t_t = jnp.concatenate([t, pad], axis=0).T
    for h in range(RET_HEADS):
        qc = t_t[:, h:h + 1]
        kc = t_t[:, RET_HEADS + h:RET_HEADS + h + 1]
        gam = gam_ref[h]
        vh = v_ref[h:h + 1, :]
        s = s0_ref[h]
        qk_dot = jnp.sum(t[h:h + 1, :] * t[RET_HEADS + h:RET_HEADS + h + 1, :], -1, keepdims=True)
        o = qk_dot * vh + gam * jnp.sum(s * qc, axis=0, keepdims=True)
        s_ref[h] = gam * s + kc * vh
        o_ref[h:h + 1, :] = _group_norm_gate(o, gn_ref[h:h + 1, :], g_ref[h:h + 1, :])


def retention_sample(proj, s0, gn_g, cos, sin):
    DB = proj.shape[0]
    qk = proj[:, :2 * RET_QK].reshape(DB, 2 * RET_HEADS, RET_DK)
    v = proj[:, 2 * RET_QK:2 * RET_QK + RET_V].reshape(DB, RET_HEADS, RET_DV)
    g = proj[:, 2 * RET_QK + RET_V:].reshape(DB, RET_HEADS, RET_DV)
    gam = jnp.broadcast_to(jnp.exp(_log_gamma())[:, None, None], (RET_HEADS, 1, RET_DV))
    full = lambda *shape: pl.BlockSpec(shape, lambda b: (0,) * len(shape))
    per_b = lambda *shape: pl.BlockSpec((None,) + shape, lambda b: (b,) + (0,) * len(shape))
    return pl.pallas_call(
        _ret_sample_kernel,
        grid=(DB,),
        in_specs=[per_b(2 * RET_HEADS, RET_DK), per_b(RET_HEADS, RET_DV), per_b(RET_HEADS, RET_DV),
                  per_b(RET_HEADS, RET_DK, RET_DV),
                  full(1, RET_DK // 2), full(1, RET_DK // 2),
                  full(RET_HEADS, 1, RET_DV), full(RET_HEADS, RET_DV)],
        out_specs=[per_b(RET_HEADS, RET_DV), per_b(RET_HEADS, RET_DK, RET_DV)],
        out_shape=[jax.ShapeDtypeStruct((DB, RET_HEADS, RET_DV), F32),
                   jax.ShapeDtypeStruct((DB, RET_HEADS, RET_DK, RET_DV), F32)],
        compiler_params=_cparams(("parallel",)),
        name="retention_sample",
    )(qk, v, g, s0, cos, sin, gam, gn_g.reshape(RET_HEADS, RET_DV))


def _out_ln_route_kernel(a_ref, w_ref, x_ref, g_ref, lng_ref, lnb_ref, sc2_ref, sh2_ref,
                         whi_ref, wlo_ref, b_ref, o_ref, gates_ref, gt_scr):
    y = jnp.dot(a_ref[...].astype(BF16), w_ref[...], preferred_element_type=F32)
    x1 = _layer_norm(ALPHA * x_ref[...] + g_ref[...] * y, lng_ref[...], lnb_ref[...])
    o_ref[...] = x1
    h2 = x1 * (1.0 + sc2_ref[...]) + sh2_ref[...]
    gates_ref[...] = _route(h2, whi_ref[...], wlo_ref[...], b_ref[...], gt_scr)


def out_proj_ln_route(a, w_bf, x, gate, ln_g, ln_b, sc2, sh2, router_w, tm):
    N, K = a.shape
    G, R, _ = gate.shape
    tpg = (N // G) // tm
    row = pl.BlockSpec((1, D_MODEL), lambda i: (0, 0))
    full = lambda r, c: pl.BlockSpec((r, c), lambda i: (0, 0))
    ms = _mod_spec(R, tpg)
    return pl.pallas_call(
        _out_ln_route_kernel,
        grid=(N // tm,),
        in_specs=[pl.BlockSpec((tm, K), lambda i: (i, 0)),
                  pl.BlockSpec((K, D_MODEL), lambda i: (0, 0)),
                  pl.BlockSpec((tm, D_MODEL), lambda i: (i, 0)),
                  ms, row, row, ms, ms,
                  full(D_MODEL, LANES), full(D_MODEL, LANES), full(1, LANES)],
        out_specs=[pl.BlockSpec((tm, D_MODEL), lambda i: (i, 0)),
                   pl.BlockSpec((tm, LANES), lambda i: (i, 0))],
        out_shape=[jax.ShapeDtypeStruct((N, D_MODEL), F32), jax.ShapeDtypeStruct((N, LANES), F32)],
        scratch_shapes=[pltpu.VMEM((LANES, tm), F32)],
        compiler_params=_cparams(("parallel",)),
        name="out_proj_ln_route",
    )(a, w_bf, x, gate, ln_g.reshape(1, D_MODEL), ln_b.reshape(1, D_MODEL), sc2, sh2, *router_w)


def _lane_roll(x, shift):
    return pltpu.roll(x, shift, axis=1)


def _route(h, whi, wlo, b, gt_scr):
    hi = h.astype(BF16)
    lo = (h - hi.astype(F32)).astype(BF16)
    logits = (jnp.dot(hi, whi, preferred_element_type=F32)
              + jnp.dot(hi, wlo, preferred_element_type=F32)
              + jnp.dot(lo, whi, preferred_element_type=F32)) + b
    lt = logits.T
    l_rows = [lt[e:e + 1, :] for e in range(N_EXPERTS)]
    m = functools.reduce(jnp.maximum, l_rows)
    ex = [jnp.exp(l - m) for l in l_rows]
    denom = functools.reduce(jnp.add, ex)
    p = [x / denom for x in ex]

    def first_max(vals):
        best, idx = vals[0], jnp.zeros_like(vals[0])
        for j, val in enumerate(vals[1:], 1):
            take = val > best
            idx = jnp.where(take, float(j), idx)
            best = jnp.where(take, val, best)
        return best, idx

    grp_score = []
    for g in range(N_GROUPS):
        a, b, c, d = p[g * EXPERTS_PER_GROUP:(g + 1) * EXPERTS_PER_GROUP]
        pairs = [a + b, a + c, a + d, b + c, b + d, c + d]
        grp_score.append(functools.reduce(jnp.maximum, pairs))
    _, g_sel = first_max(grp_score)
    neg_inf = jnp.full_like(m, -jnp.inf)
    masked = [jnp.where(g_sel == float(e // EXPERTS_PER_GROUP), p[e], neg_inf) for e in range(N_EXPERTS)]
    v1, i1 = first_max(masked)
    masked2 = [jnp.where(i1 == float(e), neg_inf, masked[e]) for e in range(N_EXPERTS)]
    v2, i2 = first_max(masked2)
    tot = v1 + v2
    w1, w2 = v1 / tot, v2 / tot
    gt_scr[...] = jnp.zeros_like(gt_scr)
    for e in range(N_EXPERTS):
        gt_scr[e:e + 1, :] = jnp.where(i1 == float(e), w1, 0.0) + jnp.where(i2 == float(e), w2, 0.0)
    for g in range(N_GROUPS):
        gt_scr[GROUP_LANE0 + g:GROUP_LANE0 + g + 1, :] = jnp.where(g_sel == float(g), 1.0, 0.0)
    return gt_scr[...].T


def router_weights(w_router, b_router):
    w_pad = jnp.zeros((D_MODEL, LANES), F32).at[:, :N_EXPERTS].set(w_router)
    w_hi = w_pad.astype(BF16)
    w_lo = (w_pad - w_hi.astype(F32)).astype(BF16)
    b_pad = jnp.zeros((1, LANES), F32).at[0, :N_EXPERTS].set(b_router)
    return w_hi, w_lo, b_pad


def _moe_kernel(x_ref, sc_ref, sh_ref, g2_ref, gates_ref, wg_ref, wu_ref, wd_ref, lng_ref, lnb_ref,
                o_ref, h_scr, acc_scr):
    e = pl.program_id(1)

    @pl.when(e == 0)
    def _():
        h_scr[...] = (x_ref[...] * (1.0 + sc_ref[...]) + sh_ref[...]).astype(BF16)
        acc_scr[...] = jnp.zeros_like(acc_scr)

    hb = h_scr[...]
    a = (_silu(jnp.dot(hb, wg_ref[...], preferred_element_type=F32))
         * jnp.dot(hb, wu_ref[...], preferred_element_type=F32))
    y = jnp.dot(a.astype(BF16), wd_ref[...], preferred_element_type=F32)
    gates = gates_ref[...]
    lane = lax.broadcasted_iota(I32, gates.shape, 1)
    gate_e = jnp.sum(jnp.where(lane == e, gates, 0.0), -1, keepdims=True)
    acc_scr[...] += gate_e * y

    @pl.when(e == pl.num_programs(1) - 1)
    def _():
        z = ALPHA * x_ref[...] + g2_ref[...] * acc_scr[...]
        o_ref[...] = _layer_norm(z, lng_ref[...], lnb_ref[...])


def moe_ln(x, sc, sh, g2, gates, l, wg_bf, wu_bf, wd_bf, ln_g, ln_b, tm):
    N = x.shape[0]
    G, R, _ = sc.shape
    tpg = (N // G) // tm
    ms = pl.BlockSpec((None, R, D_MODEL), lambda i, e: (i // tpg, 0, 0))
    row = pl.BlockSpec((1, D_MODEL), lambda i, e: (0, 0))
    return pl.pallas_call(
        _moe_kernel,
        grid=(N // tm, N_EXPERTS),
        in_specs=[pl.BlockSpec((tm, D_MODEL), lambda i, e: (i, 0)), ms, ms, ms,
                  pl.BlockSpec((tm, LANES), lambda i, e: (i, 0)),
                  pl.BlockSpec((None, None, D_MODEL, D_EXPERT), lambda i, e: (l, e, 0, 0)),
                  pl.BlockSpec((None, None, D_MODEL, D_EXPERT), lambda i, e: (l, e, 0, 0)),
                  pl.BlockSpec((None, None, D_EXPERT, D_MODEL), lambda i, e: (l, e, 0, 0)),
                  row, row],
        out_specs=pl.BlockSpec((tm, D_MODEL), lambda i, e: (i, 0)),
        out_shape=jax.ShapeDtypeStruct((N, D_MODEL), F32),
        scratch_shapes=[pltpu.VMEM((tm, D_MODEL), BF16), pltpu.VMEM((tm, D_MODEL), F32)],
        compiler_params=_cparams(("parallel", "arbitrary")),
        name="moe_ln",
    )(x, sc, sh, g2, gates, wg_bf, wu_bf, wd_bf, ln_g.reshape(1, D_MODEL), ln_b.reshape(1, D_MODEL))


def _split3_bf16(x):
    hi = x.astype(BF16)
    r = x - hi.astype(F32)
    mid = r.astype(BF16)
    lo = (r - mid.astype(F32)).astype(BF16)
    return hi, mid, lo


def _moe_group_kernel(x_ref, sc_ref, sh_ref, g2_ref, gates_ref, ltri_ref, wg_ref, wu_ref, wd_ref,
                      lng_ref, lnb_ref, o_ref,
                      h_scr, acc_scr, rank_scr, rank_t_scr, sel_t_scr, xs_scr, gs_scr, yacc_scr, cnt_smem,
                      *, cap):
    g, e = pl.program_id(1), pl.program_id(2)
    tm = x_ref.shape[0]
    lane = lax.broadcasted_iota(I32, (tm, LANES), 1)

    @pl.when((g == 0) & (e == 0))
    def _():
        h_scr[...] = (x_ref[...] * (1.0 + sc_ref[...]) + sh_ref[...]).astype(BF16)
        acc_scr[...] = jnp.zeros_like(acc_scr)
        gates = gates_ref[...]
        sel = jnp.where((lane >= GROUP_LANE0) & (lane < GROUP_LANE0 + N_GROUPS), gates, 0.0)
        rank = jnp.dot(ltri_ref[...], sel.astype(BF16), preferred_element_type=F32)
        rank_scr[...] = rank
        rank_t_scr[...] = rank.T
        sel_t_scr[...] = sel.T
        for gg in range(N_GROUPS):
            cnt_smem[gg] = jnp.sum(jnp.where(lane == GROUP_LANE0 + gg, sel, 0.0)).astype(I32)

    half = cap // 2
    cnt = cnt_smem[g]
    rem = cnt % cap
    use_tail = (rem > 0) & (rem <= half)
    n_full = cnt // cap + jnp.where(rem > half, 1, 0)
    tail_start = pl.multiple_of(n_full * cap, half)
    grp_row = pl.ds(GROUP_LANE0 + g, 1)

    def for_chunks(fn):
        def body(c, carry):
            fn(pl.multiple_of(c * cap, cap), cap)
            return carry
        lax.fori_loop(0, n_full, body, 0)

        @pl.when(use_tail)
        def _():
            fn(tail_start, half)

    @pl.when(e == 0)
    def _():
        expert_lanes = jnp.where(lane < N_EXPERTS, gates_ref[...], 0.0)
        parts = _split3_bf16(expert_lanes)
        g_parts = sum(_lane_roll(part.astype(F32), j * N_EXPERTS)
                      for j, part in enumerate(parts)).astype(BF16)
        rank_row, sel_row = rank_t_scr[grp_row, :], sel_t_scr[grp_row, :]

        def dispatch(start, size):
            rows = pl.ds(start, size)
            slot = (start + lax.broadcasted_iota(I32, (size, 1), 0)).astype(F32)
            p = jnp.where((rank_row == slot) & (sel_row > 0.5), 1.0, 0.0).astype(BF16)
            xs_scr[rows, :] = jnp.dot(p, h_scr[...], preferred_element_type=F32).astype(BF16)
            gs_scr[rows, :] = jnp.dot(p, g_parts, preferred_element_type=F32)
            yacc_scr[rows, :] = jnp.zeros((size, D_MODEL), F32)

        for_chunks(dispatch)

    def expert(start, size):
        rows = pl.ds(start, size)
        xs = xs_scr[rows, :]
        a = (_silu(jnp.dot(xs, wg_ref[...], preferred_element_type=F32))
             * jnp.dot(xs, wu_ref[...], preferred_element_type=F32))
        y = jnp.dot(a.astype(BF16), wd_ref[...], preferred_element_type=F32)
        gs = gs_scr[rows, :]
        lane_c = lax.broadcasted_iota(I32, gs.shape, 1)
        is_part = ((lane_c % N_EXPERTS) == g * EXPERTS_PER_GROUP + e) & (lane_c < 3 * N_EXPERTS)
        gate = jnp.sum(jnp.where(is_part, gs, 0.0), -1, keepdims=True)
        yacc_scr[rows, :] += gate * y

    for_chunks(expert)

    @pl.when(e == EXPERTS_PER_GROUP - 1)
    def _():
        in_grp = lane == GROUP_LANE0 + g
        rank_col = jnp.sum(jnp.where(in_grp, rank_scr[...], 0.0), -1, keepdims=True)
        sel_col = jnp.sum(jnp.where(in_grp, gates_ref[...], 0.0), -1, keepdims=True)

        def combine(start, size):
            slot = (start + lax.broadcasted_iota(I32, (1, size), 1)).astype(F32)
            p_t = jnp.where((rank_col == slot) & (sel_col > 0.5), 1.0, 0.0).astype(BF16)
            y_hi, y_mid, _ = _split3_bf16(yacc_scr[pl.ds(start, size), :])
            acc_scr[...] += (jnp.dot(p_t, y_hi, preferred_element_type=F32)
                             + jnp.dot(p_t, y_mid, preferred_element_type=F32))

        for_chunks(combine)

    @pl.when((g == N_GROUPS - 1) & (e == EXPERTS_PER_GROUP - 1))
    def _():
        z = ALPHA * x_ref[...] + g2_ref[...] * acc_scr[...]
        o_ref[...] = _layer_norm(z, lng_ref[...], lnb_ref[...])


def moe_group_ln(x, sc, sh, g2, gates, l, wg_bf, wu_bf, wd_bf, ln_g, ln_b, tm, cap):
    N = x.shape[0]
    G, R, _ = sc.shape
    tpg = (N // G) // tm
    ltri = (jnp.arange(tm)[:, None] > jnp.arange(tm)[None, :]).astype(BF16)
    ms = pl.BlockSpec((None, R, D_MODEL), lambda i, g, e: (i // tpg, 0, 0))
    row = pl.BlockSpec((1, D_MODEL), lambda i, g, e: (0, 0))
    expert = lambda i, g, e: (l, g * EXPERTS_PER_GROUP + e, 0, 0)
    w_in = pl.BlockSpec((None, None, D_MODEL, D_EXPERT), expert)
    w_out = pl.BlockSpec((None, None, D_EXPERT, D_MODEL), expert)
    return pl.pallas_call(
        functools.partial(_moe_group_kernel, cap=cap),
        grid=(N // tm, N_GROUPS, EXPERTS_PER_GROUP),
        in_specs=[pl.BlockSpec((tm, D_MODEL), lambda i, g, e: (i, 0)), ms, ms, ms,
                  pl.BlockSpec((tm, LANES), lambda i, g, e: (i, 0)),
                  pl.BlockSpec((tm, tm), lambda i, g, e: (0, 0)),
                  w_in, w_in, w_out, row, row],
        out_specs=pl.BlockSpec((tm, D_MODEL), lambda i, g, e: (i, 0)),
        out_shape=jax.ShapeDtypeStruct((N, D_MODEL), F32),
        scratch_shapes=[pltpu.VMEM((tm, D_MODEL), BF16), pltpu.VMEM((tm, D_MODEL), F32),
                        pltpu.VMEM((tm, LANES), F32), pltpu.VMEM((LANES, tm), F32),
                        pltpu.VMEM((LANES, tm), F32), pltpu.VMEM((tm, D_MODEL), BF16),
                        pltpu.VMEM((tm, LANES), F32), pltpu.VMEM((tm, D_MODEL), F32),
                        pltpu.SMEM((N_GROUPS,), I32)],
        compiler_params=_cparams(("arbitrary", "arbitrary", "arbitrary")),
        name="moe_group_ln",
    )(x, sc, sh, g2, gates, ltri, wg_bf, wu_bf, wd_bf, ln_g.reshape(1, D_MODEL), ln_b.reshape(1, D_MODEL))


LOG2_E = 1.4426950408889634
Q_PRESCALE = DSA_HEAD_DIM ** -0.5 * LOG2_E
DSA_IN_PAD = -(-DSA_IN // LANES) * LANES
DSA_QI_OFF = DSA_Q + 2 * DSA_KV
DSA_KW_OFF = DSA_QI_OFF + IDX_HEADS * IDX_DIM


def _rope_tables(pos, rot_dim, period, n_periods):
    half = rot_dim // 2
    inv_freq = 1.0 / (ROPE_THETA ** (jnp.arange(0, rot_dim, 2, dtype=F32) / rot_dim))
    ang = pos.astype(F32)[:, None] * inv_freq[None, :]
    cos, sin = jnp.cos(ang), jnp.sin(ang)
    T = pos.shape[0]
    zeros = jnp.zeros((T, period - 2 * half), F32)
    zh = jnp.zeros((T, half), F32)
    a1 = jnp.concatenate([cos, cos, zeros + 1.0], 1)
    b1 = jnp.concatenate([zh, sin, zeros], 1)
    c1 = jnp.concatenate([-sin, zh, zeros], 1)
    rest = LANES - n_periods * period
    pad1 = jnp.ones((T, rest), F32)
    pad0 = jnp.zeros((T, rest), F32)
    a = jnp.concatenate([a1] * n_periods + [pad1], 1)
    b = jnp.concatenate([b1] * n_periods + [pad0], 1)
    c = jnp.concatenate([c1] * n_periods + [pad0], 1)
    return a, b, c


def _apply_rope(x, a, b, c, half):
    return x * a + _lane_roll(x, half) * b + _lane_roll(x, LANES - half) * c


N_ROPE_TABLES = 9


def _dsa_sections(x_ref, sc_ref, sh_ref, w_ref, tab_refs):
    hb = (x_ref[...] * (1.0 + sc_ref[...]) + sh_ref[...]).astype(BF16)
    ma, mb, mc, ia, ib, ic, ka, kb, kc = (t[...] for t in tab_refs)

    def sec(off, width):
        return jnp.dot(hb, w_ref[:, off:off + width], preferred_element_type=F32)

    def lanes(x, h):
        return x[:, h * LANES:(h + 1) * LANES]

    q = sec(0, DSA_Q)
    q_heads = [_apply_rope(lanes(q, h), ma, mb, mc, ROT_DIM // 2) * Q_PRESCALE for h in range(DSA_HEADS)]
    k = sec(DSA_Q, DSA_KV)
    k = jnp.concatenate([_apply_rope(lanes(k, h), ma, mb, mc, ROT_DIM // 2) for h in range(DSA_KV_HEADS)], 1)
    v = sec(DSA_Q + DSA_KV, DSA_KV)
    qi = sec(DSA_QI_OFF, IDX_HEADS * IDX_DIM)
    qi_pairs = [_apply_rope(lanes(qi, h), ia, ib, ic, IDX_ROT_DIM // 2)
                for h in range(IDX_HEADS * IDX_DIM // LANES)]
    kw = _apply_rope(sec(DSA_KW_OFF, LANES), ka, kb, kc, IDX_ROT_DIM // 2)
    return q_heads, k, v, qi_pairs, kw


def _dsa_proj_kernel(x_ref, sc_ref, sh_ref, w_ref, *rest):
    tabs = rest[:N_ROPE_TABLES]
    q_ref, k_ref, v_ref, qi_ref, kw_ref = rest[N_ROPE_TABLES:]
    q_heads, k, v, qi_pairs, kw = _dsa_sections(x_ref, sc_ref, sh_ref, w_ref, tabs)
    for h, qh in enumerate(q_heads):
        q_ref[:, h * LANES:(h + 1) * LANES] = qh.astype(q_ref.dtype)
    for h, qp in enumerate(qi_pairs):
        qi_ref[:, h * LANES:(h + 1) * LANES] = qp.astype(qi_ref.dtype)
    k_ref[...] = k
    v_ref[...] = v
    kw_ref[...] = kw


def _dsa_proj_prompt_kernel(x_ref, sc_ref, sh_ref, w_ref, *rest, tq):
    tabs = rest[:N_ROPE_TABLES]
    k_ref, v_ref, kw_ref, kbf_ref, kwbf_ref, vt_ref, qt_ref, qit_ref, wit_ref = rest[N_ROPE_TABLES:]
    q_heads, k, v, qi_pairs, kw = _dsa_sections(x_ref, sc_ref, sh_ref, w_ref, tabs)
    tm = k.shape[0]
    k_ref[...] = k
    kbf_ref[...] = k.astype(BF16)
    v_ref[...] = v
    kw_ref[...] = kw
    kwbf_ref[...] = kw.astype(BF16)
    for n in range(DSA_KV_HEADS):
        vt_ref[n, 0] = v[:, n * DSA_HEAD_DIM:(n + 1) * DSA_HEAD_DIM].T.astype(BF16)
    heads_per_pair = LANES // IDX_DIM
    for j in range(tm // tq):
        rows = slice(j * tq, (j + 1) * tq)
        for h, qh in enumerate(q_heads):
            n, g = divmod(h, DSA_GROUP)
            qt_ref[j, n, :, g * tq:(g + 1) * tq] = qh[rows].T.astype(BF16)
        for hp, qp in enumerate(qi_pairs):
            t = qp[rows].T.astype(BF16)
            for s in range(heads_per_pair):
                h = hp * heads_per_pair + s
                qit_ref[j, :IDX_DIM, h * tq:(h + 1) * tq] = t[s * IDX_DIM:(s + 1) * IDX_DIM]
        qit_ref[j, IDX_DIM:, :] = jnp.zeros((LANES - IDX_DIM, IDX_HEADS * tq), BF16)
        wit_ref[j] = kw[rows].T[IDX_DIM:IDX_DIM + IDX_HEADS]


def _dsa_project_call(body, x, sc, sh, w_pad_bf, pos, tm, out_specs, out_shape, name):
    N = x.shape[0]
    G, R, _ = sc.shape
    tpg = (N // G) // tm
    n_tab = pos.shape[0] // tm
    tabs = (_rope_tables(pos, ROT_DIM, DSA_HEAD_DIM, 1)
            + _rope_tables(pos, IDX_ROT_DIM, IDX_DIM, 2)
            + _rope_tables(pos, IDX_ROT_DIM, IDX_DIM, 1))
    tab = pl.BlockSpec((tm, LANES), lambda i: (i % n_tab, 0))
    return pl.pallas_call(
        body,
        grid=(N // tm,),
        in_specs=[pl.BlockSpec((tm, D_MODEL), lambda i: (i, 0)), _mod_spec(R, tpg), _mod_spec(R, tpg),
                  pl.BlockSpec((D_MODEL, DSA_IN_PAD), lambda i: (0, 0))] + [tab] * N_ROPE_TABLES,
        out_specs=out_specs, out_shape=out_shape,
        compiler_params=_cparams(("parallel",)),
        name=name,
    )(x, sc, sh, w_pad_bf, *tabs)


def dsa_project(x, sc, sh, w_pad_bf, pos, tm):
    N = x.shape[0]
    out = lambda w: pl.BlockSpec((tm, w), lambda i: (i, 0))
    return _dsa_project_call(
        _dsa_proj_kernel, x, sc, sh, w_pad_bf, pos, tm,
        [out(DSA_Q), out(DSA_KV), out(DSA_KV), out(IDX_HEADS * IDX_DIM), out(LANES)],
        [jax.ShapeDtypeStruct((N, DSA_Q), BF16), jax.ShapeDtypeStruct((N, DSA_KV), F32),
         jax.ShapeDtypeStruct((N, DSA_KV), F32), jax.ShapeDtypeStruct((N, IDX_HEADS * IDX_DIM), BF16),
         jax.ShapeDtypeStruct((N, LANES), F32)], "dsa_project")


def dsa_project_prompt(x, sc, sh, w_pad_bf, pos, B, T, tm, tq):
    N = B * T
    tiles, n_qt = T // tm, tm // tq
    gq = DSA_GROUP * tq
    out = lambda w: pl.BlockSpec((tm, w), lambda i: (i, 0))
    by_tile = lambda *shape: pl.BlockSpec((None,) + shape,
                                          lambda i: (i // tiles, i % tiles) + (0,) * (len(shape) - 1))
    return _dsa_project_call(
        functools.partial(_dsa_proj_prompt_kernel, tq=tq), x, sc, sh, w_pad_bf, pos, tm,
        [out(DSA_KV), out(DSA_KV), out(LANES), out(DSA_KV), out(LANES),
         pl.BlockSpec((None, DSA_KV_HEADS, 1, DSA_HEAD_DIM, tm), lambda i: (i // tiles, 0, i % tiles, 0, 0)),
         by_tile(n_qt, DSA_KV_HEADS, DSA_HEAD_DIM, gq),
         by_tile(n_qt, LANES, IDX_HEADS * tq),
         by_tile(n_qt, IDX_HEADS, tq)],
        [jax.ShapeDtypeStruct((N, DSA_KV), F32), jax.ShapeDtypeStruct((N, DSA_KV), F32),
         jax.ShapeDtypeStruct((N, LANES), F32), jax.ShapeDtypeStruct((N, DSA_KV), BF16),
         jax.ShapeDtypeStruct((N, LANES), BF16),
         jax.ShapeDtypeStruct((B, DSA_KV_HEADS, tiles, DSA_HEAD_DIM, tm), BF16),
         jax.ShapeDtypeStruct((B, T // tq, DSA_KV_HEADS, DSA_HEAD_DIM, gq), BF16),
         jax.ShapeDtypeStruct((B, T // tq, LANES, IDX_HEADS * tq), BF16),
         jax.ShapeDtypeStruct((B, T // tq, IDX_HEADS, tq), F32)], "dsa_project_prompt")


CODE_NEG_INF = -2139095041
BISECT_FIRST = 22
BISECT_STAGE = 2


def _threshold_of_code(code):
    bits = code ^ ((code >> 31) & jnp.int32(0x7FFFFFFF))
    return jnp.where(code < CODE_NEG_INF, -jnp.inf, pltpu.bitcast(bits, F32))


def _reduce_row_groups(x, op, n_chains=8):
    parts = [x[r:r + SUBLANES] for r in range(0, x.shape[0], SUBLANES)]
    accs = parts[:n_chains]
    for j, part in enumerate(parts[n_chains:]):
        accs[j % len(accs)] = op(accs[j % len(accs)], part)
    while len(accs) > 1:
        accs = [op(a, b) for a, b in zip(accs[0::2], accs[1::2])] + (accs[-1:] if len(accs) % 2 else [])
    return accs[0]


def _sum_row_groups(x):
    return _reduce_row_groups(x, jnp.add)


def _max_row_groups(x):
    return _reduce_row_groups(x, jnp.maximum)


def _kth_largest_threshold(count_ge, shape, k):
    def body(s, p):
        cand = p + lax.shift_left(jnp.int32(1), 31 - s)
        return jnp.where(count_ge(_threshold_of_code(cand)) >= k, cand, p)
    return _threshold_of_code(lax.fori_loop(0, 32, body, jnp.full(shape, INT_MIN, I32)))


def _tie_index_bound(count_eq_below, shape, need, n_bits):
    def body(s, m):
        cand = m + lax.shift_left(jnp.int32(1), n_bits - 1 - s)
        return jnp.where(count_eq_below(cand) < need, cand, m)
    return lax.fori_loop(0, n_bits, body, jnp.zeros(shape, I32))


def _dsa_prompt_kernel(ki_ref, qit_ref, wit_ref, k_ref, vt_ref, qt_ref, o_ref,
                       score_scr, bias_scr, midx_scr, logit_scr, thr_scr, nge_scr,
                       *, tq, kb_size, cb_size, topk, idx_bits):
    i = pl.program_id(1)
    n_kb = ((i + 1) * tq + kb_size - 1) // kb_size
    n_cb = ((i + 1) * tq + cb_size - 1) // cb_size
    q_pos = i * tq + lax.broadcasted_iota(I32, (1, tq), 1)
    heads_per_dot = 2

    @pl.when(i == 0)
    def _():
        score_scr[...] = jnp.full(score_scr.shape, -jnp.inf, F32)

    def key_rows(kb):
        return pl.ds(pl.multiple_of(kb * kb_size, kb_size), kb_size)

    def l_index(kb, size=kb_size):
        return kb * size + lax.broadcasted_iota(I32, (size, 1), 0)

    def score_body(kb, carry):
        kib = ki_ref[key_rows(kb), :]
        acc = jnp.zeros((kb_size, tq), F32)
        for hp in range(IDX_HEADS // heads_per_dot):
            s = jnp.dot(kib, qit_ref[:, hp * heads_per_dot * tq:(hp + 1) * heads_per_dot * tq],
                        preferred_element_type=F32)
            for j in range(heads_per_dot):
                h = hp * heads_per_dot + j
                w = wit_ref[h:h + 1, :] * (IDX_HEADS ** -0.5 * IDX_DIM ** -0.5)
                acc = acc + jnp.maximum(s[:, j * tq:(j + 1) * tq], 0.0) * w
        allowed = l_index(kb) <= q_pos
        score_scr[key_rows(kb), :] = jnp.where(allowed, acc, -jnp.inf)
        return carry

    lax.fori_loop(0, n_kb, score_body, 0)

    def count(pred_fn):
        def body(cb, acc):
            rows = pl.ds(pl.multiple_of(cb * cb_size, cb_size), cb_size)
            m = jnp.where(pred_fn(score_scr[rows, :], cb), 1.0, 0.0)
            return acc + _sum_row_groups(m)
        acc = lax.fori_loop(0, n_cb, body, jnp.zeros((SUBLANES, tq), F32))
        return acc.sum(axis=0, keepdims=True)

    k_f = float(topk)

    def bisect(s_lo, s_hi):
        def body(s, carry):
            p, n_p = carry
            cand = p + lax.shift_left(jnp.int32(1), 31 - s)
            cand_thr = _threshold_of_code(cand)
            n_c = count(lambda score, cb: score >= cand_thr)
            take = n_c >= k_f
            return jnp.where(take, cand, p), jnp.where(take, n_c, n_p)
        p, n_p = lax.fori_loop(s_lo, s_hi, body, (thr_scr[...], nge_scr[...]))
        thr_scr[...] = p
        nge_scr[...] = n_p

    thr_scr[...] = jnp.full((1, tq), INT_MIN, I32)
    nge_scr[...] = jnp.full((1, tq), float(2 ** 30), F32)
    bisect(0, BISECT_FIRST)
    for s0 in range(BISECT_FIRST, 32, BISECT_STAGE):
        @pl.when(jnp.max(jnp.where(nge_scr[...] != k_f, 1.0, 0.0)) > 0.5)
        def _():
            bisect(s0, min(32, s0 + BISECT_STAGE))

    thr = _threshold_of_code(thr_scr[...])
    n_ge = nge_scr[...]
    n_gt = count(lambda score, cb: score > thr)
    need = k_f - n_gt
    excess = (n_ge - n_gt > need) & (thr > -jnp.inf)
    midx_scr[...] = jnp.full((1, tq), 2 ** idx_bits, I32)

    @pl.when(jnp.max(jnp.where(excess, 1.0, 0.0)) > 0.5)
    def _():
        m = _tie_index_bound(
            lambda c: count(lambda score, cb: (score == thr) & (l_index(cb, cb_size) < c)),
            (1, tq), need, idx_bits)
        midx_scr[...] = jnp.where(excess, m, 2 ** idx_bits)

    midx = midx_scr[...]

    def bias_body(kb, carry):
        score = score_scr[key_rows(kb), :]
        l = l_index(kb)
        sel = ((score > thr) | ((score == thr) & (l <= midx))) & (l <= q_pos)
        bias_scr[key_rows(kb), :] = jnp.where(sel, 0.0, NEG_BIG)
        return carry

    lax.fori_loop(0, n_kb, bias_body, 0)

    gq = DSA_GROUP * tq
    heads = range(DSA_KV_HEADS)

    def logits_of(kb):
        bias = bias_scr[key_rows(kb), :]
        bias = jnp.concatenate([bias] * DSA_GROUP, axis=1)
        blk_max = []
        for n in heads:
            kblk = k_ref[key_rows(kb), n * DSA_HEAD_DIM:(n + 1) * DSA_HEAD_DIM]
            logits = jnp.dot(kblk, qt_ref[n], preferred_element_type=F32) + bias
            logit_scr[n, key_rows(kb), :] = logits
            blk_max.append(_max_row_groups(logits))
        return tuple(blk_max)

    def absorb(kb, blk_max, state):
        new = []
        for n in heads:
            m_run, l_run, acc = state[n]
            m_new = jnp.maximum(m_run, jnp.max(blk_max[n], axis=0, keepdims=True))
            alpha = jnp.exp2(m_run - m_new)
            p = jnp.exp2(logit_scr[n, key_rows(kb), :] - m_new)
            acc = acc * alpha + jnp.dot(vt_ref[n, kb], p.astype(BF16), preferred_element_type=F32)
            new.append((m_new, l_run * alpha + _sum_row_groups(p), acc))
        return tuple(new)

    def att_body(kb, carry):
        blk_max, state = carry
        state = absorb(kb - 1, blk_max, state)
        return logits_of(kb), state

    state0 = tuple((jnp.full((1, gq), NEG_BIG, F32), jnp.zeros((SUBLANES, gq), F32),
                    jnp.zeros((DSA_HEAD_DIM, gq), F32)) for _ in heads)
    last_max, state = lax.fori_loop(1, n_kb, att_body, (logits_of(0), state0))
    fin = absorb(n_kb - 1, last_max, state)
    for n in heads:
        _, l8, acc = fin[n]
        o_t = acc / jnp.sum(l8, axis=0, keepdims=True)
        for g in range(DSA_GROUP):
            h = n * DSA_GROUP + g
            o_ref[:, h * DSA_HEAD_DIM:(h + 1) * DSA_HEAD_DIM] = (
                o_t[:, g * tq:(g + 1) * tq].T.astype(o_ref.dtype))


def dsa_prompt_attend(kw_bf, qi_t, wi_t, k_bf, v_t, q_t, B, T, topk, tq, kb_size):
    nqt = T // tq
    nkb = T // kb_size
    gq = DSA_GROUP * tq
    return pl.pallas_call(
        functools.partial(_dsa_prompt_kernel, tq=tq, kb_size=kb_size, cb_size=min(512, T), topk=topk,
                          idx_bits=max(1, (T - 1).bit_length())),
        grid=(B, nqt),
        in_specs=[pl.BlockSpec((None, T, LANES), lambda b, i: (b, 0, 0)),
                  pl.BlockSpec((None, None, LANES, IDX_HEADS * tq), lambda b, i: (b, i, 0, 0)),
                  pl.BlockSpec((None, None, IDX_HEADS, tq), lambda b, i: (b, i, 0, 0)),
                  pl.BlockSpec((None, T, DSA_KV), lambda b, i: (b, 0, 0)),
                  pl.BlockSpec((None, DSA_KV_HEADS, nkb, DSA_HEAD_DIM, kb_size),
                               lambda b, i: (b, 0, 0, 0, 0)),
                  pl.BlockSpec((None, None, DSA_KV_HEADS, DSA_HEAD_DIM, gq),
                               lambda b, i: (b, i, 0, 0, 0))],
        out_specs=pl.BlockSpec((tq, DSA_Q), lambda b, i: (b * nqt + i, 0)),
        out_shape=jax.ShapeDtypeStruct((B * T, DSA_Q), BF16),
        scratch_shapes=[pltpu.VMEM((T, tq), F32), pltpu.VMEM((T, tq), F32), pltpu.VMEM((1, tq), I32),
                        pltpu.VMEM((DSA_KV_HEADS, T, gq), F32),
                        pltpu.VMEM((1, tq), I32), pltpu.VMEM((1, tq), F32)],
        compiler_params=_cparams(("arbitrary", "arbitrary")),
        name="dsa_prompt_attend",
    )(kw_bf.reshape(B, T, LANES), qi_t, wi_t, k_bf.reshape(B, T, DSA_KV), v_t, q_t)


def _dsa_sample_score_kernel(pt_ref, qi_ref, wi_ref, kis_ref, kidx_hbm, o_ref, buf, sem, *, n_pages, page):
    slot = _fetch_pages_ahead(pt_ref, (kidx_hbm,), (buf,), sem, n_pages)
    pages = [buf.at[slot, j] for j in range(n_pages)]
    qi = qi_ref[...]
    w = wi_ref[...] * (IDX_HEADS ** -0.5 * IDX_DIM ** -0.5)
    for j in range(n_pages):
        dots = jnp.dot(qi, pages[j][...].astype(BF16), preferred_element_type=F32)
        o_ref[:, j * page:(j + 1) * page] = jnp.sum(jnp.maximum(dots, 0.0) * w, axis=0, keepdims=True)
    ki_self = kis_ref[...].astype(BF16).astype(F32)
    d_self = jnp.sum(qi.astype(F32) * ki_self, -1, keepdims=True)
    s_self = jnp.sum(jnp.maximum(d_self, 0.0) * w, axis=0, keepdims=True)
    lane = lax.broadcasted_iota(I32, (1, LANES), 1)
    o_ref[:, n_pages * page:] = jnp.where(lane == 0, s_self, -jnp.inf)


def dsa_sample_scores(page_table, qi_bf, wi, ki_self, cache_kidx_t):
    DB, n_pages = page_table.shape
    page = cache_kidx_t.shape[2]
    lp = n_pages * page + LANES
    per_b = lambda *shape: pl.BlockSpec((None,) + shape, lambda b, pt: (b,) + (0,) * len(shape))
    return pl.pallas_call(
        functools.partial(_dsa_sample_score_kernel, n_pages=n_pages, page=page),
        grid_spec=pltpu.PrefetchScalarGridSpec(
            num_scalar_prefetch=1, grid=(DB,),
            in_specs=[per_b(IDX_HEADS, IDX_DIM), per_b(IDX_HEADS, 1), per_b(1, IDX_DIM),
                      pl.BlockSpec(memory_space=pl.ANY)],
            out_specs=per_b(1, lp),
            scratch_shapes=[pltpu.VMEM((SCORE_PAGE_SLOTS, n_pages, IDX_DIM, page), F32),
                            pltpu.SemaphoreType.DMA((SCORE_PAGE_SLOTS,))]),
        out_shape=jax.ShapeDtypeStruct((DB, 1, lp), F32),
        compiler_params=_cparams(("arbitrary",)),
        name="dsa_sample_scores",
    )(page_table, qi_bf.reshape(DB, IDX_HEADS, IDX_DIM), wi.reshape(DB, IDX_HEADS, 1),
      ki_self.reshape(DB, 1, IDX_DIM), cache_kidx_t)


def _dsa_sample_select_kernel(s_ref, o_ref, *, n_keys, topk, idx_bits):
    score = s_ref[...]
    lane = lax.broadcasted_iota(I32, score.shape, 1)
    valid = lane < n_keys
    score = jnp.where(valid, score, -jnp.inf)

    def count(pred):
        return jnp.sum(jnp.where(pred, 1.0, 0.0), -1, keepdims=True)

    shape = (score.shape[0], 1)
    thr = _kth_largest_threshold(lambda t: count(score >= t), shape, float(topk))
    n_gt = count(score > thr)
    need = float(topk) - n_gt
    excess = (count(score == thr) > need) & (thr > -jnp.inf)
    m = _tie_index_bound(lambda c: count((score == thr) & (lane < c)), shape, need, idx_bits)
    midx = jnp.where(excess, m, 2 ** idx_bits)
    sel = ((score > thr) | ((score == thr) & (lane <= midx))) & valid
    o_ref[...] = jnp.where(sel, 0.0, NEG_BIG)


def dsa_sample_select(score, n_keys, topk):
    DB, lp = score.shape
    return pl.pallas_call(
        functools.partial(_dsa_sample_select_kernel, n_keys=n_keys, topk=topk,
                          idx_bits=max(1, (lp - 1).bit_length())),
        grid=(1,),
        in_specs=[pl.BlockSpec((DB, lp), lambda i: (0, 0))],
        out_specs=pl.BlockSpec((DB, lp), lambda i: (0, 0)),
        out_shape=jax.ShapeDtypeStruct((DB, lp), F32),
        compiler_params=_cparams(("arbitrary",)),
        name="dsa_sample_select",
    )(score)


SCORE_PAGE_SLOTS = 4
ATTEND_PAGE_SLOTS = 3


def _page_copies(pt_ref, b, slot, hbm_refs, bufs, sem, n_pages):
    return [pltpu.make_async_copy(hbm.at[pt_ref[b, j]], buf.at[slot, j], sem.at[slot])
            for hbm, buf in zip(hbm_refs, bufs) for j in range(n_pages)]


def _fetch_pages_ahead(pt_ref, hbm_refs, bufs, sem, n_pages):
    n_slots = bufs[0].shape[0]
    ahead = n_slots - 1
    b = pl.program_id(0)
    n_steps = pl.num_programs(0)

    @pl.when(b == 0)
    def _():
        for s in range(ahead):
            @pl.when(s < n_steps)
            def _():
                for c in _page_copies(pt_ref, s, s, hbm_refs, bufs, sem, n_pages):
                    c.start()

    @pl.when(b + ahead < n_steps)
    def _():
        for c in _page_copies(pt_ref, b + ahead, (b + ahead) % n_slots, hbm_refs, bufs, sem, n_pages):
            c.start()

    slot = b % n_slots
    for c in _page_copies(pt_ref, b, slot, hbm_refs, bufs, sem, n_pages):
        c.wait()
    return slot


def _dsa_sample_attend_kernel(pt_ref, q_ref, bias_ref, ks_ref, vs_ref, k_hbm, v_hbm, o_ref,
                              kbuf, vbuf, sem, logit_scr, *, n_pages, page):
    slot = _fetch_pages_ahead(pt_ref, (k_hbm, v_hbm), (kbuf, vbuf), sem, n_pages)
    k_pages = [kbuf.at[slot, j] for j in range(n_pages)]
    v_pages = [vbuf.at[slot, j] for j in range(n_pages)]
    q = q_ref[...]
    tail = n_pages * page
    lane = lax.broadcasted_iota(I32, (DSA_GROUP, LANES), 1)
    for n in range(DSA_KV_HEADS):
        hs = slice(n * DSA_GROUP, (n + 1) * DSA_GROUP)
        ds = slice(n * DSA_HEAD_DIM, (n + 1) * DSA_HEAD_DIM)
        head_rows = pl.ds(n, page, stride=DSA_KV_HEADS)
        qn = q[hs, :]
        for j in range(n_pages):
            lg = lax.dot_general(qn, k_pages[j][head_rows, :].astype(BF16), (((1,), (1,)), ((), ())),
                                 preferred_element_type=F32)
            logit_scr[hs, j * page:(j + 1) * page] = lg + bias_ref[:, j * page:(j + 1) * page]
        k_self = ks_ref[:, ds].astype(BF16).astype(F32)
        lg_self = jnp.sum(qn.astype(F32) * k_self, -1, keepdims=True)
        logit_scr[hs, tail:] = jnp.where(lane == 0, lg_self, 0.0) + bias_ref[:, tail:]
        logits = logit_scr[hs, :]
        m = jnp.max(logits, -1, keepdims=True)
        p = jnp.exp2(logits - m)
        p_bf = (p / jnp.sum(p, -1, keepdims=True)).astype(BF16)
        v_self = vs_ref[:, ds].astype(BF16).astype(F32)
        acc = p_bf[:, tail:].astype(F32)[:, :1] * v_self
        for j in range(n_pages):
            acc = acc + jnp.dot(p_bf[:, j * page:(j + 1) * page], v_pages[j][head_rows, :].astype(BF16),
                                preferred_element_type=F32)
        o_ref[hs, :] = acc


def dsa_sample_attend(page_table, q_bf, bias, k_self, v_self, cache_k, cache_v):
    DB, n_pages = page_table.shape
    rows = cache_k.shape[1]
    page = rows // DSA_KV_HEADS
    lp = n_pages * page + LANES
    per_b = lambda *shape: pl.BlockSpec((None,) + shape, lambda b, pt: (b,) + (0,) * len(shape))
    o = pl.pallas_call(
        functools.partial(_dsa_sample_attend_kernel, n_pages=n_pages, page=page),
        grid_spec=pltpu.PrefetchScalarGridSpec(
            num_scalar_prefetch=1, grid=(DB,),
            in_specs=[per_b(DSA_HEADS, DSA_HEAD_DIM), per_b(1, lp), per_b(1, DSA_KV), per_b(1, DSA_KV),
                      pl.BlockSpec(memory_space=pl.ANY), pl.BlockSpec(memory_space=pl.ANY)],
            out_specs=per_b(DSA_HEADS, DSA_HEAD_DIM),
            scratch_shapes=[pltpu.VMEM((ATTEND_PAGE_SLOTS, n_pages, rows, DSA_HEAD_DIM), F32),
                            pltpu.VMEM((ATTEND_PAGE_SLOTS, n_pages, rows, DSA_HEAD_DIM), F32),
                            pltpu.SemaphoreType.DMA((ATTEND_PAGE_SLOTS,)),
                            pltpu.VMEM((DSA_HEADS, lp), F32)]),
        out_shape=jax.ShapeDtypeStruct((DB, DSA_HEADS, DSA_HEAD_DIM), F32),
        compiler_params=_cparams(("arbitrary",)),
        name="dsa_sample_attend",
    )(page_table, q_bf.reshape(DB, DSA_HEADS, DSA_HEAD_DIM), bias.reshape(DB, 1, lp),
      k_self.reshape(DB, 1, DSA_KV), v_self.reshape(DB, 1, DSA_KV), cache_k, cache_v)
    return o.reshape(DB, DSA_Q)


def _retnet_tables(pos):
    inv_freq = jnp.power(RET_ANGLE_BASE, -jnp.linspace(0.0, 1.0, RET_DK // 2, dtype=F32))
    ang = pos.astype(F32)[:, None] * inv_freq[None, :]
    return jnp.cos(ang), jnp.sin(ang)


def _split_mod(mod_l, G, R):
    return [m.reshape(G, R, D_MODEL) for m in jnp.split(mod_l, 6, axis=-1)]


MOE_GROUP_TILE = 1024
MOE_GROUP_CAP = 256


def _finish_layer(a, w_out_bf, x, g1, sc2, sh2, g2, l, wts, tm):
    x, gates = out_proj_ln_route(a, w_out_bf, x, g1, wts["ln1_g"][l], wts["ln1_b"][l], sc2, sh2,
                                 wts["router"], tm)
    args = (x, sc2, sh2, g2, gates, l, wts["w_gate"], wts["w_up"], wts["w_down"],
            wts["ln2_g"][l], wts["ln2_b"][l])
    rows_per_mod = x.shape[0] // sc2.shape[0]
    if sc2.shape[1] == 1 and rows_per_mod >= 2 * MOE_GROUP_CAP:
        tile = min(MOE_GROUP_TILE, rows_per_mod)
        return moe_group_ln(*args, tile, MOE_GROUP_CAP)
    return moe_ln(*args, tm)


def kernel(x_prompt, x_sample, state_ret, cache_k, cache_v, cache_kidx, page_table,
           c_prompt, c_sample, w_mod, b_mod, ln1_g, ln1_b, ln2_g, ln2_b,
           w_in_ret, gn_ret_g, w_out_ret, w_in_dsa, w_out_dsa,
           w_router, b_router, w_gate, w_up, w_down):
    B, T, _ = x_prompt.shape
    DB = x_sample.shape[0]
    assert x_sample.shape[1] == 1
    n_pages = page_table.shape[1]
    page = cache_k.shape[2]
    past = n_pages * page
    n_pool = cache_k.shape[1]

    wts = dict(router=router_weights(w_router, b_router), ln1_g=ln1_g, ln1_b=ln1_b, ln2_g=ln2_g, ln2_b=ln2_b,
               w_gate=w_gate.astype(BF16), w_up=w_up.astype(BF16), w_down=w_down.astype(BF16))
    w_mod_bf = w_mod.astype(BF16)
    w_in_ret_bf = w_in_ret[0].astype(BF16)
    w_out_ret_bf = w_out_ret[0].astype(BF16)
    w_in_dsa_bf = jnp.pad(w_in_dsa[0], ((0, 0), (0, DSA_IN_PAD - DSA_IN))).astype(BF16)
    w_out_dsa_bf = w_out_dsa[0].astype(BF16)

    pos_p = jnp.arange(T, dtype=I32)
    pos_s = jnp.full((1,), past, I32)

    tm_p = min(512, T)
    tq = min(128, T)
    xp = x_prompt.reshape(B * T, D_MODEL)
    mod_p = modulation(c_prompt, w_mod_bf, b_mod)
    sh1, sc1, g1, sh2, sc2, g2 = _split_mod(mod_p[0], B, 1)
    proj = mod_proj(xp, sc1, sh1, w_in_ret_bf, min(1024, T), 1536, BF16)
    cos_p, sin_p = _retnet_tables(pos_p)
    gated, ret_p = retention_prompt(proj.reshape(B, T, RET_IN), gn_ret_g[0], cos_p, sin_p, min(256, T))
    xp = _finish_layer(gated.reshape(B * T, RET_V), w_out_ret_bf, xp, g1, sc2, sh2, g2, 0, wts, tm_p)

    sh1, sc1, g1, sh2, sc2, g2 = _split_mod(mod_p[1], B, 1)
    k_p, v_p, kw_p, k_bf, kw_bf, v_t, q_t, qi_t, wi_t = dsa_project_prompt(
        xp, sc1, sh1, w_in_dsa_bf, pos_p, B, T, tm_p, tq)
    o_p = dsa_prompt_attend(kw_bf, qi_t, wi_t, k_bf, v_t, q_t, B, T, min(DSA_TOPK, T // 4), tq, tm_p)
    xp = _finish_layer(o_p, w_out_dsa_bf, xp, g1, sc2, sh2, g2, 1, wts, tm_p)

    xs = x_sample.reshape(DB, D_MODEL)
    mod_s = modulation(c_sample, w_mod_bf, b_mod)
    sh1, sc1, g1, sh2, sc2, g2 = _split_mod(mod_s[0], 1, DB)
    proj_s = mod_proj(xs, sc1, sh1, w_in_ret_bf, DB, 1536, F32)
    cos_s, sin_s = _retnet_tables(pos_s)
    gated_s, ret_s = retention_sample(proj_s, state_ret[0], gn_ret_g[0], cos_s, sin_s)
    xs = _finish_layer(gated_s.reshape(DB, RET_V), w_out_ret_bf, xs, g1, sc2, sh2, g2, 0, wts, DB)

    sh1, sc1, g1, sh2, sc2, g2 = _split_mod(mod_s[1], 1, DB)
    pos_rows = jnp.full((DB,), past, I32)
    q_s, k_s, v_s, qi_s, kw_s = dsa_project(xs, sc1, sh1, w_in_dsa_bf, pos_rows, DB)
    ki_s = kw_s[:, :IDX_DIM]
    score = dsa_sample_scores(page_table, qi_s, kw_s[:, IDX_DIM:IDX_DIM + IDX_HEADS], ki_s,
                              jnp.swapaxes(cache_kidx[0], 1, 2))
    bias = dsa_sample_select(score.reshape(DB, past + LANES), past + 1, min(DSA_TOPK, (past + 1) // 4))
    o_s = dsa_sample_attend(page_table, q_s, bias, k_s, v_s,
                            cache_k[0].reshape(n_pool, page * DSA_KV_HEADS, DSA_HEAD_DIM),
                            cache_v[0].reshape(n_pool, page * DSA_KV_HEADS, DSA_HEAD_DIM))
    xs = _finish_layer(o_s, w_out_dsa_bf, xs, g1, sc2, sh2, g2, 1, wts, DB)

    kv_shape = (DSA_KV_HEADS, DSA_HEAD_DIM)
    return (xp.reshape(B, T, D_MODEL), xs.reshape(DB, 1, D_MODEL),
            ret_p[None], ret_s[None],
            k_p.reshape(1, B, T, *kv_shape), v_p.reshape(1, B, T, *kv_shape),
            kw_p[:, :IDX_DIM].reshape(1, B, T, IDX_DIM),
            k_s.reshape(1, DB, 1, *kv_shape), v_s.reshape(1, DB, 1, *kv_shape),
            ki_s.reshape(1, DB, 1, IDX_DIM))
```

```python
import functools

import jax
import jax.numpy as jnp
from jax import lax
from jax.experimental import pallas as pl
from jax.experimental.pallas import tpu as pltpu

F32 = jnp.float32
BF16 = jnp.bfloat16
I32 = jnp.int32

D_MODEL = 1024
DEPTH = 2
ALPHA = (2.0 * DEPTH) ** 0.25
LN_EPS = 1e-5
GN_EPS = 1e-6

RET_HEADS = 4
RET_DK = 256
RET_DV = 512
RET_QK = RET_HEADS * RET_DK
RET_V = RET_HEADS * RET_DV
RET_IN = 2 * RET_QK + 2 * RET_V
RET_ANGLE_BASE = 10000.0

DSA_HEADS = 8
DSA_KV_HEADS = 2
DSA_HEAD_DIM = 128
DSA_GROUP = DSA_HEADS // DSA_KV_HEADS
DSA_Q = DSA_HEADS * DSA_HEAD_DIM
DSA_KV = DSA_KV_HEADS * DSA_HEAD_DIM
IDX_HEADS = 16
IDX_DIM = 64
DSA_TOPK = 256
ROPE_THETA = 500000.0
ROT_DIM = DSA_HEAD_DIM // 4
IDX_ROT_DIM = IDX_DIM // 4
DSA_IN = DSA_Q + 2 * DSA_KV + IDX_HEADS * IDX_DIM + IDX_DIM + IDX_HEADS

N_EXPERTS = 16
N_GROUPS = 4
EXPERTS_PER_GROUP = N_EXPERTS // N_GROUPS
D_EXPERT = 512
GROUP_LANE0 = N_EXPERTS

LANES = 128
SUBLANES = 8
VMEM_LIMIT = 56 * 1024 * 1024
NEG_BIG = -1e30
INT_MIN = -2147483648


def _cparams(sem):
    return pltpu.CompilerParams(dimension_semantics=sem, vmem_limit_bytes=VMEM_LIMIT)


def _silu(x):
    return x * (1.0 / (1.0 + jnp.exp(-x)))


def _layer_norm(z, g, b):
    mu = jnp.mean(z, -1, keepdims=True)
    d = z - mu
    var = jnp.mean(d * d, -1, keepdims=True)
    return d * lax.rsqrt(var + LN_EPS) * g + b


def _mod_kernel(c_ref, w_ref, b_ref, o_ref):
    a = _silu(c_ref[...]).astype(BF16)
    o_ref[...] = jnp.dot(a, w_ref[...], preferred_element_type=F32) + b_ref[...]


def modulation(c, w_mod_bf, b_mod):
    R = c.shape[0]
    tn = 1536
    return pl.pallas_call(
        _mod_kernel,
        grid=(DEPTH, 6 * D_MODEL // tn),
        in_specs=[pl.BlockSpec((R, D_MODEL), lambda l, j: (0, 0)),
                  pl.BlockSpec((None, D_MODEL, tn), lambda l, j: (l, 0, j)),
                  pl.BlockSpec((None, 1, tn), lambda l, j: (l, 0, j))],
        out_specs=pl.BlockSpec((None, R, tn), lambda l, j: (l, 0, j)),
        out_shape=jax.ShapeDtypeStruct((DEPTH, R, 6 * D_MODEL), F32),
        compiler_params=_cparams(("parallel", "parallel")),
        name="modulation",
    )(c, w_mod_bf, b_mod.reshape(DEPTH, 1, 6 * D_MODEL))


def _proj_kernel(x_ref, sc_ref, sh_ref, w_ref, o_ref, h_scr):
    @pl.when(pl.program_id(1) == 0)
    def _():
        h_scr[...] = (x_ref[...] * (1.0 + sc_ref[...]) + sh_ref[...]).astype(BF16)

    o_ref[...] = jnp.dot(h_scr[...], w_ref[...], preferred_element_type=F32).astype(o_ref.dtype)


def _mod_spec(R, tiles_per_group):
    return pl.BlockSpec((None, R, D_MODEL), lambda i, *_: (i // tiles_per_group, 0, 0))


def mod_proj(x, sc, sh, w_bf, tm, tn, out_dtype):
    N = x.shape[0]
    G, R, _ = sc.shape
    n_out = w_bf.shape[1]
    tpg = (N // G) // tm
    return pl.pallas_call(
        _proj_kernel,
        grid=(N // tm, n_out // tn),
        in_specs=[pl.BlockSpec((tm, D_MODEL), lambda i, j: (i, 0)),
                  _mod_spec(R, tpg), _mod_spec(R, tpg),
                  pl.BlockSpec((D_MODEL, tn), lambda i, j: (0, j))],
        out_specs=pl.BlockSpec((tm, tn), lambda i, j: (i, j)),
        out_shape=jax.ShapeDtypeStruct((N, n_out), out_dtype),
        scratch_shapes=[pltpu.VMEM((tm, D_MODEL), BF16)],
        compiler_params=_cparams(("parallel", "arbitrary")),
        name="mod_proj",
    )(x, sc, sh, w_bf)


def _rot_half(x, cos, sin):
    half = cos.shape[-1]
    x1, x2 = x[:, :half], x[:, half:]
    return jnp.concatenate([x1 * cos - x2 * sin, x1 * sin + x2 * cos], axis=1)


def _group_norm_gate(o, gn, g):
    mu = jnp.mean(o, -1, keepdims=True)
    d = o - mu
    var = jnp.mean(d * d, -1, keepdims=True)
    return d * lax.rsqrt(var + GN_EPS) * gn * _silu(g)


def _ret_prompt_kernel(q_ref, k_ref, v_ref, g_ref, cos_ref, sin_ref, lg_ref, gn_ref,
                       o_ref, s_ref, s_scr, decay_scr, *, chunk):
    c = pl.program_id(2)
    lg_row = lg_ref[...]
    lg = lg_row[:, :1]

    @pl.when(c == 0)
    def _():
        s_scr[...] = jnp.zeros_like(s_scr)
        row = lax.broadcasted_iota(I32, (chunk, chunk), 0)
        col = lax.broadcasted_iota(I32, (chunk, chunk), 1)
        diff = (row - col).astype(F32)
        decay_scr[...] = jnp.where(diff >= 0, jnp.exp(lg_row * jnp.maximum(diff, 0.0)), 0.0)

    cos, sin = cos_ref[...], sin_ref[...]
    q = _rot_half(q_ref[...].astype(F32), cos, sin)
    k = _rot_half(k_ref[...].astype(F32), cos, sin) * (RET_DK ** -0.5)
    vb = v_ref[...].astype(BF16)
    qb = q.astype(BF16)

    decay = decay_scr[...]
    idx = lax.broadcasted_iota(I32, (chunk, 1), 0).astype(F32)
    q_dec = jnp.exp(lg * (idx + 1.0))
    k_dec = jnp.exp(lg * (chunk - 1.0 - idx))
    chunk_dec = jnp.exp(lg * chunk)

    s = s_scr[...]
    att = lax.dot_general(qb, k.astype(BF16), (((1,), (1,)), ((), ())),
                          preferred_element_type=F32) * decay
    o = (jnp.dot(att.astype(BF16), vb, preferred_element_type=F32)
         + jnp.dot(qb, s.astype(BF16), preferred_element_type=F32) * q_dec)
    kd_t = (k * k_dec).T.astype(BF16)
    s_new = chunk_dec * s + jnp.dot(kd_t, vb, preferred_element_type=F32)
    s_scr[...] = s_new
    o_ref[...] = _group_norm_gate(o, gn_ref[...], g_ref[...].astype(F32)).astype(o_ref.dtype)

    @pl.when(c == pl.num_programs(2) - 1)
    def _():
        s_ref[...] = s_new


def _log_gamma():
    return jnp.log(1.0 - jnp.power(2.0, -5.0 - jnp.arange(RET_HEADS, dtype=F32)))


def retention_prompt(proj, gn_g, cos, sin, chunk):
    B, T, _ = proj.shape
    lg_tab = jnp.broadcast_to(_log_gamma()[:, None, None], (RET_HEADS, 1, chunk))
    qk_blk = lambda off: pl.BlockSpec((None, chunk, RET_DK), lambda b, h, c: (b, c, off + h))
    v_blk = lambda off: pl.BlockSpec((None, chunk, RET_DV), lambda b, h, c: (b, c, off + h))
    tab = pl.BlockSpec((chunk, RET_DK // 2), lambda b, h, c: (c, 0))
    return pl.pallas_call(
        functools.partial(_ret_prompt_kernel, chunk=chunk),
        grid=(B, RET_HEADS, T // chunk),
        in_specs=[qk_blk(0), qk_blk(RET_QK // RET_DK),
                  v_blk(2 * RET_QK // RET_DV), v_blk((2 * RET_QK + RET_V) // RET_DV),
                  tab, tab,
                  pl.BlockSpec((None, 1, chunk), lambda b, h, c: (h, 0, 0)),
                  pl.BlockSpec((1, RET_DV), lambda b, h, c: (0, h))],
        out_specs=[pl.BlockSpec((None, chunk, RET_DV), lambda b, h, c: (b, c, h)),
                   pl.BlockSpec((None, None, RET_DK, RET_DV), lambda b, h, c: (b, h, 0, 0))],
        out_shape=[jax.ShapeDtypeStruct((B, T, RET_V), BF16),
                   jax.ShapeDtypeStruct((B, RET_HEADS, RET_DK, RET_DV), F32)],
        scratch_shapes=[pltpu.VMEM((RET_DK, RET_DV), F32), pltpu.VMEM((chunk, chunk), F32)],
        compiler_params=_cparams(("parallel", "parallel", "arbitrary")),
        name="retention_prompt",
    )(proj, proj, proj, proj, cos, sin, lg_tab, gn_g.reshape(1, RET_V))


def _ret_sample_kernel(qk_ref, v_ref, g_ref, s0_ref, cos_ref, sin_ref, gam_ref, gn_ref,
                       o_ref, s_ref):
    t = _rot_half(qk_ref[...], cos_ref[...], sin_ref[...])
    row = lax.broadcasted_iota(I32, t.shape, 0)
    t = jnp.where(row >= RET_HEADS, t * (RET_DK ** -0.5), t)
    pad = jnp.zeros((LANES - 2 * RET_HEADS, RET_DK), F32)
    t_t = jnp.concatenate([t, pad], axis=0).T
    for h in range(RET_HEADS):
        qc = t_t[:, h:h + 1]
        kc = t_t[:, RET_HEADS + h:RET_HEADS + h + 1]
        gam = gam_ref[h]
        vh = v_ref[h:h + 1, :]
        s = s0_ref[h]
        qk_dot = jnp.sum(t[h:h + 1, :] * t[RET_HEADS + h:RET_HEADS + h + 1, :], -1, keepdims=True)
        o = qk_dot * vh + gam * jnp.sum(s * qc, axis=0, keepdims=True)
        s_ref[h] = gam * s + kc * vh
        o_ref[h:h + 1, :] = _group_norm_gate(o, gn_ref[h:h + 1, :], g_ref[h:h + 1, :])


def retention_sample(proj, s0, gn_g, cos, sin):
    DB = proj.shape[0]
    qk = proj[:, :2 * RET_QK].reshape(DB, 2 * RET_HEADS, RET_DK)
    v = proj[:, 2 * RET_QK:2 * RET_QK + RET_V].reshape(DB, RET_HEADS, RET_DV)
    g = proj[:, 2 * RET_QK + RET_V:].reshape(DB, RET_HEADS, RET_DV)
    gam = jnp.broadcast_to(jnp.exp(_log_gamma())[:, None, None], (RET_HEADS, 1, RET_DV))
    full = lambda *shape: pl.BlockSpec(shape, lambda b: (0,) * len(shape))
    per_b = lambda *shape: pl.BlockSpec((None,) + shape, lambda b: (b,) + (0,) * len(shape))
    return pl.pallas_call(
        _ret_sample_kernel,
        grid=(DB,),
        in_specs=[per_b(2 * RET_HEADS, RET_DK), per_b(RET_HEADS, RET_DV), per_b(RET_HEADS, RET_DV),
                  per_b(RET_HEADS, RET_DK, RET_DV),
                  full(1, RET_DK // 2), full(1, RET_DK // 2),
                  full(RET_HEADS, 1, RET_DV), full(RET_HEADS, RET_DV)],
        out_specs=[per_b(RET_HEADS, RET_DV), per_b(RET_HEADS, RET_DK, RET_DV)],
        out_shape=[jax.ShapeDtypeStruct((DB, RET_HEADS, RET_DV), F32),
                   jax.ShapeDtypeStruct((DB, RET_HEADS, RET_DK, RET_DV), F32)],
        compiler_params=_cparams(("parallel",)),
        name="retention_sample",
    )(qk, v, g, s0, cos, sin, gam, gn_g.reshape(RET_HEADS, RET_DV))


def _out_ln_route_kernel(a_ref, w_ref, x_ref, g_ref, lng_ref, lnb_ref, sc2_ref, sh2_ref,
                         whi_ref, wlo_ref, b_ref, o_ref, gates_ref, gt_scr):
    y = jnp.dot(a_ref[...].astype(BF16), w_ref[...], preferred_element_type=F32)
    x1 = _layer_norm(ALPHA * x_ref[...] + g_ref[...] * y, lng_ref[...], lnb_ref[...])
    o_ref[...] = x1
    h2 = x1 * (1.0 + sc2_ref[...]) + sh2_ref[...]
    gates_ref[...] = _route(h2, whi_ref[...], wlo_ref[...], b_ref[...], gt_scr)


def out_proj_ln_route(a, w_bf, x, gate, ln_g, ln_b, sc2, sh2, router_w, tm):
    N, K = a.shape
    G, R, _ = gate.shape
    tpg = (N // G) // tm
    row = pl.BlockSpec((1, D_MODEL), lambda i: (0, 0))
    full = lambda r, c: pl.BlockSpec((r, c), lambda i: (0, 0))
    ms = _mod_spec(R, tpg)
    return pl.pallas_call(
        _out_ln_route_kernel,
        grid=(N // tm,),
        in_specs=[pl.BlockSpec((tm, K), lambda i: (i, 0)),
                  pl.BlockSpec((K, D_MODEL), lambda i: (0, 0)),
                  pl.BlockSpec((tm, D_MODEL), lambda i: (i, 0)),
                  ms, row, row, ms, ms,
                  full(D_MODEL, LANES), full(D_MODEL, LANES), full(1, LANES)],
        out_specs=[pl.BlockSpec((tm, D_MODEL), lambda i: (i, 0)),
                   pl.BlockSpec((tm, LANES), lambda i: (i, 0))],
        out_shape=[jax.ShapeDtypeStruct((N, D_MODEL), F32), jax.ShapeDtypeStruct((N, LANES), F32)],
        scratch_shapes=[pltpu.VMEM((LANES, tm), F32)],
        compiler_params=_cparams(("parallel",)),
        name="out_proj_ln_route",
    )(a, w_bf, x, gate, ln_g.reshape(1, D_MODEL), ln_b.reshape(1, D_MODEL), sc2, sh2, *router_w)


def _lane_roll(x, shift):
    return pltpu.roll(x, shift, axis=1)


def _route(h, whi, wlo, b, gt_scr):
    hi = h.astype(BF16)
    lo = (h - hi.astype(F32)).astype(BF16)
    logits = (jnp.dot(hi, whi, preferred_element_type=F32)
              + jnp.dot(hi, wlo, preferred_element_type=F32)
              + jnp.dot(lo, whi, preferred_element_type=F32)) + b
    lt = logits.T
    l_rows = [lt[e:e + 1, :] for e in range(N_EXPERTS)]
    m = functools.reduce(jnp.maximum, l_rows)
    ex = [jnp.exp(l - m) for l in l_rows]
    denom = functools.reduce(jnp.add, ex)
    p = [x / denom for x in ex]

    def first_max(vals):
        best, idx = vals[0], jnp.zeros_like(vals[0])
        for j, val in enumerate(vals[1:], 1):
            take = val > best
            idx = jnp.where(take, float(j), idx)
            best = jnp.where(take, val, best)
        return best, idx

    grp_score = []
    for g in range(N_GROUPS):
        a, b, c, d = p[g * EXPERTS_PER_GROUP:(g + 1) * EXPERTS_PER_GROUP]
        pairs = [a + b, a + c, a + d, b + c, b + d, c + d]
        grp_score.append(functools.reduce(jnp.maximum, pairs))
    _, g_sel = first_max(grp_score)
    neg_inf = jnp.full_like(m, -jnp.inf)
    masked = [jnp.where(g_sel == float(e // EXPERTS_PER_GROUP), p[e], neg_inf) for e in range(N_EXPERTS)]
    v1, i1 = first_max(masked)
    masked2 = [jnp.where(i1 == float(e), neg_inf, masked[e]) for e in range(N_EXPERTS)]
    v2, i2 = first_max(masked2)
    tot = v1 + v2
    w1, w2 = v1 / tot, v2 / tot
    gt_scr[...] = jnp.zeros_like(gt_scr)
    for e in range(N_EXPERTS):
        gt_scr[e:e + 1, :] = jnp.where(i1 == float(e), w1, 0.0) + jnp.where(i2 == float(e), w2, 0.0)
    for g in range(N_GROUPS):
        gt_scr[GROUP_LANE0 + g:GROUP_LANE0 + g + 1, :] = jnp.where(g_sel == float(g), 1.0, 0.0)
    return gt_scr[...].T


def router_weights(w_router, b_router):
    w_pad = jnp.zeros((D_MODEL, LANES), F32).at[:, :N_EXPERTS].set(w_router)
    w_hi = w_pad.astype(BF16)
    w_lo = (w_pad - w_hi.astype(F32)).astype(BF16)
    b_pad = jnp.zeros((1, LANES), F32).at[0, :N_EXPERTS].set(b_router)
    return w_hi, w_lo, b_pad


def _moe_kernel(x_ref, sc_ref, sh_ref, g2_ref, gates_ref, wg_ref, wu_ref, wd_ref, lng_ref, lnb_ref,
                o_ref, h_scr, acc_scr):
    e = pl.program_id(1)

    @pl.when(e == 0)
    def _():
        h_scr[...] = (x_ref[...] * (1.0 + sc_ref[...]) + sh_ref[...]).astype(BF16)
        acc_scr[...] = jnp.zeros_like(acc_scr)

    hb = h_scr[...]
    a = (_silu(jnp.dot(hb, wg_ref[...], preferred_element_type=F32))
         * jnp.dot(hb, wu_ref[...], preferred_element_type=F32))
    y = jnp.dot(a.astype(BF16), wd_ref[...], preferred_element_type=F32)
    gates = gates_ref[...]
    lane = lax.broadcasted_iota(I32, gates.shape, 1)
    gate_e = jnp.sum(jnp.where(lane == e, gates, 0.0), -1, keepdims=True)
    acc_scr[...] += gate_e * y

    @pl.when(e == pl.num_programs(1) - 1)
    def _():
        z = ALPHA * x_ref[...] + g2_ref[...] * acc_scr[...]
        o_ref[...] = _layer_norm(z, lng_ref[...], lnb_ref[...])


def moe_ln(x, sc, sh, g2, gates, l, wg_bf, wu_bf, wd_bf, ln_g, ln_b, tm):
    N = x.shape[0]
    G, R, _ = sc.shape
    tpg = (N // G) // tm
    ms = pl.BlockSpec((None, R, D_MODEL), lambda i, e: (i // tpg, 0, 0))
    row = pl.BlockSpec((1, D_MODEL), lambda i, e: (0, 0))
    return pl.pallas_call(
        _moe_kernel,
        grid=(N // tm, N_EXPERTS),
        in_specs=[pl.BlockSpec((tm, D_MODEL), lambda i, e: (i, 0)), ms, ms, ms,
                  pl.BlockSpec((tm, LANES), lambda i, e: (i, 0)),
                  pl.BlockSpec((None, None, D_MODEL, D_EXPERT), lambda i, e: (l, e, 0, 0)),
                  pl.BlockSpec((None, None, D_MODEL, D_EXPERT), lambda i, e: (l, e, 0, 0)),
                  pl.BlockSpec((None, None, D_EXPERT, D_MODEL), lambda i, e: (l, e, 0, 0)),
                  row, row],
        out_specs=pl.BlockSpec((tm, D_MODEL), lambda i, e: (i, 0)),
        out_shape=jax.ShapeDtypeStruct((N, D_MODEL), F32),
        scratch_shapes=[pltpu.VMEM((tm, D_MODEL), BF16), pltpu.VMEM((tm, D_MODEL), F32)],
        compiler_params=_cparams(("parallel", "arbitrary")),
        name="moe_ln",
    )(x, sc, sh, g2, gates, wg_bf, wu_bf, wd_bf, ln_g.reshape(1, D_MODEL), ln_b.reshape(1, D_MODEL))


def _split3_bf16(x):
    hi = x.astype(BF16)
    r = x - hi.astype(F32)
    mid = r.astype(BF16)
    lo = (r - mid.astype(F32)).astype(BF16)
    return hi, mid, lo


def _moe_group_kernel(x_ref, sc_ref, sh_ref, g2_ref, gates_ref, ltri_ref, wg_ref, wu_ref, wd_ref,
                      lng_ref, lnb_ref, o_ref,
                      h_scr, acc_scr, rank_scr, rank_t_scr, sel_t_scr, xs_scr, gs_scr, yacc_scr, cnt_smem,
                      *, cap):
    g, e = pl.program_id(1), pl.program_id(2)
    tm = x_ref.shape[0]
    lane = lax.broadcasted_iota(I32, (tm, LANES), 1)

    @pl.when((g == 0) & (e == 0))
    def _():
        h_scr[...] = (x_ref[...] * (1.0 + sc_ref[...]) + sh_ref[...]).astype(BF16)
        acc_scr[...] = jnp.zeros_like(acc_scr)
        gates = gates_ref[...]
        sel = jnp.where((lane >= GROUP_LANE0) & (lane < GROUP_LANE0 + N_GROUPS), gates, 0.0)
        rank = jnp.dot(ltri_ref[...], sel.astype(BF16), preferred_element_type=F32)
        rank_scr[...] = rank
        rank_t_scr[...] = rank.T
        sel_t_scr[...] = sel.T
        for gg in range(N_GROUPS):
            cnt_smem[gg] = jnp.sum(jnp.where(lane == GROUP_LANE0 + gg, sel, 0.0)).astype(I32)

    half = cap // 2
    cnt = cnt_smem[g]
    rem = cnt % cap
    use_tail = (rem > 0) & (rem <= half)
    n_full = cnt // cap + jnp.where(rem > half, 1, 0)
    tail_start = pl.multiple_of(n_full * cap, half)
    grp_row = pl.ds(GROUP_LANE0 + g, 1)

    def for_chunks(fn):
        def body(c, carry):
            fn(pl.multiple_of(c * cap, cap), cap)
            return carry
        lax.fori_loop(0, n_full, body, 0)

        @pl.when(use_tail)
        def _():
            fn(tail_start, half)

    @pl.when(e == 0)
    def _():
        expert_lanes = jnp.where(lane < N_EXPERTS, gates_ref[...], 0.0)
        parts = _split3_bf16(expert_lanes)
        g_parts = sum(_lane_roll(part.astype(F32), j * N_EXPERTS)
                      for j, part in enumerate(parts)).astype(BF16)
        rank_row, sel_row = rank_t_scr[grp_row, :], sel_t_scr[grp_row, :]

        def dispatch(start, size):
            rows = pl.ds(start, size)
            slot = (start + lax.broadcasted_iota(I32, (size, 1), 0)).astype(F32)
            p = jnp.where((rank_row == slot) & (sel_row > 0.5), 1.0, 0.0).astype(BF16)
            xs_scr[rows, :] = jnp.dot(p, h_scr[...], preferred_element_type=F32).astype(BF16)
            gs_scr[rows, :] = jnp.dot(p, g_parts, preferred_element_type=F32)
            yacc_scr[rows, :] = jnp.zeros((size, D_MODEL), F32)

        for_chunks(dispatch)

    def expert(start, size):
        rows = pl.ds(start, size)
        xs = xs_scr[rows, :]
        a = (_silu(jnp.dot(xs, wg_ref[...], preferred_element_type=F32))
             * jnp.dot(xs, wu_ref[...], preferred_element_type=F32))
        y = jnp.dot(a.astype(BF16), wd_ref[...], preferred_element_type=F32)
        gs = gs_scr[rows, :]
        lane_c = lax.broadcasted_iota(I32, gs.shape, 1)
        is_part = ((lane_c % N_EXPERTS) == g * EXPERTS_PER_GROUP + e) & (lane_c < 3 * N_EXPERTS)
        gate = jnp.sum(jnp.where(is_part, gs, 0.0), -1, keepdims=True)
        yacc_scr[rows, :] += gate * y

    for_chunks(expert)

    @pl.when(e == EXPERTS_PER_GROUP - 1)
    def _():
        in_grp = lane == GROUP_LANE0 + g
        rank_col = jnp.sum(jnp.where(in_grp, rank_scr[...], 0.0), -1, keepdims=True)
        sel_col = jnp.sum(jnp.where(in_grp, gates_ref[...], 0.0), -1, keepdims=True)

        def combine(start, size):
            slot = (start + lax.broadcasted_iota(I32, (1, size), 1)).astype(F32)
            p_t = jnp.where((rank_col == slot) & (sel_col > 0.5), 1.0, 0.0).astype(BF16)
            y_hi, y_mid, _ = _split3_bf16(yacc_scr[pl.ds(start, size), :])
            acc_scr[...] += (jnp.dot(p_t, y_hi, preferred_element_type=F32)
                             + jnp.dot(p_t, y_mid, preferred_element_type=F32))

        for_chunks(combine)

    @pl.when((g == N_GROUPS - 1) & (e == EXPERTS_PER_GROUP - 1))
    def _():
        z = ALPHA * x_ref[...] + g2_ref[...] * acc_scr[...]
        o_ref[...] = _layer_norm(z, lng_ref[...], lnb_ref[...])


def moe_group_ln(x, sc, sh, g2, gates, l, wg_bf, wu_bf, wd_bf, ln_g, ln_b, tm, cap):
    N = x.shape[0]
    G, R, _ = sc.shape
    tpg = (N // G) // tm
    ltri = (jnp.arange(tm)[:, None] > jnp.arange(tm)[None, :]).astype(BF16)
    ms = pl.BlockSpec((None, R, D_MODEL), lambda i, g, e: (i // tpg, 0, 0))
    row = pl.BlockSpec((1, D_MODEL), lambda i, g, e: (0, 0))
    expert = lambda i, g, e: (l, g * EXPERTS_PER_GROUP + e, 0, 0)
    w_in = pl.BlockSpec((None, None, D_MODEL, D_EXPERT), expert)
    w_out = pl.BlockSpec((None, None, D_EXPERT, D_MODEL), expert)
    return pl.pallas_call(
        functools.partial(_moe_group_kernel, cap=cap),
        grid=(N // tm, N_GROUPS, EXPERTS_PER_GROUP),
        in_specs=[pl.BlockSpec((tm, D_MODEL), lambda i, g, e: (i, 0)), ms, ms, ms,
                  pl.BlockSpec((tm, LANES), lambda i, g, e: (i, 0)),
                  pl.BlockSpec((tm, tm), lambda i, g, e: (0, 0)),
                  w_in, w_in, w_out, row, row],
        out_specs=pl.BlockSpec((tm, D_MODEL), lambda i, g, e: (i, 0)),
        out_shape=jax.ShapeDtypeStruct((N, D_MODEL), F32),
        scratch_shapes=[pltpu.VMEM((tm, D_MODEL), BF16), pltpu.VMEM((tm, D_MODEL), F32),
                        pltpu.VMEM((tm, LANES), F32), pltpu.VMEM((LANES, tm), F32),
                        pltpu.VMEM((LANES, tm), F32), pltpu.VMEM((tm, D_MODEL), BF16),
                        pltpu.VMEM((tm, LANES), F32), pltpu.VMEM((tm, D_MODEL), F32),
                        pltpu.SMEM((N_GROUPS,), I32)],
        compiler_params=_cparams(("arbitrary", "arbitrary", "arbitrary")),
        name="moe_group_ln",
    )(x, sc, sh, g2, gates, ltri, wg_bf, wu_bf, wd_bf, ln_g.reshape(1, D_MODEL), ln_b.reshape(1, D_MODEL))


LOG2_E = 1.4426950408889634
Q_PRESCALE = DSA_HEAD_DIM ** -0.5 * LOG2_E
DSA_IN_PAD = -(-DSA_IN // LANES) * LANES
DSA_QI_OFF = DSA_Q + 2 * DSA_KV
DSA_KW_OFF = DSA_QI_OFF + IDX_HEADS * IDX_DIM


def _rope_tables(pos, rot_dim, period, n_periods):
    half = rot_dim // 2
    inv_freq = 1.0 / (ROPE_THETA ** (jnp.arange(0, rot_dim, 2, dtype=F32) / rot_dim))
    ang = pos.astype(F32)[:, None] * inv_freq[None, :]
    cos, sin = jnp.cos(ang), jnp.sin(ang)
    T = pos.shape[0]
    zeros = jnp.zeros((T, period - 2 * half), F32)
    zh = jnp.zeros((T, half), F32)
    a1 = jnp.concatenate([cos, cos, zeros + 1.0], 1)
    b1 = jnp.concatenate([zh, sin, zeros], 1)
    c1 = jnp.concatenate([-sin, zh, zeros], 1)
    rest = LANES - n_periods * period
    pad1 = jnp.ones((T, rest), F32)
    pad0 = jnp.zeros((T, rest), F32)
    a = jnp.concatenate([a1] * n_periods + [pad1], 1)
    b = jnp.concatenate([b1] * n_periods + [pad0], 1)
    c = jnp.concatenate([c1] * n_periods + [pad0], 1)
    return a, b, c


def _apply_rope(x, a, b, c, half):
    return x * a + _lane_roll(x, half) * b + _lane_roll(x, LANES - half) * c


N_ROPE_TABLES = 9


def _dsa_sections(x_ref, sc_ref, sh_ref, w_ref, tab_refs):
    hb = (x_ref[...] * (1.0 + sc_ref[...]) + sh_ref[...]).astype(BF16)
    ma, mb, mc, ia, ib, ic, ka, kb, kc = (t[...] for t in tab_refs)

    def sec(off, width):
        return jnp.dot(hb, w_ref[:, off:off + width], preferred_element_type=F32)

    def lanes(x, h):
        return x[:, h * LANES:(h + 1) * LANES]

    q = sec(0, DSA_Q)
    q_heads = [_apply_rope(lanes(q, h), ma, mb, mc, ROT_DIM // 2) * Q_PRESCALE for h in range(DSA_HEADS)]
    k = sec(DSA_Q, DSA_KV)
    k = jnp.concatenate([_apply_rope(lanes(k, h), ma, mb, mc, ROT_DIM // 2) for h in range(DSA_KV_HEADS)], 1)
    v = sec(DSA_Q + DSA_KV, DSA_KV)
    qi = sec(DSA_QI_OFF, IDX_HEADS * IDX_DIM)
    qi_pairs = [_apply_rope(lanes(qi, h), ia, ib, ic, IDX_ROT_DIM // 2)
                for h in range(IDX_HEADS * IDX_DIM // LANES)]
    kw = _apply_rope(sec(DSA_KW_OFF, LANES), ka, kb, kc, IDX_ROT_DIM // 2)
    return q_heads, k, v, qi_pairs, kw


def _dsa_proj_kernel(x_ref, sc_ref, sh_ref, w_ref, *rest):
    tabs = rest[:N_ROPE_TABLES]
    q_ref, k_ref, v_ref, qi_ref, kw_ref = rest[N_ROPE_TABLES:]
    q_heads, k, v, qi_pairs, kw = _dsa_sections(x_ref, sc_ref, sh_ref, w_ref, tabs)
    for h, qh in enumerate(q_heads):
        q_ref[:, h * LANES:(h + 1) * LANES] = qh.astype(q_ref.dtype)
    for h, qp in enumerate(qi_pairs):
        qi_ref[:, h * LANES:(h + 1) * LANES] = qp.astype(qi_ref.dtype)
    k_ref[...] = k
    v_ref[...] = v
    kw_ref[...] = kw


def _dsa_proj_prompt_kernel(x_ref, sc_ref, sh_ref, w_ref, *rest, tq):
    tabs = rest[:N_ROPE_TABLES]
    k_ref, v_ref, kw_ref, kbf_ref, kwbf_ref, vt_ref, qt_ref, qit_ref, wit_ref = rest[N_ROPE_TABLES:]
    q_heads, k, v, qi_pairs, kw = _dsa_sections(x_ref, sc_ref, sh_ref, w_ref, tabs)
    tm = k.shape[0]
    k_ref[...] = k
    kbf_ref[...] = k.astype(BF16)
    v_ref[...] = v
    kw_ref[...] = kw
    kwbf_ref[...] = kw.astype(BF16)
    for n in range(DSA_KV_HEADS):
        vt_ref[n, 0] = v[:, n * DSA_HEAD_DIM:(n + 1) * DSA_HEAD_DIM].T.astype(BF16)
    heads_per_pair = LANES // IDX_DIM
    for j in range(tm // tq):
        rows = slice(j * tq, (j + 1) * tq)
        for h, qh in enumerate(q_heads):
            n, g = divmod(h, DSA_GROUP)
            qt_ref[j, n, :, g * tq:(g + 1) * tq] = qh[rows].T.astype(BF16)
        for hp, qp in enumerate(qi_pairs):
            t = qp[rows].T.astype(BF16)
            for s in range(heads_per_pair):
                h = hp * heads_per_pair + s
                qit_ref[j, :IDX_DIM, h * tq:(h + 1) * tq] = t[s * IDX_DIM:(s + 1) * IDX_DIM]
        qit_ref[j, IDX_DIM:, :] = jnp.zeros((LANES - IDX_DIM, IDX_HEADS * tq), BF16)
        wit_ref[j] = kw[rows].T[IDX_DIM:IDX_DIM + IDX_HEADS]


def _dsa_project_call(body, x, sc, sh, w_pad_bf, pos, tm, out_specs, out_shape, name):
    N = x.shape[0]
    G, R, _ = sc.shape
    tpg = (N // G) // tm
    n_tab = pos.shape[0] // tm
    tabs = (_rope_tables(pos, ROT_DIM, DSA_HEAD_DIM, 1)
            + _rope_tables(pos, IDX_ROT_DIM, IDX_DIM, 2)
            + _rope_tables(pos, IDX_ROT_DIM, IDX_DIM, 1))
    tab = pl.BlockSpec((tm, LANES), lambda i: (i % n_tab, 0))
    return pl.pallas_call(
        body,
        grid=(N // tm,),
        in_specs=[pl.BlockSpec((tm, D_MODEL), lambda i: (i, 0)), _mod_spec(R, tpg), _mod_spec(R, tpg),
                  pl.BlockSpec((D_MODEL, DSA_IN_PAD), lambda i: (0, 0))] + [tab] * N_ROPE_TABLES,
        out_specs=out_specs, out_shape=out_shape,
        compiler_params=_cparams(("parallel",)),
        name=name,
    )(x, sc, sh, w_pad_bf, *tabs)


def dsa_project(x, sc, sh, w_pad_bf, pos, tm):
    N = x.shape[0]
    out = lambda w: pl.BlockSpec((tm, w), lambda i: (i, 0))
    return _dsa_project_call(
        _dsa_proj_kernel, x, sc, sh, w_pad_bf, pos, tm,
        [out(DSA_Q), out(DSA_KV), out(DSA_KV), out(IDX_HEADS * IDX_DIM), out(LANES)],
        [jax.ShapeDtypeStruct((N, DSA_Q), BF16), jax.ShapeDtypeStruct((N, DSA_KV), F32),
         jax.ShapeDtypeStruct((N, DSA_KV), F32), jax.ShapeDtypeStruct((N, IDX_HEADS * IDX_DIM), BF16),
         jax.ShapeDtypeStruct((N, LANES), F32)], "dsa_project")


def dsa_project_prompt(x, sc, sh, w_pad_bf, pos, B, T, tm, tq):
    N = B * T
    tiles, n_qt = T // tm, tm // tq
    gq = DSA_GROUP * tq
    out = lambda w: pl.BlockSpec((tm, w), lambda i: (i, 0))
    by_tile = lambda *shape: pl.BlockSpec((None,) + shape,
                                          lambda i: (i // tiles, i % tiles) + (0,) * (len(shape) - 1))
    return _dsa_project_call(
        functools.partial(_dsa_proj_prompt_kernel, tq=tq), x, sc, sh, w_pad_bf, pos, tm,
        [out(DSA_KV), out(DSA_KV), out(LANES), out(DSA_KV), out(LANES),
         pl.BlockSpec((None, DSA_KV_HEADS, 1, DSA_HEAD_DIM, tm), lambda i: (i // tiles, 0, i % tiles, 0, 0)),
         by_tile(n_qt, DSA_KV_HEADS, DSA_HEAD_DIM, gq),
         by_tile(n_qt, LANES, IDX_HEADS * tq),
         by_tile(n_qt, IDX_HEADS, tq)],
        [jax.ShapeDtypeStruct((N, DSA_KV), F32), jax.ShapeDtypeStruct((N, DSA_KV), F32),
         jax.ShapeDtypeStruct((N, LANES), F32), jax.ShapeDtypeStruct((N, DSA_KV), BF16),
         jax.ShapeDtypeStruct((N, LANES), BF16),
         jax.ShapeDtypeStruct((B, DSA_KV_HEADS, tiles, DSA_HEAD_DIM, tm), BF16),
         jax.ShapeDtypeStruct((B, T // tq, DSA_KV_HEADS, DSA_HEAD_DIM, gq), BF16),
         jax.ShapeDtypeStruct((B, T // tq, LANES, IDX_HEADS * tq), BF16),
         jax.ShapeDtypeStruct((B, T // tq, IDX_HEADS, tq), F32)], "dsa_project_prompt")


CODE_NEG_INF = -2139095041
BISECT_FIRST = 22
BISECT_STAGE = 2


def _threshold_of_code(code):
    bits = code ^ ((code >> 31) & jnp.int32(0x7FFFFFFF))
    return jnp.where(code < CODE_NEG_INF, -jnp.inf, pltpu.bitcast(bits, F32))


def _reduce_row_groups(x, op, n_chains=8):
    parts = [x[r:r + SUBLANES] for r in range(0, x.shape[0], SUBLANES)]
    accs = parts[:n_chains]
    for j, part in enumerate(parts[n_chains:]):
        accs[j % len(accs)] = op(accs[j % len(accs)], part)
    while len(accs) > 1:
        accs = [op(a, b) for a, b in zip(accs[0::2], accs[1::2])] + (accs[-1:] if len(accs) % 2 else [])
    return accs[0]


def _sum_row_groups(x):
    return _reduce_row_groups(x, jnp.add)


def _max_row_groups(x):
    return _reduce_row_groups(x, jnp.maximum)


def _kth_largest_threshold(count_ge, shape, k):
    def body(s, p):
        cand = p + lax.shift_left(jnp.int32(1), 31 - s)
        return jnp.where(count_ge(_threshold_of_code(cand)) >= k, cand, p)
    return _threshold_of_code(lax.fori_loop(0, 32, body, jnp.full(shape, INT_MIN, I32)))


def _tie_index_bound(count_eq_below, shape, need, n_bits):
    def body(s, m):
        cand = m + lax.shift_left(jnp.int32(1), n_bits - 1 - s)
        return jnp.where(count_eq_below(cand) < need, cand, m)
    return lax.fori_loop(0, n_bits, body, jnp.zeros(shape, I32))


def _dsa_prompt_kernel(ki_ref, qit_ref, wit_ref, k_ref, vt_ref, qt_ref, o_ref,
                       score_scr, bias_scr, midx_scr, logit_scr, thr_scr, nge_scr,
                       *, tq, kb_size, cb_size, topk, idx_bits):
    i = pl.program_id(1)
    n_kb = ((i + 1) * tq + kb_size - 1) // kb_size
    n_cb = ((i + 1) * tq + cb_size - 1) // cb_size
    q_pos = i * tq + lax.broadcasted_iota(I32, (1, tq), 1)
    heads_per_dot = 2

    @pl.when(i == 0)
    def _():
        score_scr[...] = jnp.full(score_scr.shape, -jnp.inf, F32)

    def key_rows(kb):
        return pl.ds(pl.multiple_of(kb * kb_size, kb_size), kb_size)

    def l_index(kb, size=kb_size):
        return kb * size + lax.broadcasted_iota(I32, (size, 1), 0)

    def score_body(kb, carry):
        kib = ki_ref[key_rows(kb), :]
        acc = jnp.zeros((kb_size, tq), F32)
        for hp in range(IDX_HEADS // heads_per_dot):
            s = jnp.dot(kib, qit_ref[:, hp * heads_per_dot * tq:(hp + 1) * heads_per_dot * tq],
                        preferred_element_type=F32)
            for j in range(heads_per_dot):
                h = hp * heads_per_dot + j
                w = wit_ref[h:h + 1, :] * (IDX_HEADS ** -0.5 * IDX_DIM ** -0.5)
                acc = acc + jnp.maximum(s[:, j * tq:(j + 1) * tq], 0.0) * w
        allowed = l_index(kb) <= q_pos
        score_scr[key_rows(kb), :] = jnp.where(allowed, acc, -jnp.inf)
        return carry

    lax.fori_loop(0, n_kb, score_body, 0)

    def count(pred_fn):
        def body(cb, acc):
            rows = pl.ds(pl.multiple_of(cb * cb_size, cb_size), cb_size)
            m = jnp.where(pred_fn(score_scr[rows, :], cb), 1.0, 0.0)
            return acc + _sum_row_groups(m)
        acc = lax.fori_loop(0, n_cb, body, jnp.zeros((SUBLANES, tq), F32))
        return acc.sum(axis=0, keepdims=True)

    k_f = float(topk)

    def bisect(s_lo, s_hi):
        def body(s, carry):
            p, n_p = carry
            cand = p + lax.shift_left(jnp.int32(1), 31 - s)
            cand_thr = _threshold_of_code(cand)
            n_c = count(lambda score, cb: score >= cand_thr)
            take = n_c >= k_f
            return jnp.where(take, cand, p), jnp.where(take, n_c, n_p)
        p, n_p = lax.fori_loop(s_lo, s_hi, body, (thr_scr[...], nge_scr[...]))
        thr_scr[...] = p
        nge_scr[...] = n_p

    thr_scr[...] = jnp.full((1, tq), INT_MIN, I32)
    nge_scr[...] = jnp.full((1, tq), float(2 ** 30), F32)
    bisect(0, BISECT_FIRST)
    for s0 in range(BISECT_FIRST, 32, BISECT_STAGE):
        @pl.when(jnp.max(jnp.where(nge_scr[...] != k_f, 1.0, 0.0)) > 0.5)
        def _():
            bisect(s0, min(32, s0 + BISECT_STAGE))

    thr = _threshold_of_code(thr_scr[...])
    n_ge = nge_scr[...]
    n_gt = count(lambda score, cb: score > thr)
    need = k_f - n_gt
    excess = (n_ge - n_gt > need) & (thr > -jnp.inf)
    midx_scr[...] = jnp.full((1, tq), 2 ** idx_bits, I32)

    @pl.when(jnp.max(jnp.where(excess, 1.0, 0.0)) > 0.5)
    def _():
        m = _tie_index_bound(
            lambda c: count(lambda score, cb: (score == thr) & (l_index(cb, cb_size) < c)),
            (1, tq), need, idx_bits)
        midx_scr[...] = jnp.where(excess, m, 2 ** idx_bits)

    midx = midx_scr[...]

    def bias_body(kb, carry):
        score = score_scr[key_rows(kb), :]
        l = l_index(kb)
        sel = ((score > thr) | ((score == thr) & (l <= midx))) & (l <= q_pos)
        bias_scr[key_rows(kb), :] = jnp.where(sel, 0.0, NEG_BIG)
        return carry

    lax.fori_loop(0, n_kb, bias_body, 0)

    gq = DSA_GROUP * tq
    heads = range(DSA_KV_HEADS)

    def logits_of(kb):
        bias = bias_scr[key_rows(kb), :]
        bias = jnp.concatenate([bias] * DSA_GROUP, axis=1)
        blk_max = []
        for n in heads:
            kblk = k_ref[key_rows(kb), n * DSA_HEAD_DIM:(n + 1) * DSA_HEAD_DIM]
            logits = jnp.dot(kblk, qt_ref[n], preferred_element_type=F32) + bias
            logit_scr[n, key_rows(kb), :] = logits
            blk_max.append(_max_row_groups(logits))
        return tuple(blk_max)

    def absorb(kb, blk_max, state):
        new = []
        for n in heads:
            m_run, l_run, acc = state[n]
            m_new = jnp.maximum(m_run, jnp.max(blk_max[n], axis=0, keepdims=True))
            alpha = jnp.exp2(m_run - m_new)
            p = jnp.exp2(logit_scr[n, key_rows(kb), :] - m_new)
            acc = acc * alpha + jnp.dot(vt_ref[n, kb], p.astype(BF16), preferred_element_type=F32)
            new.append((m_new, l_run * alpha + _sum_row_groups(p), acc))
        return tuple(new)

    def att_body(kb, carry):
        blk_max, state = carry
        state = absorb(kb - 1, blk_max, state)
        return logits_of(kb), state

    state0 = tuple((jnp.full((1, gq), NEG_BIG, F32), jnp.zeros((SUBLANES, gq), F32),
                    jnp.zeros((DSA_HEAD_DIM, gq), F32)) for _ in heads)
    last_max, state = lax.fori_loop(1, n_kb, att_body, (logits_of(0), state0))
    fin = absorb(n_kb - 1, last_max, state)
    for n in heads:
        _, l8, acc = fin[n]
        o_t = acc / jnp.sum(l8, axis=0, keepdims=True)
        for g in range(DSA_GROUP):
            h = n * DSA_GROUP + g
            o_ref[:, h * DSA_HEAD_DIM:(h + 1) * DSA_HEAD_DIM] = (
                o_t[:, g * tq:(g + 1) * tq].T.astype(o_ref.dtype))


def dsa_prompt_attend(kw_bf, qi_t, wi_t, k_bf, v_t, q_t, B, T, topk, tq, kb_size):
    nqt = T // tq
    nkb = T // kb_size
    gq = DSA_GROUP * tq
    return pl.pallas_call(
        functools.partial(_dsa_prompt_kernel, tq=tq, kb_size=kb_size, cb_size=min(512, T), topk=topk,
                          idx_bits=max(1, (T - 1).bit_length())),
        grid=(B, nqt),
        in_specs=[pl.BlockSpec((None, T, LANES), lambda b, i: (b, 0, 0)),
                  pl.BlockSpec((None, None, LANES, IDX_HEADS * tq), lambda b, i: (b, i, 0, 0)),
                  pl.BlockSpec((None, None, IDX_HEADS, tq), lambda b, i: (b, i, 0, 0)),
                  pl.BlockSpec((None, T, DSA_KV), lambda b, i: (b, 0, 0)),
                  pl.BlockSpec((None, DSA_KV_HEADS, nkb, DSA_HEAD_DIM, kb_size),
                               lambda b, i: (b, 0, 0, 0, 0)),
                  pl.BlockSpec((None, None, DSA_KV_HEADS, DSA_HEAD_DIM, gq),
                               lambda b, i: (b, i, 0, 0, 0))],
        out_specs=pl.BlockSpec((tq, DSA_Q), lambda b, i: (b * nqt + i, 0)),
        out_shape=jax.ShapeDtypeStruct((B * T, DSA_Q), BF16),
        scratch_shapes=[pltpu.VMEM((T, tq), F32), pltpu.VMEM((T, tq), F32), pltpu.VMEM((1, tq), I32),
                        pltpu.VMEM((DSA_KV_HEADS, T, gq), F32),
                        pltpu.VMEM((1, tq), I32), pltpu.VMEM((1, tq), F32)],
        compiler_params=_cparams(("arbitrary", "arbitrary")),
        name="dsa_prompt_attend",
    )(kw_bf.reshape(B, T, LANES), qi_t, wi_t, k_bf.reshape(B, T, DSA_KV), v_t, q_t)


def _dsa_sample_score_kernel(pt_ref, qi_ref, wi_ref, kis_ref, kidx_hbm, o_ref, buf, sem, *, n_pages, page):
    slot = _fetch_pages_ahead(pt_ref, (kidx_hbm,), (buf,), sem, n_pages)
    pages = [buf.at[slot, j] for j in range(n_pages)]
    qi = qi_ref[...]
    w = wi_ref[...] * (IDX_HEADS ** -0.5 * IDX_DIM ** -0.5)
    for j in range(n_pages):
        dots = jnp.dot(qi, pages[j][...].astype(BF16), preferred_element_type=F32)
        o_ref[:, j * page:(j + 1) * page] = jnp.sum(jnp.maximum(dots, 0.0) * w, axis=0, keepdims=True)
    ki_self = kis_ref[...].astype(BF16).astype(F32)
    d_self = jnp.sum(qi.astype(F32) * ki_self, -1, keepdims=True)
    s_self = jnp.sum(jnp.maximum(d_self, 0.0) * w, axis=0, keepdims=True)
    lane = lax.broadcasted_iota(I32, (1, LANES), 1)
    o_ref[:, n_pages * page:] = jnp.where(lane == 0, s_self, -jnp.inf)


def dsa_sample_scores(page_table, qi_bf, wi, ki_self, cache_kidx_t):
    DB, n_pages = page_table.shape
    page = cache_kidx_t.shape[2]
    lp = n_pages * page + LANES
    per_b = lambda *shape: pl.BlockSpec((None,) + shape, lambda b, pt: (b,) + (0,) * len(shape))
    return pl.pallas_call(
        functools.partial(_dsa_sample_score_kernel, n_pages=n_pages, page=page),
        grid_spec=pltpu.PrefetchScalarGridSpec(
            num_scalar_prefetch=1, grid=(DB,),
            in_specs=[per_b(IDX_HEADS, IDX_DIM), per_b(IDX_HEADS, 1), per_b(1, IDX_DIM),
                      pl.BlockSpec(memory_space=pl.ANY)],
            out_specs=per_b(1, lp),
            scratch_shapes=[pltpu.VMEM((SCORE_PAGE_SLOTS, n_pages, IDX_DIM, page), F32),
                            pltpu.SemaphoreType.DMA((SCORE_PAGE_SLOTS,))]),
        out_shape=jax.ShapeDtypeStruct((DB, 1, lp), F32),
        compiler_params=_cparams(("arbitrary",)),
        name="dsa_sample_scores",
    )(page_table, qi_bf.reshape(DB, IDX_HEADS, IDX_DIM), wi.reshape(DB, IDX_HEADS, 1),
      ki_self.reshape(DB, 1, IDX_DIM), cache_kidx_t)


def _dsa_sample_select_kernel(s_ref, o_ref, *, n_keys, topk, idx_bits):
    score = s_ref[...]
    lane = lax.broadcasted_iota(I32, score.shape, 1)
    valid = lane < n_keys
    score = jnp.where(valid, score, -jnp.inf)

    def count(pred):
        return jnp.sum(jnp.where(pred, 1.0, 0.0), -1, keepdims=True)

    shape = (score.shape[0], 1)
    thr = _kth_largest_threshold(lambda t: count(score >= t), shape, float(topk))
    n_gt = count(score > thr)
    need = float(topk) - n_gt
    excess = (count(score == thr) > need) & (thr > -jnp.inf)
    m = _tie_index_bound(lambda c: count((score == thr) & (lane < c)), shape, need, idx_bits)
    midx = jnp.where(excess, m, 2 ** idx_bits)
    sel = ((score > thr) | ((score == thr) & (lane <= midx))) & valid
    o_ref[...] = jnp.where(sel, 0.0, NEG_BIG)


def dsa_sample_select(score, n_keys, topk):
    DB, lp = score.shape
    return pl.pallas_call(
        functools.partial(_dsa_sample_select_kernel, n_keys=n_keys, topk=topk,
                          idx_bits=max(1, (lp - 1).bit_length())),
        grid=(1,),
        in_specs=[pl.BlockSpec((DB, lp), lambda i: (0, 0))],
        out_specs=pl.BlockSpec((DB, lp), lambda i: (0, 0)),
        out_shape=jax.ShapeDtypeStruct((DB, lp), F32),
        compiler_params=_cparams(("arbitrary",)),
        name="dsa_sample_select",
    )(score)


SCORE_PAGE_SLOTS = 4
ATTEND_PAGE_SLOTS = 3
N_DMA_PRIORITIES = 2


def _page_copies(pt_ref, b, slot, hbm_refs, bufs, sem, n_pages):
    return [pltpu.make_async_copy(hbm.at[pt_ref[b, j]], buf.at[slot, j], sem.at[slot])
            for hbm, buf in zip(hbm_refs, bufs) for j in range(n_pages)]


def _fetch_pages_ahead(pt_ref, hbm_refs, bufs, sem, n_pages):
    n_slots = bufs[0].shape[0]
    ahead = n_slots - 1
    b = pl.program_id(0)
    n_steps = pl.num_programs(0)

    @pl.when(b == 0)
    def _():
        for s in range(ahead):
            @pl.when(s < n_steps)
            def _():
                for j, c in enumerate(_page_copies(pt_ref, s, s, hbm_refs, bufs, sem, n_pages)):
                    c.start(priority=j % N_DMA_PRIORITIES)

    @pl.when(b + ahead < n_steps)
    def _():
        copies = _page_copies(pt_ref, b + ahead, (b + ahead) % n_slots, hbm_refs, bufs, sem, n_pages)
        for j, c in enumerate(copies):
            c.start(priority=j % N_DMA_PRIORITIES)

    slot = b % n_slots
    for c in _page_copies(pt_ref, b, slot, hbm_refs, bufs, sem, n_pages):
        c.wait()
    return slot


def _dsa_sample_attend_kernel(pt_ref, q_ref, bias_ref, ks_ref, vs_ref, k_hbm, v_hbm, o_ref,
                              kbuf, vbuf, sem, logit_scr, *, n_pages, page):
    slot = _fetch_pages_ahead(pt_ref, (k_hbm, v_hbm), (kbuf, vbuf), sem, n_pages)
    k_pages = [kbuf.at[slot, j] for j in range(n_pages)]
    v_pages = [vbuf.at[slot, j] for j in range(n_pages)]
    q = q_ref[...]
    tail = n_pages * page
    lane = lax.broadcasted_iota(I32, (DSA_GROUP, LANES), 1)
    for n in range(DSA_KV_HEADS):
        hs = slice(n * DSA_GROUP, (n + 1) * DSA_GROUP)
        ds = slice(n * DSA_HEAD_DIM, (n + 1) * DSA_HEAD_DIM)
        head_rows = pl.ds(n, page, stride=DSA_KV_HEADS)
        qn = q[hs, :]
        for j in range(n_pages):
            lg = lax.dot_general(qn, k_pages[j][head_rows, :].astype(BF16), (((1,), (1,)), ((), ())),
                                 preferred_element_type=F32)
            logit_scr[hs, j * page:(j + 1) * page] = lg + bias_ref[:, j * page:(j + 1) * page]
        k_self = ks_ref[:, ds].astype(BF16).astype(F32)
        lg_self = jnp.sum(qn.astype(F32) * k_self, -1, keepdims=True)
        logit_scr[hs, tail:] = jnp.where(lane == 0, lg_self, 0.0) + bias_ref[:, tail:]
        logits = logit_scr[hs, :]
        m = jnp.max(logits, -1, keepdims=True)
        p = jnp.exp2(logits - m)
        p_bf = (p / jnp.sum(p, -1, keepdims=True)).astype(BF16)
        v_self = vs_ref[:, ds].astype(BF16).astype(F32)
        acc = p_bf[:, tail:].astype(F32)[:, :1] * v_self
        for j in range(n_pages):
            acc = acc + jnp.dot(p_bf[:, j * page:(j + 1) * page], v_pages[j][head_rows, :].astype(BF16),
                                preferred_element_type=F32)
        o_ref[hs, :] = acc


def dsa_sample_attend(page_table, q_bf, bias, k_self, v_self, cache_k, cache_v):
    DB, n_pages = page_table.shape
    rows = cache_k.shape[1]
    page = rows // DSA_KV_HEADS
    lp = n_pages * page + LANES
    per_b = lambda *shape: pl.BlockSpec((None,) + shape, lambda b, pt: (b,) + (0,) * len(shape))
    o = pl.pallas_call(
        functools.partial(_dsa_sample_attend_kernel, n_pages=n_pages, page=page),
        grid_spec=pltpu.PrefetchScalarGridSpec(
            num_scalar_prefetch=1, grid=(DB,),
            in_specs=[per_b(DSA_HEADS, DSA_HEAD_DIM), per_b(1, lp), per_b(1, DSA_KV), per_b(1, DSA_KV),
                      pl.BlockSpec(memory_space=pl.ANY), pl.BlockSpec(memory_space=pl.ANY)],
            out_specs=per_b(DSA_HEADS, DSA_HEAD_DIM),
            scratch_shapes=[pltpu.VMEM((ATTEND_PAGE_SLOTS, n_pages, rows, DSA_HEAD_DIM), F32),
                            pltpu.VMEM((ATTEND_PAGE_SLOTS, n_pages, rows, DSA_HEAD_DIM), F32),
                            pltpu.SemaphoreType.DMA((ATTEND_PAGE_SLOTS,)),
                            pltpu.VMEM((DSA_HEADS, lp), F32)]),
        out_shape=jax.ShapeDtypeStruct((DB, DSA_HEADS, DSA_HEAD_DIM), F32),
        compiler_params=_cparams(("arbitrary",)),
        name="dsa_sample_attend",
    )(page_table, q_bf.reshape(DB, DSA_HEADS, DSA_HEAD_DIM), bias.reshape(DB, 1, lp),
      k_self.reshape(DB, 1, DSA_KV), v_self.reshape(DB, 1, DSA_KV), cache_k, cache_v)
    return o.reshape(DB, DSA_Q)


def _retnet_tables(pos):
    inv_freq = jnp.power(RET_ANGLE_BASE, -jnp.linspace(0.0, 1.0, RET_DK // 2, dtype=F32))
    ang = pos.astype(F32)[:, None] * inv_freq[None, :]
    return jnp.cos(ang), jnp.sin(ang)


def _split_mod(mod_l, G, R):
    return [m.reshape(G, R, D_MODEL) for m in jnp.split(mod_l, 6, axis=-1)]


MOE_GROUP_TILE = 1024
MOE_GROUP_CAP = 256


def _finish_layer(a, w_out_bf, x, g1, sc2, sh2, g2, l, wts, tm):
    x, gates = out_proj_ln_route(a, w_out_bf, x, g1, wts["ln1_g"][l], wts["ln1_b"][l], sc2, sh2,
                                 wts["router"], tm)
    args = (x, sc2, sh2, g2, gates, l, wts["w_gate"], wts["w_up"], wts["w_down"],
            wts["ln2_g"][l], wts["ln2_b"][l])
    rows_per_mod = x.shape[0] // sc2.shape[0]
    if sc2.shape[1] == 1 and rows_per_mod >= 2 * MOE_GROUP_CAP:
        tile = min(MOE_GROUP_TILE, rows_per_mod)
        return moe_group_ln(*args, tile, MOE_GROUP_CAP)
    return moe_ln(*args, tm)


def kernel(x_prompt, x_sample, state_ret, cache_k, cache_v, cache_kidx, page_table,
           c_prompt, c_sample, w_mod, b_mod, ln1_g, ln1_b, ln2_g, ln2_b,
           w_in_ret, gn_ret_g, w_out_ret, w_in_dsa, w_out_dsa,
           w_router, b_router, w_gate, w_up, w_down):
    B, T, _ = x_prompt.shape
    DB = x_sample.shape[0]
    assert x_sample.shape[1] == 1
    n_pages = page_table.shape[1]
    page = cache_k.shape[2]
    past = n_pages * page
    n_pool = cache_k.shape[1]

    wts = dict(router=router_weights(w_router, b_router), ln1_g=ln1_g, ln1_b=ln1_b, ln2_g=ln2_g, ln2_b=ln2_b,
               w_gate=w_gate.astype(BF16), w_up=w_up.astype(BF16), w_down=w_down.astype(BF16))
    w_mod_bf = w_mod.astype(BF16)
    w_in_ret_bf = w_in_ret[0].astype(BF16)
    w_out_ret_bf = w_out_ret[0].astype(BF16)
    w_in_dsa_bf = jnp.pad(w_in_dsa[0], ((0, 0), (0, DSA_IN_PAD - DSA_IN))).astype(BF16)
    w_out_dsa_bf = w_out_dsa[0].astype(BF16)

    pos_p = jnp.arange(T, dtype=I32)
    pos_s = jnp.full((1,), past, I32)

    tm_p = min(512, T)
    tq = min(128, T)
    xp = x_prompt.reshape(B * T, D_MODEL)
    mod_p = modulation(c_prompt, w_mod_bf, b_mod)
    sh1, sc1, g1, sh2, sc2, g2 = _split_mod(mod_p[0], B, 1)
    proj = mod_proj(xp, sc1, sh1, w_in_ret_bf, min(1024, T), 1536, BF16)
    cos_p, sin_p = _retnet_tables(pos_p)
    gated, ret_p = retention_prompt(proj.reshape(B, T, RET_IN), gn_ret_g[0], cos_p, sin_p, min(256, T))
    xp = _finish_layer(gated.reshape(B * T, RET_V), w_out_ret_bf, xp, g1, sc2, sh2, g2, 0, wts, tm_p)

    sh1, sc1, g1, sh2, sc2, g2 = _split_mod(mod_p[1], B, 1)
    k_p, v_p, kw_p, k_bf, kw_bf, v_t, q_t, qi_t, wi_t = dsa_project_prompt(
        xp, sc1, sh1, w_in_dsa_bf, pos_p, B, T, tm_p, tq)
    o_p = dsa_prompt_attend(kw_bf, qi_t, wi_t, k_bf, v_t, q_t, B, T, min(DSA_TOPK, T // 4), tq, tm_p)
    xp = _finish_layer(o_p, w_out_dsa_bf, xp, g1, sc2, sh2, g2, 1, wts, tm_p)

    xs = x_sample.reshape(DB, D_MODEL)
    mod_s = modulation(c_sample, w_mod_bf, b_mod)
    sh1, sc1, g1, sh2, sc2, g2 = _split_mod(mod_s[0], 1, DB)
    proj_s = mod_proj(xs, sc1, sh1, w_in_ret_bf, DB, 1536, F32)
    cos_s, sin_s = _retnet_tables(pos_s)
    gated_s, ret_s = retention_sample(proj_s, state_ret[0], gn_ret_g[0], cos_s, sin_s)
    xs = _finish_layer(gated_s.reshape(DB, RET_V), w_out_ret_bf, xs, g1, sc2, sh2, g2, 0, wts, DB)

    sh1, sc1, g1, sh2, sc2, g2 = _split_mod(mod_s[1], 1, DB)
    pos_rows = jnp.full((DB,), past, I32)
    q_s, k_s, v_s, qi_s, kw_s = dsa_project(xs, sc1, sh1, w_in_dsa_bf, pos_rows, DB)
    ki_s = kw_s[:, :IDX_DIM]
    score = dsa_sample_scores(page_table, qi_s, kw_s[:, IDX_DIM:IDX_DIM + IDX_HEADS], ki_s,
                              jnp.swapaxes(cache_kidx[0], 1, 2))
    bias = dsa_sample_select(score.reshape(DB, past + LANES), past + 1, min(DSA_TOPK, (past + 1) // 4))
    o_s = dsa_sample_attend(page_table, q_s, bias, k_s, v_s,
                            cache_k[0].reshape(n_pool, page * DSA_KV_HEADS, DSA_HEAD_DIM),
                            cache_v[0].reshape(n_pool, page * DSA_KV_HEADS, DSA_HEAD_DIM))
    xs = _finish_layer(o_s, w_out_dsa_bf, xs, g1, sc2, sh2, g2, 1, wts, DB)

    kv_shape = (DSA_KV_HEADS, DSA_HEAD_DIM)
    return (xp.reshape(B, T, D_MODEL), xs.reshape(DB, 1, D_MODEL),
            ret_p[None], ret_s[None],
            k_p.reshape(1, B, T, *kv_shape), v_p.reshape(1, B, T, *kv_shape),
            kw_p[:, :IDX_DIM].reshape(1, B, T, IDX_DIM),
            k_s.reshape(1, DB, 1, *kv_shape), v_s.reshape(1, DB, 1, *kv_shape),
            ki_s.reshape(1, DB, 1, IDX_DIM))
```
